```python
import jax, jax.numpy as jnp
from jax import lax
import numpy as np

D_MODEL = 1024
BATCH = 8
SEQ = 4096
DEPTH = 2

CTX_LEN = 256
GRID_W = 64
EPS = 1e-6
M_HEADS = 4
M_HEAD_DIM = D_MODEL // M_HEADS
M_WIDTH = M_HEADS * M_HEAD_DIM
M_CHUNK = 64
QK_CONV = 3
POOL_WINDOWS = (2, 4, 8, 16)
POOL_GROUPS = len(POOL_WINDOWS)
POOL_WIDTH = D_MODEL // 2
POOL_GROUP_DIM = POOL_WIDTH // POOL_GROUPS
D_FF = 2816
N_EXPERTS = 8
TOP_K = 2
D_FF_EXPERT = 3584
N_DENSE = (DEPTH + 1) // 2
N_MOE = DEPTH // 2
Q_OFF = 0
K_OFF = M_WIDTH
V_OFF = 2 * M_WIDTH
O_OFF = 3 * M_WIDTH
IF_OFF = 4 * M_WIDTH
N_GATE_COLS = 2 * 2 * M_HEADS
POOL_OFF = IF_OFF + N_GATE_COLS
GA_OFF = POOL_OFF + POOL_WIDTH
GB_OFF = GA_OFF + D_MODEL
IN_COLS = GB_OFF + D_MODEL

kernel_name = 'hybrid_mlstm_pool_moe_dit'


def _rmsnorm(x, g):
    xf = x.astype(jnp.float32)
    r = lax.rsqrt(jnp.mean(xf * xf, axis=-1, keepdims=True) + EPS)
    return (xf * r).astype(x.dtype) * g


def _dwconv_centred(x, w):
    k = w.shape[0]
    pad = k // 2
    s = x.shape[1]
    xp = jnp.pad(x, ((0, 0), (pad, pad), (0, 0)))
    out = xp[:, 0:s] * w[0]
    for j in range(1, k):
        out = out + xp[:, j:j + s] * w[j]
    return out


def _split_heads(t):
    b, s, _ = t.shape
    return t.reshape(b, s, M_HEADS, M_HEAD_DIM).transpose(0, 2, 1, 3)


def _mlstm_chunkwise(q, k, v, ig, lf, state, with_outputs):
    b_, h_, s_, dh = q.shape
    nc = s_ // M_CHUNK

    def chunks(t):
        t = t.reshape(b_, h_, nc, M_CHUNK, *t.shape[3:])
        return jnp.moveaxis(t, 2, 0)

    lower = jnp.tril(jnp.ones((M_CHUNK, M_CHUNK), dtype=bool))

    def step(carry, xs):
        c_st, n_st, m_st = carry
        qc, kc, vc, ic, fc = xs
        bcum = jnp.cumsum(fc, axis=-1)
        b_last = bcum[..., -1]
        g_end = b_last[..., None] - bcum + ic
        m_new = jnp.maximum(b_last + m_st, jnp.max(g_end, axis=-1))
        wk = jnp.exp(g_end - m_new[..., None])
        wc = jnp.exp(b_last + m_st - m_new)
        c_new = wc[..., None, None] * c_st + jnp.einsum('bhs,bhsd,bhse->bhde', wk, kc, vc)
        n_new = wc[..., None] * n_st + jnp.einsum('bhs,bhsd->bhd', wk, kc)
        if not with_outputs:
            return (c_new, n_new, m_new), None
        a_inter = bcum + m_st[..., None]
        dmat = jnp.where(lower, bcum[..., :, None] - bcum[..., None, :] + ic[..., None, :], -jnp.inf)
        m_row = jnp.maximum(a_inter, jnp.max(dmat, axis=-1))
        w_inter = jnp.exp(a_inter - m_row)
        scores = jnp.einsum('bhjd,bhsd->bhjs', qc, kc) * jnp.exp(dmat - m_row[..., None])
        num = (w_inter[..., None] * jnp.einsum('bhjd,bhde->bhje', qc, c_st)
               + jnp.einsum('bhjs,bhse->bhje', scores, vc))
        den = w_inter * jnp.einsum('bhjd,bhd->bhj', qc, n_st) + jnp.sum(scores, axis=-1)
        hc = num / jnp.maximum(jnp.abs(den), jnp.exp(-m_row))[..., None]
        return (c_new, n_new, m_new), hc

    state, hs = lax.scan(step, state, (chunks(q), chunks(k), chunks(v), chunks(ig), chunks(lf)))
    if hs is not None:
        hs = jnp.moveaxis(hs, 0, 2).reshape(b_, h_, s_, dh)
    return state, hs


def _box_mean(x, axis, w):
    n = x.shape[axis]
    cs = jnp.cumsum(x, axis=axis)
    zero = jnp.zeros_like(lax.slice_in_dim(cs, 0, 1, axis=axis))
    cs = jnp.concatenate([zero, cs], axis=axis)
    t = jnp.arange(n)
    lo = jnp.clip(t - w // 2, 0, n)
    hi = jnp.clip(t + w // 2, 0, n)
    total = jnp.take(cs, hi, axis=axis) - jnp.take(cs, lo, axis=axis)
    shape = [1] * x.ndim
    shape[axis] = n
    return total / (hi - lo).astype(x.dtype).reshape(shape)


def _multiscale_pool(p, rows):
    b, s, _ = p.shape
    pf = p.astype(jnp.float32)
    outs = []
    for g, w in enumerate(POOL_WINDOWS):
        xg = pf[..., g * POOL_GROUP_DIM:(g + 1) * POOL_GROUP_DIM]
        if rows is None:
            pooled = _box_mean(xg, 1, w)
        else:
            grid = xg.reshape(b, rows, GRID_W, POOL_GROUP_DIM)
            pooled = _box_mean(_box_mean(grid, 2, w), 1, w).reshape(b, s, POOL_GROUP_DIM)
        outs.append(pooled - xg)
    return jnp.concatenate(outs, axis=-1).astype(p.dtype)


def _mixer(u, w_in, conv_qk, b_if, head_gain, w_pool, pool_scale, w_pa, w_pb, w_out,
           init_f, init_b, rows, with_outputs):
    b, s, _ = u.shape
    f32 = jnp.float32
    z = u @ w_in
    qk = jax.nn.silu(_dwconv_centred(z[..., Q_OFF:V_OFF], conv_qk))
    q = _split_heads(qk[..., :M_WIDTH]).astype(f32)
    k = _split_heads(qk[..., M_WIDTH:]).astype(f32) * (M_HEAD_DIM ** -0.5)
    v = _split_heads(z[..., V_OFF:O_OFF]).astype(f32)
    gates = z[..., IF_OFF:POOL_OFF].astype(f32).reshape(b, s, 2, 2, M_HEADS) + b_if.astype(f32)
    gates = gates.transpose(2, 3, 0, 4, 1)
    flip = lambda t: jnp.flip(t, axis=2)
    st_f, h_f = _mlstm_chunkwise(q, k, v, gates[0, 0], jax.nn.log_sigmoid(gates[0, 1]),
                                 init_f, with_outputs)
    st_b, h_b = _mlstm_chunkwise(flip(q), flip(k), flip(v), flip(gates[1, 0]),
                                 flip(jax.nn.log_sigmoid(gates[1, 1])), init_b, with_outputs)
    if not with_outputs:
        return None, st_f, st_b
    hm = h_f + flip(h_b)
    mu = jnp.mean(hm, axis=-1, keepdims=True)
    var = jnp.mean(jnp.square(hm - mu), axis=-1, keepdims=True)
    hm = ((hm - mu) * lax.rsqrt(var + EPS)).transpose(0, 2, 1, 3).reshape(b, s, M_WIDTH).astype(u.dtype)
    hm = jax.nn.sigmoid(z[..., O_OFF:IF_OFF]) * (hm * head_gain)
    pm = _multiscale_pool(z[..., POOL_OFF:GA_OFF], rows)
    pm = jnp.einsum('bsgc,gce->bsge', pm.reshape(b, s, POOL_GROUPS, POOL_GROUP_DIM), w_pool)
    pm = pm.reshape(b, s, POOL_WIDTH) * pool_scale
    merged = (jax.nn.sigmoid(z[..., GA_OFF:GB_OFF]) * (hm @ w_pa)
              + jax.nn.sigmoid(z[..., GB_OFF:IN_COLS]) * (pm @ w_pb))
    return merged @ w_out, st_f, st_b


def _swiglu(t, wg, wu, wd):
    return (jax.nn.silu(t @ wg) * (t @ wu)) @ wd


def _moe(u, w_router, w_g, w_u, w_d):
    b, s, d = u.shape
    t = u.reshape(b * s, d)
    logits = (t @ w_router).astype(jnp.float32)
    top_val, top_idx = lax.top_k(logits, TOP_K)
    probs = jax.nn.softmax(top_val, axis=-1)
    gate = jnp.sum(jax.nn.one_hot(top_idx, N_EXPERTS, dtype=jnp.float32) * probs[..., None], axis=1)
    gate = gate.astype(t.dtype)
    y = jnp.zeros_like(t)
    for e in range(N_EXPERTS):
        y = y + gate[:, e:e + 1] * _swiglu(t, w_g[e], w_u[e], w_d[e])
    return y.reshape(b, s, d)


def _channel_mixer(l, u, w_ff_gate, w_ff_up, w_ff_down, w_router, w_exp_gate, w_exp_up, w_exp_down):
    j = l // 2
    if l % 2 == 0:
        return _swiglu(u, w_ff_gate[j], w_ff_up[j], w_ff_down[j])
    return _moe(u, w_router[j], w_exp_gate[j], w_exp_up[j], w_exp_down[j])


def setup_inputs(seed: int = 0) -> dict:
    key = jax.random.key(seed)
    ks = jax.random.split(key, 26)
    nrm = jax.random.normal
    f32 = jnp.float32
    d = D_MODEL
    return {
        'x': nrm(ks[0], (BATCH, SEQ, d), f32),
        'c': nrm(ks[1], (BATCH, d), f32),
        'ctx': nrm(ks[2], (BATCH, CTX_LEN, d), f32),
        'c_ctx': nrm(ks[3], (d,), f32),
        'w_ada': nrm(ks[4], (DEPTH, d, 6 * d), f32) * (0.5 * d ** -0.5),
        'b_ada': 0.02 * nrm(ks[5], (DEPTH, 6 * d), f32),
        'g_mix': 1.0 + 0.1 * nrm(ks[6], (DEPTH, d), f32),
        'w_in': nrm(ks[7], (DEPTH, d, IN_COLS), f32) * d ** -0.5,
        'conv_qk': nrm(ks[8], (DEPTH, QK_CONV, 2 * M_WIDTH), f32) * QK_CONV ** -0.5,
        'b_if': jnp.array([-1.0, 3.0], f32)[None, None, :, None] + 0.3 * nrm(ks[9], (DEPTH, 2, 2, M_HEADS), f32),
        'head_gain': 1.0 + 0.1 * nrm(ks[10], (DEPTH, M_WIDTH), f32),
        'w_pool': nrm(ks[11], (DEPTH, POOL_GROUPS, POOL_GROUP_DIM, POOL_GROUP_DIM), f32) * POOL_GROUP_DIM ** -0.5,
        'pool_scale': 1.0 + 0.1 * nrm(ks[12], (DEPTH, POOL_WIDTH), f32),
        'w_pa': nrm(ks[13], (DEPTH, M_WIDTH, d), f32) * M_WIDTH ** -0.5,
        'w_pb': nrm(ks[14], (DEPTH, POOL_WIDTH, d), f32) * POOL_WIDTH ** -0.5,
        'w_out': nrm(ks[15], (DEPTH, d, d), f32) * d ** -0.5,
        'g_ffn': 1.0 + 0.1 * nrm(ks[16], (DEPTH, d), f32),
        'w_ff_gate': nrm(ks[17], (N_DENSE, d, D_FF), f32) * d ** -0.5,
        'w_ff_up': nrm(ks[18], (N_DENSE, d, D_FF), f32) * d ** -0.5,
        'w_ff_down': nrm(ks[19], (N_DENSE, D_FF, d), f32) * D_FF ** -0.5,
        'w_router': nrm(ks[20], (N_MOE, d, N_EXPERTS), f32) * d ** -0.5,
        'w_exp_gate': nrm(ks[21], (N_MOE, N_EXPERTS, d, D_FF_EXPERT), f32) * d ** -0.5,
        'w_exp_up': nrm(ks[22], (N_MOE, N_EXPERTS, d, D_FF_EXPERT), f32) * d ** -0.5,
        'w_exp_down': nrm(ks[23], (N_MOE, N_EXPERTS, D_FF_EXPERT, d), f32) * D_FF_EXPERT ** -0.5,
        'g_final': 1.0 + 0.1 * nrm(ks[24], (d,), f32),
    }


def reference(x, c, ctx, c_ctx, w_ada, b_ada, g_mix, w_in, conv_qk, b_if, head_gain, w_pool, pool_scale,
              w_pa, w_pb, w_out, g_ffn, w_ff_gate, w_ff_up, w_ff_down, w_router, w_exp_gate, w_exp_up,
              w_exp_down, g_final):
    rows = x.shape[1] // GRID_W
    bsz = ctx.shape[0]
    f32 = jnp.float32
    zero_state = (jnp.zeros((bsz, M_HEADS, M_HEAD_DIM, M_HEAD_DIM), f32),
                  jnp.zeros((bsz, M_HEADS, M_HEAD_DIM), f32),
                  jnp.zeros((bsz, M_HEADS), f32))
    h = x
    h_ctx = ctx
    ffn_w = (w_ff_gate, w_ff_up, w_ff_down, w_router, w_exp_gate, w_exp_up, w_exp_down)
    for l in range(DEPTH):
        last = l == DEPTH - 1
        mod = jax.nn.silu(c) @ w_ada[l] + b_ada[l]
        mod_c = jax.nn.silu(c_ctx) @ w_ada[l] + b_ada[l]
        sh1, sc1, gt1, sh2, sc2, gt2 = jnp.split(mod[:, None, :], 6, axis=-1)
        sh1c, sc1c, gt1c, sh2c, sc2c, gt2c = jnp.split(mod_c, 6, axis=-1)
        mix_w = (w_in[l], conv_qk[l], b_if[l], head_gain[l], w_pool[l], pool_scale[l],
                 w_pa[l], w_pb[l], w_out[l])
        u_c = _rmsnorm(h_ctx, g_mix[l]) * (1.0 + sc1c) + sh1c
        y_c, st_f, st_b = _mixer(u_c, *mix_w, zero_state, zero_state, None, not last)
        u_x = _rmsnorm(h, g_mix[l]) * (1.0 + sc1) + sh1
        y_x, _, _ = _mixer(u_x, *mix_w, st_f, st_b, rows, True)
        h = h + gt1 * y_x
        u_x = _rmsnorm(h, g_ffn[l]) * (1.0 + sc2) + sh2
        h = h + gt2 * _channel_mixer(l, u_x, *ffn_w)
        if not last:
            h_ctx = h_ctx + gt1c * y_c
            u_c = _rmsnorm(h_ctx, g_ffn[l]) * (1.0 + sc2c) + sh2c
            h_ctx = h_ctx + gt2c * _channel_mixer(l, u_c, *ffn_w)
    return _rmsnorm(h, g_final)
```

```python
import functools

import jax
import jax.numpy as jnp
from jax import lax
from jax.experimental import pallas as pl
from jax.experimental.pallas import tpu as pltpu

F32 = jnp.float32
BF16 = jnp.bfloat16
EPS = 1e-6
M_HEADS = 4
GRID_W = 64
POOL_WINDOWS = (2, 4, 8, 16)
POOL_GROUP_DIM = 128
CHUNK = 256
N_MOD = 6
GATE_PAD = 128
VMEM_LIMIT = 56 * 1024 * 1024


def _cparams(sem):
    return pltpu.CompilerParams(dimension_semantics=sem, vmem_limit_bytes=VMEM_LIMIT)


def _sigmoid(x):
    return 1.0 / (1.0 + jnp.exp(-x))


def _silu(x):
    return x * _sigmoid(x)


def _log_sigmoid(x):
    return jnp.minimum(x, 0.0) - jnp.log(1.0 + jnp.exp(-jnp.abs(x)))


def _split3(x):
    hi = x.astype(BF16)
    r1 = x - hi.astype(F32)
    mid = r1.astype(BF16)
    lo = (r1 - mid.astype(F32)).astype(BF16)
    return hi, mid, lo


def _dot(a, b):
    return jnp.dot(a, b, preferred_element_type=F32)


def _dot_nt(a, b):
    return lax.dot_general(a, b, (((1,), (1,)), ((), ())), preferred_element_type=F32)


def _dot_tn(a, b):
    return lax.dot_general(a, b, (((0,), (0,)), ((), ())), preferred_element_type=F32)


def _rms_mod(x, g, sc, sh):
    r = lax.rsqrt(jnp.mean(x * x, axis=-1, keepdims=True) + EPS)
    return (x * r) * g * (1.0 + sc) + sh


def _ada_kernel(c_ref, w_ref, b_ref, o_ref):
    c = c_ref[...]
    o_ref[0] = _dot(_silu(c).astype(BF16), w_ref[0].astype(BF16)) + b_ref[0]


def _ada(cc, w_ada, b_ada):
    depth, d, n = w_ada.shape
    tn = 1024
    return pl.pallas_call(
        _ada_kernel,
        out_shape=jax.ShapeDtypeStruct((depth, cc.shape[0], n), F32),
        grid=(depth, n // tn),
        in_specs=[pl.BlockSpec(cc.shape, lambda l, j: (0, 0)),
                  pl.BlockSpec((1, d, tn), lambda l, j: (l, 0, j)),
                  pl.BlockSpec((1, 1, tn), lambda l, j: (l, 0, j))],
        out_specs=pl.BlockSpec((1, cc.shape[0], tn), lambda l, j: (l, 0, j)),
        compiler_params=_cparams(("arbitrary", "arbitrary")),
        name="ada",
    )(cc, w_ada, b_ada.reshape(depth, 1, n))


def _inproj_kernel(x_ref, g_ref, sc_ref, sh_ref, w_ref, zm_ref, zp_ref, zg_ref, u_ref, *, n_main, n_pool, cw):
    u_ref[...] = _rms_mod(x_ref[...], g_ref[...], sc_ref[0], sh_ref[0]).astype(BF16)
    for c in range(n_main // cw):
        zm_ref[:, c * cw:(c + 1) * cw] = _dot(u_ref[...], w_ref[:, c * cw:(c + 1) * cw]).astype(BF16)
    for c in range(n_pool // cw):
        lo = n_main + c * cw
        zp_ref[:, c * cw:(c + 1) * cw] = _dot(u_ref[...], w_ref[:, lo:lo + cw])
    zg_ref[...] = _dot(u_ref[...], w_ref[:, n_main + n_pool:])


def _inproj(h, g, mod3, w_cat, mod_idx, *, tm, n_main, n_pool):
    r, d = h.shape
    ncols = w_cat.shape[1]
    kern = functools.partial(_inproj_kernel, n_main=n_main, n_pool=n_pool, cw=512)
    return pl.pallas_call(
        kern,
        out_shape=(jax.ShapeDtypeStruct((r, n_main), BF16),
                   jax.ShapeDtypeStruct((r, n_pool), F32),
                   jax.ShapeDtypeStruct((r, GATE_PAD), F32)),
        grid=(r // tm,),
        in_specs=[pl.BlockSpec((tm, d), lambda i: (i, 0)),
                  pl.BlockSpec((1, d), lambda i: (0, 0)),
                  pl.BlockSpec((1, 1, d), lambda i: (mod_idx(i) * N_MOD + 1, 0, 0)),
                  pl.BlockSpec((1, 1, d), lambda i: (mod_idx(i) * N_MOD + 0, 0, 0)),
                  pl.BlockSpec((d, ncols), lambda i: (0, 0))],
        out_specs=(pl.BlockSpec((tm, n_main), lambda i: (i, 0)),
                   pl.BlockSpec((tm, n_pool), lambda i: (i, 0)),
                   pl.BlockSpec((tm, GATE_PAD), lambda i: (i, 0))),
        scratch_shapes=[pltpu.VMEM((tm, d), BF16)],
        compiler_params=_cparams(("arbitrary",)),
        name="inproj",
    )(h, g, mod3, mod3, w_cat)


def _conv_kernel(z_ref, w_ref, o_ref, *, k_scale, n_q_blocks):
    x = z_ref[...].astype(F32)
    n = x.shape[0]
    row = lax.broadcasted_iota(jnp.int32, x.shape, 0)
    xm = jnp.where(row == 0, 0.0, pltpu.roll(x, 1, 0))
    xp = jnp.where(row == n - 1, 0.0, pltpu.roll(x, n - 1, 0))
    y = xm * w_ref[0:1, :] + x * w_ref[1:2, :] + xp * w_ref[2:3, :]
    scale = jnp.where(pl.program_id(1) >= n_q_blocks, k_scale, 1.0).astype(F32)
    o_ref[...] = (_silu(y) * scale).astype(BF16)


def _conv(zm, conv_w, out, *, n_seq, seq_len, row_block0, width, k_scale, alias):
    r = zm.shape[0]
    cb = 256
    n_cb = width // cb
    kern = functools.partial(_conv_kernel, k_scale=k_scale, n_q_blocks=n_cb // 2)
    args = [zm, conv_w]
    in_specs = [pl.BlockSpec((seq_len, cb), lambda s, j: (row_block0 + s, j)),
                pl.BlockSpec((3, cb), lambda s, j: (0, j))]
    io_alias = {}
    if alias:
        args.append(out)
        in_specs.append(pl.BlockSpec(memory_space=pl.ANY))
        io_alias = {2: 0}
        body = lambda z, w, _prev, o: kern(z, w, o)
    else:
        body = kern
    return pl.pallas_call(
        body,
        out_shape=jax.ShapeDtypeStruct((r, width), BF16),
        grid=(n_seq, n_cb),
        in_specs=in_specs,
        out_specs=pl.BlockSpec((seq_len, cb), lambda s, j: (row_block0 + s, j)),
        input_output_aliases=io_alias,
        compiler_params=_cparams(("arbitrary", "arbitrary")),
        name="conv_qk",
    )(*args)


def _mlstm_kernel(qf, kf, vf, gcf, grf, qb, kb, vb, gcb, grb, bias_c, bias_r, hf_out, hb_out,
                  c_ref, n_ref, m_ref, *, dh):
    L = CHUNK

    @pl.when(pl.program_id(1) == 0)
    def _():
        c_ref[...] = jnp.zeros_like(c_ref)
        n_ref[...] = jnp.zeros_like(n_ref)
        m_ref[...] = jnp.zeros_like(m_ref)

    ri = lax.broadcasted_iota(jnp.int32, (L, L), 0)
    ci = lax.broadcasted_iota(jnp.int32, (L, L), 1)
    lower = ci <= ri
    upper = ci >= ri
    lower_b = jnp.where(lower, 1.0, 0.0).astype(BF16)
    upper_b = jnp.where(upper, 1.0, 0.0).astype(BF16)

    dirs = ((qf, kf, vf, gcf, grf, hf_out), (qb, kb, vb, gcb, grb, hb_out))
    for d, (q_ref, k_ref, v_ref, gc_ref, gr_ref, out_ref) in enumerate(dirs):
        rev = d == 1
        gc = gc_ref[...] + bias_c[...]
        gr = gr_ref[...] + bias_r[...]
        fc = _log_sigmoid(gc)
        fr = _log_sigmoid(gr)
        t_col = upper_b if rev else lower_b
        t_row = lower_b if rev else upper_b
        bcum_c = sum(_dot(t_col, p) for p in _split3(fc))
        bcum_r = sum(_dot(p, t_row) for p in _split3(fr))
        mask = upper if rev else lower
        for h in range(M_HEADS):
            s_idx = d * M_HEADS + h
            col_i = d * 2 * M_HEADS + h
            col_f = col_i + M_HEADS
            i_col = gc[:, col_i:col_i + 1]
            i_row = gr[col_i:col_i + 1, :]
            bc_col = bcum_c[:, col_f:col_f + 1]
            bc_row = bcum_r[col_f:col_f + 1, :]
            total = bc_row[:, 0:1] if rev else bc_row[:, L - 1:L]
            m_st = m_ref[s_idx:s_idx + 1, 0:1]
            c_st = c_ref[s_idx]
            n_st = n_ref[s_idx:s_idx + 1, :]

            g_end_row = total - bc_row + i_row
            g_end_col = total - bc_col + i_col
            m_new = jnp.maximum(total + m_st, jnp.max(g_end_row, axis=1, keepdims=True))
            wk_col = jnp.exp(g_end_col - m_new)
            wc = jnp.exp(total + m_st - m_new)

            a_col = bc_col + m_st
            dmat = jnp.where(mask, bc_col - bc_row + i_row, -jnp.inf)
            m_row = jnp.maximum(a_col, jnp.max(dmat, axis=1, keepdims=True))
            w_inter = jnp.exp(a_col - m_row)
            p = jnp.exp(dmat - m_row)

            sl = slice(h * dh, (h + 1) * dh)
            q = q_ref[:, sl]
            k = k_ref[:, sl]
            v = v_ref[:, sl]
            scores = _dot_nt(q, k) * p
            num = w_inter * _dot(q, c_st.astype(BF16)) + _dot(scores.astype(BF16), v)
            den = (w_inter * jnp.sum(q.astype(F32) * n_st, axis=1, keepdims=True)
                   + jnp.sum(scores, axis=1, keepdims=True))
            inv = 1.0 / jnp.maximum(jnp.abs(den), jnp.exp(-m_row))
            out_ref[:, sl] = (num * inv).astype(BF16)

            kw = k.astype(F32) * wk_col
            c_ref[s_idx] = wc * c_st + _dot_tn(kw.astype(BF16), v)
            n_ref[s_idx:s_idx + 1, :] = wc * n_st + jnp.sum(kw, axis=0, keepdims=True)
            m_ref[s_idx:s_idx + 1, :] = jnp.broadcast_to(m_new, (1, m_ref.shape[1]))


def _mlstm(qk, zm, zg, zg_t, bias_c, bias_r, *, b, s, width):
    r = qk.shape[0]
    L = CHUNK
    n_lat = s // L
    lat_blocks = b * n_lat

    def blk_f(bi, i):
        return jnp.where(i == 0, lat_blocks + bi, bi * n_lat + i - 1)

    def blk_b(bi, i):
        return jnp.where(i == 0, lat_blocks + bi, bi * n_lat + n_lat - i)

    def specs(blk):
        return [pl.BlockSpec((L, width), lambda bi, i: (blk(bi, i), 0)),
                pl.BlockSpec((L, width), lambda bi, i: (blk(bi, i), 1)),
                pl.BlockSpec((L, width), lambda bi, i: (blk(bi, i), 2)),
                pl.BlockSpec((L, GATE_PAD), lambda bi, i: (blk(bi, i), 0)),
                pl.BlockSpec((16, L), lambda bi, i: (0, blk(bi, i)))]

    kern = functools.partial(_mlstm_kernel, dh=width // M_HEADS)
    out = jax.ShapeDtypeStruct((r, width), BF16)
    return pl.pallas_call(
        kern,
        out_shape=(out, out),
        grid=(b, n_lat + 1),
        in_specs=specs(blk_f) + specs(blk_b) + [pl.BlockSpec((1, GATE_PAD), lambda bi, i: (0, 0)),
                                                pl.BlockSpec((16, L), lambda bi, i: (0, 0))],
        out_specs=(pl.BlockSpec((L, width), lambda bi, i: (blk_f(bi, i), 0)),
                   pl.BlockSpec((L, width), lambda bi, i: (blk_b(bi, i), 0))),
        scratch_shapes=[pltpu.VMEM((2 * M_HEADS, width // M_HEADS, width // M_HEADS), F32),
                        pltpu.VMEM((2 * M_HEADS, width // M_HEADS), F32),
                        pltpu.VMEM((2 * M_HEADS, 128), F32)],
        compiler_params=_cparams(("arbitrary", "arbitrary")),
        name="mlstm",
    )(qk, qk, zm, zg, zg_t, qk, qk, zm, zg, zg_t, bias_c, bias_r)


def _box_mean(x, idx, n, w, unit):
    rows = x.shape[0]
    acc = x
    for d in range(-(w // 2), w // 2):
        if d == 0:
            continue
        shifted = pltpu.roll(x, (-d * unit) % rows, 0)
        valid = (idx + d >= 0) & (idx + d < n)
        acc = acc + jnp.where(valid, shifted, 0.0)
    cnt = jnp.minimum(idx + w // 2, n) - jnp.maximum(idx - w // 2, 0)
    return acc * (1.0 / cnt.astype(F32))


def _pool_kernel(z_ref, o_ref, *, grid_rows):
    n = z_ref.shape[0]
    pos = lax.broadcasted_iota(jnp.int32, (n, POOL_GROUP_DIM), 0)
    for g, w in enumerate(POOL_WINDOWS):
        sl = slice(g * POOL_GROUP_DIM, (g + 1) * POOL_GROUP_DIM)
        x = z_ref[:, sl]
        if grid_rows is None:
            pooled = _box_mean(x, pos, n, w, 1)
        else:
            pooled = _box_mean(x, pos & (GRID_W - 1), GRID_W, w, 1)
            pooled = _box_mean(pooled, pos >> (GRID_W.bit_length() - 1), grid_rows, w, GRID_W)
        o_ref[:, sl] = (pooled - x).astype(BF16)


def _pool(zp, out, *, n_seq, seq_len, row_block0, grid_rows, alias):
    r, width = zp.shape
    kern = functools.partial(_pool_kernel, grid_rows=grid_rows)
    args = [zp]
    in_specs = [pl.BlockSpec((seq_len, width), lambda s: (row_block0 + s, 0))]
    io_alias = {}
    if alias:
        args.append(out)
        in_specs.append(pl.BlockSpec(memory_space=pl.ANY))
        io_alias = {1: 0}
        body = lambda z, _prev, o: kern(z, o)
    else:
        body = kern
    return pl.pallas_call(
        body,
        out_shape=jax.ShapeDtypeStruct((r, width), BF16),
        grid=(n_seq,),
        in_specs=in_specs,
        out_specs=pl.BlockSpec((seq_len, width), lambda s: (row_block0 + s, 0)),
        input_output_aliases=io_alias,
        compiler_params=_cparams(("arbitrary",)),
        name="pool",
    )(*args)


def _mixout_kernel(hf_ref, hb_ref, o_ref, ga_ref, gb_ref, pm_ref, x_ref, gt_ref, hg_ref, wpool_ref, ps_ref,
                   wpa_ref, wpb_ref, wout_ref, out_ref, *, dh):
    hm = hf_ref[...].astype(F32) + hb_ref[...].astype(F32)
    parts = []
    for h in range(M_HEADS):
        t = hm[:, h * dh:(h + 1) * dh]
        mu = jnp.mean(t, axis=-1, keepdims=True)
        tc = t - mu
        var = jnp.mean(tc * tc, axis=-1, keepdims=True)
        parts.append(tc * lax.rsqrt(var + EPS))
    hn = jnp.concatenate(parts, axis=-1)
    hn = _sigmoid(o_ref[...].astype(F32)) * (hn * hg_ref[...])
    a = _dot(hn.astype(BF16), wpa_ref[...])
    pparts = []
    for g in range(len(POOL_WINDOWS)):
        sl = slice(g * POOL_GROUP_DIM, (g + 1) * POOL_GROUP_DIM)
        pparts.append(_dot(pm_ref[:, sl], wpool_ref[g]))
    pm = jnp.concatenate(pparts, axis=-1) * ps_ref[...]
    bmat = _dot(pm.astype(BF16), wpb_ref[...])
    merged = _sigmoid(ga_ref[...].astype(F32)) * a + _sigmoid(gb_ref[...].astype(F32)) * bmat
    y = _dot(merged.astype(BF16), wout_ref[...])
    out_ref[...] = x_ref[...] + gt_ref[0] * y


def _mixout(hf, hb, zm, pm, h, mod3, head_gain, w_pool, pool_scale, w_pa, w_pb, w_out, mod_idx, *, tm, n_rows):
    d = h.shape[1]
    width = hf.shape[1]
    pw = pm.shape[1]
    o_blk = 3
    full = lambda shape: pl.BlockSpec(shape, lambda i: (0,) * len(shape))
    kern = functools.partial(_mixout_kernel, dh=width // M_HEADS)
    return pl.pallas_call(
        kern,
        out_shape=jax.ShapeDtypeStruct((n_rows, d), F32),
        grid=(n_rows // tm,),
        in_specs=[pl.BlockSpec((tm, width), lambda i: (i, 0)),
                  pl.BlockSpec((tm, width), lambda i: (i, 0)),
                  pl.BlockSpec((tm, width), lambda i: (i, o_blk)),
                  pl.BlockSpec((tm, d), lambda i: (i, o_blk + 1)),
                  pl.BlockSpec((tm, d), lambda i: (i, o_blk + 2)),
                  pl.BlockSpec((tm, pw), lambda i: (i, 0)),
                  pl.BlockSpec((tm, d), lambda i: (i, 0)),
                  pl.BlockSpec((1, 1, d), lambda i: (mod_idx(i) * N_MOD + 2, 0, 0)),
                  full((1, width)), full(w_pool.shape), full((1, pw)),
                  full(w_pa.shape), full(w_pb.shape), full(w_out.shape)],
        out_specs=pl.BlockSpec((tm, d), lambda i: (i, 0)),
        compiler_params=_cparams(("arbitrary",)),
        name="mixout",
    )(hf, hb, zm, zm, zm, pm, h, mod3, head_gain, w_pool, pool_scale, w_pa, w_pb, w_out)


def _ffn_kernel(x_ref, g_ref, sc_ref, sh_ref, gt_ref, wg_ref, wu_ref, wd_ref, out_ref, u_ref, acc_ref):
    j = pl.program_id(1)

    @pl.when(j == 0)
    def _():
        u_ref[...] = _rms_mod(x_ref[...], g_ref[...], sc_ref[0], sh_ref[0]).astype(BF16)
        acc_ref[...] = jnp.zeros_like(acc_ref)

    u = u_ref[...]
    act = _silu(_dot(u, wg_ref[...])) * _dot(u, wu_ref[...])
    acc_ref[...] += _dot(act.astype(BF16), wd_ref[...])

    @pl.when(j == pl.num_programs(1) - 1)
    def _():
        out_ref[...] = x_ref[...] + gt_ref[0] * acc_ref[...]


def _ffn(h, g, mod3, wg, wu, wd, mod_idx, *, tm, tf):
    r, d = h.shape
    f = wg.shape[1]
    return pl.pallas_call(
        _ffn_kernel,
        out_shape=jax.ShapeDtypeStruct((r, d), F32),
        grid=(r // tm, f // tf),
        in_specs=[pl.BlockSpec((tm, d), lambda i, j: (i, 0)),
                  pl.BlockSpec((1, d), lambda i, j: (0, 0)),
                  pl.BlockSpec((1, 1, d), lambda i, j: (mod_idx(i) * N_MOD + 4, 0, 0)),
                  pl.BlockSpec((1, 1, d), lambda i, j: (mod_idx(i) * N_MOD + 3, 0, 0)),
                  pl.BlockSpec((1, 1, d), lambda i, j: (mod_idx(i) * N_MOD + 5, 0, 0)),
                  pl.BlockSpec((d, tf), lambda i, j: (0, j)),
                  pl.BlockSpec((d, tf), lambda i, j: (0, j)),
                  pl.BlockSpec((tf, d), lambda i, j: (j, 0))],
        out_specs=pl.BlockSpec((tm, d), lambda i, j: (i, 0)),
        scratch_shapes=[pltpu.VMEM((tm, d), BF16), pltpu.VMEM((tm, d), F32)],
        compiler_params=_cparams(("arbitrary", "arbitrary")),
        name="ffn",
    )(h, g, mod3, mod3, mod3, wg, wu, wd)


def _route(u, wr_ref, n_exp):
    parts_u = _split3(u)
    parts_w = _split3(wr_ref[...])
    logits = (_dot(parts_u[0], parts_w[0]) + _dot(parts_u[0], parts_w[1]) + _dot(parts_u[1], parts_w[0])
              + _dot(parts_u[1], parts_w[1]) + _dot(parts_u[0], parts_w[2]) + _dot(parts_u[2], parts_w[0]))
    lane = lax.broadcasted_iota(jnp.int32, logits.shape, 1)
    lg = jnp.where(lane < n_exp, logits, -jnp.inf)
    m1 = jnp.max(lg, axis=1, keepdims=True)
    i1 = jnp.min(jnp.where(lg == m1, lane, 2 * GATE_PAD), axis=1, keepdims=True)
    lg2 = jnp.where(lane == i1, -jnp.inf, lg)
    m2 = jnp.max(lg2, axis=1, keepdims=True)
    i2 = jnp.min(jnp.where(lg2 == m2, lane, 2 * GATE_PAD), axis=1, keepdims=True)
    e = jnp.exp(m2 - m1)
    p1 = 1.0 / (1.0 + e)
    p2 = e * p1
    return jnp.where(lane == i1, p1, 0.0) + jnp.where(lane == i2, p2, 0.0)


def _moe_kernel(x_ref, g_ref, sc_ref, sh_ref, gt_ref, wr_ref, wg_ref, wu_ref, wd_ref, gf_ref, out_ref,
                u_ref, gate_ref, acc_ref, *, n_exp, final):
    e = pl.program_id(1)
    j = pl.program_id(2)

    @pl.when((e == 0) & (j == 0))
    def _():
        u = _rms_mod(x_ref[...], g_ref[...], sc_ref[0], sh_ref[0])
        u_ref[...] = u.astype(BF16)
        gate_ref[...] = _route(u, wr_ref, n_exp)
        acc_ref[...] = jnp.zeros_like(acc_ref)

    u = u_ref[...]
    lane = lax.broadcasted_iota(jnp.int32, gate_ref.shape, 1)
    g_e = jnp.sum(jnp.where(lane == e, gate_ref[...], 0.0), axis=1, keepdims=True)
    act = _silu(_dot(u, wg_ref[0])) * _dot(u, wu_ref[0]) * g_e
    acc_ref[...] += _dot(act.astype(BF16), wd_ref[0])

    @pl.when((e == pl.num_programs(1) - 1) & (j == pl.num_programs(2) - 1))
    def _():
        y = x_ref[...] + gt_ref[0] * acc_ref[...]
        if final:
            r = lax.rsqrt(jnp.mean(y * y, axis=-1, keepdims=True) + EPS)
            y = (y * r) * gf_ref[...]
        out_ref[...] = y


def _moe(h, g, mod3, w_router, wg, wu, wd, g_final, mod_idx, *, tm, tf, n_rows, final):
    d = h.shape[1]
    n_exp, _, f = wg.shape
    kern = functools.partial(_moe_kernel, n_exp=n_exp, final=final)
    return pl.pallas_call(
        kern,
        out_shape=jax.ShapeDtypeStruct((n_rows, d), F32),
        grid=(n_rows // tm, n_exp, f // tf),
        in_specs=[pl.BlockSpec((tm, d), lambda i, e, j: (i, 0)),
                  pl.BlockSpec((1, d), lambda i, e, j: (0, 0)),
                  pl.BlockSpec((1, 1, d), lambda i, e, j: (mod_idx(i) * N_MOD + 4, 0, 0)),
                  pl.BlockSpec((1, 1, d), lambda i, e, j: (mod_idx(i) * N_MOD + 3, 0, 0)),
                  pl.BlockSpec((1, 1, d), lambda i, e, j: (mod_idx(i) * N_MOD + 5, 0, 0)),
                  pl.BlockSpec((d, GATE_PAD), lambda i, e, j: (0, 0)),
                  pl.BlockSpec((1, d, tf), lambda i, e, j: (e, 0, j)),
                  pl.BlockSpec((1, d, tf), lambda i, e, j: (e, 0, j)),
                  pl.BlockSpec((1, tf, d), lambda i, e, j: (e, j, 0)),
                  pl.BlockSpec((1, d), lambda i, e, j: (0, 0))],
        out_specs=pl.BlockSpec((tm, d), lambda i, e, j: (i, 0)),
        scratch_shapes=[pltpu.VMEM((tm, d), BF16), pltpu.VMEM((tm, GATE_PAD), F32), pltpu.VMEM((tm, d), F32)],
        compiler_params=_cparams(("arbitrary", "arbitrary", "arbitrary")),
        name="moe",
    )(h, g, mod3, mod3, mod3, w_router, wg, wu, wd, g_final)


def _final_norm_kernel(x_ref, g_ref, o_ref):
    x = x_ref[...]
    r = lax.rsqrt(jnp.mean(x * x, axis=-1, keepdims=True) + EPS)
    o_ref[...] = (x * r) * g_ref[...]


def _final_norm(h, g, *, tm, n_rows):
    d = h.shape[1]
    return pl.pallas_call(
        _final_norm_kernel,
        out_shape=jax.ShapeDtypeStruct((n_rows, d), F32),
        grid=(n_rows // tm,),
        in_specs=[pl.BlockSpec((tm, d), lambda i: (i, 0)), pl.BlockSpec((1, d), lambda i: (0, 0))],
        out_specs=pl.BlockSpec((tm, d), lambda i: (i, 0)),
        compiler_params=_cparams(("arbitrary",)),
        name="final_norm",
    )(h, g)


def kernel(x, c, ctx, c_ctx, w_ada, b_ada, g_mix, w_in, conv_qk, b_if, head_gain, w_pool, pool_scale, w_pa, w_pb,
           w_out, g_ffn, w_ff_gate, w_ff_up, w_ff_down, w_router, w_exp_gate, w_exp_up, w_exp_down, g_final):
    b, s, d = x.shape
    lc = ctx.shape[1]
    depth = w_ada.shape[0]
    width = w_pa.shape[1]
    pw = w_pb.shape[1]
    n_gate = b_if.shape[1] * b_if.shape[2] * b_if.shape[3]
    n_exp = w_router.shape[2]
    assert lc == CHUNK and s % CHUNK == 0 and s % GRID_W == 0 and width == d and n_gate == 16
    n_lat, n_ctx = b * s, b * lc
    tm = 512
    assert s % tm == 0 and n_ctx % tm == 0

    def mod_idx(i):
        return jnp.where(i < n_lat // tm, i // (s // tm), b)

    h = jnp.concatenate([x.reshape(n_lat, d), ctx.reshape(n_ctx, d)], axis=0)

    n_mod_rows = 16
    cc = jnp.zeros((n_mod_rows, d), F32).at[:b].set(c).at[b].set(c_ctx)
    mod = _ada(cc, w_ada, b_ada)

    if_off, pool_off, ga_off = 4 * width, 4 * width + n_gate, 4 * width + n_gate + pw
    n_main = 4 * width + 2 * d

    for l in range(depth):
        last = l == depth - 1
        mod3 = mod[l].reshape(n_mod_rows * N_MOD, 1, d)
        w_l = w_in[l]
        w_cat = jnp.concatenate(
            [w_l[:, :if_off], w_l[:, ga_off:], w_l[:, pool_off:ga_off], w_l[:, if_off:pool_off],
             jnp.zeros((d, GATE_PAD - n_gate), F32)], axis=1).astype(BF16)
        zm, zp, zg = _inproj(h, g_mix[l].reshape(1, d), mod3, w_cat, mod_idx, tm=tm, n_main=n_main, n_pool=pw)

        k_scale = float((width // M_HEADS) ** -0.5)
        qk = _conv(zm, conv_qk[l], None, n_seq=b, seq_len=s, row_block0=0, width=2 * width,
                   k_scale=k_scale, alias=False)
        qk = _conv(zm, conv_qk[l], qk, n_seq=b, seq_len=lc, row_block0=n_lat // lc, width=2 * width,
                   k_scale=k_scale, alias=True)

        bias = b_if[l].reshape(n_gate)
        bias_c = jnp.zeros((1, GATE_PAD), F32).at[0, :n_gate].set(bias)
        bias_r = jnp.broadcast_to(bias[:, None], (n_gate, CHUNK))
        hf, hb = _mlstm(qk, zm, zg, zg[:, :n_gate].T, bias_c, bias_r, b=b, s=s, width=width)

        n_rows = n_lat if last else n_lat + n_ctx
        pm = _pool(zp, None, n_seq=b, seq_len=s, row_block0=0, grid_rows=s // GRID_W, alias=False)
        if not last:
            pm = _pool(zp, pm, n_seq=b, seq_len=lc, row_block0=n_lat // lc, grid_rows=None, alias=True)

        h = _mixout(hf, hb, zm, pm, h, mod3, head_gain[l].reshape(1, width), w_pool[l].astype(BF16),
                    pool_scale[l].reshape(1, pw), w_pa[l].astype(BF16), w_pb[l].astype(BF16),
                    w_out[l].astype(BF16), mod_idx, tm=tm, n_rows=n_rows)

        j = l // 2
        if l % 2 == 0:
            h = _ffn(h, g_ffn[l].reshape(1, d), mod3, w_ff_gate[j].astype(BF16), w_ff_up[j].astype(BF16),
                     w_ff_down[j].astype(BF16), mod_idx, tm=tm, tf=256)
            if last:
                h = _final_norm(h, g_final.reshape(1, d), tm=tm, n_rows=n_lat)
        else:
            w_r = jnp.concatenate([w_router[j], jnp.zeros((d, GATE_PAD - n_exp), F32)], axis=1)
            h = _moe(h, g_ffn[l].reshape(1, d), mod3, w_r, w_exp_gate[j].astype(BF16), w_exp_up[j].astype(BF16),
                     w_exp_down[j].astype(BF16), g_final.reshape(1, d), mod_idx, tm=tm, tf=512,
                     n_rows=h.shape[0], final=last)
    return h[:n_lat].reshape(b, s, d)
```

```python
import functools

import jax
import jax.numpy as jnp
from jax import lax
from jax.experimental import pallas as pl
from jax.experimental.pallas import tpu as pltpu
from jax.experimental.pallas import tpu_sc as plsc

F32 = jnp.float32
BF16 = jnp.bfloat16
EPS = 1e-6
M_HEADS = 4
GRID_W = 64
POOL_WINDOWS = (2, 4, 8, 16)
POOL_GROUP_DIM = 128
CHUNK = 256
N_MOD = 6
TOP_K = 2
SC_WINDOW = 64
GATE_PAD = 128
VMEM_LIMIT = 56 * 1024 * 1024


def _cparams(sem):
    return pltpu.CompilerParams(dimension_semantics=sem, vmem_limit_bytes=VMEM_LIMIT)


def _sigmoid(x):
    return 1.0 / (1.0 + jnp.exp(-x))


def _silu(x):
    return x * _sigmoid(x)


def _log_sigmoid(x):
    return jnp.minimum(x, 0.0) - jnp.log(1.0 + jnp.exp(-jnp.abs(x)))


def _split3(x):
    hi = x.astype(BF16)
    r1 = x - hi.astype(F32)
    mid = r1.astype(BF16)
    lo = (r1 - mid.astype(F32)).astype(BF16)
    return hi, mid, lo


def _dot(a, b):
    return jnp.dot(a, b, preferred_element_type=F32)


def _dot_nt(a, b):
    return lax.dot_general(a, b, (((1,), (1,)), ((), ())), preferred_element_type=F32)


def _dot_tn(a, b):
    return lax.dot_general(a, b, (((0,), (0,)), ((), ())), preferred_element_type=F32)


def _rms_mod(x, g, sc, sh):
    r = lax.rsqrt(jnp.mean(x * x, axis=-1, keepdims=True) + EPS)
    return (x * r) * g * (1.0 + sc) + sh


def _ada_kernel(c_ref, w_ref, b_ref, o_ref):
    c = c_ref[...]
    o_ref[0] = _dot(_silu(c).astype(BF16), w_ref[0].astype(BF16)) + b_ref[0]


def _ada(cc, w_ada, b_ada):
    depth, d, n = w_ada.shape
    tn = 1024
    return pl.pallas_call(
        _ada_kernel,
        out_shape=jax.ShapeDtypeStruct((depth, cc.shape[0], n), F32),
        grid=(depth, n // tn),
        in_specs=[pl.BlockSpec(cc.shape, lambda l, j: (0, 0)),
                  pl.BlockSpec((1, d, tn), lambda l, j: (l, 0, j)),
                  pl.BlockSpec((1, 1, tn), lambda l, j: (l, 0, j))],
        out_specs=pl.BlockSpec((1, cc.shape[0], tn), lambda l, j: (l, 0, j)),
        compiler_params=_cparams(("arbitrary", "arbitrary")),
        name="ada",
    )(cc, w_ada, b_ada.reshape(depth, 1, n))


def _inproj_kernel(x_ref, g_ref, sc_ref, sh_ref, w_ref, zm_ref, zp_ref, zg_ref, u_ref, *, n_main, n_pool, cw):
    u_ref[...] = _rms_mod(x_ref[...], g_ref[...], sc_ref[0], sh_ref[0]).astype(BF16)
    for c in range(n_main // cw):
        zm_ref[:, c * cw:(c + 1) * cw] = _dot(u_ref[...], w_ref[:, c * cw:(c + 1) * cw]).astype(BF16)
    for c in range(n_pool // cw):
        lo = n_main + c * cw
        zp_ref[:, c * cw:(c + 1) * cw] = _dot(u_ref[...], w_ref[:, lo:lo + cw])
    zg_ref[...] = _dot(u_ref[...], w_ref[:, n_main + n_pool:])


def _inproj(h, g, mod3, w_cat, mod_idx, *, tm, n_main, n_pool):
    r, d = h.shape
    ncols = w_cat.shape[1]
    kern = functools.partial(_inproj_kernel, n_main=n_main, n_pool=n_pool, cw=512)
    return pl.pallas_call(
        kern,
        out_shape=(jax.ShapeDtypeStruct((r, n_main), BF16),
                   jax.ShapeDtypeStruct((r, n_pool), F32),
                   jax.ShapeDtypeStruct((r, GATE_PAD), F32)),
        grid=(r // tm,),
        in_specs=[pl.BlockSpec((tm, d), lambda i: (i, 0)),
                  pl.BlockSpec((1, d), lambda i: (0, 0)),
                  pl.BlockSpec((1, 1, d), lambda i: (mod_idx(i) * N_MOD + 1, 0, 0)),
                  pl.BlockSpec((1, 1, d), lambda i: (mod_idx(i) * N_MOD + 0, 0, 0)),
                  pl.BlockSpec((d, ncols), lambda i: (0, 0))],
        out_specs=(pl.BlockSpec((tm, n_main), lambda i: (i, 0)),
                   pl.BlockSpec((tm, n_pool), lambda i: (i, 0)),
                   pl.BlockSpec((tm, GATE_PAD), lambda i: (i, 0))),
        scratch_shapes=[pltpu.VMEM((tm, d), BF16)],
        compiler_params=_cparams(("arbitrary",)),
        name="inproj",
    )(h, g, mod3, mod3, w_cat)


def _conv_kernel(z_ref, w_ref, o_ref, *, k_scale, n_q_blocks):
    x = z_ref[...].astype(F32)
    n = x.shape[0]
    row = lax.broadcasted_iota(jnp.int32, x.shape, 0)
    xm = jnp.where(row == 0, 0.0, pltpu.roll(x, 1, 0))
    xp = jnp.where(row == n - 1, 0.0, pltpu.roll(x, n - 1, 0))
    y = xm * w_ref[0:1, :] + x * w_ref[1:2, :] + xp * w_ref[2:3, :]
    scale = jnp.where(pl.program_id(1) >= n_q_blocks, k_scale, 1.0).astype(F32)
    o_ref[...] = (_silu(y) * scale).astype(BF16)


def _conv(zm, conv_w, out, *, n_seq, seq_len, row_block0, width, k_scale, alias):
    r = zm.shape[0]
    cb = 256
    n_cb = width // cb
    kern = functools.partial(_conv_kernel, k_scale=k_scale, n_q_blocks=n_cb // 2)
    args = [zm, conv_w]
    in_specs = [pl.BlockSpec((seq_len, cb), lambda s, j: (row_block0 + s, j)),
                pl.BlockSpec((3, cb), lambda s, j: (0, j))]
    io_alias = {}
    if alias:
        args.append(out)
        in_specs.append(pl.BlockSpec(memory_space=pl.ANY))
        io_alias = {2: 0}
        body = lambda z, w, _prev, o: kern(z, w, o)
    else:
        body = kern
    return pl.pallas_call(
        body,
        out_shape=jax.ShapeDtypeStruct((r, width), BF16),
        grid=(n_seq, n_cb),
        in_specs=in_specs,
        out_specs=pl.BlockSpec((seq_len, cb), lambda s, j: (row_block0 + s, j)),
        input_output_aliases=io_alias,
        compiler_params=_cparams(("arbitrary", "arbitrary")),
        name="conv_qk",
    )(*args)


def _mlstm_kernel(qf, kf, vf, gcf, grf, qb, kb, vb, gcb, grb, bias_c, bias_r, hf_out, hb_out,
                  c_ref, n_ref, m_ref, *, dh):
    L = CHUNK

    @pl.when(pl.program_id(1) == 0)
    def _():
        c_ref[...] = jnp.zeros_like(c_ref)
        n_ref[...] = jnp.zeros_like(n_ref)
        m_ref[...] = jnp.zeros_like(m_ref)

    ri = lax.broadcasted_iota(jnp.int32, (L, L), 0)
    ci = lax.broadcasted_iota(jnp.int32, (L, L), 1)
    lower = ci <= ri
    upper = ci >= ri
    lower_b = jnp.where(lower, 1.0, 0.0).astype(BF16)
    upper_b = jnp.where(upper, 1.0, 0.0).astype(BF16)

    dirs = ((qf, kf, vf, gcf, grf, hf_out), (qb, kb, vb, gcb, grb, hb_out))
    for d, (q_ref, k_ref, v_ref, gc_ref, gr_ref, out_ref) in enumerate(dirs):
        rev = d == 1
        gc = gc_ref[...] + bias_c[...]
        gr = gr_ref[...] + bias_r[...]
        fc = _log_sigmoid(gc)
        fr = _log_sigmoid(gr)
        t_col = upper_b if rev else lower_b
        t_row = lower_b if rev else upper_b
        bcum_c = sum(_dot(t_col, p) for p in _split3(fc))
        bcum_r = sum(_dot(p, t_row) for p in _split3(fr))
        mask = upper if rev else lower
        for h in range(M_HEADS):
            s_idx = d * M_HEADS + h
            col_i = d * 2 * M_HEADS + h
            col_f = col_i + M_HEADS
            i_col = gc[:, col_i:col_i + 1]
            i_row = gr[col_i:col_i + 1, :]
            bc_col = bcum_c[:, col_f:col_f + 1]
            bc_row = bcum_r[col_f:col_f + 1, :]
            total = bc_row[:, 0:1] if rev else bc_row[:, L - 1:L]
            m_st = m_ref[s_idx:s_idx + 1, 0:1]
            c_st = c_ref[s_idx]
            n_st = n_ref[s_idx:s_idx + 1, :]

            g_end_row = total - bc_row + i_row
            g_end_col = total - bc_col + i_col
            m_new = jnp.maximum(total + m_st, jnp.max(g_end_row, axis=1, keepdims=True))
            wk_col = jnp.exp(g_end_col - m_new)
            wc = jnp.exp(total + m_st - m_new)

            a_col = bc_col + m_st
            dmat = jnp.where(mask, bc_col - bc_row + i_row, -jnp.inf)
            m_row = jnp.maximum(a_col, jnp.max(dmat, axis=1, keepdims=True))
            w_inter = jnp.exp(a_col - m_row)
            p = jnp.exp(dmat - m_row)

            sl = slice(h * dh, (h + 1) * dh)
            q = q_ref[:, sl]
            k = k_ref[:, sl]
            v = v_ref[:, sl]
            scores = _dot_nt(q, k) * p
            num = w_inter * _dot(q, c_st.astype(BF16)) + _dot(scores.astype(BF16), v)
            den = (w_inter * jnp.sum(q.astype(F32) * n_st, axis=1, keepdims=True)
                   + jnp.sum(scores, axis=1, keepdims=True))
            inv = 1.0 / jnp.maximum(jnp.abs(den), jnp.exp(-m_row))
            out_ref[:, sl] = (num * inv).astype(BF16)

            kw = k.astype(F32) * wk_col
            c_ref[s_idx] = wc * c_st + _dot_tn(kw.astype(BF16), v)
            n_ref[s_idx:s_idx + 1, :] = wc * n_st + jnp.sum(kw, axis=0, keepdims=True)
            m_ref[s_idx:s_idx + 1, :] = jnp.broadcast_to(m_new, (1, m_ref.shape[1]))


def _mlstm(qk, zm, zg, zg_t, bias_c, bias_r, *, b, s, width):
    r = qk.shape[0]
    L = CHUNK
    n_lat = s // L
    lat_blocks = b * n_lat

    def blk_f(bi, i):
        return jnp.where(i == 0, lat_blocks + bi, bi * n_lat + i - 1)

    def blk_b(bi, i):
        return jnp.where(i == 0, lat_blocks + bi, bi * n_lat + n_lat - i)

    def specs(blk):
        return [pl.BlockSpec((L, width), lambda bi, i: (blk(bi, i), 0)),
                pl.BlockSpec((L, width), lambda bi, i: (blk(bi, i), 1)),
                pl.BlockSpec((L, width), lambda bi, i: (blk(bi, i), 2)),
                pl.BlockSpec((L, GATE_PAD), lambda bi, i: (blk(bi, i), 0)),
                pl.BlockSpec((16, L), lambda bi, i: (0, blk(bi, i)))]

    kern = functools.partial(_mlstm_kernel, dh=width // M_HEADS)
    out = jax.ShapeDtypeStruct((r, width), BF16)
    return pl.pallas_call(
        kern,
        out_shape=(out, out),
        grid=(b, n_lat + 1),
        in_specs=specs(blk_f) + specs(blk_b) + [pl.BlockSpec((1, GATE_PAD), lambda bi, i: (0, 0)),
                                                pl.BlockSpec((16, L), lambda bi, i: (0, 0))],
        out_specs=(pl.BlockSpec((L, width), lambda bi, i: (blk_f(bi, i), 0)),
                   pl.BlockSpec((L, width), lambda bi, i: (blk_b(bi, i), 0))),
        scratch_shapes=[pltpu.VMEM((2 * M_HEADS, width // M_HEADS, width // M_HEADS), F32),
                        pltpu.VMEM((2 * M_HEADS, width // M_HEADS), F32),
                        pltpu.VMEM((2 * M_HEADS, 128), F32)],
        compiler_params=_cparams(("arbitrary", "arbitrary")),
        name="mlstm",
    )(qk, qk, zm, zg, zg_t, qk, qk, zm, zg, zg_t, bias_c, bias_r)


def _box_mean(x, idx, n, w, unit):
    rows = x.shape[0]
    acc = x
    for d in range(-(w // 2), w // 2):
        if d == 0:
            continue
        shifted = pltpu.roll(x, (-d * unit) % rows, 0)
        valid = (idx + d >= 0) & (idx + d < n)
        acc = acc + jnp.where(valid, shifted, 0.0)
    cnt = jnp.minimum(idx + w // 2, n) - jnp.maximum(idx - w // 2, 0)
    return acc * (1.0 / cnt.astype(F32))


def _pool_kernel(z_ref, o_ref, *, grid_rows):
    n = z_ref.shape[0]
    pos = lax.broadcasted_iota(jnp.int32, (n, POOL_GROUP_DIM), 0)
    for g, w in enumerate(POOL_WINDOWS):
        sl = slice(g * POOL_GROUP_DIM, (g + 1) * POOL_GROUP_DIM)
        x = z_ref[:, sl]
        if grid_rows is None:
            pooled = _box_mean(x, pos, n, w, 1)
        else:
            pooled = _box_mean(x, pos & (GRID_W - 1), GRID_W, w, 1)
            pooled = _box_mean(pooled, pos >> (GRID_W.bit_length() - 1), grid_rows, w, GRID_W)
        o_ref[:, sl] = (pooled - x).astype(BF16)


def _pool(zp, out, *, n_seq, seq_len, row_block0, grid_rows, alias):
    r, width = zp.shape
    kern = functools.partial(_pool_kernel, grid_rows=grid_rows)
    args = [zp]
    in_specs = [pl.BlockSpec((seq_len, width), lambda s: (row_block0 + s, 0))]
    io_alias = {}
    if alias:
        args.append(out)
        in_specs.append(pl.BlockSpec(memory_space=pl.ANY))
        io_alias = {1: 0}
        body = lambda z, _prev, o: kern(z, o)
    else:
        body = kern
    return pl.pallas_call(
        body,
        out_shape=jax.ShapeDtypeStruct((r, width), BF16),
        grid=(n_seq,),
        in_specs=in_specs,
        out_specs=pl.BlockSpec((seq_len, width), lambda s: (row_block0 + s, 0)),
        input_output_aliases=io_alias,
        compiler_params=_cparams(("arbitrary",)),
        name="pool",
    )(*args)


def _mixout_kernel(hf_ref, hb_ref, o_ref, ga_ref, gb_ref, pm_ref, x_ref, gt_ref, hg_ref, wpool_ref, ps_ref,
                   wpa_ref, wpb_ref, wout_ref, out_ref, *, dh):
    hm = hf_ref[...].astype(F32) + hb_ref[...].astype(F32)
    parts = []
    for h in range(M_HEADS):
        t = hm[:, h * dh:(h + 1) * dh]
        mu = jnp.mean(t, axis=-1, keepdims=True)
        tc = t - mu
        var = jnp.mean(tc * tc, axis=-1, keepdims=True)
        parts.append(tc * lax.rsqrt(var + EPS))
    hn = jnp.concatenate(parts, axis=-1)
    hn = _sigmoid(o_ref[...].astype(F32)) * (hn * hg_ref[...])
    a = _dot(hn.astype(BF16), wpa_ref[...])
    pparts = []
    for g in range(len(POOL_WINDOWS)):
        sl = slice(g * POOL_GROUP_DIM, (g + 1) * POOL_GROUP_DIM)
        pparts.append(_dot(pm_ref[:, sl], wpool_ref[g]))
    pm = jnp.concatenate(pparts, axis=-1) * ps_ref[...]
    bmat = _dot(pm.astype(BF16), wpb_ref[...])
    merged = _sigmoid(ga_ref[...].astype(F32)) * a + _sigmoid(gb_ref[...].astype(F32)) * bmat
    y = _dot(merged.astype(BF16), wout_ref[...])
    out_ref[...] = x_ref[...] + gt_ref[0] * y


def _mixout(hf, hb, zm, pm, h, mod3, head_gain, w_pool, pool_scale, w_pa, w_pb, w_out, mod_idx, *, tm, n_rows):
    d = h.shape[1]
    width = hf.shape[1]
    pw = pm.shape[1]
    o_blk = 3
    full = lambda shape: pl.BlockSpec(shape, lambda i: (0,) * len(shape))
    kern = functools.partial(_mixout_kernel, dh=width // M_HEADS)
    return pl.pallas_call(
        kern,
        out_shape=jax.ShapeDtypeStruct((n_rows, d), F32),
        grid=(n_rows // tm,),
        in_specs=[pl.BlockSpec((tm, width), lambda i: (i, 0)),
                  pl.BlockSpec((tm, width), lambda i: (i, 0)),
                  pl.BlockSpec((tm, width), lambda i: (i, o_blk)),
                  pl.BlockSpec((tm, d), lambda i: (i, o_blk + 1)),
                  pl.BlockSpec((tm, d), lambda i: (i, o_blk + 2)),
                  pl.BlockSpec((tm, pw), lambda i: (i, 0)),
                  pl.BlockSpec((tm, d), lambda i: (i, 0)),
                  pl.BlockSpec((1, 1, d), lambda i: (mod_idx(i) * N_MOD + 2, 0, 0)),
                  full((1, width)), full(w_pool.shape), full((1, pw)),
                  full(w_pa.shape), full(w_pb.shape), full(w_out.shape)],
        out_specs=pl.BlockSpec((tm, d), lambda i: (i, 0)),
        compiler_params=_cparams(("arbitrary",)),
        name="mixout",
    )(hf, hb, zm, zm, zm, pm, h, mod3, head_gain, w_pool, pool_scale, w_pa, w_pb, w_out)


def _ffn_kernel(x_ref, g_ref, sc_ref, sh_ref, gt_ref, wg_ref, wu_ref, wd_ref, out_ref, u_ref, acc_ref):
    j = pl.program_id(1)

    @pl.when(j == 0)
    def _():
        u_ref[...] = _rms_mod(x_ref[...], g_ref[...], sc_ref[0], sh_ref[0]).astype(BF16)
        acc_ref[...] = jnp.zeros_like(acc_ref)

    u = u_ref[...]
    act = _silu(_dot(u, wg_ref[...])) * _dot(u, wu_ref[...])
    acc_ref[...] += _dot(act.astype(BF16), wd_ref[...])

    @pl.when(j == pl.num_programs(1) - 1)
    def _():
        out_ref[...] = x_ref[...] + gt_ref[0] * acc_ref[...]


def _ffn(h, g, mod3, wg, wu, wd, mod_idx, *, tm, tf):
    r, d = h.shape
    f = wg.shape[1]
    return pl.pallas_call(
        _ffn_kernel,
        out_shape=jax.ShapeDtypeStruct((r, d), F32),
        grid=(r // tm, f // tf),
        in_specs=[pl.BlockSpec((tm, d), lambda i, j: (i, 0)),
                  pl.BlockSpec((1, d), lambda i, j: (0, 0)),
                  pl.BlockSpec((1, 1, d), lambda i, j: (mod_idx(i) * N_MOD + 4, 0, 0)),
                  pl.BlockSpec((1, 1, d), lambda i, j: (mod_idx(i) * N_MOD + 3, 0, 0)),
                  pl.BlockSpec((1, 1, d), lambda i, j: (mod_idx(i) * N_MOD + 5, 0, 0)),
                  pl.BlockSpec((d, tf), lambda i, j: (0, j)),
                  pl.BlockSpec((d, tf), lambda i, j: (0, j)),
                  pl.BlockSpec((tf, d), lambda i, j: (j, 0))],
        out_specs=pl.BlockSpec((tm, d), lambda i, j: (i, 0)),
        scratch_shapes=[pltpu.VMEM((tm, d), BF16), pltpu.VMEM((tm, d), F32)],
        compiler_params=_cparams(("arbitrary", "arbitrary")),
        name="ffn",
    )(h, g, mod3, mod3, mod3, wg, wu, wd)


def _route(u, wr_ref, n_exp):
    parts_u = _split3(u)
    parts_w = _split3(wr_ref[...])
    logits = (_dot(parts_u[0], parts_w[0]) + _dot(parts_u[0], parts_w[1]) + _dot(parts_u[1], parts_w[0])
              + _dot(parts_u[1], parts_w[1]) + _dot(parts_u[0], parts_w[2]) + _dot(parts_u[2], parts_w[0]))
    lane = lax.broadcasted_iota(jnp.int32, logits.shape, 1)
    lg = jnp.where(lane < n_exp, logits, -jnp.inf)
    m1 = jnp.max(lg, axis=1, keepdims=True)
    i1 = jnp.min(jnp.where(lg == m1, lane, 2 * GATE_PAD), axis=1, keepdims=True)
    lg2 = jnp.where(lane == i1, -jnp.inf, lg)
    m2 = jnp.max(lg2, axis=1, keepdims=True)
    i2 = jnp.min(jnp.where(lg2 == m2, lane, 2 * GATE_PAD), axis=1, keepdims=True)
    e = jnp.exp(m2 - m1)
    p1 = 1.0 / (1.0 + e)
    p2 = e * p1
    return i1, i2, p1, p2


LANE_I1, LANE_I2, LANE_P1, LANE_P2, LANE_R1, LANE_R2 = 0, 1, 2, 3, 4, 5


def _pack_bf16_pairs(x):
    k = x.shape[1] // 2
    lo = lax.bitcast_convert_type(x[:, :k].astype(BF16).astype(F32), jnp.uint32)
    hi = lax.bitcast_convert_type(x[:, k:].astype(BF16).astype(F32), jnp.uint32)
    return (lo >> 16) | (hi & jnp.uint32(0xFFFF0000))


def _unpack_bf16_pairs(w):
    lo = lax.bitcast_convert_type(w << 16, F32)
    hi = lax.bitcast_convert_type(w & jnp.uint32(0xFFFF0000), F32)
    return jnp.concatenate([lo, hi], axis=1)


def _route_kernel(x_ref, g_ref, sc_ref, sh_ref, wr_ref, up_ref, info_ref, cnt_ref, carry_ref, *, n_exp):
    @pl.when(pl.program_id(0) == 0)
    def _():
        carry_ref[...] = jnp.zeros_like(carry_ref)

    u = _rms_mod(x_ref[...], g_ref[...], sc_ref[0], sh_ref[0])
    up_ref[...] = _pack_bf16_pairs(u)
    i1, i2, p1, p2 = _route(u, wr_ref, n_exp)
    tm = u.shape[0]
    lane = lax.broadcasted_iota(jnp.int32, (tm, GATE_PAD), 1)
    sel = jnp.where((lane == i1) | (lane == i2), 1.0, 0.0)
    ri = lax.broadcasted_iota(jnp.int32, (tm, tm), 0)
    ci = lax.broadcasted_iota(jnp.int32, (tm, tm), 1)
    before = jnp.where(ci < ri, 1.0, 0.0).astype(BF16)
    rank = carry_ref[0:1, :] + _dot(before, sel.astype(BF16))
    r1 = jnp.sum(jnp.where(lane == i1, rank, 0.0), axis=1, keepdims=True)
    r2 = jnp.sum(jnp.where(lane == i2, rank, 0.0), axis=1, keepdims=True)
    carry_ref[0:1, :] = carry_ref[0:1, :] + jnp.sum(sel, axis=0, keepdims=True)
    cnt_ref[...] = carry_ref[...]
    info = jnp.zeros((tm, GATE_PAD), F32)
    for ln, val in ((LANE_I1, i1.astype(F32)), (LANE_I2, i2.astype(F32)), (LANE_P1, p1), (LANE_P2, p2),
                    (LANE_R1, r1), (LANE_R2, r2)):
        info = jnp.where(lane == ln, val, info)
    info_ref[...] = info


def _route_call(h, g, mod3, w_router, mod_idx, *, tm, n_rows):
    d = h.shape[1]
    n_exp = w_router.shape[1]
    w_r = jnp.concatenate([w_router, jnp.zeros((d, GATE_PAD - n_exp), F32)], axis=1)
    kern = functools.partial(_route_kernel, n_exp=n_exp)
    return pl.pallas_call(
        kern,
        out_shape=(jax.ShapeDtypeStruct((n_rows, d // 2), jnp.uint32),
                   jax.ShapeDtypeStruct((n_rows, GATE_PAD), F32),
                   jax.ShapeDtypeStruct((8, GATE_PAD), F32)),
        grid=(n_rows // tm,),
        in_specs=[pl.BlockSpec((tm, d), lambda i: (i, 0)),
                  pl.BlockSpec((1, d), lambda i: (0, 0)),
                  pl.BlockSpec((1, 1, d), lambda i: (mod_idx(i) * N_MOD + 4, 0, 0)),
                  pl.BlockSpec((1, 1, d), lambda i: (mod_idx(i) * N_MOD + 3, 0, 0)),
                  pl.BlockSpec((d, GATE_PAD), lambda i: (0, 0))],
        out_specs=(pl.BlockSpec((tm, d // 2), lambda i: (i, 0)),
                   pl.BlockSpec((tm, GATE_PAD), lambda i: (i, 0)),
                   pl.BlockSpec((8, GATE_PAD), lambda i: (0, 0))),
        scratch_shapes=[pltpu.VMEM((8, GATE_PAD), F32)],
        compiler_params=_cparams(("arbitrary",)),
        name="route",
    )(h, g, mod3, mod3, w_r)


def _sc_workers():
    info = pltpu.get_tpu_info().sparse_core
    return info.num_cores, info.num_subcores


def _sc_scatter_rows(x, pos, n_out):
    n, dw = x.shape
    nc, ns = _sc_workers()
    nw = nc * ns
    t_per_w = n // nw
    w = min(SC_WINDOW, t_per_w // 2)
    n_chunks = t_per_w // w
    assert n % nw == 0 and t_per_w % (2 * w) == 0
    pos_w = pos.reshape(2, nw, n_chunks, w).transpose(1, 0, 2, 3)
    mesh = plsc.VectorSubcoreMesh(core_axis_name="c", subcore_axis_name="s")

    @functools.partial(pl.kernel, mesh=mesh, out_type=jax.ShapeDtypeStruct((n_out, dw), x.dtype),
                       scratch_types=[pltpu.VMEM((2, n_chunks, w), jnp.int32), pltpu.VMEM((2, w, dw), x.dtype),
                                      pltpu.SemaphoreType.DMA((2,)), pltpu.SemaphoreType.DMA((2,))])
    def scatter(x_hbm, pos_hbm, out_hbm, idx_v, rows_v, gsem, osem):
        wid = lax.axis_index("s") * nc + lax.axis_index("c")
        base = wid * t_per_w
        pltpu.sync_copy(pos_hbm.at[wid], idx_v)

        def get(g, slot):
            return pltpu.make_async_copy(x_hbm.at[pl.ds(base + g * w, w)], rows_v.at[slot], gsem.at[slot])

        def put(g, slot, k):
            return pltpu.make_async_copy(rows_v.at[slot], out_hbm.at[idx_v.at[k, g]], osem.at[slot])

        get(0, 0).start()

        @pl.loop(0, n_chunks, step=2)
        def _(g):
            for b in range(2):
                gg = g + b
                get(gg, b).wait()

                @pl.when(gg + 1 < n_chunks)
                def _():
                    @pl.when(gg >= 1)
                    def _():
                        put(gg - 1, 1 - b, 0).wait()
                        put(gg - 1, 1 - b, 1).wait()
                    get(gg + 1, 1 - b).start()

                put(gg, b, 0).start()
                put(gg, b, 1).start()

        for k in range(2):
            put(n_chunks - 2, 0, k).wait()
            put(n_chunks - 1, 1, k).wait()

    return scatter(x, pos_w)


def _sc_gather_rows(table, idx):
    n = idx.shape[0]
    dw = table.shape[1]
    nc, ns = _sc_workers()
    nw = nc * ns
    b_per_w = n // nw
    w = min(SC_WINDOW, b_per_w // 2)
    n_chunks = b_per_w // w
    assert n % nw == 0 and b_per_w % (2 * w) == 0
    mesh = plsc.VectorSubcoreMesh(core_axis_name="c", subcore_axis_name="s")

    @functools.partial(pl.kernel, mesh=mesh, out_type=jax.ShapeDtypeStruct((n, dw), table.dtype),
                       scratch_types=[pltpu.VMEM((b_per_w,), jnp.int32), pltpu.VMEM((2, w, dw), table.dtype),
                                      pltpu.SemaphoreType.DMA((2,)), pltpu.SemaphoreType.DMA((2,))])
    def gather(table_hbm, idx_hbm, out_hbm, idx_v, rows_v, gsem, osem):
        wid = lax.axis_index("s") * nc + lax.axis_index("c")
        base = wid * b_per_w
        pltpu.sync_copy(idx_hbm.at[pl.ds(base, b_per_w)], idx_v)

        def get(g, slot):
            return pltpu.make_async_copy(table_hbm.at[idx_v.at[pl.ds(g * w, w)]], rows_v.at[slot], gsem.at[slot])

        def put(g, slot):
            return pltpu.make_async_copy(rows_v.at[slot], out_hbm.at[pl.ds(base + g * w, w)], osem.at[slot])

        get(0, 0).start()

        @pl.loop(0, n_chunks, step=2)
        def _(g):
            for b in range(2):
                gg = g + b
                get(gg, b).wait()

                @pl.when(gg + 1 < n_chunks)
                def _():
                    @pl.when(gg >= 1)
                    def _():
                        put(gg - 1, 1 - b).wait()
                    get(gg + 1, 1 - b).start()

                put(gg, b).start()

        put(n_chunks - 2, 0).wait()
        put(n_chunks - 1, 1).wait()

    return gather(table, idx)


def _gmm_kernel(te_ref, nv_ref, xs_ref, wg_ref, wu_ref, wd_ref, ys_ref, u_ref, acc_ref):
    i = pl.program_id(0)
    j = pl.program_id(1)

    @pl.when(i < nv_ref[0])
    def _():
        @pl.when(j == 0)
        def _():
            u_ref[...] = _unpack_bf16_pairs(xs_ref[...]).astype(BF16)
            acc_ref[...] = jnp.zeros_like(acc_ref)

        u = u_ref[...]
        act = _silu(_dot(u, wg_ref[0])) * _dot(u, wu_ref[0])
        acc_ref[...] += _dot(act.astype(BF16), wd_ref[0])

        @pl.when(j == pl.num_programs(1) - 1)
        def _():
            ys_ref[...] = _pack_bf16_pairs(acc_ref[...])


def _gmm(xs, tile_expert, n_valid, wg, wu, wd, *, tg, tf):
    p, dw = xs.shape
    d = 2 * dw
    f = wg.shape[2]
    n_f = f // tf

    def jj(i, j, nv):
        return jnp.where(i < nv[0], j, n_f - 1)

    grid_spec = pltpu.PrefetchScalarGridSpec(
        num_scalar_prefetch=2,
        grid=(p // tg, n_f),
        in_specs=[pl.BlockSpec((tg, dw), lambda i, j, te, nv: (i, 0)),
                  pl.BlockSpec((1, d, tf), lambda i, j, te, nv: (te[i], 0, jj(i, j, nv))),
                  pl.BlockSpec((1, d, tf), lambda i, j, te, nv: (te[i], 0, jj(i, j, nv))),
                  pl.BlockSpec((1, tf, d), lambda i, j, te, nv: (te[i], jj(i, j, nv), 0))],
        out_specs=pl.BlockSpec((tg, dw), lambda i, j, te, nv: (i, 0)),
        scratch_shapes=[pltpu.VMEM((tg, d), BF16), pltpu.VMEM((tg, d), F32)])
    return pl.pallas_call(
        _gmm_kernel,
        out_shape=jax.ShapeDtypeStruct((p, dw), jnp.uint32),
        grid_spec=grid_spec,
        compiler_params=_cparams(("arbitrary", "arbitrary")),
        name="gmm",
    )(tile_expert, n_valid, xs, wg, wu, wd)


def _combine_kernel(x_ref, info_ref, y1_ref, y2_ref, gt_ref, gf_ref, out_ref, *, final):
    info = info_ref[...]
    p1 = info[:, LANE_P1:LANE_P1 + 1]
    p2 = info[:, LANE_P2:LANE_P2 + 1]
    moe = p1 * _unpack_bf16_pairs(y1_ref[...]) + p2 * _unpack_bf16_pairs(y2_ref[...])
    y = x_ref[...] + gt_ref[0] * moe
    if final:
        r = lax.rsqrt(jnp.mean(y * y, axis=-1, keepdims=True) + EPS)
        y = (y * r) * gf_ref[...]
    out_ref[...] = y


def _combine(h, info, yg, mod3, g_final, mod_idx, *, tm, n_rows, final):
    d = h.shape[1]
    n_t = n_rows // tm
    kern = functools.partial(_combine_kernel, final=final)
    return pl.pallas_call(
        kern,
        out_shape=jax.ShapeDtypeStruct((n_rows, d), F32),
        grid=(n_t,),
        in_specs=[pl.BlockSpec((tm, d), lambda i: (i, 0)),
                  pl.BlockSpec((tm, GATE_PAD), lambda i: (i, 0)),
                  pl.BlockSpec((tm, d // 2), lambda i: (i, 0)),
                  pl.BlockSpec((tm, d // 2), lambda i: (n_t + i, 0)),
                  pl.BlockSpec((1, 1, d), lambda i: (mod_idx(i) * N_MOD + 5, 0, 0)),
                  pl.BlockSpec((1, d), lambda i: (0, 0))],
        out_specs=pl.BlockSpec((tm, d), lambda i: (i, 0)),
        compiler_params=_cparams(("arbitrary",)),
        name="combine",
    )(h, info, yg, yg, mod3, g_final)


def _moe(h, g, mod3, w_router, wg, wu, wd, g_final, mod_idx, *, tm, tg, tf, n_rows, final):
    n_exp = wg.shape[0]
    up, info, cnt = _route_call(h, g, mod3, w_router, mod_idx, tm=tm, n_rows=n_rows)
    i1 = info[:, LANE_I1].astype(jnp.int32)
    i2 = info[:, LANE_I2].astype(jnp.int32)
    r1 = info[:, LANE_R1].astype(jnp.int32)
    r2 = info[:, LANE_R2].astype(jnp.int32)
    counts = cnt[0, :n_exp].astype(jnp.int32)
    padded = (counts + tg - 1) // tg * tg
    ends = jnp.cumsum(padded)
    starts = ends - padded
    eye = jnp.arange(n_exp, dtype=jnp.int32)
    pos1 = jnp.sum(jnp.where(i1[:, None] == eye, starts, 0), axis=1) + r1
    pos2 = jnp.sum(jnp.where(i2[:, None] == eye, starts, 0), axis=1) + r2
    p_rows = -(-(TOP_K * n_rows + n_exp * (tg - 1)) // tg) * tg
    tile_row = jnp.arange(p_rows // tg, dtype=jnp.int32) * tg
    tile_expert = jnp.minimum(jnp.sum(tile_row[:, None] >= ends[None, :], axis=1), n_exp - 1).astype(jnp.int32)
    n_valid = (ends[-1] // tg).reshape(1).astype(jnp.int32)
    xs = _sc_scatter_rows(up, jnp.stack([pos1, pos2]), p_rows)
    ys = _gmm(xs, tile_expert, n_valid, wg, wu, wd, tg=tg, tf=tf)
    yg = _sc_gather_rows(ys, jnp.concatenate([pos1, pos2]))
    return _combine(h, info, yg, mod3, g_final, mod_idx, tm=tm, n_rows=n_rows, final=final)


def _final_norm_kernel(x_ref, g_ref, o_ref):
    x = x_ref[...]
    r = lax.rsqrt(jnp.mean(x * x, axis=-1, keepdims=True) + EPS)
    o_ref[...] = (x * r) * g_ref[...]


def _final_norm(h, g, *, tm, n_rows):
    d = h.shape[1]
    return pl.pallas_call(
        _final_norm_kernel,
        out_shape=jax.ShapeDtypeStruct((n_rows, d), F32),
        grid=(n_rows // tm,),
        in_specs=[pl.BlockSpec((tm, d), lambda i: (i, 0)), pl.BlockSpec((1, d), lambda i: (0, 0))],
        out_specs=pl.BlockSpec((tm, d), lambda i: (i, 0)),
        compiler_params=_cparams(("arbitrary",)),
        name="final_norm",
    )(h, g)


def kernel(x, c, ctx, c_ctx, w_ada, b_ada, g_mix, w_in, conv_qk, b_if, head_gain, w_pool, pool_scale, w_pa, w_pb,
           w_out, g_ffn, w_ff_gate, w_ff_up, w_ff_down, w_router, w_exp_gate, w_exp_up, w_exp_down, g_final):
    b, s, d = x.shape
    lc = ctx.shape[1]
    depth = w_ada.shape[0]
    width = w_pa.shape[1]
    pw = w_pb.shape[1]
    n_gate = b_if.shape[1] * b_if.shape[2] * b_if.shape[3]
    assert lc == CHUNK and s % CHUNK == 0 and s % GRID_W == 0 and width == d and n_gate == 16
    n_lat, n_ctx = b * s, b * lc
    tm = 512
    assert s % tm == 0 and n_ctx % tm == 0

    def mod_idx(i):
        return jnp.where(i < n_lat // tm, i // (s // tm), b)

    h = jnp.concatenate([x.reshape(n_lat, d), ctx.reshape(n_ctx, d)], axis=0)

    n_mod_rows = 16
    cc = jnp.zeros((n_mod_rows, d), F32).at[:b].set(c).at[b].set(c_ctx)
    mod = _ada(cc, w_ada, b_ada)

    if_off, pool_off, ga_off = 4 * width, 4 * width + n_gate, 4 * width + n_gate + pw
    n_main = 4 * width + 2 * d

    for l in range(depth):
        last = l == depth - 1
        mod3 = mod[l].reshape(n_mod_rows * N_MOD, 1, d)
        w_l = w_in[l]
        w_cat = jnp.concatenate(
            [w_l[:, :if_off], w_l[:, ga_off:], w_l[:, pool_off:ga_off], w_l[:, if_off:pool_off],
             jnp.zeros((d, GATE_PAD - n_gate), F32)], axis=1).astype(BF16)
        zm, zp, zg = _inproj(h, g_mix[l].reshape(1, d), mod3, w_cat, mod_idx, tm=tm, n_main=n_main, n_pool=pw)

        k_scale = float((width // M_HEADS) ** -0.5)
        qk = _conv(zm, conv_qk[l], None, n_seq=b, seq_len=s, row_block0=0, width=2 * width,
                   k_scale=k_scale, alias=False)
        qk = _conv(zm, conv_qk[l], qk, n_seq=b, seq_len=lc, row_block0=n_lat // lc, width=2 * width,
                   k_scale=k_scale, alias=True)

        bias = b_if[l].reshape(n_gate)
        bias_c = jnp.zeros((1, GATE_PAD), F32).at[0, :n_gate].set(bias)
        bias_r = jnp.broadcast_to(bias[:, None], (n_gate, CHUNK))
        hf, hb = _mlstm(qk, zm, zg, zg[:, :n_gate].T, bias_c, bias_r, b=b, s=s, width=width)

        n_rows = n_lat if last else n_lat + n_ctx
        pm = _pool(zp, None, n_seq=b, seq_len=s, row_block0=0, grid_rows=s // GRID_W, alias=False)
        if not last:
            pm = _pool(zp, pm, n_seq=b, seq_len=lc, row_block0=n_lat // lc, grid_rows=None, alias=True)

        h = _mixout(hf, hb, zm, pm, h, mod3, head_gain[l].reshape(1, width), w_pool[l].astype(BF16),
                    pool_scale[l].reshape(1, pw), w_pa[l].astype(BF16), w_pb[l].astype(BF16),
                    w_out[l].astype(BF16), mod_idx, tm=tm, n_rows=n_rows)

        j = l // 2
        if l % 2 == 0:
            h = _ffn(h, g_ffn[l].reshape(1, d), mod3, w_ff_gate[j].astype(BF16), w_ff_up[j].astype(BF16),
                     w_ff_down[j].astype(BF16), mod_idx, tm=tm, tf=256)
            if last:
                h = _final_norm(h, g_final.reshape(1, d), tm=tm, n_rows=n_lat)
        else:
            h = _moe(h, g_ffn[l].reshape(1, d), mod3, w_router[j], w_exp_gate[j].astype(BF16),
                     w_exp_up[j].astype(BF16), w_exp_down[j].astype(BF16), g_final.reshape(1, d), mod_idx,
                     tm=tm, tg=512, tf=512, n_rows=h.shape[0], final=last)
    return h[:n_lat].reshape(b, s, d)
```

```python
import functools

import jax
import jax.numpy as jnp
from jax import lax
from jax.experimental import pallas as pl
from jax.experimental.pallas import tpu as pltpu
from jax.experimental.pallas import tpu_sc as plsc

F32 = jnp.float32
BF16 = jnp.bfloat16
EPS = 1e-6
M_HEADS = 4
GRID_W = 64
POOL_WINDOWS = (2, 4, 8, 16)
POOL_GROUP_DIM = 128
CHUNK = 256
N_MOD = 6
TOP_K = 2
SC_WINDOW = 64
GATE_PAD = 128
VMEM_LIMIT = 56 * 1024 * 1024


def _cparams(sem):
    return pltpu.CompilerParams(dimension_semantics=sem, vmem_limit_bytes=VMEM_LIMIT)


def _sigmoid(x):
    return 1.0 / (1.0 + jnp.exp(-x))


def _silu(x):
    return x * _sigmoid(x)


def _log_sigmoid(x):
    return jnp.minimum(x, 0.0) - jnp.log(1.0 + jnp.exp(-jnp.abs(x)))


def _split3(x):
    hi = x.astype(BF16)
    r1 = x - hi.astype(F32)
    mid = r1.astype(BF16)
    lo = (r1 - mid.astype(F32)).astype(BF16)
    return hi, mid, lo


def _dot(a, b):
    return jnp.dot(a, b, preferred_element_type=F32)


def _dot_nt(a, b):
    return lax.dot_general(a, b, (((1,), (1,)), ((), ())), preferred_element_type=F32)


def _dot_tn(a, b):
    return lax.dot_general(a, b, (((0,), (0,)), ((), ())), preferred_element_type=F32)


def _rms_mod(x, g, sc, sh):
    r = lax.rsqrt(jnp.mean(x * x, axis=-1, keepdims=True) + EPS)
    return (x * r) * g * (1.0 + sc) + sh


def _ada_kernel(c_ref, w_ref, b_ref, o_ref):
    c = c_ref[...]
    o_ref[0] = _dot(_silu(c).astype(BF16), w_ref[0].astype(BF16)) + b_ref[0]


def _ada(cc, w_ada, b_ada):
    depth, d, n = w_ada.shape
    tn = 1024
    return pl.pallas_call(
        _ada_kernel,
        out_shape=jax.ShapeDtypeStruct((depth, cc.shape[0], n), F32),
        grid=(depth, n // tn),
        in_specs=[pl.BlockSpec(cc.shape, lambda l, j: (0, 0)),
                  pl.BlockSpec((1, d, tn), lambda l, j: (l, 0, j)),
                  pl.BlockSpec((1, 1, tn), lambda l, j: (l, 0, j))],
        out_specs=pl.BlockSpec((1, cc.shape[0], tn), lambda l, j: (l, 0, j)),
        compiler_params=_cparams(("arbitrary", "arbitrary")),
        name="ada",
    )(cc, w_ada, b_ada.reshape(depth, 1, n))


def _inproj_kernel(x_ref, xp_ref, xn_ref, g_ref, sc_ref, sh_ref, w_ref, cw_ref, zm_ref, zp_ref, zg_ref, u_ref, uh_ref,
                   *, n_main, n_pool, n_qk, cw, k_scale, n_lat_tiles, lat_len, ctx_len):
    i = pl.program_id(0)
    tm = x_ref.shape[0]
    g, sc, sh = g_ref[...], sc_ref[0], sh_ref[0]
    u_ref[...] = _rms_mod(x_ref[...], g, sc, sh).astype(BF16)
    uh_ref[0:8, :] = _rms_mod(xp_ref[...], g, sc, sh).astype(BF16)
    uh_ref[8:16, :] = _rms_mod(xn_ref[...], g, sc, sh).astype(BF16)

    row = lax.broadcasted_iota(jnp.int32, (tm, cw), 0)
    is_ctx = i >= n_lat_tiles
    pos0 = lax.rem(i * tm, lat_len)
    lat_first_row = jnp.where(pos0 == 0, 0, -1)
    lat_last_row = jnp.where(pos0 + tm == lat_len, tm - 1, -1)
    in_ctx = row & (ctx_len - 1)
    first = jnp.where(is_ctx, in_ctx, row - lat_first_row) == 0
    last = jnp.where(is_ctx, in_ctx - (ctx_len - 1), row - lat_last_row) == 0

    for c in range(n_qk // cw):
        cs = slice(c * cw, (c + 1) * cw)
        z = _dot(u_ref[...], w_ref[:, cs])
        zh = _dot(uh_ref[...], w_ref[:, cs])
        zm1 = jnp.where(row == 0, zh[7:8, :], pltpu.roll(z, 1, 0))
        zm1 = jnp.where(first, 0.0, zm1)
        zp1 = jnp.where(row == tm - 1, zh[8:9, :], pltpu.roll(z, tm - 1, 0))
        zp1 = jnp.where(last, 0.0, zp1)
        y = _silu(zm1 * cw_ref[0:1, cs] + z * cw_ref[1:2, cs] + zp1 * cw_ref[2:3, cs])
        if c * cw >= n_qk // 2:
            y = y * k_scale
        zm_ref[:, cs] = y.astype(BF16)
    for c in range(n_qk // cw, n_main // cw):
        zm_ref[:, c * cw:(c + 1) * cw] = _dot(u_ref[...], w_ref[:, c * cw:(c + 1) * cw]).astype(BF16)
    for c in range(n_pool // cw):
        lo = n_main + c * cw
        zp_ref[:, c * cw:(c + 1) * cw] = _dot(u_ref[...], w_ref[:, lo:lo + cw])
    zg_ref[...] = _dot(u_ref[...], w_ref[:, n_main + n_pool:])


def _inproj(h, g, mod3, w_cat, conv_w, mod_idx, *, tm, n_main, n_pool, n_qk, k_scale, n_lat, lat_len, ctx_len):
    r, d = h.shape
    ncols = w_cat.shape[1]
    halo = 8
    assert tm % ctx_len == 0 and lat_len % tm == 0 and ctx_len & (ctx_len - 1) == 0
    kern = functools.partial(_inproj_kernel, n_main=n_main, n_pool=n_pool, n_qk=n_qk, cw=512, k_scale=k_scale,
                             n_lat_tiles=n_lat // tm, lat_len=lat_len, ctx_len=ctx_len)
    return pl.pallas_call(
        kern,
        out_shape=(jax.ShapeDtypeStruct((r, n_main), BF16),
                   jax.ShapeDtypeStruct((r, n_pool), F32),
                   jax.ShapeDtypeStruct((r, GATE_PAD), F32)),
        grid=(r // tm,),
        in_specs=[pl.BlockSpec((tm, d), lambda i: (i, 0)),
                  pl.BlockSpec((halo, d), lambda i: (jnp.maximum(i * (tm // halo) - 1, 0), 0)),
                  pl.BlockSpec((halo, d), lambda i: (jnp.minimum((i + 1) * (tm // halo), r // halo - 1), 0)),
                  pl.BlockSpec((1, d), lambda i: (0, 0)),
                  pl.BlockSpec((1, 1, d), lambda i: (mod_idx(i) * N_MOD + 1, 0, 0)),
                  pl.BlockSpec((1, 1, d), lambda i: (mod_idx(i) * N_MOD + 0, 0, 0)),
                  pl.BlockSpec((d, ncols), lambda i: (0, 0)),
                  pl.BlockSpec((3, n_qk), lambda i: (0, 0))],
        out_specs=(pl.BlockSpec((tm, n_main), lambda i: (i, 0)),
                   pl.BlockSpec((tm, n_pool), lambda i: (i, 0)),
                   pl.BlockSpec((tm, GATE_PAD), lambda i: (i, 0))),
        scratch_shapes=[pltpu.VMEM((tm, d), BF16), pltpu.VMEM((2 * halo, d), BF16)],
        compiler_params=_cparams(("arbitrary",)),
        name="inproj",
    )(h, h, h, g, mod3, mod3, w_cat, conv_w)


def _mlstm_kernel(qf, kf, vf, gcf, grf, qb, kb, vb, gcb, grb, bias_c, bias_r, hf_out, hb_out,
                  c_ref, n_ref, m_ref, *, dh):
    L = CHUNK

    @pl.when(pl.program_id(1) == 0)
    def _():
        c_ref[...] = jnp.zeros_like(c_ref)
        n_ref[...] = jnp.zeros_like(n_ref)
        m_ref[...] = jnp.zeros_like(m_ref)

    ri = lax.broadcasted_iota(jnp.int32, (L, L), 0)
    ci = lax.broadcasted_iota(jnp.int32, (L, L), 1)
    lower = ci <= ri
    upper = ci >= ri
    lower_b = jnp.where(lower, 1.0, 0.0).astype(BF16)
    upper_b = jnp.where(upper, 1.0, 0.0).astype(BF16)

    dirs = ((qf, kf, vf, gcf, grf, hf_out), (qb, kb, vb, gcb, grb, hb_out))
    for d, (q_ref, k_ref, v_ref, gc_ref, gr_ref, out_ref) in enumerate(dirs):
        rev = d == 1
        gc = gc_ref[...] + bias_c[...]
        gr = gr_ref[...] + bias_r[...]
        fc = _log_sigmoid(gc)
        fr = _log_sigmoid(gr)
        t_col = upper_b if rev else lower_b
        t_row = lower_b if rev else upper_b
        bcum_c = sum(_dot(t_col, p) for p in _split3(fc))
        bcum_r = sum(_dot(p, t_row) for p in _split3(fr))
        mask = upper if rev else lower
        for h in range(M_HEADS):
            s_idx = d * M_HEADS + h
            col_i = d * 2 * M_HEADS + h
            col_f = col_i + M_HEADS
            i_col = gc[:, col_i:col_i + 1]
            i_row = gr[col_i:col_i + 1, :]
            bc_col = bcum_c[:, col_f:col_f + 1]
            bc_row = bcum_r[col_f:col_f + 1, :]
            total = bc_row[:, 0:1] if rev else bc_row[:, L - 1:L]
            m_st = m_ref[s_idx:s_idx + 1, 0:1]
            c_st = c_ref[s_idx]
            n_st = n_ref[s_idx:s_idx + 1, :]

            g_end_row = total - bc_row + i_row
            g_end_col = total - bc_col + i_col
            m_new = jnp.maximum(total + m_st, jnp.max(g_end_row, axis=1, keepdims=True))
            wk_col = jnp.exp(g_end_col - m_new)
            wc = jnp.exp(total + m_st - m_new)

            a_col = bc_col + m_st
            dmat = jnp.where(mask, bc_col - bc_row + i_row, -jnp.inf)
            m_row = jnp.maximum(a_col, jnp.max(dmat, axis=1, keepdims=True))
            w_inter = jnp.exp(a_col - m_row)
            p = jnp.exp(dmat - m_row)

            sl = slice(h * dh, (h + 1) * dh)
            q = q_ref[:, sl]
            k = k_ref[:, sl]
            v = v_ref[:, sl]
            scores = _dot_nt(q, k) * p
            num = w_inter * _dot(q, c_st.astype(BF16)) + _dot(scores.astype(BF16), v)
            den = (w_inter * jnp.sum(q.astype(F32) * n_st, axis=1, keepdims=True)
                   + jnp.sum(scores, axis=1, keepdims=True))
            inv = 1.0 / jnp.maximum(jnp.abs(den), jnp.exp(-m_row))
            out_ref[:, sl] = (num * inv).astype(BF16)

            kw = k.astype(F32) * wk_col
            c_ref[s_idx] = wc * c_st + _dot_tn(kw.astype(BF16), v)
            n_ref[s_idx:s_idx + 1, :] = wc * n_st + jnp.sum(kw, axis=0, keepdims=True)
            m_ref[s_idx:s_idx + 1, :] = jnp.broadcast_to(m_new, (1, m_ref.shape[1]))


def _mlstm(zm, zg, zg_t, bias_c, bias_r, *, b, s, width):
    r = zm.shape[0]
    L = CHUNK
    n_lat = s // L
    lat_blocks = b * n_lat

    def blk_f(bi, i):
        return jnp.where(i == 0, lat_blocks + bi, bi * n_lat + i - 1)

    def blk_b(bi, i):
        return jnp.where(i == 0, lat_blocks + bi, bi * n_lat + n_lat - i)

    def specs(blk):
        return [pl.BlockSpec((L, width), lambda bi, i: (blk(bi, i), 0)),
                pl.BlockSpec((L, width), lambda bi, i: (blk(bi, i), 1)),
                pl.BlockSpec((L, width), lambda bi, i: (blk(bi, i), 2)),
                pl.BlockSpec((L, GATE_PAD), lambda bi, i: (blk(bi, i), 0)),
                pl.BlockSpec((16, L), lambda bi, i: (0, blk(bi, i)))]

    kern = functools.partial(_mlstm_kernel, dh=width // M_HEADS)
    out = jax.ShapeDtypeStruct((r, width), BF16)
    return pl.pallas_call(
        kern,
        out_shape=(out, out),
        grid=(b, n_lat + 1),
        in_specs=specs(blk_f) + specs(blk_b) + [pl.BlockSpec((1, GATE_PAD), lambda bi, i: (0, 0)),
                                                pl.BlockSpec((16, L), lambda bi, i: (0, 0))],
        out_specs=(pl.BlockSpec((L, width), lambda bi, i: (blk_f(bi, i), 0)),
                   pl.BlockSpec((L, width), lambda bi, i: (blk_b(bi, i), 0))),
        scratch_shapes=[pltpu.VMEM((2 * M_HEADS, width // M_HEADS, width // M_HEADS), F32),
                        pltpu.VMEM((2 * M_HEADS, width // M_HEADS), F32),
                        pltpu.VMEM((2 * M_HEADS, 128), F32)],
        compiler_params=_cparams(("arbitrary", "arbitrary")),
        name="mlstm",
    )(zm, zm, zm, zg, zg_t, zm, zm, zm, zg, zg_t, bias_c, bias_r)


POOL_BLOCK = 256


def _band(n, w, seg):
    r = lax.broadcasted_iota(jnp.int32, (n, n), 0)
    c = lax.broadcasted_iota(jnp.int32, (n, n), 1)
    d = c - r
    shift = seg.bit_length() - 1
    ok = (d >= -(w // 2)) & (d <= w // 2 - 1) & ((r >> shift) == (c >> shift))
    return jnp.where(ok, 1.0, 0.0).astype(BF16)


def _window_count(idx, n, w):
    return (jnp.minimum(idx + w // 2, n) - jnp.maximum(idx - w // 2, 0)).astype(F32)


def _pool_kernel(z_ref, o_ref, pad_ref, *, grid_rows):
    n = z_ref.shape[0]
    blk = min(POOL_BLOCK, n)
    seg = blk if grid_rows is None else GRID_W
    seg_len = n if grid_rows is None else GRID_W
    halo = (max(POOL_WINDOWS) // 2) * GRID_W
    rowi = lax.broadcasted_iota(jnp.int32, (blk, POOL_GROUP_DIM), 0)
    if grid_rows is not None:
        zeros = jnp.zeros((halo, POOL_GROUP_DIM), F32)
        pad_ref[0:halo, :] = zeros
        pad_ref[halo + n:, :] = zeros
    for g, w in enumerate(POOL_WINDOWS):
        sl = slice(g * POOL_GROUP_DIM, (g + 1) * POOL_GROUP_DIM)
        band = _band(blk, w, seg)
        inv_w = 1.0 / _window_count(rowi & (seg - 1), seg_len, w)
        for b in range(n // blk):
            rows = slice(b * blk, (b + 1) * blk)
            x = z_ref[rows, sl]
            hi = x.astype(BF16)
            lo = (x - hi.astype(F32)).astype(BF16)
            y2 = _dot(band, jnp.concatenate([hi, lo], axis=1))
            y = (y2[:, :POOL_GROUP_DIM] + y2[:, POOL_GROUP_DIM:]) * inv_w
            if grid_rows is None:
                o_ref[rows, sl] = (y - x).astype(BF16)
            else:
                pad_ref[halo + b * blk:halo + (b + 1) * blk, :] = y
        if grid_rows is not None:
            shift = GRID_W.bit_length() - 1
            for b in range(n // blk):
                acc = None
                for d in range(-(w // 2), w // 2):
                    lo_r = halo + b * blk + d * GRID_W
                    t = pad_ref[lo_r:lo_r + blk, :]
                    acc = t if acc is None else acc + t
                inv_h = 1.0 / _window_count((rowi + b * blk) >> shift, grid_rows, w)
                rows = slice(b * blk, (b + 1) * blk)
                o_ref[rows, sl] = (acc * inv_h - z_ref[rows, sl]).astype(BF16)


def _pool(zp, *, n_seq, seq_len, row_block0, grid_rows):
    width = zp.shape[1]
    assert seq_len % POOL_BLOCK == 0 if grid_rows is not None else seq_len <= POOL_BLOCK
    kern = functools.partial(_pool_kernel, grid_rows=grid_rows)
    halo = (max(POOL_WINDOWS) // 2) * GRID_W
    return pl.pallas_call(
        kern,
        out_shape=jax.ShapeDtypeStruct((n_seq * seq_len, width), BF16),
        grid=(n_seq,),
        in_specs=[pl.BlockSpec((seq_len, width), lambda s: (row_block0 + s, 0))],
        out_specs=pl.BlockSpec((seq_len, width), lambda s: (s, 0)),
        scratch_shapes=[pltpu.VMEM((seq_len + 2 * halo, POOL_GROUP_DIM), F32)],
        compiler_params=_cparams(("arbitrary",)),
        name="pool",
    )(zp)


def _mixout_kernel(hf_ref, hb_ref, o_ref, ga_ref, gb_ref, pml_ref, pmc_ref, x_ref, gt_ref, hg_ref, wpool_ref, ps_ref,
                   wpa_ref, wpb_ref, wout_ref, out_ref, *, dh, n_lat_tiles):
    hm = hf_ref[...].astype(F32) + hb_ref[...].astype(F32)
    parts = []
    for h in range(M_HEADS):
        t = hm[:, h * dh:(h + 1) * dh]
        mu = jnp.mean(t, axis=-1, keepdims=True)
        tc = t - mu
        var = jnp.mean(tc * tc, axis=-1, keepdims=True)
        parts.append(tc * lax.rsqrt(var + EPS))
    hn = jnp.concatenate(parts, axis=-1)
    hn = _sigmoid(o_ref[...].astype(F32)) * (hn * hg_ref[...])
    a = _dot(hn.astype(BF16), wpa_ref[...])
    pm_in = jnp.where(pl.program_id(0) >= n_lat_tiles, pmc_ref[...], pml_ref[...])
    pparts = []
    for g in range(len(POOL_WINDOWS)):
        sl = slice(g * POOL_GROUP_DIM, (g + 1) * POOL_GROUP_DIM)
        pparts.append(_dot(pm_in[:, sl], wpool_ref[g]))
    pm = jnp.concatenate(pparts, axis=-1) * ps_ref[...]
    bmat = _dot(pm.astype(BF16), wpb_ref[...])
    merged = _sigmoid(ga_ref[...].astype(F32)) * a + _sigmoid(gb_ref[...].astype(F32)) * bmat
    y = _dot(merged.astype(BF16), wout_ref[...])
    out_ref[...] = x_ref[...] + gt_ref[0] * y


def _mixout(hf, hb, zm, pm_lat, pm_ctx, h, mod3, head_gain, w_pool, pool_scale, w_pa, w_pb, w_out, mod_idx, *, tm,
            n_rows):
    d = h.shape[1]
    width = hf.shape[1]
    pw = pm_lat.shape[1]
    n_lat_tiles = pm_lat.shape[0] // tm
    o_blk = 3
    full = lambda shape: pl.BlockSpec(shape, lambda i: (0,) * len(shape))
    kern = functools.partial(_mixout_kernel, dh=width // M_HEADS, n_lat_tiles=n_lat_tiles)
    return pl.pallas_call(
        kern,
        out_shape=jax.ShapeDtypeStruct((n_rows, d), F32),
        grid=(n_rows // tm,),
        in_specs=[pl.BlockSpec((tm, width), lambda i: (i, 0)),
                  pl.BlockSpec((tm, width), lambda i: (i, 0)),
                  pl.BlockSpec((tm, width), lambda i: (i, o_blk)),
                  pl.BlockSpec((tm, d), lambda i: (i, o_blk + 1)),
                  pl.BlockSpec((tm, d), lambda i: (i, o_blk + 2)),
                  pl.BlockSpec((tm, pw), lambda i: (jnp.minimum(i, n_lat_tiles - 1), 0)),
                  pl.BlockSpec((tm, pw), lambda i: (jnp.maximum(i - n_lat_tiles, 0), 0)),
                  pl.BlockSpec((tm, d), lambda i: (i, 0)),
                  pl.BlockSpec((1, 1, d), lambda i: (mod_idx(i) * N_MOD + 2, 0, 0)),
                  full((1, width)), full(w_pool.shape), full((1, pw)),
                  full(w_pa.shape), full(w_pb.shape), full(w_out.shape)],
        out_specs=pl.BlockSpec((tm, d), lambda i: (i, 0)),
        compiler_params=_cparams(("arbitrary",)),
        name="mixout",
    )(hf, hb, zm, zm, zm, pm_lat, pm_ctx, h, mod3, head_gain, w_pool, pool_scale, w_pa, w_pb, w_out)


def _ffn_kernel(x_ref, g_ref, sc_ref, sh_ref, gt_ref, wg_ref, wu_ref, wd_ref, out_ref, u_ref, acc_ref):
    j = pl.program_id(1)

    @pl.when(j == 0)
    def _():
        u_ref[...] = _rms_mod(x_ref[...], g_ref[...], sc_ref[0], sh_ref[0]).astype(BF16)
        acc_ref[...] = jnp.zeros_like(acc_ref)

    u = u_ref[...]
    act = _silu(_dot(u, wg_ref[...])) * _dot(u, wu_ref[...])
    acc_ref[...] += _dot(act.astype(BF16), wd_ref[...])

    @pl.when(j == pl.num_programs(1) - 1)
    def _():
        out_ref[...] = x_ref[...] + gt_ref[0] * acc_ref[...]


def _ffn(h, g, mod3, wg, wu, wd, mod_idx, *, tm, tf):
    r, d = h.shape
    f = wg.shape[1]
    return pl.pallas_call(
        _ffn_kernel,
        out_shape=jax.ShapeDtypeStruct((r, d), F32),
        grid=(r // tm, f // tf),
        in_specs=[pl.BlockSpec((tm, d), lambda i, j: (i, 0)),
                  pl.BlockSpec((1, d), lambda i, j: (0, 0)),
                  pl.BlockSpec((1, 1, d), lambda i, j: (mod_idx(i) * N_MOD + 4, 0, 0)),
                  pl.BlockSpec((1, 1, d), lambda i, j: (mod_idx(i) * N_MOD + 3, 0, 0)),
                  pl.BlockSpec((1, 1, d), lambda i, j: (mod_idx(i) * N_MOD + 5, 0, 0)),
                  pl.BlockSpec((d, tf), lambda i, j: (0, j)),
                  pl.BlockSpec((d, tf), lambda i, j: (0, j)),
                  pl.BlockSpec((tf, d), lambda i, j: (j, 0))],
        out_specs=pl.BlockSpec((tm, d), lambda i, j: (i, 0)),
        scratch_shapes=[pltpu.VMEM((tm, d), BF16), pltpu.VMEM((tm, d), F32)],
        compiler_params=_cparams(("arbitrary", "arbitrary")),
        name="ffn",
    )(h, g, mod3, mod3, mod3, wg, wu, wd)


def _route(u, wr_ref, n_exp):
    parts_u = _split3(u)
    parts_w = _split3(wr_ref[...])
    logits = (_dot(parts_u[0], parts_w[0]) + _dot(parts_u[0], parts_w[1]) + _dot(parts_u[1], parts_w[0])
              + _dot(parts_u[1], parts_w[1]) + _dot(parts_u[0], parts_w[2]) + _dot(parts_u[2], parts_w[0]))
    lane = lax.broadcasted_iota(jnp.int32, logits.shape, 1)
    lg = jnp.where(lane < n_exp, logits, -jnp.inf)
    m1 = jnp.max(lg, axis=1, keepdims=True)
    i1 = jnp.min(jnp.where(lg == m1, lane, 2 * GATE_PAD), axis=1, keepdims=True)
    lg2 = jnp.where(lane == i1, -jnp.inf, lg)
    m2 = jnp.max(lg2, axis=1, keepdims=True)
    i2 = jnp.min(jnp.where(lg2 == m2, lane, 2 * GATE_PAD), axis=1, keepdims=True)
    e = jnp.exp(m2 - m1)
    p1 = 1.0 / (1.0 + e)
    p2 = e * p1
    return i1, i2, p1, p2


LANE_I1, LANE_I2, LANE_P1, LANE_P2, LANE_R1, LANE_R2 = 0, 1, 2, 3, 4, 5


def _pack_bf16_pairs(x):
    k = x.shape[1] // 2
    lo = lax.bitcast_convert_type(x[:, :k].astype(BF16).astype(F32), jnp.uint32)
    hi = lax.bitcast_convert_type(x[:, k:].astype(BF16).astype(F32), jnp.uint32)
    return (lo >> 16) | (hi & jnp.uint32(0xFFFF0000))


def _unpack_bf16_pairs(w):
    lo = lax.bitcast_convert_type(w << 16, F32)
    hi = lax.bitcast_convert_type(w & jnp.uint32(0xFFFF0000), F32)
    return jnp.concatenate([lo, hi], axis=1)


def _route_kernel(x_ref, g_ref, sc_ref, sh_ref, wr_ref, up_ref, info_ref, cnt_ref, carry_ref, *, n_exp):
    @pl.when(pl.program_id(0) == 0)
    def _():
        carry_ref[...] = jnp.zeros_like(carry_ref)

    u = _rms_mod(x_ref[...], g_ref[...], sc_ref[0], sh_ref[0])
    up_ref[...] = _pack_bf16_pairs(u)
    i1, i2, p1, p2 = _route(u, wr_ref, n_exp)
    tm = u.shape[0]
    lane = lax.broadcasted_iota(jnp.int32, (tm, GATE_PAD), 1)
    sel = jnp.where((lane == i1) | (lane == i2), 1.0, 0.0)
    ri = lax.broadcasted_iota(jnp.int32, (tm, tm), 0)
    ci = lax.broadcasted_iota(jnp.int32, (tm, tm), 1)
    before = jnp.where(ci < ri, 1.0, 0.0).astype(BF16)
    rank = carry_ref[0:1, :] + _dot(before, sel.astype(BF16))
    r1 = jnp.sum(jnp.where(lane == i1, rank, 0.0), axis=1, keepdims=True)
    r2 = jnp.sum(jnp.where(lane == i2, rank, 0.0), axis=1, keepdims=True)
    carry_ref[0:1, :] = carry_ref[0:1, :] + jnp.sum(sel, axis=0, keepdims=True)
    cnt_ref[...] = carry_ref[...]
    info = jnp.zeros((tm, GATE_PAD), F32)
    for ln, val in ((LANE_I1, i1.astype(F32)), (LANE_I2, i2.astype(F32)), (LANE_P1, p1), (LANE_P2, p2),
                    (LANE_R1, r1), (LANE_R2, r2)):
        info = jnp.where(lane == ln, val, info)
    info_ref[...] = info


def _route_call(h, g, mod3, w_router, mod_idx, *, tm, n_rows):
    d = h.shape[1]
    n_exp = w_router.shape[1]
    w_r = jnp.concatenate([w_router, jnp.zeros((d, GATE_PAD - n_exp), F32)], axis=1)
    kern = functools.partial(_route_kernel, n_exp=n_exp)
    return pl.pallas_call(
        kern,
        out_shape=(jax.ShapeDtypeStruct((n_rows, d // 2), jnp.uint32),
                   jax.ShapeDtypeStruct((n_rows, GATE_PAD), F32),
                   jax.ShapeDtypeStruct((8, GATE_PAD), F32)),
        grid=(n_rows // tm,),
        in_specs=[pl.BlockSpec((tm, d), lambda i: (i, 0)),
                  pl.BlockSpec((1, d), lambda i: (0, 0)),
                  pl.BlockSpec((1, 1, d), lambda i: (mod_idx(i) * N_MOD + 4, 0, 0)),
                  pl.BlockSpec((1, 1, d), lambda i: (mod_idx(i) * N_MOD + 3, 0, 0)),
                  pl.BlockSpec((d, GATE_PAD), lambda i: (0, 0))],
        out_specs=(pl.BlockSpec((tm, d // 2), lambda i: (i, 0)),
                   pl.BlockSpec((tm, GATE_PAD), lambda i: (i, 0)),
                   pl.BlockSpec((8, GATE_PAD), lambda i: (0, 0))),
        scratch_shapes=[pltpu.VMEM((8, GATE_PAD), F32)],
        compiler_params=_cparams(("arbitrary",)),
        name="route",
    )(h, g, mod3, mod3, w_r)


def _sc_workers():
    info = pltpu.get_tpu_info().sparse_core
    return info.num_cores, info.num_subcores


def _sc_scatter_rows(x, pos, n_out):
    n, dw = x.shape
    nc, ns = _sc_workers()
    nw = nc * ns
    t_per_w = n // nw
    w = min(SC_WINDOW, t_per_w // 2)
    n_chunks = t_per_w // w
    assert n % nw == 0 and t_per_w % (2 * w) == 0
    pos_w = pos.reshape(2, nw, n_chunks, w).transpose(1, 0, 2, 3)
    mesh = plsc.VectorSubcoreMesh(core_axis_name="c", subcore_axis_name="s")

    @functools.partial(pl.kernel, mesh=mesh, out_type=jax.ShapeDtypeStruct((n_out, dw), x.dtype),
                       scratch_types=[pltpu.VMEM((2, n_chunks, w), jnp.int32), pltpu.VMEM((2, w, dw), x.dtype),
                                      pltpu.SemaphoreType.DMA((2,)), pltpu.SemaphoreType.DMA((2,))])
    def scatter(x_hbm, pos_hbm, out_hbm, idx_v, rows_v, gsem, osem):
        wid = lax.axis_index("s") * nc + lax.axis_index("c")
        base = wid * t_per_w
        pltpu.sync_copy(pos_hbm.at[wid], idx_v)

        def get(g, slot):
            return pltpu.make_async_copy(x_hbm.at[pl.ds(base + g * w, w)], rows_v.at[slot], gsem.at[slot])

        def put(g, slot, k):
            return pltpu.make_async_copy(rows_v.at[slot], out_hbm.at[idx_v.at[k, g]], osem.at[slot])

        get(0, 0).start()

        @pl.loop(0, n_chunks, step=2)
        def _(g):
            for b in range(2):
                gg = g + b
                get(gg, b).wait()

                @pl.when(gg + 1 < n_chunks)
                def _():
                    @pl.when(gg >= 1)
                    def _():
                        put(gg - 1, 1 - b, 0).wait()
                        put(gg - 1, 1 - b, 1).wait()
                    get(gg + 1, 1 - b).start()

                put(gg, b, 0).start()
                put(gg, b, 1).start()

        for k in range(2):
            put(n_chunks - 2, 0, k).wait()
            put(n_chunks - 1, 1, k).wait()

    return scatter(x, pos_w)


def _sc_gather_rows(table, idx):
    n = idx.shape[0]
    dw = table.shape[1]
    nc, ns = _sc_workers()
    nw = nc * ns
    b_per_w = n // nw
    w = min(SC_WINDOW, b_per_w // 2)
    n_chunks = b_per_w // w
    assert n % nw == 0 and b_per_w % (2 * w) == 0
    mesh = plsc.VectorSubcoreMesh(core_axis_name="c", subcore_axis_name="s")

    @functools.partial(pl.kernel, mesh=mesh, out_type=jax.ShapeDtypeStruct((n, dw), table.dtype),
                       scratch_types=[pltpu.VMEM((b_per_w,), jnp.int32), pltpu.VMEM((2, w, dw), table.dtype),
                                      pltpu.SemaphoreType.DMA((2,)), pltpu.SemaphoreType.DMA((2,))])
    def gather(table_hbm, idx_hbm, out_hbm, idx_v, rows_v, gsem, osem):
        wid = lax.axis_index("s") * nc + lax.axis_index("c")
        base = wid * b_per_w
        pltpu.sync_copy(idx_hbm.at[pl.ds(base, b_per_w)], idx_v)

        def get(g, slot):
            return pltpu.make_async_copy(table_hbm.at[idx_v.at[pl.ds(g * w, w)]], rows_v.at[slot], gsem.at[slot])

        def put(g, slot):
            return pltpu.make_async_copy(rows_v.at[slot], out_hbm.at[pl.ds(base + g * w, w)], osem.at[slot])

        get(0, 0).start()

        @pl.loop(0, n_chunks, step=2)
        def _(g):
            for b in range(2):
                gg = g + b
                get(gg, b).wait()

                @pl.when(gg + 1 < n_chunks)
                def _():
                    @pl.when(gg >= 1)
                    def _():
                        put(gg - 1, 1 - b).wait()
                    get(gg + 1, 1 - b).start()

                put(gg, b).start()

        put(n_chunks - 2, 0).wait()
        put(n_chunks - 1, 1).wait()

    return gather(table, idx)


def _gmm_kernel(te_ref, nv_ref, xs_ref, wg_ref, wu_ref, wd_ref, ys_ref, u_ref, acc_ref):
    i = pl.program_id(0)
    j = pl.program_id(1)

    @pl.when(i < nv_ref[0])
    def _():
        @pl.when(j == 0)
        def _():
            u_ref[...] = _unpack_bf16_pairs(xs_ref[...]).astype(BF16)
            acc_ref[...] = jnp.zeros_like(acc_ref)

        u = u_ref[...]
        act = _silu(_dot(u, wg_ref[0])) * _dot(u, wu_ref[0])
        acc_ref[...] += _dot(act.astype(BF16), wd_ref[0])

        @pl.when(j == pl.num_programs(1) - 1)
        def _():
            ys_ref[...] = _pack_bf16_pairs(acc_ref[...])


def _gmm(xs, tile_expert, n_valid, wg, wu, wd, *, tg, tf):
    p, dw = xs.shape
    d = 2 * dw
    f = wg.shape[2]
    n_f = f // tf

    def jj(i, j, nv):
        return jnp.where(i < nv[0], j, n_f - 1)

    grid_spec = pltpu.PrefetchScalarGridSpec(
        num_scalar_prefetch=2,
        grid=(p // tg, n_f),
        in_specs=[pl.BlockSpec((tg, dw), lambda i, j, te, nv: (i, 0)),
                  pl.BlockSpec((1, d, tf), lambda i, j, te, nv: (te[i], 0, jj(i, j, nv))),
                  pl.BlockSpec((1, d, tf), lambda i, j, te, nv: (te[i], 0, jj(i, j, nv))),
                  pl.BlockSpec((1, tf, d), lambda i, j, te, nv: (te[i], jj(i, j, nv), 0))],
        out_specs=pl.BlockSpec((tg, dw), lambda i, j, te, nv: (i, 0)),
        scratch_shapes=[pltpu.VMEM((tg, d), BF16), pltpu.VMEM((tg, d), F32)])
    return pl.pallas_call(
        _gmm_kernel,
        out_shape=jax.ShapeDtypeStruct((p, dw), jnp.uint32),
        grid_spec=grid_spec,
        compiler_params=_cparams(("arbitrary", "arbitrary")),
        name="gmm",
    )(tile_expert, n_valid, xs, wg, wu, wd)


def _combine_kernel(x_ref, info_ref, y1_ref, y2_ref, gt_ref, gf_ref, out_ref, *, final):
    info = info_ref[...]
    p1 = info[:, LANE_P1:LANE_P1 + 1]
    p2 = info[:, LANE_P2:LANE_P2 + 1]
    moe = p1 * _unpack_bf16_pairs(y1_ref[...]) + p2 * _unpack_bf16_pairs(y2_ref[...])
    y = x_ref[...] + gt_ref[0] * moe
    if final:
        r = lax.rsqrt(jnp.mean(y * y, axis=-1, keepdims=True) + EPS)
        y = (y * r) * gf_ref[...]
    out_ref[...] = y


def _combine(h, info, yg, mod3, g_final, mod_idx, *, tm, n_rows, final):
    d = h.shape[1]
    n_t = n_rows // tm
    kern = functools.partial(_combine_kernel, final=final)
    return pl.pallas_call(
        kern,
        out_shape=jax.ShapeDtypeStruct((n_rows, d), F32),
        grid=(n_t,),
        in_specs=[pl.BlockSpec((tm, d), lambda i: (i, 0)),
                  pl.BlockSpec((tm, GATE_PAD), lambda i: (i, 0)),
                  pl.BlockSpec((tm, d // 2), lambda i: (i, 0)),
                  pl.BlockSpec((tm, d // 2), lambda i: (n_t + i, 0)),
                  pl.BlockSpec((1, 1, d), lambda i: (mod_idx(i) * N_MOD + 5, 0, 0)),
                  pl.BlockSpec((1, d), lambda i: (0, 0))],
        out_specs=pl.BlockSpec((tm, d), lambda i: (i, 0)),
        compiler_params=_cparams(("arbitrary",)),
        name="combine",
    )(h, info, yg, yg, mod3, g_final)


def _moe(h, g, mod3, w_router, wg, wu, wd, g_final, mod_idx, *, tm, tg, tf, n_rows, final):
    n_exp = wg.shape[0]
    up, info, cnt = _route_call(h, g, mod3, w_router, mod_idx, tm=tm, n_rows=n_rows)
    i1 = info[:, LANE_I1].astype(jnp.int32)
    i2 = info[:, LANE_I2].astype(jnp.int32)
    r1 = info[:, LANE_R1].astype(jnp.int32)
    r2 = info[:, LANE_R2].astype(jnp.int32)
    counts = cnt[0, :n_exp].astype(jnp.int32)
    padded = (counts + tg - 1) // tg * tg
    ends = jnp.cumsum(padded)
    starts = ends - padded
    eye = jnp.arange(n_exp, dtype=jnp.int32)
    pos1 = jnp.sum(jnp.where(i1[:, None] == eye, starts, 0), axis=1) + r1
    pos2 = jnp.sum(jnp.where(i2[:, None] == eye, starts, 0), axis=1) + r2
    p_rows = -(-(TOP_K * n_rows + n_exp * (tg - 1)) // tg) * tg
    tile_row = jnp.arange(p_rows // tg, dtype=jnp.int32) * tg
    tile_expert = jnp.minimum(jnp.sum(tile_row[:, None] >= ends[None, :], axis=1), n_exp - 1).astype(jnp.int32)
    n_valid = (ends[-1] // tg).reshape(1).astype(jnp.int32)
    xs = _sc_scatter_rows(up, jnp.stack([pos1, pos2]), p_rows)
    ys = _gmm(xs, tile_expert, n_valid, wg, wu, wd, tg=tg, tf=tf)
    yg = _sc_gather_rows(ys, jnp.concatenate([pos1, pos2]))
    return _combine(h, info, yg, mod3, g_final, mod_idx, tm=tm, n_rows=n_rows, final=final)


def _final_norm_kernel(x_ref, g_ref, o_ref):
    x = x_ref[...]
    r = lax.rsqrt(jnp.mean(x * x, axis=-1, keepdims=True) + EPS)
    o_ref[...] = (x * r) * g_ref[...]


def _final_norm(h, g, *, tm, n_rows):
    d = h.shape[1]
    return pl.pallas_call(
        _final_norm_kernel,
        out_shape=jax.ShapeDtypeStruct((n_rows, d), F32),
        grid=(n_rows // tm,),
        in_specs=[pl.BlockSpec((tm, d), lambda i: (i, 0)), pl.BlockSpec((1, d), lambda i: (0, 0))],
        out_specs=pl.BlockSpec((tm, d), lambda i: (i, 0)),
        compiler_params=_cparams(("arbitrary",)),
        name="final_norm",
    )(h, g)


def kernel(x, c, ctx, c_ctx, w_ada, b_ada, g_mix, w_in, conv_qk, b_if, head_gain, w_pool, pool_scale, w_pa, w_pb,
           w_out, g_ffn, w_ff_gate, w_ff_up, w_ff_down, w_router, w_exp_gate, w_exp_up, w_exp_down, g_final):
    b, s, d = x.shape
    lc = ctx.shape[1]
    depth = w_ada.shape[0]
    width = w_pa.shape[1]
    pw = w_pb.shape[1]
    n_gate = b_if.shape[1] * b_if.shape[2] * b_if.shape[3]
    assert lc == CHUNK and s % CHUNK == 0 and s % GRID_W == 0 and width == d and n_gate == 16
    n_lat, n_ctx = b * s, b * lc
    tm, tm_ffn = 512, 1024
    assert s % tm_ffn == 0 and n_ctx % tm_ffn == 0 and tm_ffn % tm == 0

    def mod_idx(t):
        return lambda i: jnp.where(i < n_lat // t, i // (s // t), b)

    h = jnp.concatenate([x.reshape(n_lat, d), ctx.reshape(n_ctx, d)], axis=0)

    n_mod_rows = 16
    cc = jnp.zeros((n_mod_rows, d), F32).at[:b].set(c).at[b].set(c_ctx)
    mod = _ada(cc, w_ada, b_ada)

    if_off, pool_off, ga_off = 4 * width, 4 * width + n_gate, 4 * width + n_gate + pw
    n_main = 4 * width + 2 * d

    for l in range(depth):
        last = l == depth - 1
        mod3 = mod[l].reshape(n_mod_rows * N_MOD, 1, d)
        w_l = w_in[l]
        w_cat = jnp.concatenate(
            [w_l[:, :if_off], w_l[:, ga_off:], w_l[:, pool_off:ga_off], w_l[:, if_off:pool_off],
             jnp.zeros((d, GATE_PAD - n_gate), F32)], axis=1).astype(BF16)
        k_scale = float((width // M_HEADS) ** -0.5)
        zm, zp, zg = _inproj(h, g_mix[l].reshape(1, d), mod3, w_cat, conv_qk[l], mod_idx(tm), tm=tm, n_main=n_main,
                             n_pool=pw, n_qk=2 * width, k_scale=k_scale, n_lat=n_lat, lat_len=s, ctx_len=lc)

        bias = b_if[l].reshape(n_gate)
        bias_c = jnp.zeros((1, GATE_PAD), F32).at[0, :n_gate].set(bias)
        bias_r = jnp.broadcast_to(bias[:, None], (n_gate, CHUNK))
        hf, hb = _mlstm(zm, zg, zg[:, :n_gate].T, bias_c, bias_r, b=b, s=s, width=width)

        n_rows = n_lat if last else n_lat + n_ctx
        pm_lat = _pool(zp, n_seq=b, seq_len=s, row_block0=0, grid_rows=s // GRID_W)
        pm_ctx = pm_lat if last else _pool(zp, n_seq=b, seq_len=lc, row_block0=n_lat // lc, grid_rows=None)

        h = _mixout(hf, hb, zm, pm_lat, pm_ctx, h, mod3, head_gain[l].reshape(1, width), w_pool[l].astype(BF16),
                    pool_scale[l].reshape(1, pw), w_pa[l].astype(BF16), w_pb[l].astype(BF16),
                    w_out[l].astype(BF16), mod_idx(tm), tm=tm, n_rows=n_rows)

        j = l // 2
        if l % 2 == 0:
            h = _ffn(h, g_ffn[l].reshape(1, d), mod3, w_ff_gate[j].astype(BF16), w_ff_up[j].astype(BF16),
                     w_ff_down[j].astype(BF16), mod_idx(tm_ffn), tm=tm_ffn, tf=256)
            if last:
                h = _final_norm(h, g_final.reshape(1, d), tm=tm, n_rows=n_lat)
        else:
            h = _moe(h, g_ffn[l].reshape(1, d), mod3, w_router[j], w_exp_gate[j].astype(BF16),
                     w_exp_up[j].astype(BF16), w_exp_down[j].astype(BF16), g_final.reshape(1, d), mod_idx(tm),
                     tm=tm, tg=512, tf=512, n_rows=h.shape[0], final=last)
    return h[:n_lat].reshape(b, s, d)
```

```python
import functools

import jax
import jax.numpy as jnp
from jax import lax
from jax.experimental import pallas as pl
from jax.experimental.pallas import tpu as pltpu
from jax.experimental.pallas import tpu_sc as plsc

F32 = jnp.float32
BF16 = jnp.bfloat16
EPS = 1e-6
M_HEADS = 4
GRID_W = 64
POOL_WINDOWS = (2, 4, 8, 16)
POOL_GROUP_DIM = 128
CHUNK = 256
N_MOD = 6
TOP_K = 2
SC_WINDOW = 64
GATE_PAD = 128
VMEM_LIMIT = 56 * 1024 * 1024


def _pick_tile(n, target, quantum=128):
    t = min(n, target) // quantum * quantum
    while n % t:
        t -= quantum
    return t


def _cparams(sem):
    return pltpu.CompilerParams(dimension_semantics=sem, vmem_limit_bytes=VMEM_LIMIT)


def _sigmoid(x):
    return 1.0 / (1.0 + jnp.exp(-x))


def _silu(x):
    return x * _sigmoid(x)


def _log_sigmoid(x):
    return jnp.minimum(x, 0.0) - jnp.log(1.0 + jnp.exp(-jnp.abs(x)))


def _split3(x):
    hi = x.astype(BF16)
    r1 = x - hi.astype(F32)
    mid = r1.astype(BF16)
    lo = (r1 - mid.astype(F32)).astype(BF16)
    return hi, mid, lo


def _dot(a, b):
    return jnp.dot(a, b, preferred_element_type=F32)


def _dot_nt(a, b):
    return lax.dot_general(a, b, (((1,), (1,)), ((), ())), preferred_element_type=F32)


def _dot_tn(a, b):
    return lax.dot_general(a, b, (((0,), (0,)), ((), ())), preferred_element_type=F32)


def _rms_mod(x, g, sc, sh):
    r = lax.rsqrt(jnp.mean(x * x, axis=-1, keepdims=True) + EPS)
    return (x * r) * g * (1.0 + sc) + sh


def _ada_kernel(c_ref, w_ref, b_ref, o_ref):
    c = c_ref[...]
    o_ref[0] = _dot(_silu(c).astype(BF16), w_ref[0].astype(BF16)) + b_ref[0]


def _ada(cc, w_ada, b_ada):
    depth, d, n = w_ada.shape
    tn = 1024
    return pl.pallas_call(
        _ada_kernel,
        out_shape=jax.ShapeDtypeStruct((depth, cc.shape[0], n), F32),
        grid=(depth, n // tn),
        in_specs=[pl.BlockSpec(cc.shape, lambda l, j: (0, 0)),
                  pl.BlockSpec((1, d, tn), lambda l, j: (l, 0, j)),
                  pl.BlockSpec((1, 1, tn), lambda l, j: (l, 0, j))],
        out_specs=pl.BlockSpec((1, cc.shape[0], tn), lambda l, j: (l, 0, j)),
        compiler_params=_cparams(("arbitrary", "arbitrary")),
        name="ada",
    )(cc, w_ada, b_ada.reshape(depth, 1, n))


def _inproj_kernel(x_ref, xp_ref, xn_ref, g_ref, sc_ref, sh_ref, w_ref, cw_ref, zm_ref, zp_ref, zg_ref, u_ref, uh_ref,
                   *, n_main, n_pool, n_qk, cw, k_scale, n_lat_tiles, lat_len, ctx_len):
    i = pl.program_id(0)
    tm = x_ref.shape[0]
    g, sc, sh = g_ref[...], sc_ref[0], sh_ref[0]
    u_ref[...] = _rms_mod(x_ref[...], g, sc, sh).astype(BF16)
    uh_ref[0:8, :] = _rms_mod(xp_ref[...], g, sc, sh).astype(BF16)
    uh_ref[8:16, :] = _rms_mod(xn_ref[...], g, sc, sh).astype(BF16)

    row = lax.broadcasted_iota(jnp.int32, (tm, cw), 0)
    is_ctx = i >= n_lat_tiles
    pos0 = lax.rem(i * tm, lat_len)
    lat_first_row = jnp.where(pos0 == 0, 0, -1)
    lat_last_row = jnp.where(pos0 + tm == lat_len, tm - 1, -1)
    in_ctx = row & (ctx_len - 1)
    first = jnp.where(is_ctx, in_ctx, row - lat_first_row) == 0
    last = jnp.where(is_ctx, in_ctx - (ctx_len - 1), row - lat_last_row) == 0

    for c in range(n_qk // cw):
        cs = slice(c * cw, (c + 1) * cw)
        z = _dot(u_ref[...], w_ref[:, cs])
        zh = _dot(uh_ref[...], w_ref[:, cs])
        zm1 = jnp.where(row == 0, zh[7:8, :], pltpu.roll(z, 1, 0))
        zm1 = jnp.where(first, 0.0, zm1)
        zp1 = jnp.where(row == tm - 1, zh[8:9, :], pltpu.roll(z, tm - 1, 0))
        zp1 = jnp.where(last, 0.0, zp1)
        y = _silu(zm1 * cw_ref[0:1, cs] + z * cw_ref[1:2, cs] + zp1 * cw_ref[2:3, cs])
        if c * cw >= n_qk // 2:
            y = y * k_scale
        zm_ref[:, cs] = y.astype(BF16)
    for c in range(n_qk // cw, n_main // cw):
        zm_ref[:, c * cw:(c + 1) * cw] = _dot(u_ref[...], w_ref[:, c * cw:(c + 1) * cw]).astype(BF16)
    for c in range(n_pool // cw):
        lo = n_main + c * cw
        zp_ref[:, c * cw:(c + 1) * cw] = _dot(u_ref[...], w_ref[:, lo:lo + cw])
    zg_ref[...] = _dot(u_ref[...], w_ref[:, n_main + n_pool:])


def _inproj(h, g, mod3, w_cat, conv_w, mod_idx, *, tm, n_main, n_pool, n_qk, k_scale, n_lat, lat_len, ctx_len):
    r, d = h.shape
    ncols = w_cat.shape[1]
    halo = 8
    assert tm % ctx_len == 0 and lat_len % tm == 0 and ctx_len & (ctx_len - 1) == 0
    kern = functools.partial(_inproj_kernel, n_main=n_main, n_pool=n_pool, n_qk=n_qk, cw=512, k_scale=k_scale,
                             n_lat_tiles=n_lat // tm, lat_len=lat_len, ctx_len=ctx_len)
    return pl.pallas_call(
        kern,
        out_shape=(jax.ShapeDtypeStruct((r, n_main), BF16),
                   jax.ShapeDtypeStruct((r, n_pool), F32),
                   jax.ShapeDtypeStruct((r, GATE_PAD), F32)),
        grid=(r // tm,),
        in_specs=[pl.BlockSpec((tm, d), lambda i: (i, 0)),
                  pl.BlockSpec((halo, d), lambda i: (jnp.maximum(i * (tm // halo) - 1, 0), 0)),
                  pl.BlockSpec((halo, d), lambda i: (jnp.minimum((i + 1) * (tm // halo), r // halo - 1), 0)),
                  pl.BlockSpec((1, d), lambda i: (0, 0)),
                  pl.BlockSpec((1, 1, d), lambda i: (mod_idx(i) * N_MOD + 1, 0, 0)),
                  pl.BlockSpec((1, 1, d), lambda i: (mod_idx(i) * N_MOD + 0, 0, 0)),
                  pl.BlockSpec((d, ncols), lambda i: (0, 0)),
                  pl.BlockSpec((3, n_qk), lambda i: (0, 0))],
        out_specs=(pl.BlockSpec((tm, n_main), lambda i: (i, 0)),
                   pl.BlockSpec((tm, n_pool), lambda i: (i, 0)),
                   pl.BlockSpec((tm, GATE_PAD), lambda i: (i, 0))),
        scratch_shapes=[pltpu.VMEM((tm, d), BF16), pltpu.VMEM((2 * halo, d), BF16)],
        compiler_params=_cparams(("arbitrary",)),
        name="inproj",
    )(h, h, h, g, mod3, mod3, w_cat, conv_w)


def _running_max_rows(x, rev):
    n = x.shape[0]
    row = lax.broadcasted_iota(jnp.int32, x.shape, 0)
    sh = 1
    while sh < n:
        if rev:
            shifted = jnp.where(row < n - sh, pltpu.roll(x, n - sh, 0), -jnp.inf)
        else:
            shifted = jnp.where(row >= sh, pltpu.roll(x, sh, 0), -jnp.inf)
        x = jnp.maximum(x, shifted)
        sh *= 2
    return x


def _mlstm_kernel(qf, kf, vf, gcf, grf, qb, kb, vb, gcb, grb, bias_c, bias_r, hf_out, hb_out,
                  c_ref, c16_ref, n_ref, m_ref, *, dh):
    L = CHUNK

    @pl.when(pl.program_id(1) == 0)
    def _():
        c_ref[...] = jnp.zeros_like(c_ref)
        c16_ref[...] = jnp.zeros_like(c16_ref)
        n_ref[...] = jnp.zeros_like(n_ref)
        m_ref[...] = jnp.zeros_like(m_ref)

    ri = lax.broadcasted_iota(jnp.int32, (L, L), 0)
    ci = lax.broadcasted_iota(jnp.int32, (L, L), 1)
    lower = ci <= ri
    upper = ci >= ri
    lower_b = jnp.where(lower, 1.0, 0.0).astype(BF16)
    upper_b = jnp.where(upper, 1.0, 0.0).astype(BF16)

    dirs = ((qf, kf, vf, gcf, grf, hf_out), (qb, kb, vb, gcb, grb, hb_out))
    for d, (q_ref, k_ref, v_ref, gc_ref, gr_ref, out_ref) in enumerate(dirs):
        rev = d == 1
        gc = gc_ref[...] + bias_c[...]
        gr = gr_ref[...] + bias_r[...]
        fc = _log_sigmoid(gc)
        fr = _log_sigmoid(gr)
        t_col = upper_b if rev else lower_b
        t_row = lower_b if rev else upper_b
        bcum_c = sum(_dot(t_col, p) for p in _split3(fc))
        bcum_r = sum(_dot(p, t_row) for p in _split3(fr))
        mask = upper if rev else lower
        end = 0 if rev else L - 1

        b_al = pltpu.roll(bcum_c, GATE_PAD - M_HEADS, axis=1)
        g_col = gc - b_al
        m_st = m_ref[d:d + 1, :]
        mx = jnp.maximum(m_st, _running_max_rows(g_col, rev))
        w_inter_all = jnp.exp(m_st - mx)
        e_neg_all = jnp.exp(-(b_al + mx))
        mx_end = mx[end:end + 1, :]
        wk_all = jnp.exp(g_col - mx_end)
        wc_all = jnp.exp(m_st - mx_end)
        m_ref[d:d + 1, :] = b_al[end:end + 1, :] + mx_end

        for h in range(M_HEADS):
            s_idx = d * M_HEADS + h
            col = d * 2 * M_HEADS + h
            g_row = gr[col:col + 1, :] - bcum_r[col + M_HEADS:col + M_HEADS + 1, :]
            p = jnp.exp(jnp.where(mask, g_row - mx[:, col:col + 1], -jnp.inf))
            w_inter = w_inter_all[:, col:col + 1]
            wc = wc_all[:, col:col + 1]
            n_st = n_ref[s_idx:s_idx + 1, :]

            sl = slice(h * dh, (h + 1) * dh)
            q = q_ref[:, sl]
            k = k_ref[:, sl]
            v = v_ref[:, sl]
            scores = _dot_nt(q, k) * p
            num = w_inter * _dot(q, c16_ref[s_idx]) + _dot(scores.astype(BF16), v)
            den = (w_inter * jnp.sum(q.astype(F32) * n_st, axis=1, keepdims=True)
                   + jnp.sum(scores, axis=1, keepdims=True))
            inv = 1.0 / jnp.maximum(jnp.abs(den), e_neg_all[:, col:col + 1])
            out_ref[:, sl] = (num * inv).astype(BF16)

            kw = (k.astype(F32) * wk_all[:, col:col + 1]).astype(BF16)
            c_new = wc * c_ref[s_idx] + _dot_tn(kw, v)
            c_ref[s_idx] = c_new
            c16_ref[s_idx] = c_new.astype(BF16)
            wk_row = jnp.exp(g_row - mx_end[:, col:col + 1])
            n_ref[s_idx:s_idx + 1, :] = wc * n_st + _dot(jnp.broadcast_to(wk_row, (8, L)).astype(BF16), k)[0:1, :]


def _mlstm(zm, zg, zg_t, bias_c, bias_r, *, b, s, width):
    r = zm.shape[0]
    L = CHUNK
    n_lat = s // L
    lat_blocks = b * n_lat

    def blk_f(bi, i):
        return jnp.where(i == 0, lat_blocks + bi, bi * n_lat + i - 1)

    def blk_b(bi, i):
        return jnp.where(i == 0, lat_blocks + bi, bi * n_lat + n_lat - i)

    def specs(blk):
        return [pl.BlockSpec((L, width), lambda bi, i: (blk(bi, i), 0)),
                pl.BlockSpec((L, width), lambda bi, i: (blk(bi, i), 1)),
                pl.BlockSpec((L, width), lambda bi, i: (blk(bi, i), 2)),
                pl.BlockSpec((L, GATE_PAD), lambda bi, i: (blk(bi, i), 0)),
                pl.BlockSpec((16, L), lambda bi, i: (0, blk(bi, i)))]

    kern = functools.partial(_mlstm_kernel, dh=width // M_HEADS)
    out = jax.ShapeDtypeStruct((r, width), BF16)
    return pl.pallas_call(
        kern,
        out_shape=(out, out),
        grid=(b, n_lat + 1),
        in_specs=specs(blk_f) + specs(blk_b) + [pl.BlockSpec((1, GATE_PAD), lambda bi, i: (0, 0)),
                                                pl.BlockSpec((16, L), lambda bi, i: (0, 0))],
        out_specs=(pl.BlockSpec((L, width), lambda bi, i: (blk_f(bi, i), 0)),
                   pl.BlockSpec((L, width), lambda bi, i: (blk_b(bi, i), 0))),
        scratch_shapes=[pltpu.VMEM((2 * M_HEADS, width // M_HEADS, width // M_HEADS), F32),
                        pltpu.VMEM((2 * M_HEADS, width // M_HEADS, width // M_HEADS), BF16),
                        pltpu.VMEM((2 * M_HEADS, width // M_HEADS), F32),
                        pltpu.VMEM((8, GATE_PAD), F32)],
        compiler_params=_cparams(("arbitrary", "arbitrary")),
        name="mlstm",
    )(zm, zm, zm, zg, zg_t, zm, zm, zm, zg, zg_t, bias_c, bias_r)


POOL_BLOCK = 256


def _band(n, w, seg):
    r = lax.broadcasted_iota(jnp.int32, (n, n), 0)
    c = lax.broadcasted_iota(jnp.int32, (n, n), 1)
    d = c - r
    shift = seg.bit_length() - 1
    ok = (d >= -(w // 2)) & (d <= w // 2 - 1) & ((r >> shift) == (c >> shift))
    return jnp.where(ok, 1.0, 0.0).astype(BF16)


def _window_count(idx, n, w):
    return (jnp.minimum(idx + w // 2, n) - jnp.maximum(idx - w // 2, 0)).astype(F32)


def _pool_kernel(z_ref, o_ref, pad_ref, *, grid_rows):
    n = z_ref.shape[0]
    blk = min(POOL_BLOCK, n)
    seg = blk if grid_rows is None else GRID_W
    seg_len = n if grid_rows is None else GRID_W
    halo = (max(POOL_WINDOWS) // 2) * GRID_W
    rowi = lax.broadcasted_iota(jnp.int32, (blk, POOL_GROUP_DIM), 0)
    if grid_rows is not None:
        zeros = jnp.zeros((halo, POOL_GROUP_DIM), F32)
        pad_ref[0:halo, :] = zeros
        pad_ref[halo + n:, :] = zeros
    for g, w in enumerate(POOL_WINDOWS):
        sl = slice(g * POOL_GROUP_DIM, (g + 1) * POOL_GROUP_DIM)
        band = _band(blk, w, seg)
        inv_w = 1.0 / _window_count(rowi & (seg - 1), seg_len, w)
        for b in range(n // blk):
            rows = slice(b * blk, (b + 1) * blk)
            x = z_ref[rows, sl]
            hi = x.astype(BF16)
            lo = (x - hi.astype(F32)).astype(BF16)
            y2 = _dot(band, jnp.concatenate([hi, lo], axis=1))
            y = (y2[:, :POOL_GROUP_DIM] + y2[:, POOL_GROUP_DIM:]) * inv_w
            if grid_rows is None:
                o_ref[rows, sl] = (y - x).astype(BF16)
            else:
                pad_ref[halo + b * blk:halo + (b + 1) * blk, :] = y
        if grid_rows is not None:
            shift = GRID_W.bit_length() - 1
            for b in range(n // blk):
                acc = None
                for d in range(-(w // 2), w // 2):
                    lo_r = halo + b * blk + d * GRID_W
                    t = pad_ref[lo_r:lo_r + blk, :]
                    acc = t if acc is None else acc + t
                inv_h = 1.0 / _window_count((rowi + b * blk) >> shift, grid_rows, w)
                rows = slice(b * blk, (b + 1) * blk)
                o_ref[rows, sl] = (acc * inv_h - z_ref[rows, sl]).astype(BF16)


def _pool(zp, *, n_seq, seq_len, row_block0, grid_rows):
    width = zp.shape[1]
    assert seq_len % POOL_BLOCK == 0 if grid_rows is not None else seq_len <= POOL_BLOCK
    kern = functools.partial(_pool_kernel, grid_rows=grid_rows)
    halo = (max(POOL_WINDOWS) // 2) * GRID_W
    return pl.pallas_call(
        kern,
        out_shape=jax.ShapeDtypeStruct((n_seq * seq_len, width), BF16),
        grid=(n_seq,),
        in_specs=[pl.BlockSpec((seq_len, width), lambda s: (row_block0 + s, 0))],
        out_specs=pl.BlockSpec((seq_len, width), lambda s: (s, 0)),
        scratch_shapes=[pltpu.VMEM((seq_len + 2 * halo, POOL_GROUP_DIM), F32)],
        compiler_params=_cparams(("arbitrary",)),
        name="pool",
    )(zp)


def _mixout_kernel(hf_ref, hb_ref, o_ref, ga_ref, gb_ref, pml_ref, pmc_ref, x_ref, gt_ref, hg_ref, wpool_ref, ps_ref,
                   wpa_ref, wpb_ref, wout_ref, out_ref, *, dh, n_lat_tiles):
    hm = hf_ref[...].astype(F32) + hb_ref[...].astype(F32)
    parts = []
    for h in range(M_HEADS):
        t = hm[:, h * dh:(h + 1) * dh]
        mu = jnp.mean(t, axis=-1, keepdims=True)
        tc = t - mu
        var = jnp.mean(tc * tc, axis=-1, keepdims=True)
        parts.append(tc * lax.rsqrt(var + EPS))
    hn = jnp.concatenate(parts, axis=-1)
    hn = _sigmoid(o_ref[...].astype(F32)) * (hn * hg_ref[...])
    a = _dot(hn.astype(BF16), wpa_ref[...])
    pm_in = jnp.where(pl.program_id(0) >= n_lat_tiles, pmc_ref[...], pml_ref[...])
    pparts = []
    for g in range(len(POOL_WINDOWS)):
        sl = slice(g * POOL_GROUP_DIM, (g + 1) * POOL_GROUP_DIM)
        pparts.append(_dot(pm_in[:, sl], wpool_ref[g]))
    pm = jnp.concatenate(pparts, axis=-1) * ps_ref[...]
    bmat = _dot(pm.astype(BF16), wpb_ref[...])
    merged = _sigmoid(ga_ref[...].astype(F32)) * a + _sigmoid(gb_ref[...].astype(F32)) * bmat
    y = _dot(merged.astype(BF16), wout_ref[...])
    out_ref[...] = x_ref[...] + gt_ref[0] * y


def _mixout(hf, hb, zm, pm_lat, pm_ctx, h, mod3, head_gain, w_pool, pool_scale, w_pa, w_pb, w_out, mod_idx, *, tm,
            n_rows):
    d = h.shape[1]
    width = hf.shape[1]
    pw = pm_lat.shape[1]
    n_lat_tiles = pm_lat.shape[0] // tm
    o_blk = 3
    full = lambda shape: pl.BlockSpec(shape, lambda i: (0,) * len(shape))
    kern = functools.partial(_mixout_kernel, dh=width // M_HEADS, n_lat_tiles=n_lat_tiles)
    return pl.pallas_call(
        kern,
        out_shape=jax.ShapeDtypeStruct((n_rows, d), F32),
        grid=(n_rows // tm,),
        in_specs=[pl.BlockSpec((tm, width), lambda i: (i, 0)),
                  pl.BlockSpec((tm, width), lambda i: (i, 0)),
                  pl.BlockSpec((tm, width), lambda i: (i, o_blk)),
                  pl.BlockSpec((tm, d), lambda i: (i, o_blk + 1)),
                  pl.BlockSpec((tm, d), lambda i: (i, o_blk + 2)),
                  pl.BlockSpec((tm, pw), lambda i: (jnp.minimum(i, n_lat_tiles - 1), 0)),
                  pl.BlockSpec((tm, pw), lambda i: (jnp.maximum(i - n_lat_tiles, 0), 0)),
                  pl.BlockSpec((tm, d), lambda i: (i, 0)),
                  pl.BlockSpec((1, 1, d), lambda i: (mod_idx(i) * N_MOD + 2, 0, 0)),
                  full((1, width)), full(w_pool.shape), full((1, pw)),
                  full(w_pa.shape), full(w_pb.shape), full(w_out.shape)],
        out_specs=pl.BlockSpec((tm, d), lambda i: (i, 0)),
        compiler_params=_cparams(("arbitrary",)),
        name="mixout",
    )(hf, hb, zm, zm, zm, pm_lat, pm_ctx, h, mod3, head_gain, w_pool, pool_scale, w_pa, w_pb, w_out)


SWIGLU_SUB = 512


def _swiglu_partial(u_ref, wg, wu, wd):
    tf = wg.shape[1]
    acc = None
    for c0 in range(0, tf, SWIGLU_SUB):
        cs = slice(c0, min(c0 + SWIGLU_SUB, tf))
        u = u_ref[...]
        act = (_silu(_dot(u, wg[:, cs])) * _dot(u, wu[:, cs])).astype(BF16)
        part = _dot(act, wd[cs, :])
        acc = part if acc is None else acc + part
    return acc


def _ffn_kernel(x_ref, g_ref, sc_ref, sh_ref, gt_ref, wg_ref, wu_ref, wd_ref, out_ref, u_ref, acc_ref):
    j = pl.program_id(1)

    @pl.when(j == 0)
    def _():
        u_ref[...] = _rms_mod(x_ref[...], g_ref[...], sc_ref[0], sh_ref[0]).astype(BF16)
        acc_ref[...] = jnp.zeros_like(acc_ref)

    acc_ref[...] += _swiglu_partial(u_ref, wg_ref, wu_ref, wd_ref)

    @pl.when(j == pl.num_programs(1) - 1)
    def _():
        out_ref[...] = x_ref[...] + gt_ref[0] * acc_ref[...]


def _ffn(h, g, mod3, wg, wu, wd, mod_idx, *, tm, tf):
    r, d = h.shape
    f = wg.shape[1]
    return pl.pallas_call(
        _ffn_kernel,
        out_shape=jax.ShapeDtypeStruct((r, d), F32),
        grid=(r // tm, f // tf),
        in_specs=[pl.BlockSpec((tm, d), lambda i, j: (i, 0)),
                  pl.BlockSpec((1, d), lambda i, j: (0, 0)),
                  pl.BlockSpec((1, 1, d), lambda i, j: (mod_idx(i) * N_MOD + 4, 0, 0)),
                  pl.BlockSpec((1, 1, d), lambda i, j: (mod_idx(i) * N_MOD + 3, 0, 0)),
                  pl.BlockSpec((1, 1, d), lambda i, j: (mod_idx(i) * N_MOD + 5, 0, 0)),
                  pl.BlockSpec((d, tf), lambda i, j: (0, j)),
                  pl.BlockSpec((d, tf), lambda i, j: (0, j)),
                  pl.BlockSpec((tf, d), lambda i, j: (j, 0))],
        out_specs=pl.BlockSpec((tm, d), lambda i, j: (i, 0)),
        scratch_shapes=[pltpu.VMEM((tm, d), BF16), pltpu.VMEM((tm, d), F32)],
        compiler_params=_cparams(("arbitrary", "arbitrary")),
        name="ffn",
    )(h, g, mod3, mod3, mod3, wg, wu, wd)


def _route(u, wr_ref, n_exp):
    parts_u = _split3(u)
    parts_w = _split3(wr_ref[...])
    logits = (_dot(parts_u[0], parts_w[0]) + _dot(parts_u[0], parts_w[1]) + _dot(parts_u[1], parts_w[0])
              + _dot(parts_u[1], parts_w[1]) + _dot(parts_u[0], parts_w[2]) + _dot(parts_u[2], parts_w[0]))
    lane = lax.broadcasted_iota(jnp.int32, logits.shape, 1)
    lg = jnp.where(lane < n_exp, logits, -jnp.inf)
    m1 = jnp.max(lg, axis=1, keepdims=True)
    i1 = jnp.min(jnp.where(lg == m1, lane, 2 * GATE_PAD), axis=1, keepdims=True)
    lg2 = jnp.where(lane == i1, -jnp.inf, lg)
    m2 = jnp.max(lg2, axis=1, keepdims=True)
    i2 = jnp.min(jnp.where(lg2 == m2, lane, 2 * GATE_PAD), axis=1, keepdims=True)
    e = jnp.exp(m2 - m1)
    p1 = 1.0 / (1.0 + e)
    p2 = e * p1
    return i1, i2, p1, p2


LANE_I1, LANE_I2, LANE_P1, LANE_P2, LANE_R1, LANE_R2 = 0, 1, 2, 3, 4, 5


def _pack_bf16_pairs(x):
    k = x.shape[1] // 2
    lo = lax.bitcast_convert_type(x[:, :k].astype(BF16).astype(F32), jnp.uint32)
    hi = lax.bitcast_convert_type(x[:, k:].astype(BF16).astype(F32), jnp.uint32)
    return (lo >> 16) | (hi & jnp.uint32(0xFFFF0000))


def _unpack_bf16_pairs(w):
    lo = lax.bitcast_convert_type(w << 16, F32)
    hi = lax.bitcast_convert_type(w & jnp.uint32(0xFFFF0000), F32)
    return jnp.concatenate([lo, hi], axis=1)


def _route_kernel(x_ref, g_ref, sc_ref, sh_ref, wr_ref, up_ref, info_ref, cnt_ref, carry_ref, *, n_exp):
    @pl.when(pl.program_id(0) == 0)
    def _():
        carry_ref[...] = jnp.zeros_like(carry_ref)

    u = _rms_mod(x_ref[...], g_ref[...], sc_ref[0], sh_ref[0])
    up_ref[...] = _pack_bf16_pairs(u)
    i1, i2, p1, p2 = _route(u, wr_ref, n_exp)
    tm = u.shape[0]
    lane = lax.broadcasted_iota(jnp.int32, (tm, GATE_PAD), 1)
    sel = jnp.where((lane == i1) | (lane == i2), 1.0, 0.0)
    ri = lax.broadcasted_iota(jnp.int32, (tm, tm), 0)
    ci = lax.broadcasted_iota(jnp.int32, (tm, tm), 1)
    before = jnp.where(ci < ri, 1.0, 0.0).astype(BF16)
    rank = carry_ref[0:1, :] + _dot(before, sel.astype(BF16))
    r1 = jnp.sum(jnp.where(lane == i1, rank, 0.0), axis=1, keepdims=True)
    r2 = jnp.sum(jnp.where(lane == i2, rank, 0.0), axis=1, keepdims=True)
    carry_ref[0:1, :] = carry_ref[0:1, :] + jnp.sum(sel, axis=0, keepdims=True)
    cnt_ref[...] = carry_ref[...]
    info = jnp.zeros((tm, GATE_PAD), F32)
    for ln, val in ((LANE_I1, i1.astype(F32)), (LANE_I2, i2.astype(F32)), (LANE_P1, p1), (LANE_P2, p2),
                    (LANE_R1, r1), (LANE_R2, r2)):
        info = jnp.where(lane == ln, val, info)
    info_ref[...] = info


def _route_call(h, g, mod3, w_router, mod_idx, *, tm, n_rows):
    d = h.shape[1]
    n_exp = w_router.shape[1]
    w_r = jnp.concatenate([w_router, jnp.zeros((d, GATE_PAD - n_exp), F32)], axis=1)
    kern = functools.partial(_route_kernel, n_exp=n_exp)
    return pl.pallas_call(
        kern,
        out_shape=(jax.ShapeDtypeStruct((n_rows, d // 2), jnp.uint32),
                   jax.ShapeDtypeStruct((n_rows, GATE_PAD), F32),
                   jax.ShapeDtypeStruct((8, GATE_PAD), F32)),
        grid=(n_rows // tm,),
        in_specs=[pl.BlockSpec((tm, d), lambda i: (i, 0)),
                  pl.BlockSpec((1, d), lambda i: (0, 0)),
                  pl.BlockSpec((1, 1, d), lambda i: (mod_idx(i) * N_MOD + 4, 0, 0)),
                  pl.BlockSpec((1, 1, d), lambda i: (mod_idx(i) * N_MOD + 3, 0, 0)),
                  pl.BlockSpec((d, GATE_PAD), lambda i: (0, 0))],
        out_specs=(pl.BlockSpec((tm, d // 2), lambda i: (i, 0)),
                   pl.BlockSpec((tm, GATE_PAD), lambda i: (i, 0)),
                   pl.BlockSpec((8, GATE_PAD), lambda i: (0, 0))),
        scratch_shapes=[pltpu.VMEM((8, GATE_PAD), F32)],
        compiler_params=_cparams(("arbitrary",)),
        name="route",
    )(h, g, mod3, mod3, w_r)


def _sc_workers():
    info = pltpu.get_tpu_info().sparse_core
    return info.num_cores, info.num_subcores


def _sc_scatter_rows(x, pos, n_out):
    n, dw = x.shape
    nc, ns = _sc_workers()
    nw = nc * ns
    t_per_w = n // nw
    w = min(SC_WINDOW, t_per_w // 2)
    n_chunks = t_per_w // w
    assert n % nw == 0 and t_per_w % (2 * w) == 0
    pos_w = pos.reshape(2, nw, n_chunks, w).transpose(1, 0, 2, 3)
    mesh = plsc.VectorSubcoreMesh(core_axis_name="c", subcore_axis_name="s")

    @functools.partial(pl.kernel, mesh=mesh, out_type=jax.ShapeDtypeStruct((n_out, dw), x.dtype),
                       scratch_types=[pltpu.VMEM((2, n_chunks, w), jnp.int32), pltpu.VMEM((2, w, dw), x.dtype),
                                      pltpu.SemaphoreType.DMA((2,)), pltpu.SemaphoreType.DMA((2,))])
    def scatter(x_hbm, pos_hbm, out_hbm, idx_v, rows_v, gsem, osem):
        wid = lax.axis_index("s") * nc + lax.axis_index("c")
        base = wid * t_per_w
        pltpu.sync_copy(pos_hbm.at[wid], idx_v)

        def get(g, slot):
            return pltpu.make_async_copy(x_hbm.at[pl.ds(base + g * w, w)], rows_v.at[slot], gsem.at[slot])

        def put(g, slot, k):
            return pltpu.make_async_copy(rows_v.at[slot], out_hbm.at[idx_v.at[k, g]], osem.at[slot])

        get(0, 0).start()

        @pl.loop(0, n_chunks, step=2)
        def _(g):
            for b in range(2):
                gg = g + b
                get(gg, b).wait()

                @pl.when(gg + 1 < n_chunks)
                def _():
                    @pl.when(gg >= 1)
                    def _():
                        put(gg - 1, 1 - b, 0).wait()
                        put(gg - 1, 1 - b, 1).wait()
                    get(gg + 1, 1 - b).start()

                put(gg, b, 0).start()
                put(gg, b, 1).start()

        for k in range(2):
            put(n_chunks - 2, 0, k).wait()
            put(n_chunks - 1, 1, k).wait()

    return scatter(x, pos_w)


def _sc_gather_rows(table, idx):
    n = idx.shape[0]
    dw = table.shape[1]
    nc, ns = _sc_workers()
    nw = nc * ns
    b_per_w = n // nw
    w = min(SC_WINDOW, b_per_w // 2)
    n_chunks = b_per_w // w
    assert n % nw == 0 and b_per_w % (2 * w) == 0
    mesh = plsc.VectorSubcoreMesh(core_axis_name="c", subcore_axis_name="s")

    @functools.partial(pl.kernel, mesh=mesh, out_type=jax.ShapeDtypeStruct((n, dw), table.dtype),
                       scratch_types=[pltpu.VMEM((b_per_w,), jnp.int32), pltpu.VMEM((2, w, dw), table.dtype),
                                      pltpu.SemaphoreType.DMA((2,)), pltpu.SemaphoreType.DMA((2,))])
    def gather(table_hbm, idx_hbm, out_hbm, idx_v, rows_v, gsem, osem):
        wid = lax.axis_index("s") * nc + lax.axis_index("c")
        base = wid * b_per_w
        pltpu.sync_copy(idx_hbm.at[pl.ds(base, b_per_w)], idx_v)

        def get(g, slot):
            return pltpu.make_async_copy(table_hbm.at[idx_v.at[pl.ds(g * w, w)]], rows_v.at[slot], gsem.at[slot])

        def put(g, slot):
            return pltpu.make_async_copy(rows_v.at[slot], out_hbm.at[pl.ds(base + g * w, w)], osem.at[slot])

        get(0, 0).start()

        @pl.loop(0, n_chunks, step=2)
        def _(g):
            for b in range(2):
                gg = g + b
                get(gg, b).wait()

                @pl.when(gg + 1 < n_chunks)
                def _():
                    @pl.when(gg >= 1)
                    def _():
                        put(gg - 1, 1 - b).wait()
                    get(gg + 1, 1 - b).start()

                put(gg, b).start()

        put(n_chunks - 2, 0).wait()
        put(n_chunks - 1, 1).wait()

    return gather(table, idx)


def _gmm_kernel(te_ref, nv_ref, xs_ref, wg_ref, wu_ref, wd_ref, ys_ref, u_ref, acc_ref):
    i = pl.program_id(0)
    j = pl.program_id(1)

    @pl.when(i < nv_ref[0])
    def _():
        @pl.when(j == 0)
        def _():
            u_ref[...] = _unpack_bf16_pairs(xs_ref[...]).astype(BF16)
            acc_ref[...] = jnp.zeros_like(acc_ref)

        acc_ref[...] += _swiglu_partial(u_ref, wg_ref.at[0], wu_ref.at[0], wd_ref.at[0])

        @pl.when(j == pl.num_programs(1) - 1)
        def _():
            ys_ref[...] = _pack_bf16_pairs(acc_ref[...])


def _gmm(xs, tile_expert, n_valid, wg, wu, wd, *, tg, tf):
    p, dw = xs.shape
    d = 2 * dw
    f = wg.shape[2]
    n_f = f // tf

    def jj(i, j, nv):
        return jnp.where(i < nv[0], j, n_f - 1)

    grid_spec = pltpu.PrefetchScalarGridSpec(
        num_scalar_prefetch=2,
        grid=(p // tg, n_f),
        in_specs=[pl.BlockSpec((tg, dw), lambda i, j, te, nv: (i, 0)),
                  pl.BlockSpec((1, d, tf), lambda i, j, te, nv: (te[i], 0, jj(i, j, nv))),
                  pl.BlockSpec((1, d, tf), lambda i, j, te, nv: (te[i], 0, jj(i, j, nv))),
                  pl.BlockSpec((1, tf, d), lambda i, j, te, nv: (te[i], jj(i, j, nv), 0))],
        out_specs=pl.BlockSpec((tg, dw), lambda i, j, te, nv: (i, 0)),
        scratch_shapes=[pltpu.VMEM((tg, d), BF16), pltpu.VMEM((tg, d), F32)])
    return pl.pallas_call(
        _gmm_kernel,
        out_shape=jax.ShapeDtypeStruct((p, dw), jnp.uint32),
        grid_spec=grid_spec,
        compiler_params=_cparams(("arbitrary", "arbitrary")),
        name="gmm",
    )(tile_expert, n_valid, xs, wg, wu, wd)


def _combine_kernel(x_ref, info_ref, y1_ref, y2_ref, gt_ref, gf_ref, out_ref, *, final):
    info = info_ref[...]
    p1 = info[:, LANE_P1:LANE_P1 + 1]
    p2 = info[:, LANE_P2:LANE_P2 + 1]
    moe = p1 * _unpack_bf16_pairs(y1_ref[...]) + p2 * _unpack_bf16_pairs(y2_ref[...])
    y = x_ref[...] + gt_ref[0] * moe
    if final:
        r = lax.rsqrt(jnp.mean(y * y, axis=-1, keepdims=True) + EPS)
        y = (y * r) * gf_ref[...]
    out_ref[...] = y


def _combine(h, info, yg, mod3, g_final, mod_idx, *, tm, n_rows, final):
    d = h.shape[1]
    n_t = n_rows // tm
    kern = functools.partial(_combine_kernel, final=final)
    return pl.pallas_call(
        kern,
        out_shape=jax.ShapeDtypeStruct((n_rows, d), F32),
        grid=(n_t,),
        in_specs=[pl.BlockSpec((tm, d), lambda i: (i, 0)),
                  pl.BlockSpec((tm, GATE_PAD), lambda i: (i, 0)),
                  pl.BlockSpec((tm, d // 2), lambda i: (i, 0)),
                  pl.BlockSpec((tm, d // 2), lambda i: (n_t + i, 0)),
                  pl.BlockSpec((1, 1, d), lambda i: (mod_idx(i) * N_MOD + 5, 0, 0)),
                  pl.BlockSpec((1, d), lambda i: (0, 0))],
        out_specs=pl.BlockSpec((tm, d), lambda i: (i, 0)),
        compiler_params=_cparams(("arbitrary",)),
        name="combine",
    )(h, info, yg, yg, mod3, g_final)


def _moe(h, g, mod3, w_router, wg, wu, wd, g_final, mod_idx, *, tm, tg, tf, n_rows, final):
    n_exp = wg.shape[0]
    up, info, cnt = _route_call(h, g, mod3, w_router, mod_idx, tm=tm, n_rows=n_rows)
    i1 = info[:, LANE_I1].astype(jnp.int32)
    i2 = info[:, LANE_I2].astype(jnp.int32)
    r1 = info[:, LANE_R1].astype(jnp.int32)
    r2 = info[:, LANE_R2].astype(jnp.int32)
    counts = cnt[0, :n_exp].astype(jnp.int32)
    padded = (counts + tg - 1) // tg * tg
    ends = jnp.cumsum(padded)
    starts = ends - padded
    eye = jnp.arange(n_exp, dtype=jnp.int32)
    pos1 = jnp.sum(jnp.where(i1[:, None] == eye, starts, 0), axis=1) + r1
    pos2 = jnp.sum(jnp.where(i2[:, None] == eye, starts, 0), axis=1) + r2
    p_rows = -(-(TOP_K * n_rows + n_exp * (tg - 1)) // tg) * tg
    tile_row = jnp.arange(p_rows // tg, dtype=jnp.int32) * tg
    tile_expert = jnp.minimum(jnp.sum(tile_row[:, None] >= ends[None, :], axis=1), n_exp - 1).astype(jnp.int32)
    n_valid = (ends[-1] // tg).reshape(1).astype(jnp.int32)
    xs = _sc_scatter_rows(up, jnp.stack([pos1, pos2]), p_rows)
    ys = _gmm(xs, tile_expert, n_valid, wg, wu, wd, tg=tg, tf=tf)
    yg = _sc_gather_rows(ys, jnp.concatenate([pos1, pos2]))
    return _combine(h, info, yg, mod3, g_final, mod_idx, tm=tm, n_rows=n_rows, final=final)


def _final_norm_kernel(x_ref, g_ref, o_ref):
    x = x_ref[...]
    r = lax.rsqrt(jnp.mean(x * x, axis=-1, keepdims=True) + EPS)
    o_ref[...] = (x * r) * g_ref[...]


def _final_norm(h, g, *, tm, n_rows):
    d = h.shape[1]
    return pl.pallas_call(
        _final_norm_kernel,
        out_shape=jax.ShapeDtypeStruct((n_rows, d), F32),
        grid=(n_rows // tm,),
        in_specs=[pl.BlockSpec((tm, d), lambda i: (i, 0)), pl.BlockSpec((1, d), lambda i: (0, 0))],
        out_specs=pl.BlockSpec((tm, d), lambda i: (i, 0)),
        compiler_params=_cparams(("arbitrary",)),
        name="final_norm",
    )(h, g)


def kernel(x, c, ctx, c_ctx, w_ada, b_ada, g_mix, w_in, conv_qk, b_if, head_gain, w_pool, pool_scale, w_pa, w_pb,
           w_out, g_ffn, w_ff_gate, w_ff_up, w_ff_down, w_router, w_exp_gate, w_exp_up, w_exp_down, g_final):
    b, s, d = x.shape
    lc = ctx.shape[1]
    depth = w_ada.shape[0]
    width = w_pa.shape[1]
    pw = w_pb.shape[1]
    n_gate = b_if.shape[1] * b_if.shape[2] * b_if.shape[3]
    assert lc == CHUNK and s % CHUNK == 0 and s % GRID_W == 0 and width == d and n_gate == 16
    n_lat, n_ctx = b * s, b * lc
    tm, tm_ffn = 512, 1024
    assert s % tm_ffn == 0 and n_ctx % tm_ffn == 0 and tm_ffn % tm == 0

    def mod_idx(t):
        return lambda i: jnp.where(i < n_lat // t, i // (s // t), b)

    h = jnp.concatenate([x.reshape(n_lat, d), ctx.reshape(n_ctx, d)], axis=0)

    n_mod_rows = 16
    cc = jnp.zeros((n_mod_rows, d), F32).at[:b].set(c).at[b].set(c_ctx)
    mod = _ada(cc, w_ada, b_ada)

    if_off, pool_off, ga_off = 4 * width, 4 * width + n_gate, 4 * width + n_gate + pw
    n_main = 4 * width + 2 * d

    for l in range(depth):
        last = l == depth - 1
        mod3 = mod[l].reshape(n_mod_rows * N_MOD, 1, d)
        w_l = w_in[l]
        w_cat = jnp.concatenate(
            [w_l[:, :if_off], w_l[:, ga_off:], w_l[:, pool_off:ga_off], w_l[:, if_off:pool_off],
             jnp.zeros((d, GATE_PAD - n_gate), F32)], axis=1).astype(BF16)
        k_scale = float((width // M_HEADS) ** -0.5)
        zm, zp, zg = _inproj(h, g_mix[l].reshape(1, d), mod3, w_cat, conv_qk[l], mod_idx(tm), tm=tm, n_main=n_main,
                             n_pool=pw, n_qk=2 * width, k_scale=k_scale, n_lat=n_lat, lat_len=s, ctx_len=lc)

        bias = b_if[l].reshape(n_gate)
        bias_c = jnp.zeros((1, GATE_PAD), F32).at[0, :n_gate].set(bias)
        bias_r = jnp.broadcast_to(bias[:, None], (n_gate, CHUNK))
        hf, hb = _mlstm(zm, zg, zg[:, :n_gate].T, bias_c, bias_r, b=b, s=s, width=width)

        n_rows = n_lat if last else n_lat + n_ctx
        pm_lat = _pool(zp, n_seq=b, seq_len=s, row_block0=0, grid_rows=s // GRID_W)
        pm_ctx = pm_lat if last else _pool(zp, n_seq=b, seq_len=lc, row_block0=n_lat // lc, grid_rows=None)

        h = _mixout(hf, hb, zm, pm_lat, pm_ctx, h, mod3, head_gain[l].reshape(1, width), w_pool[l].astype(BF16),
                    pool_scale[l].reshape(1, pw), w_pa[l].astype(BF16), w_pb[l].astype(BF16),
                    w_out[l].astype(BF16), mod_idx(tm), tm=tm, n_rows=n_rows)

        j = l // 2
        if l % 2 == 0:
            h = _ffn(h, g_ffn[l].reshape(1, d), mod3, w_ff_gate[j].astype(BF16), w_ff_up[j].astype(BF16),
                     w_ff_down[j].astype(BF16), mod_idx(tm_ffn), tm=tm_ffn,
                     tf=_pick_tile(w_ff_gate.shape[2], 1408))
            if last:
                h = _final_norm(h, g_final.reshape(1, d), tm=tm, n_rows=n_lat)
        else:
            h = _moe(h, g_ffn[l].reshape(1, d), mod3, w_router[j], w_exp_gate[j].astype(BF16),
                     w_exp_up[j].astype(BF16), w_exp_down[j].astype(BF16), g_final.reshape(1, d), mod_idx(tm),
                     tm=tm, tg=512, tf=_pick_tile(w_exp_gate.shape[3], 1792), n_rows=h.shape[0], final=last)
    return h[:n_lat].reshape(b, s, d)
```

```python
import functools

import jax
import jax.numpy as jnp
from jax import lax
from jax.experimental import pallas as pl
from jax.experimental.pallas import tpu as pltpu
from jax.experimental.pallas import tpu_sc as plsc

F32 = jnp.float32
BF16 = jnp.bfloat16
EPS = 1e-6
M_HEADS = 4
GRID_W = 64
POOL_WINDOWS = (2, 4, 8, 16)
POOL_GROUP_DIM = 128
CHUNK = 256
N_MOD = 6
TOP_K = 2
SC_WINDOW = 64
GATE_PAD = 128
VMEM_LIMIT = 56 * 1024 * 1024


def _pick_tile(n, target, quantum=128):
    t = min(n, target) // quantum * quantum
    while n % t:
        t -= quantum
    return t


def _cparams(sem):
    return pltpu.CompilerParams(dimension_semantics=sem, vmem_limit_bytes=VMEM_LIMIT)


def _sigmoid(x):
    return 1.0 / (1.0 + jnp.exp(-x))


def _silu(x):
    return x * _sigmoid(x)


def _log_sigmoid(x):
    return jnp.minimum(x, 0.0) - jnp.log(1.0 + jnp.exp(-jnp.abs(x)))


def _split3(x):
    hi = x.astype(BF16)
    r1 = x - hi.astype(F32)
    mid = r1.astype(BF16)
    lo = (r1 - mid.astype(F32)).astype(BF16)
    return hi, mid, lo


def _dot(a, b):
    return jnp.dot(a, b, preferred_element_type=F32)


def _dot_nt(a, b):
    return lax.dot_general(a, b, (((1,), (1,)), ((), ())), preferred_element_type=F32)


def _dot_tn(a, b):
    return lax.dot_general(a, b, (((0,), (0,)), ((), ())), preferred_element_type=F32)


def _rms_mod(x, g, sc, sh):
    r = lax.rsqrt(jnp.mean(x * x, axis=-1, keepdims=True) + EPS)
    return (x * r) * g * (1.0 + sc) + sh


def _ada_kernel(c_ref, w_ref, b_ref, o_ref):
    c = c_ref[...]
    o_ref[0] = _dot(_silu(c).astype(BF16), w_ref[0].astype(BF16)) + b_ref[0]


def _ada(cc, w_ada, b_ada):
    depth, d, n = w_ada.shape
    tn = 1024
    return pl.pallas_call(
        _ada_kernel,
        out_shape=jax.ShapeDtypeStruct((depth, cc.shape[0], n), F32),
        grid=(depth, n // tn),
        in_specs=[pl.BlockSpec(cc.shape, lambda l, j: (0, 0)),
                  pl.BlockSpec((1, d, tn), lambda l, j: (l, 0, j)),
                  pl.BlockSpec((1, 1, tn), lambda l, j: (l, 0, j))],
        out_specs=pl.BlockSpec((1, cc.shape[0], tn), lambda l, j: (l, 0, j)),
        compiler_params=_cparams(("arbitrary", "arbitrary")),
        name="ada",
    )(cc, w_ada, b_ada.reshape(depth, 1, n))


def _inproj_kernel(xa_ref, xb_ref, xpa_ref, xpb_ref, xna_ref, xnb_ref, g_ref, sc_ref, sh_ref, w_ref, cw_ref,
                   zm_ref, zp_ref, zg_ref, u_ref, uh_ref,
                   *, n_main, n_pool, n_qk, cw, k_scale, n_lat_tiles, lat_len, ctx_len):
    i = pl.program_id(0)
    tm = xa_ref.shape[0]
    is_ctx = i >= n_lat_tiles
    g, sc, sh = g_ref[...], sc_ref[0], sh_ref[0]
    u_ref[...] = _rms_mod(jnp.where(is_ctx, xb_ref[...], xa_ref[...]), g, sc, sh).astype(BF16)
    uh_ref[0:8, :] = _rms_mod(jnp.where(is_ctx, xpb_ref[...], xpa_ref[...]), g, sc, sh).astype(BF16)
    uh_ref[8:16, :] = _rms_mod(jnp.where(is_ctx, xnb_ref[...], xna_ref[...]), g, sc, sh).astype(BF16)

    row = lax.broadcasted_iota(jnp.int32, (tm, cw), 0)
    pos0 = lax.rem(i * tm, lat_len)
    lat_first_row = jnp.where(pos0 == 0, 0, -1)
    lat_last_row = jnp.where(pos0 + tm == lat_len, tm - 1, -1)
    in_ctx = row & (ctx_len - 1)
    first = jnp.where(is_ctx, in_ctx, row - lat_first_row) == 0
    last = jnp.where(is_ctx, in_ctx - (ctx_len - 1), row - lat_last_row) == 0

    for c in range(n_qk // cw):
        cs = slice(c * cw, (c + 1) * cw)
        z = _dot(u_ref[...], w_ref[:, cs])
        zh = _dot(uh_ref[...], w_ref[:, cs])
        zm1 = jnp.where(row == 0, zh[7:8, :], pltpu.roll(z, 1, 0))
        zm1 = jnp.where(first, 0.0, zm1)
        zp1 = jnp.where(row == tm - 1, zh[8:9, :], pltpu.roll(z, tm - 1, 0))
        zp1 = jnp.where(last, 0.0, zp1)
        y = _silu(zm1 * cw_ref[0:1, cs] + z * cw_ref[1:2, cs] + zp1 * cw_ref[2:3, cs])
        if c * cw >= n_qk // 2:
            y = y * k_scale
        zm_ref[:, cs] = y.astype(BF16)
    for c in range(n_qk // cw, n_main // cw):
        zm_ref[:, c * cw:(c + 1) * cw] = _dot(u_ref[...], w_ref[:, c * cw:(c + 1) * cw]).astype(BF16)
    for c in range(n_pool // cw):
        lo = n_main + c * cw
        zp_ref[:, c * cw:(c + 1) * cw] = _dot(u_ref[...], w_ref[:, lo:lo + cw])
    zg_ref[...] = _dot(u_ref[...], w_ref[:, n_main + n_pool:])


def _inproj(src, g, mod3, w_cat, conv_w, mod_idx, *, tm, n_main, n_pool, n_qk, k_scale, n_lat, lat_len, ctx_len):
    a, b, b_off = src
    d = a.shape[1]
    r = n_lat + b.shape[0] - b_off
    ncols = w_cat.shape[1]
    halo = 8
    assert tm % ctx_len == 0 and lat_len % tm == 0 and ctx_len & (ctx_len - 1) == 0 and b_off % tm == 0
    na_t, hb = n_lat // tm, tm // halo
    a_last, b_last = a.shape[0] // halo - 1, b.shape[0] // halo - 1
    kern = functools.partial(_inproj_kernel, n_main=n_main, n_pool=n_pool, n_qk=n_qk, cw=512, k_scale=k_scale,
                             n_lat_tiles=na_t, lat_len=lat_len, ctx_len=ctx_len)
    return pl.pallas_call(
        kern,
        out_shape=(jax.ShapeDtypeStruct((r, n_main), BF16),
                   jax.ShapeDtypeStruct((r, n_pool), F32),
                   jax.ShapeDtypeStruct((r, GATE_PAD), F32)),
        grid=(r // tm,),
        in_specs=[pl.BlockSpec((tm, d), lambda i: (jnp.minimum(i, na_t - 1), 0)),
                  pl.BlockSpec((tm, d), lambda i: (jnp.maximum(i - na_t, 0) + b_off // tm, 0)),
                  pl.BlockSpec((halo, d), lambda i: (jnp.clip(i * hb - 1, 0, a_last), 0)),
                  pl.BlockSpec((halo, d), lambda i: (jnp.clip(b_off // halo + (i - na_t) * hb - 1, 0, b_last), 0)),
                  pl.BlockSpec((halo, d), lambda i: (jnp.clip((i + 1) * hb, 0, a_last), 0)),
                  pl.BlockSpec((halo, d), lambda i: (jnp.clip(b_off // halo + (i - na_t + 1) * hb, 0, b_last), 0)),
                  pl.BlockSpec((1, d), lambda i: (0, 0)),
                  pl.BlockSpec((1, 1, d), lambda i: (mod_idx(i) * N_MOD + 1, 0, 0)),
                  pl.BlockSpec((1, 1, d), lambda i: (mod_idx(i) * N_MOD + 0, 0, 0)),
                  pl.BlockSpec((d, ncols), lambda i: (0, 0)),
                  pl.BlockSpec((3, n_qk), lambda i: (0, 0))],
        out_specs=(pl.BlockSpec((tm, n_main), lambda i: (i, 0)),
                   pl.BlockSpec((tm, n_pool), lambda i: (i, 0)),
                   pl.BlockSpec((tm, GATE_PAD), lambda i: (i, 0))),
        scratch_shapes=[pltpu.VMEM((tm, d), BF16), pltpu.VMEM((2 * halo, d), BF16)],
        compiler_params=_cparams(("arbitrary",)),
        name="inproj",
    )(a, b, a, b, a, b, g, mod3, mod3, w_cat, conv_w)


def _running_max_rows(x, rev):
    n = x.shape[0]
    row = lax.broadcasted_iota(jnp.int32, x.shape, 0)
    sh = 1
    while sh < n:
        if rev:
            shifted = jnp.where(row < n - sh, pltpu.roll(x, n - sh, 0), -jnp.inf)
        else:
            shifted = jnp.where(row >= sh, pltpu.roll(x, sh, 0), -jnp.inf)
        x = jnp.maximum(x, shifted)
        sh *= 2
    return x


def _mlstm_kernel(qf, kf, vf, gcf, grf, qb, kb, vb, gcb, grb, bias_c, bias_r, hf_out, hb_out,
                  c_ref, c16_ref, n_ref, n16_ref, m_ref, *, dh):
    L = CHUNK

    @pl.when(pl.program_id(1) == 0)
    def _():
        c_ref[...] = jnp.zeros_like(c_ref)
        c16_ref[...] = jnp.zeros_like(c16_ref)
        n_ref[...] = jnp.zeros_like(n_ref)
        n16_ref[...] = jnp.zeros_like(n16_ref)
        m_ref[...] = jnp.zeros_like(m_ref)

    ri = lax.broadcasted_iota(jnp.int32, (L, L), 0)
    ci = lax.broadcasted_iota(jnp.int32, (L, L), 1)
    lower = ci <= ri
    upper = ci >= ri
    lower_b = jnp.where(lower, 1.0, 0.0).astype(BF16)
    upper_b = jnp.where(upper, 1.0, 0.0).astype(BF16)

    dirs = ((qf, kf, vf, gcf, grf, hf_out), (qb, kb, vb, gcb, grb, hb_out))
    for d, (q_ref, k_ref, v_ref, gc_ref, gr_ref, out_ref) in enumerate(dirs):
        rev = d == 1
        gc = gc_ref[...] + bias_c[...]
        gr = gr_ref[...] + bias_r[...]
        fc = _log_sigmoid(gc)
        fr = _log_sigmoid(gr)
        t_col = upper_b if rev else lower_b
        t_row = lower_b if rev else upper_b
        bcum_c = sum(_dot(t_col, p) for p in _split3(fc))
        bcum_r = sum(_dot(p, t_row) for p in _split3(fr))
        mask = upper if rev else lower
        end = 0 if rev else L - 1

        b_al = pltpu.roll(bcum_c, GATE_PAD - M_HEADS, axis=1)
        g_col = gc - b_al
        m_st = m_ref[d:d + 1, :]
        mx = jnp.maximum(m_st, _running_max_rows(g_col, rev))
        e_neg_all = jnp.exp(-(b_al + mx))
        mx_end = mx[end:end + 1, :]
        wk_all = jnp.exp(g_col - mx_end)
        wc_all = jnp.exp(m_st - mx_end)
        m_ref[d:d + 1, :] = b_al[end:end + 1, :] + mx_end

        ones_rep = jnp.ones((L, GATE_PAD), BF16)
        for h in range(M_HEADS):
            s_idx = d * M_HEADS + h
            col = d * 2 * M_HEADS + h

            def rep(x_all):
                return jnp.broadcast_to(x_all[:, col:col + 1], (x_all.shape[0], GATE_PAD))

            def wide(x_rep):
                return jnp.concatenate([x_rep] * (dh // GATE_PAD), axis=1)

            mx_rep = rep(mx)
            g_row = gr[col:col + 1, :] - bcum_r[col + M_HEADS:col + M_HEADS + 1, :]
            p = jnp.exp(jnp.where(mask, g_row - wide(mx_rep), -jnp.inf))
            w_inter = jnp.exp(rep(m_st) - mx_rep)
            e_neg = rep(e_neg_all)
            wk16 = rep(wk_all).astype(BF16)
            wc = wc_all[:, col:col + 1]

            sl = slice(h * dh, (h + 1) * dh)
            q = q_ref[:, sl]
            k = k_ref[:, sl]
            v = v_ref[:, sl]
            scores = (_dot_nt(q, k) * p).astype(BF16)
            num = wide(w_inter) * _dot(q, c16_ref[s_idx]) + _dot(scores, v)
            den = w_inter * _dot(q, n16_ref[s_idx]) + _dot(scores, ones_rep)
            inv = 1.0 / jnp.maximum(jnp.abs(den), e_neg)
            out_ref[:, sl] = (num * wide(inv)).astype(BF16)

            c_new = wc * c_ref[s_idx] + _dot_tn(k * wide(wk16), v)
            c_ref[s_idx] = c_new
            c16_ref[s_idx] = c_new.astype(BF16)
            n_new = wc * n_ref[s_idx] + _dot_tn(k, wk16)
            n_ref[s_idx] = n_new
            n16_ref[s_idx] = n_new.astype(BF16)


def _mlstm(zm, zg, zg_t, bias_c, bias_r, *, b, s, width):
    r = zm.shape[0]
    L = CHUNK
    n_lat = s // L
    lat_blocks = b * n_lat

    def blk_f(bi, i):
        return jnp.where(i == 0, lat_blocks + bi, bi * n_lat + i - 1)

    def blk_b(bi, i):
        return jnp.where(i == 0, lat_blocks + bi, bi * n_lat + n_lat - i)

    def specs(blk):
        return [pl.BlockSpec((L, width), lambda bi, i: (blk(bi, i), 0)),
                pl.BlockSpec((L, width), lambda bi, i: (blk(bi, i), 1)),
                pl.BlockSpec((L, width), lambda bi, i: (blk(bi, i), 2)),
                pl.BlockSpec((L, GATE_PAD), lambda bi, i: (blk(bi, i), 0)),
                pl.BlockSpec((16, L), lambda bi, i: (0, blk(bi, i)))]

    kern = functools.partial(_mlstm_kernel, dh=width // M_HEADS)
    out = jax.ShapeDtypeStruct((r, width), BF16)
    return pl.pallas_call(
        kern,
        out_shape=(out, out),
        grid=(b, n_lat + 1),
        in_specs=specs(blk_f) + specs(blk_b) + [pl.BlockSpec((1, GATE_PAD), lambda bi, i: (0, 0)),
                                                pl.BlockSpec((16, L), lambda bi, i: (0, 0))],
        out_specs=(pl.BlockSpec((L, width), lambda bi, i: (blk_f(bi, i), 0)),
                   pl.BlockSpec((L, width), lambda bi, i: (blk_b(bi, i), 0))),
        scratch_shapes=[pltpu.VMEM((2 * M_HEADS, width // M_HEADS, width // M_HEADS), F32),
                        pltpu.VMEM((2 * M_HEADS, width // M_HEADS, width // M_HEADS), BF16),
                        pltpu.VMEM((2 * M_HEADS, width // M_HEADS, GATE_PAD), F32),
                        pltpu.VMEM((2 * M_HEADS, width // M_HEADS, GATE_PAD), BF16),
                        pltpu.VMEM((8, GATE_PAD), F32)],
        compiler_params=_cparams(("arbitrary", "arbitrary")),
        name="mlstm",
    )(zm, zm, zm, zg, zg_t, zm, zm, zm, zg, zg_t, bias_c, bias_r)


POOL_BLOCK = 256


def _band(n, w, seg):
    r = lax.broadcasted_iota(jnp.int32, (n, n), 0)
    c = lax.broadcasted_iota(jnp.int32, (n, n), 1)
    d = c - r
    shift = seg.bit_length() - 1
    ok = (d >= -(w // 2)) & (d <= w // 2 - 1) & ((r >> shift) == (c >> shift))
    return jnp.where(ok, 1.0, 0.0).astype(BF16)


def _window_count(idx, n, w):
    return (jnp.minimum(idx + w // 2, n) - jnp.maximum(idx - w // 2, 0)).astype(F32)


def _pool_kernel(z_ref, o_ref, pad_ref, *, grid_rows):
    n = z_ref.shape[0]
    blk = min(POOL_BLOCK, n)
    seg = blk if grid_rows is None else GRID_W
    seg_len = n if grid_rows is None else GRID_W
    halo = (max(POOL_WINDOWS) // 2) * GRID_W
    rowi = lax.broadcasted_iota(jnp.int32, (blk, POOL_GROUP_DIM), 0)
    if grid_rows is not None:
        zeros = jnp.zeros((halo, POOL_GROUP_DIM), F32)
        pad_ref[0:halo, :] = zeros
        pad_ref[halo + n:, :] = zeros
    for g, w in enumerate(POOL_WINDOWS):
        sl = slice(g * POOL_GROUP_DIM, (g + 1) * POOL_GROUP_DIM)
        band = _band(blk, w, seg)
        inv_w = 1.0 / _window_count(rowi & (seg - 1), seg_len, w)
        for b in range(n // blk):
            rows = slice(b * blk, (b + 1) * blk)
            x = z_ref[rows, sl]
            hi = x.astype(BF16)
            lo = (x - hi.astype(F32)).astype(BF16)
            y2 = _dot(band, jnp.concatenate([hi, lo], axis=1))
            y = (y2[:, :POOL_GROUP_DIM] + y2[:, POOL_GROUP_DIM:]) * inv_w
            if grid_rows is None:
                o_ref[rows, sl] = (y - x).astype(BF16)
            else:
                pad_ref[halo + b * blk:halo + (b + 1) * blk, :] = y
        if grid_rows is not None:
            shift = GRID_W.bit_length() - 1
            for b in range(n // blk):
                acc = None
                for d in range(-(w // 2), w // 2):
                    lo_r = halo + b * blk + d * GRID_W
                    t = pad_ref[lo_r:lo_r + blk, :]
                    acc = t if acc is None else acc + t
                inv_h = 1.0 / _window_count((rowi + b * blk) >> shift, grid_rows, w)
                rows = slice(b * blk, (b + 1) * blk)
                o_ref[rows, sl] = (acc * inv_h - z_ref[rows, sl]).astype(BF16)


def _pool(zp, *, n_seq, seq_len, row_block0, grid_rows):
    width = zp.shape[1]
    assert seq_len % POOL_BLOCK == 0 if grid_rows is not None else seq_len <= POOL_BLOCK
    kern = functools.partial(_pool_kernel, grid_rows=grid_rows)
    halo = (max(POOL_WINDOWS) // 2) * GRID_W
    return pl.pallas_call(
        kern,
        out_shape=jax.ShapeDtypeStruct((n_seq * seq_len, width), BF16),
        grid=(n_seq,),
        in_specs=[pl.BlockSpec((seq_len, width), lambda s: (row_block0 + s, 0))],
        out_specs=pl.BlockSpec((seq_len, width), lambda s: (s, 0)),
        scratch_shapes=[pltpu.VMEM((seq_len + 2 * halo, POOL_GROUP_DIM), F32)],
        compiler_params=_cparams(("arbitrary",)),
        name="pool",
    )(zp)


def _mixout_kernel(hf_ref, hb_ref, o_ref, ga_ref, gb_ref, pml_ref, pmc_ref, xa_ref, xb_ref, gt_ref, hg_ref, wpool_ref,
                   ps_ref, wpa_ref, wpb_ref, wout_ref, out_ref, *, dh, n_lat_tiles):
    is_ctx = pl.program_id(0) >= n_lat_tiles
    hm = hf_ref[...].astype(F32) + hb_ref[...].astype(F32)
    parts = []
    for h in range(M_HEADS):
        t = hm[:, h * dh:(h + 1) * dh]
        mu = jnp.mean(t, axis=-1, keepdims=True)
        tc = t - mu
        var = jnp.mean(tc * tc, axis=-1, keepdims=True)
        parts.append(tc * lax.rsqrt(var + EPS))
    hn = jnp.concatenate(parts, axis=-1)
    hn = _sigmoid(o_ref[...].astype(F32)) * (hn * hg_ref[...])
    a = _dot(hn.astype(BF16), wpa_ref[...])
    pm_in = jnp.where(is_ctx, pmc_ref[...], pml_ref[...])
    pparts = []
    for g in range(len(POOL_WINDOWS)):
        sl = slice(g * POOL_GROUP_DIM, (g + 1) * POOL_GROUP_DIM)
        pparts.append(_dot(pm_in[:, sl], wpool_ref[g]))
    pm = jnp.concatenate(pparts, axis=-1) * ps_ref[...]
    bmat = _dot(pm.astype(BF16), wpb_ref[...])
    merged = _sigmoid(ga_ref[...].astype(F32)) * a + _sigmoid(gb_ref[...].astype(F32)) * bmat
    y = _dot(merged.astype(BF16), wout_ref[...])
    out_ref[...] = jnp.where(is_ctx, xb_ref[...], xa_ref[...]) + gt_ref[0] * y


def _mixout(hf, hb, zm, pm_lat, pm_ctx, src, mod3, head_gain, w_pool, pool_scale, w_pa, w_pb, w_out, mod_idx, *, tm,
            n_rows):
    xa, xb, b_off = src
    d = xa.shape[1]
    width = hf.shape[1]
    pw = pm_lat.shape[1]
    n_lat_tiles = pm_lat.shape[0] // tm
    o_blk = 3
    full = lambda shape: pl.BlockSpec(shape, lambda i: (0,) * len(shape))
    kern = functools.partial(_mixout_kernel, dh=width // M_HEADS, n_lat_tiles=n_lat_tiles)
    return pl.pallas_call(
        kern,
        out_shape=jax.ShapeDtypeStruct((n_rows, d), F32),
        grid=(n_rows // tm,),
        in_specs=[pl.BlockSpec((tm, width), lambda i: (i, 0)),
                  pl.BlockSpec((tm, width), lambda i: (i, 0)),
                  pl.BlockSpec((tm, width), lambda i: (i, o_blk)),
                  pl.BlockSpec((tm, d), lambda i: (i, o_blk + 1)),
                  pl.BlockSpec((tm, d), lambda i: (i, o_blk + 2)),
                  pl.BlockSpec((tm, pw), lambda i: (jnp.minimum(i, n_lat_tiles - 1), 0)),
                  pl.BlockSpec((tm, pw), lambda i: (jnp.maximum(i - n_lat_tiles, 0), 0)),
                  pl.BlockSpec((tm, d), lambda i: (jnp.minimum(i, n_lat_tiles - 1), 0)),
                  pl.BlockSpec((tm, d), lambda i: (jnp.maximum(i - n_lat_tiles, 0) + b_off // tm, 0)),
                  pl.BlockSpec((1, 1, d), lambda i: (mod_idx(i) * N_MOD + 2, 0, 0)),
                  full((1, width)), full(w_pool.shape), full((1, pw)),
                  full(w_pa.shape), full(w_pb.shape), full(w_out.shape)],
        out_specs=pl.BlockSpec((tm, d), lambda i: (i, 0)),
        compiler_params=_cparams(("arbitrary",)),
        name="mixout",
    )(hf, hb, zm, zm, zm, pm_lat, pm_ctx, xa, xb, mod3, head_gain, w_pool, pool_scale, w_pa, w_pb, w_out)


SWIGLU_SUB = 512


def _swiglu_partial(u_ref, wg, wu, wd):
    tf = wg.shape[1]
    acc = None
    for c0 in range(0, tf, SWIGLU_SUB):
        cs = slice(c0, min(c0 + SWIGLU_SUB, tf))
        u = u_ref[...]
        act = (_silu(_dot(u, wg[:, cs])) * _dot(u, wu[:, cs])).astype(BF16)
        part = _dot(act, wd[cs, :])
        acc = part if acc is None else acc + part
    return acc


def _ffn_kernel(x_ref, g_ref, sc_ref, sh_ref, gt_ref, wg_ref, wu_ref, wd_ref, out_ref, u_ref, acc_ref):
    j = pl.program_id(1)

    @pl.when(j == 0)
    def _():
        u_ref[...] = _rms_mod(x_ref[...], g_ref[...], sc_ref[0], sh_ref[0]).astype(BF16)
        acc_ref[...] = jnp.zeros_like(acc_ref)

    acc_ref[...] += _swiglu_partial(u_ref, wg_ref, wu_ref, wd_ref)

    @pl.when(j == pl.num_programs(1) - 1)
    def _():
        out_ref[...] = x_ref[...] + gt_ref[0] * acc_ref[...]


def _ffn(h, g, mod3, wg, wu, wd, mod_idx, *, tm, tf):
    r, d = h.shape
    f = wg.shape[1]
    return pl.pallas_call(
        _ffn_kernel,
        out_shape=jax.ShapeDtypeStruct((r, d), F32),
        grid=(r // tm, f // tf),
        in_specs=[pl.BlockSpec((tm, d), lambda i, j: (i, 0)),
                  pl.BlockSpec((1, d), lambda i, j: (0, 0)),
                  pl.BlockSpec((1, 1, d), lambda i, j: (mod_idx(i) * N_MOD + 4, 0, 0)),
                  pl.BlockSpec((1, 1, d), lambda i, j: (mod_idx(i) * N_MOD + 3, 0, 0)),
                  pl.BlockSpec((1, 1, d), lambda i, j: (mod_idx(i) * N_MOD + 5, 0, 0)),
                  pl.BlockSpec((d, tf), lambda i, j: (0, j)),
                  pl.BlockSpec((d, tf), lambda i, j: (0, j)),
                  pl.BlockSpec((tf, d), lambda i, j: (j, 0))],
        out_specs=pl.BlockSpec((tm, d), lambda i, j: (i, 0)),
        scratch_shapes=[pltpu.VMEM((tm, d), BF16), pltpu.VMEM((tm, d), F32)],
        compiler_params=_cparams(("arbitrary", "arbitrary")),
        name="ffn",
    )(h, g, mod3, mod3, mod3, wg, wu, wd)


def _route(u, wr_ref, n_exp):
    parts_u = _split3(u)
    parts_w = _split3(wr_ref[...])
    logits = (_dot(parts_u[0], parts_w[0]) + _dot(parts_u[0], parts_w[1]) + _dot(parts_u[1], parts_w[0])
              + _dot(parts_u[1], parts_w[1]) + _dot(parts_u[0], parts_w[2]) + _dot(parts_u[2], parts_w[0]))
    lane = lax.broadcasted_iota(jnp.int32, logits.shape, 1)
    lg = jnp.where(lane < n_exp, logits, -jnp.inf)
    m1 = jnp.max(lg, axis=1, keepdims=True)
    i1 = jnp.min(jnp.where(lg == m1, lane, 2 * GATE_PAD), axis=1, keepdims=True)
    lg2 = jnp.where(lane == i1, -jnp.inf, lg)
    m2 = jnp.max(lg2, axis=1, keepdims=True)
    i2 = jnp.min(jnp.where(lg2 == m2, lane, 2 * GATE_PAD), axis=1, keepdims=True)
    e = jnp.exp(m2 - m1)
    p1 = 1.0 / (1.0 + e)
    p2 = e * p1
    return i1, i2, p1, p2


LANE_I1, LANE_I2, LANE_P1, LANE_P2, LANE_R1, LANE_R2 = 0, 1, 2, 3, 4, 5


def _pack_bf16_pairs(x):
    k = x.shape[1] // 2
    lo = lax.bitcast_convert_type(x[:, :k].astype(BF16).astype(F32), jnp.uint32)
    hi = lax.bitcast_convert_type(x[:, k:].astype(BF16).astype(F32), jnp.uint32)
    return (lo >> 16) | (hi & jnp.uint32(0xFFFF0000))


def _unpack_bf16_pairs(w):
    lo = lax.bitcast_convert_type(w << 16, F32)
    hi = lax.bitcast_convert_type(w & jnp.uint32(0xFFFF0000), F32)
    return jnp.concatenate([lo, hi], axis=1)


def _route_kernel(x_ref, g_ref, sc_ref, sh_ref, wr_ref, up_ref, info_ref, cnt_ref, carry_ref, *, n_exp):
    @pl.when(pl.program_id(0) == 0)
    def _():
        carry_ref[...] = jnp.zeros_like(carry_ref)

    u = _rms_mod(x_ref[...], g_ref[...], sc_ref[0], sh_ref[0])
    up_ref[...] = _pack_bf16_pairs(u)
    i1, i2, p1, p2 = _route(u, wr_ref, n_exp)
    tm = u.shape[0]
    lane = lax.broadcasted_iota(jnp.int32, (tm, GATE_PAD), 1)
    sel = jnp.where((lane == i1) | (lane == i2), 1.0, 0.0)
    ri = lax.broadcasted_iota(jnp.int32, (tm, tm), 0)
    ci = lax.broadcasted_iota(jnp.int32, (tm, tm), 1)
    before = jnp.where(ci < ri, 1.0, 0.0).astype(BF16)
    rank = carry_ref[0:1, :] + _dot(before, sel.astype(BF16))
    r1 = jnp.sum(jnp.where(lane == i1, rank, 0.0), axis=1, keepdims=True)
    r2 = jnp.sum(jnp.where(lane == i2, rank, 0.0), axis=1, keepdims=True)
    carry_ref[0:1, :] = carry_ref[0:1, :] + jnp.sum(sel, axis=0, keepdims=True)
    cnt_ref[...] = carry_ref[...]
    info = jnp.zeros((tm, GATE_PAD), F32)
    for ln, val in ((LANE_I1, i1.astype(F32)), (LANE_I2, i2.astype(F32)), (LANE_P1, p1), (LANE_P2, p2),
                    (LANE_R1, r1), (LANE_R2, r2)):
        info = jnp.where(lane == ln, val, info)
    info_ref[...] = info


def _route_call(h, g, mod3, w_router, mod_idx, *, tm, n_rows):
    d = h.shape[1]
    n_exp = w_router.shape[1]
    w_r = jnp.concatenate([w_router, jnp.zeros((d, GATE_PAD - n_exp), F32)], axis=1)
    kern = functools.partial(_route_kernel, n_exp=n_exp)
    return pl.pallas_call(
        kern,
        out_shape=(jax.ShapeDtypeStruct((n_rows, d // 2), jnp.uint32),
                   jax.ShapeDtypeStruct((n_rows, GATE_PAD), F32),
                   jax.ShapeDtypeStruct((8, GATE_PAD), F32)),
        grid=(n_rows // tm,),
        in_specs=[pl.BlockSpec((tm, d), lambda i: (i, 0)),
                  pl.BlockSpec((1, d), lambda i: (0, 0)),
                  pl.BlockSpec((1, 1, d), lambda i: (mod_idx(i) * N_MOD + 4, 0, 0)),
                  pl.BlockSpec((1, 1, d), lambda i: (mod_idx(i) * N_MOD + 3, 0, 0)),
                  pl.BlockSpec((d, GATE_PAD), lambda i: (0, 0))],
        out_specs=(pl.BlockSpec((tm, d // 2), lambda i: (i, 0)),
                   pl.BlockSpec((tm, GATE_PAD), lambda i: (i, 0)),
                   pl.BlockSpec((8, GATE_PAD), lambda i: (0, 0))),
        scratch_shapes=[pltpu.VMEM((8, GATE_PAD), F32)],
        compiler_params=_cparams(("arbitrary",)),
        name="route",
    )(h, g, mod3, mod3, w_r)


def _sc_workers():
    info = pltpu.get_tpu_info().sparse_core
    return info.num_cores, info.num_subcores


def _sc_scatter_rows(x, pos, n_out):
    n, dw = x.shape
    nc, ns = _sc_workers()
    nw = nc * ns
    t_per_w = n // nw
    w = min(SC_WINDOW, t_per_w // 2)
    n_chunks = t_per_w // w
    assert n % nw == 0 and t_per_w % (2 * w) == 0
    pos_w = pos.reshape(2, nw, n_chunks, w).transpose(1, 0, 2, 3)
    mesh = plsc.VectorSubcoreMesh(core_axis_name="c", subcore_axis_name="s")

    @functools.partial(pl.kernel, mesh=mesh, out_type=jax.ShapeDtypeStruct((n_out, dw), x.dtype),
                       scratch_types=[pltpu.VMEM((2, n_chunks, w), jnp.int32), pltpu.VMEM((2, w, dw), x.dtype),
                                      pltpu.SemaphoreType.DMA((2,)), pltpu.SemaphoreType.DMA((2,))])
    def scatter(x_hbm, pos_hbm, out_hbm, idx_v, rows_v, gsem, osem):
        wid = lax.axis_index("s") * nc + lax.axis_index("c")
        base = wid * t_per_w
        pltpu.sync_copy(pos_hbm.at[wid], idx_v)

        def get(g, slot):
            return pltpu.make_async_copy(x_hbm.at[pl.ds(base + g * w, w)], rows_v.at[slot], gsem.at[slot])

        def put(g, slot, k):
            return pltpu.make_async_copy(rows_v.at[slot], out_hbm.at[idx_v.at[k, g]], osem.at[slot])

        get(0, 0).start()

        @pl.loop(0, n_chunks, step=2)
        def _(g):
            for b in range(2):
                gg = g + b
                get(gg, b).wait()

                @pl.when(gg + 1 < n_chunks)
                def _():
                    @pl.when(gg >= 1)
                    def _():
                        put(gg - 1, 1 - b, 0).wait()
                        put(gg - 1, 1 - b, 1).wait()
                    get(gg + 1, 1 - b).start()

                put(gg, b, 0).start()
                put(gg, b, 1).start()

        for k in range(2):
            put(n_chunks - 2, 0, k).wait()
            put(n_chunks - 1, 1, k).wait()

    return scatter(x, pos_w)


def _sc_gather_rows(table, idx):
    n = idx.shape[0]
    dw = table.shape[1]
    nc, ns = _sc_workers()
    nw = nc * ns
    b_per_w = n // nw
    w = min(SC_WINDOW, b_per_w // 2)
    n_chunks = b_per_w // w
    assert n % nw == 0 and b_per_w % (2 * w) == 0
    mesh = plsc.VectorSubcoreMesh(core_axis_name="c", subcore_axis_name="s")

    @functools.partial(pl.kernel, mesh=mesh, out_type=jax.ShapeDtypeStruct((n, dw), table.dtype),
                       scratch_types=[pltpu.VMEM((b_per_w,), jnp.int32), pltpu.VMEM((2, w, dw), table.dtype),
                                      pltpu.SemaphoreType.DMA((2,)), pltpu.SemaphoreType.DMA((2,))])
    def gather(table_hbm, idx_hbm, out_hbm, idx_v, rows_v, gsem, osem):
        wid = lax.axis_index("s") * nc + lax.axis_index("c")
        base = wid * b_per_w
        pltpu.sync_copy(idx_hbm.at[pl.ds(base, b_per_w)], idx_v)

        def get(g, slot):
            return pltpu.make_async_copy(table_hbm.at[idx_v.at[pl.ds(g * w, w)]], rows_v.at[slot], gsem.at[slot])

        def put(g, slot):
            return pltpu.make_async_copy(rows_v.at[slot], out_hbm.at[pl.ds(base + g * w, w)], osem.at[slot])

        get(0, 0).start()

        @pl.loop(0, n_chunks, step=2)
        def _(g):
            for b in range(2):
                gg = g + b
                get(gg, b).wait()

                @pl.when(gg + 1 < n_chunks)
                def _():
                    @pl.when(gg >= 1)
                    def _():
                        put(gg - 1, 1 - b).wait()
                    get(gg + 1, 1 - b).start()

                put(gg, b).start()

        put(n_chunks - 2, 0).wait()
        put(n_chunks - 1, 1).wait()

    return gather(table, idx)


def _gmm_kernel(te_ref, nv_ref, xs_ref, wg_ref, wu_ref, wd_ref, ys_ref, u_ref, acc_ref):
    i = pl.program_id(0)
    j = pl.program_id(1)

    @pl.when(i < nv_ref[0])
    def _():
        @pl.when(j == 0)
        def _():
            u_ref[...] = _unpack_bf16_pairs(xs_ref[...]).astype(BF16)
            acc_ref[...] = jnp.zeros_like(acc_ref)

        acc_ref[...] += _swiglu_partial(u_ref, wg_ref.at[0], wu_ref.at[0], wd_ref.at[0])

        @pl.when(j == pl.num_programs(1) - 1)
        def _():
            ys_ref[...] = _pack_bf16_pairs(acc_ref[...])


def _gmm(xs, tile_expert, n_valid, wg, wu, wd, *, tg, tf):
    p, dw = xs.shape
    d = 2 * dw
    f = wg.shape[2]
    n_f = f // tf

    def jj(i, j, nv):
        return jnp.where(i < nv[0], j, n_f - 1)

    grid_spec = pltpu.PrefetchScalarGridSpec(
        num_scalar_prefetch=2,
        grid=(p // tg, n_f),
        in_specs=[pl.BlockSpec((tg, dw), lambda i, j, te, nv: (i, 0)),
                  pl.BlockSpec((1, d, tf), lambda i, j, te, nv: (te[i], 0, jj(i, j, nv))),
                  pl.BlockSpec((1, d, tf), lambda i, j, te, nv: (te[i], 0, jj(i, j, nv))),
                  pl.BlockSpec((1, tf, d), lambda i, j, te, nv: (te[i], jj(i, j, nv), 0))],
        out_specs=pl.BlockSpec((tg, dw), lambda i, j, te, nv: (i, 0)),
        scratch_shapes=[pltpu.VMEM((tg, d), BF16), pltpu.VMEM((tg, d), F32)])
    return pl.pallas_call(
        _gmm_kernel,
        out_shape=jax.ShapeDtypeStruct((p, dw), jnp.uint32),
        grid_spec=grid_spec,
        compiler_params=_cparams(("arbitrary", "arbitrary")),
        name="gmm",
    )(tile_expert, n_valid, xs, wg, wu, wd)


def _combine_kernel(x_ref, info_ref, y1_ref, y2_ref, gt_ref, gf_ref, out_ref, *, final):
    info = info_ref[...]
    p1 = info[:, LANE_P1:LANE_P1 + 1]
    p2 = info[:, LANE_P2:LANE_P2 + 1]
    moe = p1 * _unpack_bf16_pairs(y1_ref[...]) + p2 * _unpack_bf16_pairs(y2_ref[...])
    y = x_ref[...] + gt_ref[0] * moe
    if final:
        r = lax.rsqrt(jnp.mean(y * y, axis=-1, keepdims=True) + EPS)
        y = (y * r) * gf_ref[...]
    out_ref[...] = y


def _combine(h, info, yg, mod3, g_final, mod_idx, *, tm, n_rows, final):
    d = h.shape[1]
    n_t = n_rows // tm
    kern = functools.partial(_combine_kernel, final=final)
    return pl.pallas_call(
        kern,
        out_shape=jax.ShapeDtypeStruct((n_rows, d), F32),
        grid=(n_t,),
        in_specs=[pl.BlockSpec((tm, d), lambda i: (i, 0)),
                  pl.BlockSpec((tm, GATE_PAD), lambda i: (i, 0)),
                  pl.BlockSpec((tm, d // 2), lambda i: (i, 0)),
                  pl.BlockSpec((tm, d // 2), lambda i: (n_t + i, 0)),
                  pl.BlockSpec((1, 1, d), lambda i: (mod_idx(i) * N_MOD + 5, 0, 0)),
                  pl.BlockSpec((1, d), lambda i: (0, 0))],
        out_specs=pl.BlockSpec((tm, d), lambda i: (i, 0)),
        compiler_params=_cparams(("arbitrary",)),
        name="combine",
    )(h, info, yg, yg, mod3, g_final)


def _moe(h, g, mod3, w_router, wg, wu, wd, g_final, mod_idx, *, tm, tg, tf, n_rows, final):
    n_exp = wg.shape[0]
    up, info, cnt = _route_call(h, g, mod3, w_router, mod_idx, tm=tm, n_rows=n_rows)
    i1 = info[:, LANE_I1].astype(jnp.int32)
    i2 = info[:, LANE_I2].astype(jnp.int32)
    r1 = info[:, LANE_R1].astype(jnp.int32)
    r2 = info[:, LANE_R2].astype(jnp.int32)
    counts = cnt[0, :n_exp].astype(jnp.int32)
    padded = (counts + tg - 1) // tg * tg
    ends = jnp.cumsum(padded)
    starts = ends - padded
    eye = jnp.arange(n_exp, dtype=jnp.int32)
    pos1 = jnp.sum(jnp.where(i1[:, None] == eye, starts, 0), axis=1) + r1
    pos2 = jnp.sum(jnp.where(i2[:, None] == eye, starts, 0), axis=1) + r2
    p_rows = -(-(TOP_K * n_rows + n_exp * (tg - 1)) // tg) * tg
    tile_row = jnp.arange(p_rows // tg, dtype=jnp.int32) * tg
    tile_expert = jnp.minimum(jnp.sum(tile_row[:, None] >= ends[None, :], axis=1), n_exp - 1).astype(jnp.int32)
    n_valid = (ends[-1] // tg).reshape(1).astype(jnp.int32)
    xs = _sc_scatter_rows(up, jnp.stack([pos1, pos2]), p_rows)
    ys = _gmm(xs, tile_expert, n_valid, wg, wu, wd, tg=tg, tf=tf)
    yg = _sc_gather_rows(ys, jnp.concatenate([pos1, pos2]))
    return _combine(h, info, yg, mod3, g_final, mod_idx, tm=tm, n_rows=n_rows, final=final)


def _final_norm_kernel(x_ref, g_ref, o_ref):
    x = x_ref[...]
    r = lax.rsqrt(jnp.mean(x * x, axis=-1, keepdims=True) + EPS)
    o_ref[...] = (x * r) * g_ref[...]


def _final_norm(h, g, *, tm, n_rows):
    d = h.shape[1]
    return pl.pallas_call(
        _final_norm_kernel,
        out_shape=jax.ShapeDtypeStruct((n_rows, d), F32),
        grid=(n_rows // tm,),
        in_specs=[pl.BlockSpec((tm, d), lambda i: (i, 0)), pl.BlockSpec((1, d), lambda i: (0, 0))],
        out_specs=pl.BlockSpec((tm, d), lambda i: (i, 0)),
        compiler_params=_cparams(("arbitrary",)),
        name="final_norm",
    )(h, g)


def kernel(x, c, ctx, c_ctx, w_ada, b_ada, g_mix, w_in, conv_qk, b_if, head_gain, w_pool, pool_scale, w_pa, w_pb,
           w_out, g_ffn, w_ff_gate, w_ff_up, w_ff_down, w_router, w_exp_gate, w_exp_up, w_exp_down, g_final):
    b, s, d = x.shape
    lc = ctx.shape[1]
    depth = w_ada.shape[0]
    width = w_pa.shape[1]
    pw = w_pb.shape[1]
    n_gate = b_if.shape[1] * b_if.shape[2] * b_if.shape[3]
    assert lc == CHUNK and s % CHUNK == 0 and s % GRID_W == 0 and width == d and n_gate == 16
    n_lat, n_ctx = b * s, b * lc
    tm, tm_ffn = 512, 1024
    assert s % tm_ffn == 0 and n_ctx % tm_ffn == 0 and tm_ffn % tm == 0

    def mod_idx(t):
        return lambda i: jnp.where(i < n_lat // t, i // (s // t), b)

    src = (x.reshape(n_lat, d), ctx.reshape(n_ctx, d), 0)

    n_mod_rows = 16
    cc = jnp.zeros((n_mod_rows, d), F32).at[:b].set(c).at[b].set(c_ctx)
    mod = _ada(cc, w_ada, b_ada)

    if_off, pool_off, ga_off = 4 * width, 4 * width + n_gate, 4 * width + n_gate + pw
    n_main = 4 * width + 2 * d

    for l in range(depth):
        last = l == depth - 1
        mod3 = mod[l].reshape(n_mod_rows * N_MOD, 1, d)
        w_l = w_in[l]
        w_cat = jnp.concatenate(
            [w_l[:, :if_off], w_l[:, ga_off:], w_l[:, pool_off:ga_off], w_l[:, if_off:pool_off],
             jnp.zeros((d, GATE_PAD - n_gate), F32)], axis=1).astype(BF16)
        k_scale = float((width // M_HEADS) ** -0.5)
        zm, zp, zg = _inproj(src, g_mix[l].reshape(1, d), mod3, w_cat, conv_qk[l], mod_idx(tm), tm=tm, n_main=n_main,
                             n_pool=pw, n_qk=2 * width, k_scale=k_scale, n_lat=n_lat, lat_len=s, ctx_len=lc)

        bias = b_if[l].reshape(n_gate)
        bias_c = jnp.zeros((1, GATE_PAD), F32).at[0, :n_gate].set(bias)
        bias_r = jnp.broadcast_to(bias[:, None], (n_gate, CHUNK))
        hf, hb = _mlstm(zm, zg, zg[:, :n_gate].T, bias_c, bias_r, b=b, s=s, width=width)

        n_rows = n_lat if last else n_lat + n_ctx
        pm_lat = _pool(zp, n_seq=b, seq_len=s, row_block0=0, grid_rows=s // GRID_W)
        pm_ctx = pm_lat if last else _pool(zp, n_seq=b, seq_len=lc, row_block0=n_lat // lc, grid_rows=None)

        h = _mixout(hf, hb, zm, pm_lat, pm_ctx, src, mod3, head_gain[l].reshape(1, width), w_pool[l].astype(BF16),
                    pool_scale[l].reshape(1, pw), w_pa[l].astype(BF16), w_pb[l].astype(BF16),
                    w_out[l].astype(BF16), mod_idx(tm), tm=tm, n_rows=n_rows)

        j = l // 2
        if l % 2 == 0:
            h = _ffn(h, g_ffn[l].reshape(1, d), mod3, w_ff_gate[j].astype(BF16), w_ff_up[j].astype(BF16),
                     w_ff_down[j].astype(BF16), mod_idx(tm_ffn), tm=tm_ffn,
                     tf=_pick_tile(w_ff_gate.shape[2], 1408))
            if last:
                h = _final_norm(h, g_final.reshape(1, d), tm=tm, n_rows=n_lat)
        else:
            h = _moe(h, g_ffn[l].reshape(1, d), mod3, w_router[j], w_exp_gate[j].astype(BF16),
                     w_exp_up[j].astype(BF16), w_exp_down[j].astype(BF16), g_final.reshape(1, d), mod_idx(tm),
                     tm=tm, tg=512, tf=_pick_tile(w_exp_gate.shape[3], 1792), n_rows=h.shape[0], final=last)
        src = (h, h, n_lat)
    return h[:n_lat].reshape(b, s, d)
```

```python
import functools

import jax
import jax.numpy as jnp
from jax import lax
from jax.experimental import pallas as pl
from jax.experimental.pallas import tpu as pltpu
from jax.experimental.pallas import tpu_sc as plsc

F32 = jnp.float32
BF16 = jnp.bfloat16
EPS = 1e-6
M_HEADS = 4
GRID_W = 64
POOL_WINDOWS = (2, 4, 8, 16)
POOL_GROUP_DIM = 128
CHUNK = 256
N_MOD = 6
TOP_K = 2
SC_WINDOW = 64
GATE_PAD = 128
VMEM_LIMIT = 56 * 1024 * 1024


def _pick_tile(n, target, quantum=128):
    t = min(n, target) // quantum * quantum
    while n % t:
        t -= quantum
    return t


def _cparams(sem):
    return pltpu.CompilerParams(dimension_semantics=sem, vmem_limit_bytes=VMEM_LIMIT)


def _sigmoid(x):
    return 0.5 * jnp.tanh(0.5 * x) + 0.5


def _silu(x):
    return x * _sigmoid(x)


def _log_sigmoid(x):
    return jnp.minimum(x, 0.0) - jnp.log(1.0 + jnp.exp(-jnp.abs(x)))


def _split3(x):
    hi = x.astype(BF16)
    r1 = x - hi.astype(F32)
    mid = r1.astype(BF16)
    lo = (r1 - mid.astype(F32)).astype(BF16)
    return hi, mid, lo


def _dot(a, b):
    return jnp.dot(a, b, preferred_element_type=F32)


def _dot_nt(a, b):
    return lax.dot_general(a, b, (((1,), (1,)), ((), ())), preferred_element_type=F32)


def _dot_tn(a, b):
    return lax.dot_general(a, b, (((0,), (0,)), ((), ())), preferred_element_type=F32)


def _rms_mod(x, g, sc, sh):
    r = lax.rsqrt(jnp.mean(x * x, axis=-1, keepdims=True) + EPS)
    return (x * r) * g * (1.0 + sc) + sh


def _ada_kernel(c_ref, w_ref, b_ref, o_ref):
    c = c_ref[...]
    o_ref[0] = _dot(_silu(c).astype(BF16), w_ref[0].astype(BF16)) + b_ref[0]


def _ada(cc, w_ada, b_ada):
    depth, d, n = w_ada.shape
    tn = 1024
    return pl.pallas_call(
        _ada_kernel,
        out_shape=jax.ShapeDtypeStruct((depth, cc.shape[0], n), F32),
        grid=(depth, n // tn),
        in_specs=[pl.BlockSpec(cc.shape, lambda l, j: (0, 0)),
                  pl.BlockSpec((1, d, tn), lambda l, j: (l, 0, j)),
                  pl.BlockSpec((1, 1, tn), lambda l, j: (l, 0, j))],
        out_specs=pl.BlockSpec((1, cc.shape[0], tn), lambda l, j: (l, 0, j)),
        compiler_params=_cparams(("arbitrary", "arbitrary")),
        name="ada",
    )(cc, w_ada, b_ada.reshape(depth, 1, n))


def _inproj_kernel(xa_ref, xb_ref, xpa_ref, xpb_ref, xna_ref, xnb_ref, g_ref, sc_ref, sh_ref, w_ref, cw_ref,
                   zm_ref, zp_ref, zg_ref, u_ref, uh_ref,
                   *, n_main, n_pool, n_qk, cw, k_scale, n_lat_tiles, lat_len, ctx_len):
    i = pl.program_id(0)
    tm = xa_ref.shape[0]
    is_ctx = i >= n_lat_tiles
    g, sc, sh = g_ref[...], sc_ref[0], sh_ref[0]
    u_ref[...] = _rms_mod(jnp.where(is_ctx, xb_ref[...], xa_ref[...]), g, sc, sh).astype(BF16)
    uh_ref[0:8, :] = _rms_mod(jnp.where(is_ctx, xpb_ref[...], xpa_ref[...]), g, sc, sh).astype(BF16)
    uh_ref[8:16, :] = _rms_mod(jnp.where(is_ctx, xnb_ref[...], xna_ref[...]), g, sc, sh).astype(BF16)

    pos0 = lax.rem(i * tm, lat_len)
    lat_first_row = jnp.where(pos0 == 0, 0, -1)
    lat_last_row = jnp.where(pos0 + tm == lat_len, tm - 1, -1)
    sub = 16
    srow = lax.broadcasted_iota(jnp.int32, (sub, cw), 0)
    slabs = sorted({0, tm - sub} | {k * ctx_len - sub for k in range(1, tm // ctx_len)}
                   | {k * ctx_len for k in range(1, tm // ctx_len)})

    for c in range(n_qk // cw):
        cs = slice(c * cw, (c + 1) * cw)
        w0, w1, w2 = cw_ref[0:1, cs], cw_ref[1:2, cs], cw_ref[2:3, cs]
        scale = k_scale if c * cw >= n_qk // 2 else 1.0
        z = _dot(u_ref[...], w_ref[:, cs])
        zh = _dot(uh_ref[...], w_ref[:, cs])
        y = _silu(pltpu.roll(z, 1, 0) * w0 + z * w1 + pltpu.roll(z, tm - 1, 0) * w2) * scale
        zm_ref[:, cs] = y.astype(BF16)
        for r0 in slabs:
            zs = z[r0:r0 + sub, :]
            above = zh[7:8, :] if r0 == 0 else z[r0 - 1:r0, :]
            below = zh[8:9, :] if r0 + sub == tm else z[r0 + sub:r0 + sub + 1, :]
            grow = srow + r0
            in_ctx = grow & (ctx_len - 1)
            first = jnp.where(is_ctx, in_ctx, grow - lat_first_row) == 0
            last = jnp.where(is_ctx, in_ctx - (ctx_len - 1), grow - lat_last_row) == 0
            zm1 = jnp.where(srow == 0, above, pltpu.roll(zs, 1, 0))
            zm1 = jnp.where(first, 0.0, zm1)
            zp1 = jnp.where(srow == sub - 1, below, pltpu.roll(zs, sub - 1, 0))
            zp1 = jnp.where(last, 0.0, zp1)
            ys = _silu(zm1 * w0 + zs * w1 + zp1 * w2) * scale
            zm_ref[r0:r0 + sub, cs] = ys.astype(BF16)
    for c in range(n_qk // cw, n_main // cw):
        zm_ref[:, c * cw:(c + 1) * cw] = _dot(u_ref[...], w_ref[:, c * cw:(c + 1) * cw]).astype(BF16)
    for c in range(n_pool // cw):
        lo = n_main + c * cw
        zp_ref[:, c * cw:(c + 1) * cw] = _dot(u_ref[...], w_ref[:, lo:lo + cw])
    zg_ref[...] = _dot(u_ref[...], w_ref[:, n_main + n_pool:])


def _inproj(src, g, mod3, w_cat, conv_w, mod_idx, *, tm, n_main, n_pool, n_qk, k_scale, n_lat, lat_len, ctx_len):
    a, b, b_off = src
    d = a.shape[1]
    r = n_lat + b.shape[0] - b_off
    ncols = w_cat.shape[1]
    halo = 8
    assert tm % ctx_len == 0 and lat_len % tm == 0 and ctx_len & (ctx_len - 1) == 0 and b_off % tm == 0
    na_t, hb = n_lat // tm, tm // halo
    a_last, b_last = a.shape[0] // halo - 1, b.shape[0] // halo - 1
    kern = functools.partial(_inproj_kernel, n_main=n_main, n_pool=n_pool, n_qk=n_qk, cw=512, k_scale=k_scale,
                             n_lat_tiles=na_t, lat_len=lat_len, ctx_len=ctx_len)
    return pl.pallas_call(
        kern,
        out_shape=(jax.ShapeDtypeStruct((r, n_main), BF16),
                   jax.ShapeDtypeStruct((r, n_pool), F32),
                   jax.ShapeDtypeStruct((r, GATE_PAD), F32)),
        grid=(r // tm,),
        in_specs=[pl.BlockSpec((tm, d), lambda i: (jnp.minimum(i, na_t - 1), 0)),
                  pl.BlockSpec((tm, d), lambda i: (jnp.maximum(i - na_t, 0) + b_off // tm, 0)),
                  pl.BlockSpec((halo, d), lambda i: (jnp.clip(i * hb - 1, 0, a_last), 0)),
                  pl.BlockSpec((halo, d), lambda i: (jnp.clip(b_off // halo + (i - na_t) * hb - 1, 0, b_last), 0)),
                  pl.BlockSpec((halo, d), lambda i: (jnp.clip((i + 1) * hb, 0, a_last), 0)),
                  pl.BlockSpec((halo, d), lambda i: (jnp.clip(b_off // halo + (i - na_t + 1) * hb, 0, b_last), 0)),
                  pl.BlockSpec((1, d), lambda i: (0, 0)),
                  pl.BlockSpec((1, 1, d), lambda i: (mod_idx(i) * N_MOD + 1, 0, 0)),
                  pl.BlockSpec((1, 1, d), lambda i: (mod_idx(i) * N_MOD + 0, 0, 0)),
                  pl.BlockSpec((d, ncols), lambda i: (0, 0)),
                  pl.BlockSpec((3, n_qk), lambda i: (0, 0))],
        out_specs=(pl.BlockSpec((tm, n_main), lambda i: (i, 0)),
                   pl.BlockSpec((tm, n_pool), lambda i: (i, 0)),
                   pl.BlockSpec((tm, GATE_PAD), lambda i: (i, 0))),
        scratch_shapes=[pltpu.VMEM((tm, d), BF16), pltpu.VMEM((2 * halo, d), BF16)],
        compiler_params=_cparams(("arbitrary",)),
        name="inproj",
    )(a, b, a, b, a, b, g, mod3, mod3, w_cat, conv_w)


def _running_max_rows(x, rev):
    n = x.shape[0]
    row = lax.broadcasted_iota(jnp.int32, x.shape, 0)
    sh = 1
    while sh < n:
        if rev:
            shifted = jnp.where(row < n - sh, pltpu.roll(x, n - sh, 0), -jnp.inf)
        else:
            shifted = jnp.where(row >= sh, pltpu.roll(x, sh, 0), -jnp.inf)
        x = jnp.maximum(x, shifted)
        sh *= 2
    return x


def _mlstm_kernel(qf, kf, vf, gcf, grf, qb, kb, vb, gcb, grb, bias_c, bias_r, hf_out, hb_out,
                  c_ref, c16_ref, n_ref, n16_ref, m_ref, *, dh):
    L = CHUNK

    @pl.when(pl.program_id(1) == 0)
    def _():
        c_ref[...] = jnp.zeros_like(c_ref)
        c16_ref[...] = jnp.zeros_like(c16_ref)
        n_ref[...] = jnp.zeros_like(n_ref)
        n16_ref[...] = jnp.zeros_like(n16_ref)
        m_ref[...] = jnp.zeros_like(m_ref)

    ri = lax.broadcasted_iota(jnp.int32, (L, L), 0)
    ci = lax.broadcasted_iota(jnp.int32, (L, L), 1)
    lower = ci <= ri
    upper = ci >= ri
    lower_b = jnp.where(lower, 1.0, 0.0).astype(BF16)
    upper_b = jnp.where(upper, 1.0, 0.0).astype(BF16)

    dirs = ((qf, kf, vf, gcf, grf, hf_out), (qb, kb, vb, gcb, grb, hb_out))
    for d, (q_ref, k_ref, v_ref, gc_ref, gr_ref, out_ref) in enumerate(dirs):
        rev = d == 1
        gc = gc_ref[...] + bias_c[...]
        gr = gr_ref[...] + bias_r[...]
        fc = _log_sigmoid(gc)
        fr = _log_sigmoid(gr)
        t_col = upper_b if rev else lower_b
        t_row = lower_b if rev else upper_b
        bcum_c = sum(_dot(t_col, p) for p in _split3(fc))
        bcum_r = sum(_dot(p, t_row) for p in _split3(fr))
        mask = upper if rev else lower
        end = 0 if rev else L - 1

        b_al = pltpu.roll(bcum_c, GATE_PAD - M_HEADS, axis=1)
        g_col = gc - b_al
        m_st = m_ref[d:d + 1, :]
        mx = jnp.maximum(m_st, _running_max_rows(g_col, rev))
        e_neg_all = jnp.exp(-(b_al + mx))
        mx_end = mx[end:end + 1, :]
        wk_all = jnp.exp(g_col - mx_end)
        wc_all = jnp.exp(m_st - mx_end)
        m_ref[d:d + 1, :] = b_al[end:end + 1, :] + mx_end

        ones_rep = jnp.ones((L, GATE_PAD), BF16)
        for h in range(M_HEADS):
            s_idx = d * M_HEADS + h
            col = d * 2 * M_HEADS + h

            def rep(x_all):
                return jnp.broadcast_to(x_all[:, col:col + 1], (x_all.shape[0], GATE_PAD))

            def wide(x_rep):
                return jnp.concatenate([x_rep] * (dh // GATE_PAD), axis=1)

            mx_rep = rep(mx)
            g_row = gr[col:col + 1, :] - bcum_r[col + M_HEADS:col + M_HEADS + 1, :]
            p = jnp.exp(jnp.where(mask, g_row - wide(mx_rep), -jnp.inf))
            w_inter = jnp.exp(rep(m_st) - mx_rep)
            e_neg = rep(e_neg_all)
            wk16 = rep(wk_all).astype(BF16)
            wc = wc_all[:, col:col + 1]

            sl = slice(h * dh, (h + 1) * dh)
            q = q_ref[:, sl]
            k = k_ref[:, sl]
            v = v_ref[:, sl]
            scores = (_dot_nt(q, k) * p).astype(BF16)
            num = wide(w_inter) * _dot(q, c16_ref[s_idx]) + _dot(scores, v)
            den = w_inter * _dot(q, n16_ref[s_idx]) + _dot(scores, ones_rep)
            inv = 1.0 / jnp.maximum(jnp.abs(den), e_neg)
            out_ref[:, sl] = (num * wide(inv)).astype(BF16)

            c_new = wc * c_ref[s_idx] + _dot_tn(k * wide(wk16), v)
            c_ref[s_idx] = c_new
            c16_ref[s_idx] = c_new.astype(BF16)
            n_new = wc * n_ref[s_idx] + _dot_tn(k, wk16)
            n_ref[s_idx] = n_new
            n16_ref[s_idx] = n_new.astype(BF16)


def _mlstm(zm, zg, zg_t, bias_c, bias_r, *, b, s, width):
    r = zm.shape[0]
    L = CHUNK
    n_lat = s // L
    lat_blocks = b * n_lat

    def blk_f(bi, i):
        return jnp.where(i == 0, lat_blocks + bi, bi * n_lat + i - 1)

    def blk_b(bi, i):
        return jnp.where(i == 0, lat_blocks + bi, bi * n_lat + n_lat - i)

    def specs(blk):
        return [pl.BlockSpec((L, width), lambda bi, i: (blk(bi, i), 0)),
                pl.BlockSpec((L, width), lambda bi, i: (blk(bi, i), 1)),
                pl.BlockSpec((L, width), lambda bi, i: (blk(bi, i), 2)),
                pl.BlockSpec((L, GATE_PAD), lambda bi, i: (blk(bi, i), 0)),
                pl.BlockSpec((16, L), lambda bi, i: (0, blk(bi, i)))]

    kern = functools.partial(_mlstm_kernel, dh=width // M_HEADS)
    out = jax.ShapeDtypeStruct((r, width), BF16)
    return pl.pallas_call(
        kern,
        out_shape=(out, out),
        grid=(b, n_lat + 1),
        in_specs=specs(blk_f) + specs(blk_b) + [pl.BlockSpec((1, GATE_PAD), lambda bi, i: (0, 0)),
                                                pl.BlockSpec((16, L), lambda bi, i: (0, 0))],
        out_specs=(pl.BlockSpec((L, width), lambda bi, i: (blk_f(bi, i), 0)),
                   pl.BlockSpec((L, width), lambda bi, i: (blk_b(bi, i), 0))),
        scratch_shapes=[pltpu.VMEM((2 * M_HEADS, width // M_HEADS, width // M_HEADS), F32),
                        pltpu.VMEM((2 * M_HEADS, width // M_HEADS, width // M_HEADS), BF16),
                        pltpu.VMEM((2 * M_HEADS, width // M_HEADS, GATE_PAD), F32),
                        pltpu.VMEM((2 * M_HEADS, width // M_HEADS, GATE_PAD), BF16),
                        pltpu.VMEM((8, GATE_PAD), F32)],
        compiler_params=_cparams(("arbitrary", "arbitrary")),
        name="mlstm",
    )(zm, zm, zm, zg, zg_t, zm, zm, zm, zg, zg_t, bias_c, bias_r)


POOL_BLOCK = 256


def _band(n, w, seg):
    r = lax.broadcasted_iota(jnp.int32, (n, n), 0)
    c = lax.broadcasted_iota(jnp.int32, (n, n), 1)
    d = c - r
    shift = seg.bit_length() - 1
    ok = (d >= -(w // 2)) & (d <= w // 2 - 1) & ((r >> shift) == (c >> shift))
    return jnp.where(ok, 1.0, 0.0).astype(BF16)


def _window_count(idx, n, w):
    return (jnp.minimum(idx + w // 2, n) - jnp.maximum(idx - w // 2, 0)).astype(F32)


def _pool_kernel(z_ref, o_ref, pad_ref, *, grid_rows):
    n = z_ref.shape[0]
    blk = min(POOL_BLOCK, n)
    seg = blk if grid_rows is None else GRID_W
    seg_len = n if grid_rows is None else GRID_W
    halo = (max(POOL_WINDOWS) // 2) * GRID_W
    rowi = lax.broadcasted_iota(jnp.int32, (blk, POOL_GROUP_DIM), 0)
    if grid_rows is not None:
        zeros = jnp.zeros((halo, POOL_GROUP_DIM), F32)
        pad_ref[0:halo, :] = zeros
        pad_ref[halo + n:, :] = zeros
    for g, w in enumerate(POOL_WINDOWS):
        sl = slice(g * POOL_GROUP_DIM, (g + 1) * POOL_GROUP_DIM)
        band = _band(blk, w, seg)
        inv_w = 1.0 / _window_count(rowi & (seg - 1), seg_len, w)
        for b in range(n // blk):
            rows = slice(b * blk, (b + 1) * blk)
            x = z_ref[rows, sl]
            hi = x.astype(BF16)
            lo = (x - hi.astype(F32)).astype(BF16)
            y2 = _dot(band, jnp.concatenate([hi, lo], axis=1))
            y = (y2[:, :POOL_GROUP_DIM] + y2[:, POOL_GROUP_DIM:]) * inv_w
            if grid_rows is None:
                o_ref[rows, sl] = (y - x).astype(BF16)
            else:
                pad_ref[halo + b * blk:halo + (b + 1) * blk, :] = y
        if grid_rows is not None:
            shift = GRID_W.bit_length() - 1
            for b in range(n // blk):
                acc = None
                for d in range(-(w // 2), w // 2):
                    lo_r = halo + b * blk + d * GRID_W
                    t = pad_ref[lo_r:lo_r + blk, :]
                    acc = t if acc is None else acc + t
                inv_h = 1.0 / _window_count((rowi + b * blk) >> shift, grid_rows, w)
                rows = slice(b * blk, (b + 1) * blk)
                o_ref[rows, sl] = (acc * inv_h - z_ref[rows, sl]).astype(BF16)


def _pool(zp, *, n_seq, seq_len, row_block0, grid_rows):
    width = zp.shape[1]
    assert seq_len % POOL_BLOCK == 0 if grid_rows is not None else seq_len <= POOL_BLOCK
    kern = functools.partial(_pool_kernel, grid_rows=grid_rows)
    halo = (max(POOL_WINDOWS) // 2) * GRID_W
    return pl.pallas_call(
        kern,
        out_shape=jax.ShapeDtypeStruct((n_seq * seq_len, width), BF16),
        grid=(n_seq,),
        in_specs=[pl.BlockSpec((seq_len, width), lambda s: (row_block0 + s, 0))],
        out_specs=pl.BlockSpec((seq_len, width), lambda s: (s, 0)),
        scratch_shapes=[pltpu.VMEM((seq_len + 2 * halo, POOL_GROUP_DIM), F32)],
        compiler_params=_cparams(("arbitrary",)),
        name="pool",
    )(zp)


def _mixout_kernel(hf_ref, hb_ref, o_ref, ga_ref, gb_ref, pml_ref, pmc_ref, xa_ref, xb_ref, gt_ref, hg_ref, wpool_ref,
                   ps_ref, wpa_ref, wpb_ref, wout_ref, out_ref, *, dh, n_lat_tiles):
    is_ctx = pl.program_id(0) >= n_lat_tiles
    hm = hf_ref[...].astype(F32) + hb_ref[...].astype(F32)
    parts = []
    for h in range(M_HEADS):
        t = hm[:, h * dh:(h + 1) * dh]
        mu = jnp.mean(t, axis=-1, keepdims=True)
        tc = t - mu
        var = jnp.mean(tc * tc, axis=-1, keepdims=True)
        parts.append(tc * lax.rsqrt(var + EPS))
    hn = jnp.concatenate(parts, axis=-1)
    hn = _sigmoid(o_ref[...].astype(F32)) * (hn * hg_ref[...])
    a = _dot(hn.astype(BF16), wpa_ref[...])
    pm_in = jnp.where(is_ctx, pmc_ref[...], pml_ref[...])
    pparts = []
    for g in range(len(POOL_WINDOWS)):
        sl = slice(g * POOL_GROUP_DIM, (g + 1) * POOL_GROUP_DIM)
        pparts.append(_dot(pm_in[:, sl], wpool_ref[g]))
    pm = jnp.concatenate(pparts, axis=-1) * ps_ref[...]
    bmat = _dot(pm.astype(BF16), wpb_ref[...])
    merged = _sigmoid(ga_ref[...].astype(F32)) * a + _sigmoid(gb_ref[...].astype(F32)) * bmat
    y = _dot(merged.astype(BF16), wout_ref[...])
    out_ref[...] = jnp.where(is_ctx, xb_ref[...], xa_ref[...]) + gt_ref[0] * y


def _mixout(hf, hb, zm, pm_lat, pm_ctx, src, mod3, head_gain, w_pool, pool_scale, w_pa, w_pb, w_out, mod_idx, *, tm,
            n_rows):
    xa, xb, b_off = src
    d = xa.shape[1]
    width = hf.shape[1]
    pw = pm_lat.shape[1]
    n_lat_tiles = pm_lat.shape[0] // tm
    o_blk = 3
    full = lambda shape: pl.BlockSpec(shape, lambda i: (0,) * len(shape))
    kern = functools.partial(_mixout_kernel, dh=width // M_HEADS, n_lat_tiles=n_lat_tiles)
    return pl.pallas_call(
        kern,
        out_shape=jax.ShapeDtypeStruct((n_rows, d), F32),
        grid=(n_rows // tm,),
        in_specs=[pl.BlockSpec((tm, width), lambda i: (i, 0)),
                  pl.BlockSpec((tm, width), lambda i: (i, 0)),
                  pl.BlockSpec((tm, width), lambda i: (i, o_blk)),
                  pl.BlockSpec((tm, d), lambda i: (i, o_blk + 1)),
                  pl.BlockSpec((tm, d), lambda i: (i, o_blk + 2)),
                  pl.BlockSpec((tm, pw), lambda i: (jnp.minimum(i, n_lat_tiles - 1), 0)),
                  pl.BlockSpec((tm, pw), lambda i: (jnp.maximum(i - n_lat_tiles, 0), 0)),
                  pl.BlockSpec((tm, d), lambda i: (jnp.minimum(i, n_lat_tiles - 1), 0)),
                  pl.BlockSpec((tm, d), lambda i: (jnp.maximum(i - n_lat_tiles, 0) + b_off // tm, 0)),
                  pl.BlockSpec((1, 1, d), lambda i: (mod_idx(i) * N_MOD + 2, 0, 0)),
                  full((1, width)), full(w_pool.shape), full((1, pw)),
                  full(w_pa.shape), full(w_pb.shape), full(w_out.shape)],
        out_specs=pl.BlockSpec((tm, d), lambda i: (i, 0)),
        compiler_params=_cparams(("arbitrary",)),
        name="mixout",
    )(hf, hb, zm, zm, zm, pm_lat, pm_ctx, xa, xb, mod3, head_gain, w_pool, pool_scale, w_pa, w_pb, w_out)


SWIGLU_SUB = 512


def _swiglu_partial(u_ref, wg, wu, wd):
    tf = wg.shape[1]
    acc = None
    for c0 in range(0, tf, SWIGLU_SUB):
        cs = slice(c0, min(c0 + SWIGLU_SUB, tf))
        u = u_ref[...]
        act = (_silu(_dot(u, wg[:, cs])) * _dot(u, wu[:, cs])).astype(BF16)
        part = _dot(act, wd[cs, :])
        acc = part if acc is None else acc + part
    return acc


def _ffn_kernel(x_ref, g_ref, sc_ref, sh_ref, gt_ref, wg_ref, wu_ref, wd_ref, out_ref, u_ref, acc_ref):
    j = pl.program_id(1)

    @pl.when(j == 0)
    def _():
        u_ref[...] = _rms_mod(x_ref[...], g_ref[...], sc_ref[0], sh_ref[0]).astype(BF16)
        acc_ref[...] = jnp.zeros_like(acc_ref)

    acc_ref[...] += _swiglu_partial(u_ref, wg_ref, wu_ref, wd_ref)

    @pl.when(j == pl.num_programs(1) - 1)
    def _():
        out_ref[...] = x_ref[...] + gt_ref[0] * acc_ref[...]


def _ffn(h, g, mod3, wg, wu, wd, mod_idx, *, tm, tf):
    r, d = h.shape
    f = wg.shape[1]
    return pl.pallas_call(
        _ffn_kernel,
        out_shape=jax.ShapeDtypeStruct((r, d), F32),
        grid=(r // tm, f // tf),
        in_specs=[pl.BlockSpec((tm, d), lambda i, j: (i, 0)),
                  pl.BlockSpec((1, d), lambda i, j: (0, 0)),
                  pl.BlockSpec((1, 1, d), lambda i, j: (mod_idx(i) * N_MOD + 4, 0, 0)),
                  pl.BlockSpec((1, 1, d), lambda i, j: (mod_idx(i) * N_MOD + 3, 0, 0)),
                  pl.BlockSpec((1, 1, d), lambda i, j: (mod_idx(i) * N_MOD + 5, 0, 0)),
                  pl.BlockSpec((d, tf), lambda i, j: (0, j)),
                  pl.BlockSpec((d, tf), lambda i, j: (0, j)),
                  pl.BlockSpec((tf, d), lambda i, j: (j, 0))],
        out_specs=pl.BlockSpec((tm, d), lambda i, j: (i, 0)),
        scratch_shapes=[pltpu.VMEM((tm, d), BF16), pltpu.VMEM((tm, d), F32)],
        compiler_params=_cparams(("arbitrary", "arbitrary")),
        name="ffn",
    )(h, g, mod3, mod3, mod3, wg, wu, wd)


def _route(u, wr_ref, n_exp):
    parts_u = _split3(u)
    parts_w = _split3(wr_ref[...])
    logits = (_dot(parts_u[0], parts_w[0]) + _dot(parts_u[0], parts_w[1]) + _dot(parts_u[1], parts_w[0])
              + _dot(parts_u[1], parts_w[1]) + _dot(parts_u[0], parts_w[2]) + _dot(parts_u[2], parts_w[0]))
    lane = lax.broadcasted_iota(jnp.int32, logits.shape, 1)
    lg = jnp.where(lane < n_exp, logits, -jnp.inf)
    m1 = jnp.max(lg, axis=1, keepdims=True)
    i1 = jnp.min(jnp.where(lg == m1, lane, 2 * GATE_PAD), axis=1, keepdims=True)
    lg2 = jnp.where(lane == i1, -jnp.inf, lg)
    m2 = jnp.max(lg2, axis=1, keepdims=True)
    i2 = jnp.min(jnp.where(lg2 == m2, lane, 2 * GATE_PAD), axis=1, keepdims=True)
    e = jnp.exp(m2 - m1)
    p1 = 1.0 / (1.0 + e)
    p2 = e * p1
    return i1, i2, p1, p2


LANE_I1, LANE_I2, LANE_P1, LANE_P2, LANE_R1, LANE_R2 = 0, 1, 2, 3, 4, 5


def _pack_bf16_pairs(x):
    k = x.shape[1] // 2
    lo = lax.bitcast_convert_type(x[:, :k].astype(BF16).astype(F32), jnp.uint32)
    hi = lax.bitcast_convert_type(x[:, k:].astype(BF16).astype(F32), jnp.uint32)
    return (lo >> 16) | (hi & jnp.uint32(0xFFFF0000))


def _unpack_bf16_pairs(w):
    lo = lax.bitcast_convert_type(w << 16, F32)
    hi = lax.bitcast_convert_type(w & jnp.uint32(0xFFFF0000), F32)
    return jnp.concatenate([lo, hi], axis=1)


def _route_kernel(x_ref, g_ref, sc_ref, sh_ref, wr_ref, up_ref, info_ref, cnt_ref, carry_ref, *, n_exp):
    @pl.when(pl.program_id(0) == 0)
    def _():
        carry_ref[...] = jnp.zeros_like(carry_ref)

    u = _rms_mod(x_ref[...], g_ref[...], sc_ref[0], sh_ref[0])
    up_ref[...] = _pack_bf16_pairs(u)
    i1, i2, p1, p2 = _route(u, wr_ref, n_exp)
    tm = u.shape[0]
    lane = lax.broadcasted_iota(jnp.int32, (tm, GATE_PAD), 1)
    sel = jnp.where((lane == i1) | (lane == i2), 1.0, 0.0)
    ri = lax.broadcasted_iota(jnp.int32, (tm, tm), 0)
    ci = lax.broadcasted_iota(jnp.int32, (tm, tm), 1)
    before = jnp.where(ci < ri, 1.0, 0.0).astype(BF16)
    rank = carry_ref[0:1, :] + _dot(before, sel.astype(BF16))
    r1 = jnp.sum(jnp.where(lane == i1, rank, 0.0), axis=1, keepdims=True)
    r2 = jnp.sum(jnp.where(lane == i2, rank, 0.0), axis=1, keepdims=True)
    carry_ref[0:1, :] = carry_ref[0:1, :] + jnp.sum(sel, axis=0, keepdims=True)
    cnt_ref[...] = carry_ref[...]
    info = jnp.zeros((tm, GATE_PAD), F32)
    for ln, val in ((LANE_I1, i1.astype(F32)), (LANE_I2, i2.astype(F32)), (LANE_P1, p1), (LANE_P2, p2),
                    (LANE_R1, r1), (LANE_R2, r2)):
        info = jnp.where(lane == ln, val, info)
    info_ref[...] = info


def _route_call(h, g, mod3, w_router, mod_idx, *, tm, n_rows):
    d = h.shape[1]
    n_exp = w_router.shape[1]
    w_r = jnp.concatenate([w_router, jnp.zeros((d, GATE_PAD - n_exp), F32)], axis=1)
    kern = functools.partial(_route_kernel, n_exp=n_exp)
    return pl.pallas_call(
        kern,
        out_shape=(jax.ShapeDtypeStruct((n_rows, d // 2), jnp.uint32),
                   jax.ShapeDtypeStruct((n_rows, GATE_PAD), F32),
                   jax.ShapeDtypeStruct((8, GATE_PAD), F32)),
        grid=(n_rows // tm,),
        in_specs=[pl.BlockSpec((tm, d), lambda i: (i, 0)),
                  pl.BlockSpec((1, d), lambda i: (0, 0)),
                  pl.BlockSpec((1, 1, d), lambda i: (mod_idx(i) * N_MOD + 4, 0, 0)),
                  pl.BlockSpec((1, 1, d), lambda i: (mod_idx(i) * N_MOD + 3, 0, 0)),
                  pl.BlockSpec((d, GATE_PAD), lambda i: (0, 0))],
        out_specs=(pl.BlockSpec((tm, d // 2), lambda i: (i, 0)),
                   pl.BlockSpec((tm, GATE_PAD), lambda i: (i, 0)),
                   pl.BlockSpec((8, GATE_PAD), lambda i: (0, 0))),
        scratch_shapes=[pltpu.VMEM((8, GATE_PAD), F32)],
        compiler_params=_cparams(("arbitrary",)),
        name="route",
    )(h, g, mod3, mod3, w_r)


def _sc_workers():
    info = pltpu.get_tpu_info().sparse_core
    return info.num_cores, info.num_subcores


def _sc_scatter_rows(x, pos, n_out):
    n, dw = x.shape
    nc, ns = _sc_workers()
    nw = nc * ns
    t_per_w = n // nw
    w = min(SC_WINDOW, t_per_w // 2)
    n_chunks = t_per_w // w
    assert n % nw == 0 and t_per_w % (2 * w) == 0
    pos_w = pos.reshape(2, nw, n_chunks, w).transpose(1, 0, 2, 3)
    mesh = plsc.VectorSubcoreMesh(core_axis_name="c", subcore_axis_name="s")

    @functools.partial(pl.kernel, mesh=mesh, out_type=jax.ShapeDtypeStruct((n_out, dw), x.dtype),
                       scratch_types=[pltpu.VMEM((2, n_chunks, w), jnp.int32), pltpu.VMEM((2, w, dw), x.dtype),
                                      pltpu.SemaphoreType.DMA((2,)), pltpu.SemaphoreType.DMA((2,))])
    def scatter(x_hbm, pos_hbm, out_hbm, idx_v, rows_v, gsem, osem):
        wid = lax.axis_index("s") * nc + lax.axis_index("c")
        base = wid * t_per_w
        pltpu.sync_copy(pos_hbm.at[wid], idx_v)

        def get(g, slot):
            return pltpu.make_async_copy(x_hbm.at[pl.ds(base + g * w, w)], rows_v.at[slot], gsem.at[slot])

        def put(g, slot, k):
            return pltpu.make_async_copy(rows_v.at[slot], out_hbm.at[idx_v.at[k, g]], osem.at[slot])

        get(0, 0).start()

        @pl.loop(0, n_chunks, step=2)
        def _(g):
            for b in range(2):
                gg = g + b
                get(gg, b).wait()

                @pl.when(gg + 1 < n_chunks)
                def _():
                    @pl.when(gg >= 1)
                    def _():
                        put(gg - 1, 1 - b, 0).wait()
                        put(gg - 1, 1 - b, 1).wait()
                    get(gg + 1, 1 - b).start()

                put(gg, b, 0).start()
                put(gg, b, 1).start()

        for k in range(2):
            put(n_chunks - 2, 0, k).wait()
            put(n_chunks - 1, 1, k).wait()

    return scatter(x, pos_w)


def _sc_gather_rows(table, idx):
    n = idx.shape[0]
    dw = table.shape[1]
    nc, ns = _sc_workers()
    nw = nc * ns
    b_per_w = n // nw
    w = min(SC_WINDOW, b_per_w // 2)
    n_chunks = b_per_w // w
    assert n % nw == 0 and b_per_w % (2 * w) == 0
    mesh = plsc.VectorSubcoreMesh(core_axis_name="c", subcore_axis_name="s")

    @functools.partial(pl.kernel, mesh=mesh, out_type=jax.ShapeDtypeStruct((n, dw), table.dtype),
                       scratch_types=[pltpu.VMEM((b_per_w,), jnp.int32), pltpu.VMEM((2, w, dw), table.dtype),
                                      pltpu.SemaphoreType.DMA((2,)), pltpu.SemaphoreType.DMA((2,))])
    def gather(table_hbm, idx_hbm, out_hbm, idx_v, rows_v, gsem, osem):
        wid = lax.axis_index("s") * nc + lax.axis_index("c")
        base = wid * b_per_w
        pltpu.sync_copy(idx_hbm.at[pl.ds(base, b_per_w)], idx_v)

        def get(g, slot):
            return pltpu.make_async_copy(table_hbm.at[idx_v.at[pl.ds(g * w, w)]], rows_v.at[slot], gsem.at[slot])

        def put(g, slot):
            return pltpu.make_async_copy(rows_v.at[slot], out_hbm.at[pl.ds(base + g * w, w)], osem.at[slot])

        get(0, 0).start()

        @pl.loop(0, n_chunks, step=2)
        def _(g):
            for b in range(2):
                gg = g + b
                get(gg, b).wait()

                @pl.when(gg + 1 < n_chunks)
                def _():
                    @pl.when(gg >= 1)
                    def _():
                        put(gg - 1, 1 - b).wait()
                    get(gg + 1, 1 - b).start()

                put(gg, b).start()

        put(n_chunks - 2, 0).wait()
        put(n_chunks - 1, 1).wait()

    return gather(table, idx)


def _gmm_kernel(te_ref, nv_ref, xs_ref, wg_ref, wu_ref, wd_ref, ys_ref, u_ref, acc_ref):
    i = pl.program_id(0)
    j = pl.program_id(1)

    @pl.when(i < nv_ref[0])
    def _():
        @pl.when(j == 0)
        def _():
            u_ref[...] = _unpack_bf16_pairs(xs_ref[...]).astype(BF16)
            acc_ref[...] = jnp.zeros_like(acc_ref)

        acc_ref[...] += _swiglu_partial(u_ref, wg_ref.at[0], wu_ref.at[0], wd_ref.at[0])

        @pl.when(j == pl.num_programs(1) - 1)
        def _():
            ys_ref[...] = _pack_bf16_pairs(acc_ref[...])


def _gmm(xs, tile_expert, n_valid, wg, wu, wd, *, tg, tf):
    p, dw = xs.shape
    d = 2 * dw
    f = wg.shape[2]
    n_f = f // tf

    def jj(i, j, nv):
        return jnp.where(i < nv[0], j, n_f - 1)

    grid_spec = pltpu.PrefetchScalarGridSpec(
        num_scalar_prefetch=2,
        grid=(p // tg, n_f),
        in_specs=[pl.BlockSpec((tg, dw), lambda i, j, te, nv: (i, 0)),
                  pl.BlockSpec((1, d, tf), lambda i, j, te, nv: (te[i], 0, jj(i, j, nv))),
                  pl.BlockSpec((1, d, tf), lambda i, j, te, nv: (te[i], 0, jj(i, j, nv))),
                  pl.BlockSpec((1, tf, d), lambda i, j, te, nv: (te[i], jj(i, j, nv), 0))],
        out_specs=pl.BlockSpec((tg, dw), lambda i, j, te, nv: (i, 0)),
        scratch_shapes=[pltpu.VMEM((tg, d), BF16), pltpu.VMEM((tg, d), F32)])
    return pl.pallas_call(
        _gmm_kernel,
        out_shape=jax.ShapeDtypeStruct((p, dw), jnp.uint32),
        grid_spec=grid_spec,
        compiler_params=_cparams(("arbitrary", "arbitrary")),
        name="gmm",
    )(tile_expert, n_valid, xs, wg, wu, wd)


def _combine_kernel(x_ref, info_ref, y1_ref, y2_ref, gt_ref, gf_ref, out_ref, *, final):
    info = info_ref[...]
    p1 = info[:, LANE_P1:LANE_P1 + 1]
    p2 = info[:, LANE_P2:LANE_P2 + 1]
    moe = p1 * _unpack_bf16_pairs(y1_ref[...]) + p2 * _unpack_bf16_pairs(y2_ref[...])
    y = x_ref[...] + gt_ref[0] * moe
    if final:
        r = lax.rsqrt(jnp.mean(y * y, axis=-1, keepdims=True) + EPS)
        y = (y * r) * gf_ref[...]
    out_ref[...] = y


def _combine(h, info, yg, mod3, g_final, mod_idx, *, tm, n_rows, final):
    d = h.shape[1]
    n_t = n_rows // tm
    kern = functools.partial(_combine_kernel, final=final)
    return pl.pallas_call(
        kern,
        out_shape=jax.ShapeDtypeStruct((n_rows, d), F32),
        grid=(n_t,),
        in_specs=[pl.BlockSpec((tm, d), lambda i: (i, 0)),
                  pl.BlockSpec((tm, GATE_PAD), lambda i: (i, 0)),
                  pl.BlockSpec((tm, d // 2), lambda i: (i, 0)),
                  pl.BlockSpec((tm, d // 2), lambda i: (n_t + i, 0)),
                  pl.BlockSpec((1, 1, d), lambda i: (mod_idx(i) * N_MOD + 5, 0, 0)),
                  pl.BlockSpec((1, d), lambda i: (0, 0))],
        out_specs=pl.BlockSpec((tm, d), lambda i: (i, 0)),
        compiler_params=_cparams(("arbitrary",)),
        name="combine",
    )(h, info, yg, yg, mod3, g_final)


def _moe(h, g, mod3, w_router, wg, wu, wd, g_final, mod_idx, *, tm, tg, tf, n_rows, final):
    n_exp = wg.shape[0]
    up, info, cnt = _route_call(h, g, mod3, w_router, mod_idx, tm=tm, n_rows=n_rows)
    i1 = info[:, LANE_I1].astype(jnp.int32)
    i2 = info[:, LANE_I2].astype(jnp.int32)
    r1 = info[:, LANE_R1].astype(jnp.int32)
    r2 = info[:, LANE_R2].astype(jnp.int32)
    counts = cnt[0, :n_exp].astype(jnp.int32)
    padded = (counts + tg - 1) // tg * tg
    ends = jnp.cumsum(padded)
    starts = ends - padded
    eye = jnp.arange(n_exp, dtype=jnp.int32)
    pos1 = jnp.sum(jnp.where(i1[:, None] == eye, starts, 0), axis=1) + r1
    pos2 = jnp.sum(jnp.where(i2[:, None] == eye, starts, 0), axis=1) + r2
    p_rows = -(-(TOP_K * n_rows + n_exp * (tg - 1)) // tg) * tg
    tile_row = jnp.arange(p_rows // tg, dtype=jnp.int32) * tg
    tile_expert = jnp.minimum(jnp.sum(tile_row[:, None] >= ends[None, :], axis=1), n_exp - 1).astype(jnp.int32)
    n_valid = (ends[-1] // tg).reshape(1).astype(jnp.int32)
    xs = _sc_scatter_rows(up, jnp.stack([pos1, pos2]), p_rows)
    ys = _gmm(xs, tile_expert, n_valid, wg, wu, wd, tg=tg, tf=tf)
    yg = _sc_gather_rows(ys, jnp.concatenate([pos1, pos2]))
    return _combine(h, info, yg, mod3, g_final, mod_idx, tm=tm, n_rows=n_rows, final=final)


def _final_norm_kernel(x_ref, g_ref, o_ref):
    x = x_ref[...]
    r = lax.rsqrt(jnp.mean(x * x, axis=-1, keepdims=True) + EPS)
    o_ref[...] = (x * r) * g_ref[...]


def _final_norm(h, g, *, tm, n_rows):
    d = h.shape[1]
    return pl.pallas_call(
        _final_norm_kernel,
        out_shape=jax.ShapeDtypeStruct((n_rows, d), F32),
        grid=(n_rows // tm,),
        in_specs=[pl.BlockSpec((tm, d), lambda i: (i, 0)), pl.BlockSpec((1, d), lambda i: (0, 0))],
        out_specs=pl.BlockSpec((tm, d), lambda i: (i, 0)),
        compiler_params=_cparams(("arbitrary",)),
        name="final_norm",
    )(h, g)


def kernel(x, c, ctx, c_ctx, w_ada, b_ada, g_mix, w_in, conv_qk, b_if, head_gain, w_pool, pool_scale, w_pa, w_pb,
           w_out, g_ffn, w_ff_gate, w_ff_up, w_ff_down, w_router, w_exp_gate, w_exp_up, w_exp_down, g_final):
    b, s, d = x.shape
    lc = ctx.shape[1]
    depth = w_ada.shape[0]
    width = w_pa.shape[1]
    pw = w_pb.shape[1]
    n_gate = b_if.shape[1] * b_if.shape[2] * b_if.shape[3]
    assert lc == CHUNK and s % CHUNK == 0 and s % GRID_W == 0 and width == d and n_gate == 16
    n_lat, n_ctx = b * s, b * lc
    tm, tm_ffn = 512, 1024
    assert s % tm_ffn == 0 and n_ctx % tm_ffn == 0 and tm_ffn % tm == 0

    def mod_idx(t):
        return lambda i: jnp.where(i < n_lat // t, i // (s // t), b)

    src = (x.reshape(n_lat, d), ctx.reshape(n_ctx, d), 0)

    n_mod_rows = 16
    cc = jnp.zeros((n_mod_rows, d), F32).at[:b].set(c).at[b].set(c_ctx)
    mod = _ada(cc, w_ada, b_ada)

    if_off, pool_off, ga_off = 4 * width, 4 * width + n_gate, 4 * width + n_gate + pw
    n_main = 4 * width + 2 * d

    for l in range(depth):
        last = l == depth - 1
        mod3 = mod[l].reshape(n_mod_rows * N_MOD, 1, d)
        w_l = w_in[l]
        w_cat = jnp.concatenate(
            [w_l[:, :if_off], w_l[:, ga_off:], w_l[:, pool_off:ga_off], w_l[:, if_off:pool_off],
             jnp.zeros((d, GATE_PAD - n_gate), F32)], axis=1).astype(BF16)
        k_scale = float((width // M_HEADS) ** -0.5)
        zm, zp, zg = _inproj(src, g_mix[l].reshape(1, d), mod3, w_cat, conv_qk[l], mod_idx(tm), tm=tm, n_main=n_main,
                             n_pool=pw, n_qk=2 * width, k_scale=k_scale, n_lat=n_lat, lat_len=s, ctx_len=lc)

        bias = b_if[l].reshape(n_gate)
        bias_c = jnp.zeros((1, GATE_PAD), F32).at[0, :n_gate].set(bias)
        bias_r = jnp.broadcast_to(bias[:, None], (n_gate, CHUNK))
        hf, hb = _mlstm(zm, zg, zg[:, :n_gate].T, bias_c, bias_r, b=b, s=s, width=width)

        n_rows = n_lat if last else n_lat + n_ctx
        pm_lat = _pool(zp, n_seq=b, seq_len=s, row_block0=0, grid_rows=s // GRID_W)
        pm_ctx = pm_lat if last else _pool(zp, n_seq=b, seq_len=lc, row_block0=n_lat // lc, grid_rows=None)

        h = _mixout(hf, hb, zm, pm_lat, pm_ctx, src, mod3, head_gain[l].reshape(1, width), w_pool[l].astype(BF16),
                    pool_scale[l].reshape(1, pw), w_pa[l].astype(BF16), w_pb[l].astype(BF16),
                    w_out[l].astype(BF16), mod_idx(tm), tm=tm, n_rows=n_rows)

        j = l // 2
        if l % 2 == 0:
            h = _ffn(h, g_ffn[l].reshape(1, d), mod3, w_ff_gate[j].astype(BF16), w_ff_up[j].astype(BF16),
                     w_ff_down[j].astype(BF16), mod_idx(tm_ffn), tm=tm_ffn,
                     tf=_pick_tile(w_ff_gate.shape[2], 1408))
            if last:
                h = _final_norm(h, g_final.reshape(1, d), tm=tm, n_rows=n_lat)
        else:
            h = _moe(h, g_ffn[l].reshape(1, d), mod3, w_router[j], w_exp_gate[j].astype(BF16),
                     w_exp_up[j].astype(BF16), w_exp_down[j].astype(BF16), g_final.reshape(1, d), mod_idx(tm),
                     tm=tm, tg=512, tf=_pick_tile(w_exp_gate.shape[3], 1792), n_rows=h.shape[0], final=last)
        src = (h, h, n_lat)
    return h[:n_lat].reshape(b, s, d)
```

```python
import functools

import jax
import jax.numpy as jnp
from jax import lax
from jax.experimental import pallas as pl
from jax.experimental.pallas import tpu as pltpu
from jax.experimental.pallas import tpu_sc as plsc

F32 = jnp.float32
BF16 = jnp.bfloat16
EPS = 1e-6
M_HEADS = 4
GRID_W = 64
POOL_WINDOWS = (2, 4, 8, 16)
POOL_GROUP_DIM = 128
CHUNK = 256
N_MOD = 6
TOP_K = 2
SC_WINDOW = 64
GATE_PAD = 128
VMEM_LIMIT = 56 * 1024 * 1024


def _pick_tile(n, target, quantum=128):
    t = min(n, target) // quantum * quantum
    while n % t:
        t -= quantum
    return t


def _cparams(sem):
    return pltpu.CompilerParams(dimension_semantics=sem, vmem_limit_bytes=VMEM_LIMIT)


def _sigmoid(x):
    return 0.5 * jnp.tanh(0.5 * x) + 0.5


def _silu(x):
    return x * _sigmoid(x)


def _log_sigmoid(x):
    return jnp.minimum(x, 0.0) - jnp.log(1.0 + jnp.exp(-jnp.abs(x)))


def _split3(x):
    hi = x.astype(BF16)
    r1 = x - hi.astype(F32)
    mid = r1.astype(BF16)
    lo = (r1 - mid.astype(F32)).astype(BF16)
    return hi, mid, lo


def _dot(a, b):
    return jnp.dot(a, b, preferred_element_type=F32)


def _dot_nt(a, b):
    return lax.dot_general(a, b, (((1,), (1,)), ((), ())), preferred_element_type=F32)


def _dot_tn(a, b):
    return lax.dot_general(a, b, (((0,), (0,)), ((), ())), preferred_element_type=F32)


def _rms_mod(x, g, sc, sh):
    r = lax.rsqrt(jnp.mean(x * x, axis=-1, keepdims=True) + EPS)
    return (x * r) * g * (1.0 + sc) + sh


def _ada_kernel(c_ref, w_ref, b_ref, o_ref):
    c = c_ref[...]
    o_ref[0] = _dot(_silu(c).astype(BF16), w_ref[0].astype(BF16)) + b_ref[0]


def _ada(cc, w_ada, b_ada):
    depth, d, n = w_ada.shape
    tn = 1024
    return pl.pallas_call(
        _ada_kernel,
        out_shape=jax.ShapeDtypeStruct((depth, cc.shape[0], n), F32),
        grid=(depth, n // tn),
        in_specs=[pl.BlockSpec(cc.shape, lambda l, j: (0, 0)),
                  pl.BlockSpec((1, d, tn), lambda l, j: (l, 0, j)),
                  pl.BlockSpec((1, 1, tn), lambda l, j: (l, 0, j))],
        out_specs=pl.BlockSpec((1, cc.shape[0], tn), lambda l, j: (l, 0, j)),
        compiler_params=_cparams(("arbitrary", "arbitrary")),
        name="ada",
    )(cc, w_ada, b_ada.reshape(depth, 1, n))


def _inproj_kernel(xa_ref, xb_ref, xpa_ref, xpb_ref, xna_ref, xnb_ref, g_ref, sc_ref, sh_ref, w_ref, cw_ref,
                   zm_ref, zp_ref, zg_ref, u_ref, uh_ref,
                   *, n_main, n_pool, n_qk, cw, k_scale, n_lat_tiles, lat_len, ctx_len):
    i = pl.program_id(0)
    tm = xa_ref.shape[0]
    is_ctx = i >= n_lat_tiles
    g, sc, sh = g_ref[...], sc_ref[0], sh_ref[0]
    u_ref[...] = _rms_mod(jnp.where(is_ctx, xb_ref[...], xa_ref[...]), g, sc, sh).astype(BF16)
    uh_ref[0:8, :] = _rms_mod(jnp.where(is_ctx, xpb_ref[...], xpa_ref[...]), g, sc, sh).astype(BF16)
    uh_ref[8:16, :] = _rms_mod(jnp.where(is_ctx, xnb_ref[...], xna_ref[...]), g, sc, sh).astype(BF16)

    pos0 = lax.rem(i * tm, lat_len)
    lat_first_row = jnp.where(pos0 == 0, 0, -1)
    lat_last_row = jnp.where(pos0 + tm == lat_len, tm - 1, -1)
    sub = 16
    srow = lax.broadcasted_iota(jnp.int32, (sub, cw), 0)
    slabs = sorted({0, tm - sub} | {k * ctx_len - sub for k in range(1, tm // ctx_len)}
                   | {k * ctx_len for k in range(1, tm // ctx_len)})

    for c in range(n_qk // cw):
        cs = slice(c * cw, (c + 1) * cw)
        w0, w1, w2 = cw_ref[0:1, cs], cw_ref[1:2, cs], cw_ref[2:3, cs]
        scale = k_scale if c * cw >= n_qk // 2 else 1.0
        z = _dot(u_ref[...], w_ref[:, cs])
        zh = _dot(uh_ref[...], w_ref[:, cs])
        y = _silu(pltpu.roll(z, 1, 0) * w0 + z * w1 + pltpu.roll(z, tm - 1, 0) * w2) * scale
        zm_ref[:, cs] = y.astype(BF16)
        for r0 in slabs:
            zs = z[r0:r0 + sub, :]
            above = zh[7:8, :] if r0 == 0 else z[r0 - 1:r0, :]
            below = zh[8:9, :] if r0 + sub == tm else z[r0 + sub:r0 + sub + 1, :]
            grow = srow + r0
            in_ctx = grow & (ctx_len - 1)
            first = jnp.where(is_ctx, in_ctx, grow - lat_first_row) == 0
            last = jnp.where(is_ctx, in_ctx - (ctx_len - 1), grow - lat_last_row) == 0
            zm1 = jnp.where(srow == 0, above, pltpu.roll(zs, 1, 0))
            zm1 = jnp.where(first, 0.0, zm1)
            zp1 = jnp.where(srow == sub - 1, below, pltpu.roll(zs, sub - 1, 0))
            zp1 = jnp.where(last, 0.0, zp1)
            ys = _silu(zm1 * w0 + zs * w1 + zp1 * w2) * scale
            zm_ref[r0:r0 + sub, cs] = ys.astype(BF16)
    for c in range(n_qk // cw, n_main // cw):
        zm_ref[:, c * cw:(c + 1) * cw] = _dot(u_ref[...], w_ref[:, c * cw:(c + 1) * cw]).astype(BF16)
    for c in range(n_pool // cw):
        lo = n_main + c * cw
        zp_ref[:, c * cw:(c + 1) * cw] = _dot(u_ref[...], w_ref[:, lo:lo + cw])
    zg_ref[...] = _dot(u_ref[...], w_ref[:, n_main + n_pool:])


def _inproj(src, g, mod3, w_cat, conv_w, mod_idx, *, tm, n_main, n_pool, n_qk, k_scale, n_lat, lat_len, ctx_len):
    a, b, b_off = src
    d = a.shape[1]
    r = n_lat + b.shape[0] - b_off
    ncols = w_cat.shape[1]
    halo = 8
    assert tm % ctx_len == 0 and lat_len % tm == 0 and ctx_len & (ctx_len - 1) == 0 and b_off % tm == 0
    na_t, hb = n_lat // tm, tm // halo
    a_last, b_last = a.shape[0] // halo - 1, b.shape[0] // halo - 1
    kern = functools.partial(_inproj_kernel, n_main=n_main, n_pool=n_pool, n_qk=n_qk, cw=512, k_scale=k_scale,
                             n_lat_tiles=na_t, lat_len=lat_len, ctx_len=ctx_len)
    return pl.pallas_call(
        kern,
        out_shape=(jax.ShapeDtypeStruct((r, n_main), BF16),
                   jax.ShapeDtypeStruct((r, n_pool), F32),
                   jax.ShapeDtypeStruct((r, GATE_PAD), F32)),
        grid=(r // tm,),
        in_specs=[pl.BlockSpec((tm, d), lambda i: (jnp.minimum(i, na_t - 1), 0)),
                  pl.BlockSpec((tm, d), lambda i: (jnp.maximum(i - na_t, 0) + b_off // tm, 0)),
                  pl.BlockSpec((halo, d), lambda i: (jnp.clip(i * hb - 1, 0, a_last), 0)),
                  pl.BlockSpec((halo, d), lambda i: (jnp.clip(b_off // halo + (i - na_t) * hb - 1, 0, b_last), 0)),
                  pl.BlockSpec((halo, d), lambda i: (jnp.clip((i + 1) * hb, 0, a_last), 0)),
                  pl.BlockSpec((halo, d), lambda i: (jnp.clip(b_off // halo + (i - na_t + 1) * hb, 0, b_last), 0)),
                  pl.BlockSpec((1, d), lambda i: (0, 0)),
                  pl.BlockSpec((1, 1, d), lambda i: (mod_idx(i) * N_MOD + 1, 0, 0)),
                  pl.BlockSpec((1, 1, d), lambda i: (mod_idx(i) * N_MOD + 0, 0, 0)),
                  pl.BlockSpec((d, ncols), lambda i: (0, 0)),
                  pl.BlockSpec((3, n_qk), lambda i: (0, 0))],
        out_specs=(pl.BlockSpec((tm, n_main), lambda i: (i, 0)),
                   pl.BlockSpec((tm, n_pool), lambda i: (i, 0)),
                   pl.BlockSpec((tm, GATE_PAD), lambda i: (i, 0))),
        scratch_shapes=[pltpu.VMEM((tm, d), BF16), pltpu.VMEM((2 * halo, d), BF16)],
        compiler_params=_cparams(("arbitrary",)),
        name="inproj",
    )(a, b, a, b, a, b, g, mod3, mod3, w_cat, conv_w)


def _running_max_rows(x, rev):
    n = x.shape[0]
    row = lax.broadcasted_iota(jnp.int32, x.shape, 0)
    sh = 1
    while sh < n:
        if rev:
            shifted = jnp.where(row < n - sh, pltpu.roll(x, n - sh, 0), -jnp.inf)
        else:
            shifted = jnp.where(row >= sh, pltpu.roll(x, sh, 0), -jnp.inf)
        x = jnp.maximum(x, shifted)
        sh *= 2
    return x


def _mlstm_kernel(qf, kf, vf, gcf, grf, qb, kb, vb, gcb, grb, bias_c, bias_r, hf_out, hb_out,
                  c_ref, c16_ref, n_ref, n16_ref, m_ref, *, dh):
    L = CHUNK

    @pl.when(pl.program_id(1) == 0)
    def _():
        c_ref[...] = jnp.zeros_like(c_ref)
        c16_ref[...] = jnp.zeros_like(c16_ref)
        n_ref[...] = jnp.zeros_like(n_ref)
        n16_ref[...] = jnp.zeros_like(n16_ref)
        m_ref[...] = jnp.zeros_like(m_ref)

    ri = lax.broadcasted_iota(jnp.int32, (L, L), 0)
    ci = lax.broadcasted_iota(jnp.int32, (L, L), 1)
    lower = ci <= ri
    upper = ci >= ri
    lower_b = jnp.where(lower, 1.0, 0.0).astype(BF16)
    upper_b = jnp.where(upper, 1.0, 0.0).astype(BF16)

    dirs = ((qf, kf, vf, gcf, grf, hf_out), (qb, kb, vb, gcb, grb, hb_out))
    for d, (q_ref, k_ref, v_ref, gc_ref, gr_ref, out_ref) in enumerate(dirs):
        rev = d == 1
        gc = gc_ref[...] + bias_c[...]
        gr = gr_ref[...] + bias_r[...]
        fc = _log_sigmoid(gc)
        fr = _log_sigmoid(gr)
        t_col = upper_b if rev else lower_b
        t_row = lower_b if rev else upper_b
        bc3 = _dot(t_col, jnp.concatenate(_split3(fc), axis=1))
        bcum_c = bc3[:, :GATE_PAD] + bc3[:, GATE_PAD:2 * GATE_PAD] + bc3[:, 2 * GATE_PAD:]
        br3 = _dot(jnp.concatenate(_split3(fr), axis=0), t_row)
        bcum_r = br3[0:16, :] + br3[16:32, :] + br3[32:48, :]
        mask = upper if rev else lower
        end = 0 if rev else L - 1

        b_al = pltpu.roll(bcum_c, GATE_PAD - M_HEADS, axis=1)
        g_col = gc - b_al
        m_st = m_ref[d:d + 1, :]
        mx = jnp.maximum(m_st, _running_max_rows(g_col, rev))
        e_neg_all = jnp.exp(-(b_al + mx))
        mx_end = mx[end:end + 1, :]
        wk_all = jnp.exp(g_col - mx_end)
        wc_all = jnp.exp(m_st - mx_end)
        m_ref[d:d + 1, :] = b_al[end:end + 1, :] + mx_end

        ones_rep = jnp.ones((L, GATE_PAD), BF16)
        for h in range(M_HEADS):
            s_idx = d * M_HEADS + h
            col = d * 2 * M_HEADS + h

            def rep(x_all):
                return jnp.broadcast_to(x_all[:, col:col + 1], (x_all.shape[0], GATE_PAD))

            def wide(x_rep):
                return jnp.concatenate([x_rep] * (dh // GATE_PAD), axis=1)

            mx_rep = rep(mx)
            g_row = gr[col:col + 1, :] - bcum_r[col + M_HEADS:col + M_HEADS + 1, :]
            p = jnp.exp(jnp.where(mask, g_row - wide(mx_rep), -jnp.inf))
            w_inter = jnp.exp(rep(m_st) - mx_rep)
            e_neg = rep(e_neg_all)
            wk16 = rep(wk_all).astype(BF16)
            wc = wc_all[:, col:col + 1]

            sl = slice(h * dh, (h + 1) * dh)
            q = q_ref[:, sl]
            k = k_ref[:, sl]
            v = v_ref[:, sl]
            scores = (_dot_nt(q, k) * p).astype(BF16)
            num = wide(w_inter) * _dot(q, c16_ref[s_idx]) + _dot(scores, v)
            den = w_inter * _dot(q, n16_ref[s_idx]) + _dot(scores, ones_rep)
            inv = 1.0 / jnp.maximum(jnp.abs(den), e_neg)
            out_ref[:, sl] = (num * wide(inv)).astype(BF16)

            upd = _dot_tn(k, jnp.concatenate([v * wide(wk16), wk16], axis=1))
            c_new = wc * c_ref[s_idx] + upd[:, :dh]
            c_ref[s_idx] = c_new
            c16_ref[s_idx] = c_new.astype(BF16)
            n_new = wc * n_ref[s_idx] + upd[:, dh:]
            n_ref[s_idx] = n_new
            n16_ref[s_idx] = n_new.astype(BF16)


def _mlstm(zm, zg, zg_t, bias_c, bias_r, *, b, s, width):
    r = zm.shape[0]
    L = CHUNK
    n_lat = s // L
    lat_blocks = b * n_lat

    def blk_f(bi, i):
        return jnp.where(i == 0, lat_blocks + bi, bi * n_lat + i - 1)

    def blk_b(bi, i):
        return jnp.where(i == 0, lat_blocks + bi, bi * n_lat + n_lat - i)

    def specs(blk):
        return [pl.BlockSpec((L, width), lambda bi, i: (blk(bi, i), 0)),
                pl.BlockSpec((L, width), lambda bi, i: (blk(bi, i), 1)),
                pl.BlockSpec((L, width), lambda bi, i: (blk(bi, i), 2)),
                pl.BlockSpec((L, GATE_PAD), lambda bi, i: (blk(bi, i), 0)),
                pl.BlockSpec((16, L), lambda bi, i: (0, blk(bi, i)))]

    kern = functools.partial(_mlstm_kernel, dh=width // M_HEADS)
    out = jax.ShapeDtypeStruct((r, width), BF16)
    return pl.pallas_call(
        kern,
        out_shape=(out, out),
        grid=(b, n_lat + 1),
        in_specs=specs(blk_f) + specs(blk_b) + [pl.BlockSpec((1, GATE_PAD), lambda bi, i: (0, 0)),
                                                pl.BlockSpec((16, L), lambda bi, i: (0, 0))],
        out_specs=(pl.BlockSpec((L, width), lambda bi, i: (blk_f(bi, i), 0)),
                   pl.BlockSpec((L, width), lambda bi, i: (blk_b(bi, i), 0))),
        scratch_shapes=[pltpu.VMEM((2 * M_HEADS, width // M_HEADS, width // M_HEADS), F32),
                        pltpu.VMEM((2 * M_HEADS, width // M_HEADS, width // M_HEADS), BF16),
                        pltpu.VMEM((2 * M_HEADS, width // M_HEADS, GATE_PAD), F32),
                        pltpu.VMEM((2 * M_HEADS, width // M_HEADS, GATE_PAD), BF16),
                        pltpu.VMEM((8, GATE_PAD), F32)],
        compiler_params=_cparams(("arbitrary", "arbitrary")),
        name="mlstm",
    )(zm, zm, zm, zg, zg_t, zm, zm, zm, zg, zg_t, bias_c, bias_r)


POOL_BLOCK = 256


def _band(n, w, seg):
    r = lax.broadcasted_iota(jnp.int32, (n, n), 0)
    c = lax.broadcasted_iota(jnp.int32, (n, n), 1)
    d = c - r
    shift = seg.bit_length() - 1
    ok = (d >= -(w // 2)) & (d <= w // 2 - 1) & ((r >> shift) == (c >> shift))
    return jnp.where(ok, 1.0, 0.0).astype(BF16)


def _window_count(idx, n, w):
    return (jnp.minimum(idx + w // 2, n) - jnp.maximum(idx - w // 2, 0)).astype(F32)


def _pool_kernel(z_ref, o_ref, pad_ref, *, grid_rows):
    n = z_ref.shape[0]
    blk = min(POOL_BLOCK, n)
    seg = blk if grid_rows is None else GRID_W
    seg_len = n if grid_rows is None else GRID_W
    halo = (max(POOL_WINDOWS) // 2) * GRID_W
    rowi = lax.broadcasted_iota(jnp.int32, (blk, POOL_GROUP_DIM), 0)
    if grid_rows is not None:
        zeros = jnp.zeros((halo, POOL_GROUP_DIM), F32)
        pad_ref[0:halo, :] = zeros
        pad_ref[halo + n:, :] = zeros
    for g, w in enumerate(POOL_WINDOWS):
        sl = slice(g * POOL_GROUP_DIM, (g + 1) * POOL_GROUP_DIM)
        band = _band(blk, w, seg)
        inv_w = 1.0 / _window_count(rowi & (seg - 1), seg_len, w)
        for b in range(n // blk):
            rows = slice(b * blk, (b + 1) * blk)
            x = z_ref[rows, sl]
            hi = x.astype(BF16)
            lo = (x - hi.astype(F32)).astype(BF16)
            y2 = _dot(band, jnp.concatenate([hi, lo], axis=1))
            y = (y2[:, :POOL_GROUP_DIM] + y2[:, POOL_GROUP_DIM:]) * inv_w
            if grid_rows is None:
                o_ref[rows, sl] = (y - x).astype(BF16)
            else:
                pad_ref[halo + b * blk:halo + (b + 1) * blk, :] = y
        if grid_rows is not None:
            shift = GRID_W.bit_length() - 1
            for b in range(n // blk):
                acc = None
                for d in range(-(w // 2), w // 2):
                    lo_r = halo + b * blk + d * GRID_W
                    t = pad_ref[lo_r:lo_r + blk, :]
                    acc = t if acc is None else acc + t
                inv_h = 1.0 / _window_count((rowi + b * blk) >> shift, grid_rows, w)
                rows = slice(b * blk, (b + 1) * blk)
                o_ref[rows, sl] = (acc * inv_h - z_ref[rows, sl]).astype(BF16)


def _pool(zp, *, n_seq, seq_len, row_block0, grid_rows):
    width = zp.shape[1]
    assert seq_len % POOL_BLOCK == 0 if grid_rows is not None else seq_len <= POOL_BLOCK
    kern = functools.partial(_pool_kernel, grid_rows=grid_rows)
    halo = (max(POOL_WINDOWS) // 2) * GRID_W
    return pl.pallas_call(
        kern,
        out_shape=jax.ShapeDtypeStruct((n_seq * seq_len, width), BF16),
        grid=(n_seq,),
        in_specs=[pl.BlockSpec((seq_len, width), lambda s: (row_block0 + s, 0))],
        out_specs=pl.BlockSpec((seq_len, width), lambda s: (s, 0)),
        scratch_shapes=[pltpu.VMEM((seq_len + 2 * halo, POOL_GROUP_DIM), F32)],
        compiler_params=_cparams(("arbitrary",)),
        name="pool",
    )(zp)


def _mixout_kernel(hf_ref, hb_ref, o_ref, ga_ref, gb_ref, pml_ref, pmc_ref, xa_ref, xb_ref, gt_ref, hg_ref, wpool_ref,
                   ps_ref, wpa_ref, wpb_ref, wout_ref, out_ref, *, dh, n_lat_tiles):
    is_ctx = pl.program_id(0) >= n_lat_tiles
    hm = hf_ref[...].astype(F32) + hb_ref[...].astype(F32)
    parts = []
    for h in range(M_HEADS):
        t = hm[:, h * dh:(h + 1) * dh]
        mu = jnp.mean(t, axis=-1, keepdims=True)
        tc = t - mu
        var = jnp.mean(tc * tc, axis=-1, keepdims=True)
        parts.append(tc * lax.rsqrt(var + EPS))
    hn = jnp.concatenate(parts, axis=-1)
    hn = _sigmoid(o_ref[...].astype(F32)) * (hn * hg_ref[...])
    a = _dot(hn.astype(BF16), wpa_ref[...])
    pm_in = jnp.where(is_ctx, pmc_ref[...], pml_ref[...])
    pparts = []
    for g in range(len(POOL_WINDOWS)):
        sl = slice(g * POOL_GROUP_DIM, (g + 1) * POOL_GROUP_DIM)
        pparts.append(_dot(pm_in[:, sl], wpool_ref[g]))
    pm = jnp.concatenate(pparts, axis=-1) * ps_ref[...]
    bmat = _dot(pm.astype(BF16), wpb_ref[...])
    merged = _sigmoid(ga_ref[...].astype(F32)) * a + _sigmoid(gb_ref[...].astype(F32)) * bmat
    y = _dot(merged.astype(BF16), wout_ref[...])
    out_ref[...] = jnp.where(is_ctx, xb_ref[...], xa_ref[...]) + gt_ref[0] * y


def _mixout(hf, hb, zm, pm_lat, pm_ctx, src, mod3, head_gain, w_pool, pool_scale, w_pa, w_pb, w_out, mod_idx, *, tm,
            n_rows):
    xa, xb, b_off = src
    d = xa.shape[1]
    width = hf.shape[1]
    pw = pm_lat.shape[1]
    n_lat_tiles = pm_lat.shape[0] // tm
    o_blk = 3
    full = lambda shape: pl.BlockSpec(shape, lambda i: (0,) * len(shape))
    kern = functools.partial(_mixout_kernel, dh=width // M_HEADS, n_lat_tiles=n_lat_tiles)
    return pl.pallas_call(
        kern,
        out_shape=jax.ShapeDtypeStruct((n_rows, d), F32),
        grid=(n_rows // tm,),
        in_specs=[pl.BlockSpec((tm, width), lambda i: (i, 0)),
                  pl.BlockSpec((tm, width), lambda i: (i, 0)),
                  pl.BlockSpec((tm, width), lambda i: (i, o_blk)),
                  pl.BlockSpec((tm, d), lambda i: (i, o_blk + 1)),
                  pl.BlockSpec((tm, d), lambda i: (i, o_blk + 2)),
                  pl.BlockSpec((tm, pw), lambda i: (jnp.minimum(i, n_lat_tiles - 1), 0)),
                  pl.BlockSpec((tm, pw), lambda i: (jnp.maximum(i - n_lat_tiles, 0), 0)),
                  pl.BlockSpec((tm, d), lambda i: (jnp.minimum(i, n_lat_tiles - 1), 0)),
                  pl.BlockSpec((tm, d), lambda i: (jnp.maximum(i - n_lat_tiles, 0) + b_off // tm, 0)),
                  pl.BlockSpec((1, 1, d), lambda i: (mod_idx(i) * N_MOD + 2, 0, 0)),
                  full((1, width)), full(w_pool.shape), full((1, pw)),
                  full(w_pa.shape), full(w_pb.shape), full(w_out.shape)],
        out_specs=pl.BlockSpec((tm, d), lambda i: (i, 0)),
        compiler_params=_cparams(("arbitrary",)),
        name="mixout",
    )(hf, hb, zm, zm, zm, pm_lat, pm_ctx, xa, xb, mod3, head_gain, w_pool, pool_scale, w_pa, w_pb, w_out)


SWIGLU_SUB = 512


def _swiglu_partial(u_ref, wg, wu, wd):
    tf = wg.shape[1]
    acc = None
    for c0 in range(0, tf, SWIGLU_SUB):
        cs = slice(c0, min(c0 + SWIGLU_SUB, tf))
        u = u_ref[...]
        act = (_silu(_dot(u, wg[:, cs])) * _dot(u, wu[:, cs])).astype(BF16)
        part = _dot(act, wd[cs, :])
        acc = part if acc is None else acc + part
    return acc


def _ffn_kernel(x_ref, g_ref, sc_ref, sh_ref, gt_ref, wg_ref, wu_ref, wd_ref, out_ref, u_ref):
    j = pl.program_id(1)
    u_ref[...] = _rms_mod(x_ref[...], g_ref[...], sc_ref[0], sh_ref[0]).astype(BF16)
    contrib = gt_ref[0] * _swiglu_partial(u_ref, wg_ref, wu_ref, wd_ref)

    @pl.when(j == 0)
    def _():
        out_ref[...] = x_ref[...] + contrib

    @pl.when(j > 0)
    def _():
        out_ref[...] += contrib


def _ffn(h, g, mod3, wg, wu, wd, mod_idx, *, tm, tf):
    r, d = h.shape
    f = wg.shape[1]
    return pl.pallas_call(
        _ffn_kernel,
        out_shape=jax.ShapeDtypeStruct((r, d), F32),
        grid=(r // tm, f // tf),
        in_specs=[pl.BlockSpec((tm, d), lambda i, j: (i, 0)),
                  pl.BlockSpec((1, d), lambda i, j: (0, 0)),
                  pl.BlockSpec((1, 1, d), lambda i, j: (mod_idx(i) * N_MOD + 4, 0, 0)),
                  pl.BlockSpec((1, 1, d), lambda i, j: (mod_idx(i) * N_MOD + 3, 0, 0)),
                  pl.BlockSpec((1, 1, d), lambda i, j: (mod_idx(i) * N_MOD + 5, 0, 0)),
                  pl.BlockSpec((d, tf), lambda i, j: (0, j)),
                  pl.BlockSpec((d, tf), lambda i, j: (0, j)),
                  pl.BlockSpec((tf, d), lambda i, j: (j, 0))],
        out_specs=pl.BlockSpec((tm, d), lambda i, j: (i, 0)),
        scratch_shapes=[pltpu.VMEM((tm, d), BF16)],
        compiler_params=_cparams(("arbitrary", "arbitrary")),
        name="ffn",
    )(h, g, mod3, mod3, mod3, wg, wu, wd)


def _route(u, wr_ref, n_exp):
    u_hi = u.astype(BF16)
    u_mid = (u - u_hi.astype(F32)).astype(BF16)
    w_hi, w_mid, _ = _split3(wr_ref[...])
    both = _dot(u_hi, jnp.concatenate([w_hi, w_mid], axis=1))
    logits = both[:, :GATE_PAD] + both[:, GATE_PAD:] + _dot(u_mid, w_hi)
    lane = lax.broadcasted_iota(jnp.int32, logits.shape, 1)
    lg = jnp.where(lane < n_exp, logits, -jnp.inf)
    m1 = jnp.max(lg, axis=1, keepdims=True)
    i1 = jnp.min(jnp.where(lg == m1, lane, 2 * GATE_PAD), axis=1, keepdims=True)
    lg2 = jnp.where(lane == i1, -jnp.inf, lg)
    m2 = jnp.max(lg2, axis=1, keepdims=True)
    i2 = jnp.min(jnp.where(lg2 == m2, lane, 2 * GATE_PAD), axis=1, keepdims=True)
    e = jnp.exp(m2 - m1)
    p1 = 1.0 / (1.0 + e)
    p2 = e * p1
    return i1, i2, p1, p2


LANE_I1, LANE_I2, LANE_P1, LANE_P2, LANE_R1, LANE_R2 = 0, 1, 2, 3, 4, 5


def _pack_bf16_pairs(x):
    k = x.shape[1] // 2
    lo = lax.bitcast_convert_type(x[:, :k].astype(BF16).astype(F32), jnp.uint32)
    hi = lax.bitcast_convert_type(x[:, k:].astype(BF16).astype(F32), jnp.uint32)
    return (lo >> 16) | (hi & jnp.uint32(0xFFFF0000))


def _unpack_bf16_pairs(w):
    lo = lax.bitcast_convert_type(w << 16, F32)
    hi = lax.bitcast_convert_type(w & jnp.uint32(0xFFFF0000), F32)
    return jnp.concatenate([lo, hi], axis=1)


def _route_kernel(x_ref, g_ref, sc_ref, sh_ref, wr_ref, up_ref, info_ref, cnt_ref, carry_ref, *, n_exp):
    @pl.when(pl.program_id(0) == 0)
    def _():
        carry_ref[...] = jnp.zeros_like(carry_ref)

    u = _rms_mod(x_ref[...], g_ref[...], sc_ref[0], sh_ref[0])
    up_ref[...] = _pack_bf16_pairs(u)
    i1, i2, p1, p2 = _route(u, wr_ref, n_exp)
    tm = u.shape[0]
    lane = lax.broadcasted_iota(jnp.int32, (tm, GATE_PAD), 1)
    sel = jnp.where((lane == i1) | (lane == i2), 1.0, 0.0)
    ri = lax.broadcasted_iota(jnp.int32, (tm, tm), 0)
    ci = lax.broadcasted_iota(jnp.int32, (tm, tm), 1)
    before = jnp.where(ci < ri, 1.0, 0.0).astype(BF16)
    rank = carry_ref[0:1, :] + _dot(before, sel.astype(BF16))
    r1 = jnp.sum(jnp.where(lane == i1, rank, 0.0), axis=1, keepdims=True)
    r2 = jnp.sum(jnp.where(lane == i2, rank, 0.0), axis=1, keepdims=True)
    carry_ref[0:1, :] = carry_ref[0:1, :] + jnp.sum(sel, axis=0, keepdims=True)
    cnt_ref[...] = carry_ref[...]
    info = jnp.zeros((tm, GATE_PAD), F32)
    for ln, val in ((LANE_I1, i1.astype(F32)), (LANE_I2, i2.astype(F32)), (LANE_P1, p1), (LANE_P2, p2),
                    (LANE_R1, r1), (LANE_R2, r2)):
        info = jnp.where(lane == ln, val, info)
    info_ref[...] = info


def _route_call(h, g, mod3, w_router, mod_idx, *, tm, n_rows):
    d = h.shape[1]
    n_exp = w_router.shape[1]
    w_r = jnp.concatenate([w_router, jnp.zeros((d, GATE_PAD - n_exp), F32)], axis=1)
    kern = functools.partial(_route_kernel, n_exp=n_exp)
    return pl.pallas_call(
        kern,
        out_shape=(jax.ShapeDtypeStruct((n_rows, d // 2), jnp.uint32),
                   jax.ShapeDtypeStruct((n_rows, GATE_PAD), F32),
                   jax.ShapeDtypeStruct((8, GATE_PAD), F32)),
        grid=(n_rows // tm,),
        in_specs=[pl.BlockSpec((tm, d), lambda i: (i, 0)),
                  pl.BlockSpec((1, d), lambda i: (0, 0)),
                  pl.BlockSpec((1, 1, d), lambda i: (mod_idx(i) * N_MOD + 4, 0, 0)),
                  pl.BlockSpec((1, 1, d), lambda i: (mod_idx(i) * N_MOD + 3, 0, 0)),
                  pl.BlockSpec((d, GATE_PAD), lambda i: (0, 0))],
        out_specs=(pl.BlockSpec((tm, d // 2), lambda i: (i, 0)),
                   pl.BlockSpec((tm, GATE_PAD), lambda i: (i, 0)),
                   pl.BlockSpec((8, GATE_PAD), lambda i: (0, 0))),
        scratch_shapes=[pltpu.VMEM((8, GATE_PAD), F32)],
        compiler_params=_cparams(("arbitrary",)),
        name="route",
    )(h, g, mod3, mod3, w_r)


def _sc_workers():
    info = pltpu.get_tpu_info().sparse_core
    return info.num_cores, info.num_subcores


def _sc_scatter_rows(x, pos, n_out):
    n, dw = x.shape
    nc, ns = _sc_workers()
    nw = nc * ns
    t_per_w = n // nw
    w = min(SC_WINDOW, t_per_w // 2)
    n_chunks = t_per_w // w
    assert n % nw == 0 and t_per_w % (2 * w) == 0
    pos_w = pos.reshape(2, nw, n_chunks, w).transpose(1, 0, 2, 3)
    mesh = plsc.VectorSubcoreMesh(core_axis_name="c", subcore_axis_name="s")

    @functools.partial(pl.kernel, mesh=mesh, out_type=jax.ShapeDtypeStruct((n_out, dw), x.dtype),
                       scratch_types=[pltpu.VMEM((2, n_chunks, w), jnp.int32), pltpu.VMEM((2, w, dw), x.dtype),
                                      pltpu.SemaphoreType.DMA((2,)), pltpu.SemaphoreType.DMA((2,))])
    def scatter(x_hbm, pos_hbm, out_hbm, idx_v, rows_v, gsem, osem):
        wid = lax.axis_index("s") * nc + lax.axis_index("c")
        base = wid * t_per_w
        pltpu.sync_copy(pos_hbm.at[wid], idx_v)

        def get(g, slot):
            return pltpu.make_async_copy(x_hbm.at[pl.ds(base + g * w, w)], rows_v.at[slot], gsem.at[slot])

        def put(g, slot, k):
            return pltpu.make_async_copy(rows_v.at[slot], out_hbm.at[idx_v.at[k, g]], osem.at[slot])

        get(0, 0).start()

        @pl.loop(0, n_chunks, step=2)
        def _(g):
            for b in range(2):
                gg = g + b
                get(gg, b).wait()

                @pl.when(gg + 1 < n_chunks)
                def _():
                    @pl.when(gg >= 1)
                    def _():
                        put(gg - 1, 1 - b, 0).wait()
                        put(gg - 1, 1 - b, 1).wait()
                    get(gg + 1, 1 - b).start()

                put(gg, b, 0).start()
                put(gg, b, 1).start()

        for k in range(2):
            put(n_chunks - 2, 0, k).wait()
            put(n_chunks - 1, 1, k).wait()

    return scatter(x, pos_w)


def _sc_gather_rows(table, idx):
    n = idx.shape[0]
    dw = table.shape[1]
    nc, ns = _sc_workers()
    nw = nc * ns
    b_per_w = n // nw
    w = min(SC_WINDOW, b_per_w // 2)
    n_chunks = b_per_w // w
    assert n % nw == 0 and b_per_w % (2 * w) == 0
    mesh = plsc.VectorSubcoreMesh(core_axis_name="c", subcore_axis_name="s")

    @functools.partial(pl.kernel, mesh=mesh, out_type=jax.ShapeDtypeStruct((n, dw), table.dtype),
                       scratch_types=[pltpu.VMEM((b_per_w,), jnp.int32), pltpu.VMEM((2, w, dw), table.dtype),
                                      pltpu.SemaphoreType.DMA((2,)), pltpu.SemaphoreType.DMA((2,))])
    def gather(table_hbm, idx_hbm, out_hbm, idx_v, rows_v, gsem, osem):
        wid = lax.axis_index("s") * nc + lax.axis_index("c")
        base = wid * b_per_w
        pltpu.sync_copy(idx_hbm.at[pl.ds(base, b_per_w)], idx_v)

        def get(g, slot):
            return pltpu.make_async_copy(table_hbm.at[idx_v.at[pl.ds(g * w, w)]], rows_v.at[slot], gsem.at[slot])

        def put(g, slot):
            return pltpu.make_async_copy(rows_v.at[slot], out_hbm.at[pl.ds(base + g * w, w)], osem.at[slot])

        get(0, 0).start()

        @pl.loop(0, n_chunks, step=2)
        def _(g):
            for b in range(2):
                gg = g + b
                get(gg, b).wait()

                @pl.when(gg + 1 < n_chunks)
                def _():
                    @pl.when(gg >= 1)
                    def _():
                        put(gg - 1, 1 - b).wait()
                    get(gg + 1, 1 - b).start()

                put(gg, b).start()

        put(n_chunks - 2, 0).wait()
        put(n_chunks - 1, 1).wait()

    return gather(table, idx)


def _gmm_kernel(te_ref, nv_ref, xs_ref, wg_ref, wu_ref, wd_ref, ys_ref, u_ref, acc_ref):
    i = pl.program_id(0)
    j = pl.program_id(1)

    @pl.when(i < nv_ref[0])
    def _():
        @pl.when(j == 0)
        def _():
            u_ref[...] = _unpack_bf16_pairs(xs_ref[...]).astype(BF16)
            acc_ref[...] = jnp.zeros_like(acc_ref)

        acc_ref[...] += _swiglu_partial(u_ref, wg_ref.at[0], wu_ref.at[0], wd_ref.at[0])

        @pl.when(j == pl.num_programs(1) - 1)
        def _():
            ys_ref[...] = _pack_bf16_pairs(acc_ref[...])


def _gmm(xs, tile_expert, n_valid, wg, wu, wd, *, tg, tf):
    p, dw = xs.shape
    d = 2 * dw
    f = wg.shape[2]
    n_f = f // tf

    def jj(i, j, nv):
        return jnp.where(i < nv[0], j, n_f - 1)

    grid_spec = pltpu.PrefetchScalarGridSpec(
        num_scalar_prefetch=2,
        grid=(p // tg, n_f),
        in_specs=[pl.BlockSpec((tg, dw), lambda i, j, te, nv: (i, 0)),
                  pl.BlockSpec((1, d, tf), lambda i, j, te, nv: (te[i], 0, jj(i, j, nv))),
                  pl.BlockSpec((1, d, tf), lambda i, j, te, nv: (te[i], 0, jj(i, j, nv))),
                  pl.BlockSpec((1, tf, d), lambda i, j, te, nv: (te[i], jj(i, j, nv), 0))],
        out_specs=pl.BlockSpec((tg, dw), lambda i, j, te, nv: (i, 0)),
        scratch_shapes=[pltpu.VMEM((tg, d), BF16), pltpu.VMEM((tg, d), F32)])
    return pl.pallas_call(
        _gmm_kernel,
        out_shape=jax.ShapeDtypeStruct((p, dw), jnp.uint32),
        grid_spec=grid_spec,
        compiler_params=_cparams(("arbitrary", "arbitrary")),
        name="gmm",
    )(tile_expert, n_valid, xs, wg, wu, wd)


def _combine_kernel(x_ref, info_ref, y1_ref, y2_ref, gt_ref, gf_ref, out_ref, *, final):
    info = info_ref[...]
    p1 = info[:, LANE_P1:LANE_P1 + 1]
    p2 = info[:, LANE_P2:LANE_P2 + 1]
    moe = p1 * _unpack_bf16_pairs(y1_ref[...]) + p2 * _unpack_bf16_pairs(y2_ref[...])
    y = x_ref[...] + gt_ref[0] * moe
    if final:
        r = lax.rsqrt(jnp.mean(y * y, axis=-1, keepdims=True) + EPS)
        y = (y * r) * gf_ref[...]
    out_ref[...] = y


def _combine(h, info, yg, mod3, g_final, mod_idx, *, tm, n_rows, final):
    d = h.shape[1]
    n_t = n_rows // tm
    kern = functools.partial(_combine_kernel, final=final)
    return pl.pallas_call(
        kern,
        out_shape=jax.ShapeDtypeStruct((n_rows, d), F32),
        grid=(n_t,),
        in_specs=[pl.BlockSpec((tm, d), lambda i: (i, 0)),
                  pl.BlockSpec((tm, GATE_PAD), lambda i: (i, 0)),
                  pl.BlockSpec((tm, d // 2), lambda i: (i, 0)),
                  pl.BlockSpec((tm, d // 2), lambda i: (n_t + i, 0)),
                  pl.BlockSpec((1, 1, d), lambda i: (mod_idx(i) * N_MOD + 5, 0, 0)),
                  pl.BlockSpec((1, d), lambda i: (0, 0))],
        out_specs=pl.BlockSpec((tm, d), lambda i: (i, 0)),
        compiler_params=_cparams(("arbitrary",)),
        name="combine",
    )(h, info, yg, yg, mod3, g_final)


def _moe(h, g, mod3, w_router, wg, wu, wd, g_final, mod_idx, *, tm, tg, tf, n_rows, final):
    n_exp = wg.shape[0]
    up, info, cnt = _route_call(h, g, mod3, w_router, mod_idx, tm=tm, n_rows=n_rows)
    i1 = info[:, LANE_I1].astype(jnp.int32)
    i2 = info[:, LANE_I2].astype(jnp.int32)
    r1 = info[:, LANE_R1].astype(jnp.int32)
    r2 = info[:, LANE_R2].astype(jnp.int32)
    counts = cnt[0, :n_exp].astype(jnp.int32)
    padded = (counts + tg - 1) // tg * tg
    ends = jnp.cumsum(padded)
    starts = ends - padded
    eye = jnp.arange(n_exp, dtype=jnp.int32)
    pos1 = jnp.sum(jnp.where(i1[:, None] == eye, starts, 0), axis=1) + r1
    pos2 = jnp.sum(jnp.where(i2[:, None] == eye, starts, 0), axis=1) + r2
    p_rows = -(-(TOP_K * n_rows + n_exp * (tg - 1)) // tg) * tg
    tile_row = jnp.arange(p_rows // tg, dtype=jnp.int32) * tg
    tile_expert = jnp.minimum(jnp.sum(tile_row[:, None] >= ends[None, :], axis=1), n_exp - 1).astype(jnp.int32)
    n_valid = (ends[-1] // tg).reshape(1).astype(jnp.int32)
    xs = _sc_scatter_rows(up, jnp.stack([pos1, pos2]), p_rows)
    ys = _gmm(xs, tile_expert, n_valid, wg, wu, wd, tg=tg, tf=tf)
    yg = _sc_gather_rows(ys, jnp.concatenate([pos1, pos2]))
    return _combine(h, info, yg, mod3, g_final, mod_idx, tm=tm, n_rows=n_rows, final=final)


def _final_norm_kernel(x_ref, g_ref, o_ref):
    x = x_ref[...]
    r = lax.rsqrt(jnp.mean(x * x, axis=-1, keepdims=True) + EPS)
    o_ref[...] = (x * r) * g_ref[...]


def _final_norm(h, g, *, tm, n_rows):
    d = h.shape[1]
    return pl.pallas_call(
        _final_norm_kernel,
        out_shape=jax.ShapeDtypeStruct((n_rows, d), F32),
        grid=(n_rows // tm,),
        in_specs=[pl.BlockSpec((tm, d), lambda i: (i, 0)), pl.BlockSpec((1, d), lambda i: (0, 0))],
        out_specs=pl.BlockSpec((tm, d), lambda i: (i, 0)),
        compiler_params=_cparams(("arbitrary",)),
        name="final_norm",
    )(h, g)


def kernel(x, c, ctx, c_ctx, w_ada, b_ada, g_mix, w_in, conv_qk, b_if, head_gain, w_pool, pool_scale, w_pa, w_pb,
           w_out, g_ffn, w_ff_gate, w_ff_up, w_ff_down, w_router, w_exp_gate, w_exp_up, w_exp_down, g_final):
    b, s, d = x.shape
    lc = ctx.shape[1]
    depth = w_ada.shape[0]
    width = w_pa.shape[1]
    pw = w_pb.shape[1]
    n_gate = b_if.shape[1] * b_if.shape[2] * b_if.shape[3]
    assert lc == CHUNK and s % CHUNK == 0 and s % GRID_W == 0 and width == d and n_gate == 16
    n_lat, n_ctx = b * s, b * lc
    tm, tm_ffn = 512, 512
    assert s % tm_ffn == 0 and n_ctx % tm_ffn == 0 and tm_ffn % tm == 0

    def mod_idx(t):
        return lambda i: jnp.where(i < n_lat // t, i // (s // t), b)

    src = (x.reshape(n_lat, d), ctx.reshape(n_ctx, d), 0)

    n_mod_rows = 16
    cc = jnp.zeros((n_mod_rows, d), F32).at[:b].set(c).at[b].set(c_ctx)
    mod = _ada(cc, w_ada, b_ada)

    if_off, pool_off, ga_off = 4 * width, 4 * width + n_gate, 4 * width + n_gate + pw
    n_main = 4 * width + 2 * d

    for l in range(depth):
        last = l == depth - 1
        mod3 = mod[l].reshape(n_mod_rows * N_MOD, 1, d)
        w_l = w_in[l]
        w_cat = jnp.concatenate(
            [w_l[:, :if_off], w_l[:, ga_off:], w_l[:, pool_off:ga_off], w_l[:, if_off:pool_off],
             jnp.zeros((d, GATE_PAD - n_gate), F32)], axis=1).astype(BF16)
        k_scale = float((width // M_HEADS) ** -0.5)
        zm, zp, zg = _inproj(src, g_mix[l].reshape(1, d), mod3, w_cat, conv_qk[l], mod_idx(tm), tm=tm, n_main=n_main,
                             n_pool=pw, n_qk=2 * width, k_scale=k_scale, n_lat=n_lat, lat_len=s, ctx_len=lc)

        bias = b_if[l].reshape(n_gate)
        bias_c = jnp.zeros((1, GATE_PAD), F32).at[0, :n_gate].set(bias)
        bias_r = jnp.broadcast_to(bias[:, None], (n_gate, CHUNK))
        hf, hb = _mlstm(zm, zg, zg[:, :n_gate].T, bias_c, bias_r, b=b, s=s, width=width)

        n_rows = n_lat if last else n_lat + n_ctx
        pm_lat = _pool(zp, n_seq=b, seq_len=s, row_block0=0, grid_rows=s // GRID_W)
        pm_ctx = pm_lat if last else _pool(zp, n_seq=b, seq_len=lc, row_block0=n_lat // lc, grid_rows=None)

        h = _mixout(hf, hb, zm, pm_lat, pm_ctx, src, mod3, head_gain[l].reshape(1, width), w_pool[l].astype(BF16),
                    pool_scale[l].reshape(1, pw), w_pa[l].astype(BF16), w_pb[l].astype(BF16),
                    w_out[l].astype(BF16), mod_idx(tm), tm=tm, n_rows=n_rows)

        j = l // 2
        if l % 2 == 0:
            h = _ffn(h, g_ffn[l].reshape(1, d), mod3, w_ff_gate[j].astype(BF16), w_ff_up[j].astype(BF16),
                     w_ff_down[j].astype(BF16), mod_idx(tm_ffn), tm=tm_ffn,
                     tf=_pick_tile(w_ff_gate.shape[2], 2816))
            if last:
                h = _final_norm(h, g_final.reshape(1, d), tm=tm, n_rows=n_lat)
        else:
            h = _moe(h, g_ffn[l].reshape(1, d), mod3, w_router[j], w_exp_gate[j].astype(BF16),
                     w_exp_up[j].astype(BF16), w_exp_down[j].astype(BF16), g_final.reshape(1, d), mod_idx(tm),
                     tm=tm, tg=512, tf=_pick_tile(w_exp_gate.shape[3], 1792), n_rows=h.shape[0], final=last)
        src = (h, h, n_lat)
    return h[:n_lat].reshape(b, s, d)
```

```python
import functools

import jax
import jax.numpy as jnp
from jax import lax
from jax.experimental import pallas as pl
from jax.experimental.pallas import tpu as pltpu
from jax.experimental.pallas import tpu_sc as plsc

F32 = jnp.float32
BF16 = jnp.bfloat16
EPS = 1e-6
M_HEADS = 4
GRID_W = 64
POOL_WINDOWS = (2, 4, 8, 16)
POOL_GROUP_DIM = 128
CHUNK = 256
N_MOD = 6
TOP_K = 2
SC_WINDOW = 64
GATE_PAD = 128
VMEM_LIMIT = 56 * 1024 * 1024


def _pick_tile(n, target, quantum=128):
    t = min(n, target) // quantum * quantum
    while n % t:
        t -= quantum
    return t


def _cparams(sem):
    return pltpu.CompilerParams(dimension_semantics=sem, vmem_limit_bytes=VMEM_LIMIT)


def _sigmoid(x):
    return 0.5 * jnp.tanh(0.5 * x) + 0.5


def _silu(x):
    return x * _sigmoid(x)


def _log_sigmoid(x):
    return jnp.minimum(x, 0.0) - jnp.log(1.0 + jnp.exp(-jnp.abs(x)))


def _split3(x):
    hi = x.astype(BF16)
    r1 = x - hi.astype(F32)
    mid = r1.astype(BF16)
    lo = (r1 - mid.astype(F32)).astype(BF16)
    return hi, mid, lo


def _dot(a, b):
    return jnp.dot(a, b, preferred_element_type=F32)


def _dot_nt(a, b):
    return lax.dot_general(a, b, (((1,), (1,)), ((), ())), preferred_element_type=F32)


def _dot_tn(a, b):
    return lax.dot_general(a, b, (((0,), (0,)), ((), ())), preferred_element_type=F32)


def _rms_mod(x, g, sc, sh):
    r = lax.rsqrt(jnp.mean(x * x, axis=-1, keepdims=True) + EPS)
    return (x * r) * g * (1.0 + sc) + sh


def _ada_kernel(c_ref, w_ref, b_ref, o_ref):
    c = c_ref[...]
    o_ref[0] = _dot(_silu(c).astype(BF16), w_ref[0].astype(BF16)) + b_ref[0]


def _ada(cc, w_ada, b_ada):
    depth, d, n = w_ada.shape
    tn = 1024
    return pl.pallas_call(
        _ada_kernel,
        out_shape=jax.ShapeDtypeStruct((depth, cc.shape[0], n), F32),
        grid=(depth, n // tn),
        in_specs=[pl.BlockSpec(cc.shape, lambda l, j: (0, 0)),
                  pl.BlockSpec((1, d, tn), lambda l, j: (l, 0, j)),
                  pl.BlockSpec((1, 1, tn), lambda l, j: (l, 0, j))],
        out_specs=pl.BlockSpec((1, cc.shape[0], tn), lambda l, j: (l, 0, j)),
        compiler_params=_cparams(("arbitrary", "arbitrary")),
        name="ada",
    )(cc, w_ada, b_ada.reshape(depth, 1, n))


def _inproj_kernel(xa_ref, xb_ref, xpa_ref, xpb_ref, xna_ref, xnb_ref, g_ref, sc_ref, sh_ref, w_ref, cw_ref,
                   zm_ref, zp_ref, zg_ref, u_ref, uh_ref,
                   *, n_main, n_pool, n_qk, cw, k_scale, n_lat_tiles, lat_len, ctx_len):
    i = pl.program_id(0)
    tm = xa_ref.shape[0]
    is_ctx = i >= n_lat_tiles
    g, sc, sh = g_ref[...], sc_ref[0], sh_ref[0]
    u_ref[...] = _rms_mod(jnp.where(is_ctx, xb_ref[...], xa_ref[...]), g, sc, sh).astype(BF16)
    uh_ref[0:8, :] = _rms_mod(jnp.where(is_ctx, xpb_ref[...], xpa_ref[...]), g, sc, sh).astype(BF16)
    uh_ref[8:16, :] = _rms_mod(jnp.where(is_ctx, xnb_ref[...], xna_ref[...]), g, sc, sh).astype(BF16)

    pos0 = lax.rem(i * tm, lat_len)
    lat_first_row = jnp.where(pos0 == 0, 0, -1)
    lat_last_row = jnp.where(pos0 + tm == lat_len, tm - 1, -1)
    sub = 16
    srow = lax.broadcasted_iota(jnp.int32, (sub, cw), 0)
    slabs = sorted({0, tm - sub} | {k * ctx_len - sub for k in range(1, tm // ctx_len)}
                   | {k * ctx_len for k in range(1, tm // ctx_len)})

    for c in range(n_qk // cw):
        cs = slice(c * cw, (c + 1) * cw)
        w0, w1, w2 = cw_ref[0:1, cs], cw_ref[1:2, cs], cw_ref[2:3, cs]
        scale = k_scale if c * cw >= n_qk // 2 else 1.0
        z = _dot(u_ref[...], w_ref[:, cs])
        zh = _dot(uh_ref[...], w_ref[:, cs])
        y = _silu(pltpu.roll(z, 1, 0) * w0 + z * w1 + pltpu.roll(z, tm - 1, 0) * w2) * scale
        zm_ref[:, cs] = y.astype(BF16)
        for r0 in slabs:
            zs = z[r0:r0 + sub, :]
            above = zh[7:8, :] if r0 == 0 else z[r0 - 1:r0, :]
            below = zh[8:9, :] if r0 + sub == tm else z[r0 + sub:r0 + sub + 1, :]
            grow = srow + r0
            in_ctx = grow & (ctx_len - 1)
            first = jnp.where(is_ctx, in_ctx, grow - lat_first_row) == 0
            last = jnp.where(is_ctx, in_ctx - (ctx_len - 1), grow - lat_last_row) == 0
            zm1 = jnp.where(srow == 0, above, pltpu.roll(zs, 1, 0))
            zm1 = jnp.where(first, 0.0, zm1)
            zp1 = jnp.where(srow == sub - 1, below, pltpu.roll(zs, sub - 1, 0))
            zp1 = jnp.where(last, 0.0, zp1)
            ys = _silu(zm1 * w0 + zs * w1 + zp1 * w2) * scale
            zm_ref[r0:r0 + sub, cs] = ys.astype(BF16)
    for c in range(n_qk // cw, n_main // cw):
        zm_ref[:, c * cw:(c + 1) * cw] = _dot(u_ref[...], w_ref[:, c * cw:(c + 1) * cw]).astype(BF16)
    for c in range(n_pool // cw):
        lo = n_main + c * cw
        zp_ref[:, c * cw:(c + 1) * cw] = _dot(u_ref[...], w_ref[:, lo:lo + cw])
    zg_ref[...] = _dot(u_ref[...], w_ref[:, n_main + n_pool:])


def _inproj(src, g, mod3, w_cat, conv_w, mod_idx, *, tm, n_main, n_pool, n_qk, k_scale, n_lat, lat_len, ctx_len):
    a, b, b_off = src
    d = a.shape[1]
    r = n_lat + b.shape[0] - b_off
    ncols = w_cat.shape[1]
    halo = 8
    assert tm % ctx_len == 0 and lat_len % tm == 0 and ctx_len & (ctx_len - 1) == 0 and b_off % tm == 0
    na_t, hb = n_lat // tm, tm // halo
    a_last, b_last = a.shape[0] // halo - 1, b.shape[0] // halo - 1
    kern = functools.partial(_inproj_kernel, n_main=n_main, n_pool=n_pool, n_qk=n_qk, cw=512, k_scale=k_scale,
                             n_lat_tiles=na_t, lat_len=lat_len, ctx_len=ctx_len)
    return pl.pallas_call(
        kern,
        out_shape=(jax.ShapeDtypeStruct((r, n_main), BF16),
                   jax.ShapeDtypeStruct((r, n_pool), F32),
                   jax.ShapeDtypeStruct((r, GATE_PAD), F32)),
        grid=(r // tm,),
        in_specs=[pl.BlockSpec((tm, d), lambda i: (jnp.minimum(i, na_t - 1), 0)),
                  pl.BlockSpec((tm, d), lambda i: (jnp.maximum(i - na_t, 0) + b_off // tm, 0)),
                  pl.BlockSpec((halo, d), lambda i: (jnp.clip(i * hb - 1, 0, a_last), 0)),
                  pl.BlockSpec((halo, d), lambda i: (jnp.clip(b_off // halo + (i - na_t) * hb - 1, 0, b_last), 0)),
                  pl.BlockSpec((halo, d), lambda i: (jnp.clip((i + 1) * hb, 0, a_last), 0)),
                  pl.BlockSpec((halo, d), lambda i: (jnp.clip(b_off // halo + (i - na_t + 1) * hb, 0, b_last), 0)),
                  pl.BlockSpec((1, d), lambda i: (0, 0)),
                  pl.BlockSpec((1, 1, d), lambda i: (mod_idx(i) * N_MOD + 1, 0, 0)),
                  pl.BlockSpec((1, 1, d), lambda i: (mod_idx(i) * N_MOD + 0, 0, 0)),
                  pl.BlockSpec((d, ncols), lambda i: (0, 0)),
                  pl.BlockSpec((3, n_qk), lambda i: (0, 0))],
        out_specs=(pl.BlockSpec((tm, n_main), lambda i: (i, 0)),
                   pl.BlockSpec((tm, n_pool), lambda i: (i, 0)),
                   pl.BlockSpec((tm, GATE_PAD), lambda i: (i, 0))),
        scratch_shapes=[pltpu.VMEM((tm, d), BF16), pltpu.VMEM((2 * halo, d), BF16)],
        compiler_params=_cparams(("arbitrary",)),
        name="inproj",
    )(a, b, a, b, a, b, g, mod3, mod3, w_cat, conv_w)


def _running_max_rows(x, rev):
    n = x.shape[0]
    row = lax.broadcasted_iota(jnp.int32, x.shape, 0)
    sh = 1
    while sh < n:
        if rev:
            shifted = jnp.where(row < n - sh, pltpu.roll(x, n - sh, 0), -jnp.inf)
        else:
            shifted = jnp.where(row >= sh, pltpu.roll(x, sh, 0), -jnp.inf)
        x = jnp.maximum(x, shifted)
        sh *= 2
    return x


def _mlstm_kernel(qf, kf, vf, gcf, grf, qb, kb, vb, gcb, grb, bias_c, bias_r, hf_out, hb_out,
                  c_ref, c16_ref, n_ref, n16_ref, m_ref, *, dh):
    L = CHUNK

    @pl.when(pl.program_id(1) == 0)
    def _():
        c_ref[...] = jnp.zeros_like(c_ref)
        c16_ref[...] = jnp.zeros_like(c16_ref)
        n_ref[...] = jnp.zeros_like(n_ref)
        n16_ref[...] = jnp.zeros_like(n16_ref)
        m_ref[...] = jnp.zeros_like(m_ref)

    ri = lax.broadcasted_iota(jnp.int32, (L, L), 0)
    ci = lax.broadcasted_iota(jnp.int32, (L, L), 1)
    lower = ci <= ri
    upper = ci >= ri
    lower_b = jnp.where(lower, 1.0, 0.0).astype(BF16)
    upper_b = jnp.where(upper, 1.0, 0.0).astype(BF16)

    dirs = ((qf, kf, vf, gcf, grf, hf_out), (qb, kb, vb, gcb, grb, hb_out))
    for d, (q_ref, k_ref, v_ref, gc_ref, gr_ref, out_ref) in enumerate(dirs):
        rev = d == 1
        gc = gc_ref[...] + bias_c[...]
        gr = gr_ref[...] + bias_r[...]
        fc = _log_sigmoid(gc)
        fr = _log_sigmoid(gr)
        t_col = upper_b if rev else lower_b
        t_row = lower_b if rev else upper_b
        bc3 = _dot(t_col, jnp.concatenate(_split3(fc), axis=1))
        bcum_c = bc3[:, :GATE_PAD] + bc3[:, GATE_PAD:2 * GATE_PAD] + bc3[:, 2 * GATE_PAD:]
        br3 = _dot(jnp.concatenate(_split3(fr), axis=0), t_row)
        bcum_r = br3[0:16, :] + br3[16:32, :] + br3[32:48, :]
        mask = upper if rev else lower
        end = 0 if rev else L - 1

        b_al = pltpu.roll(bcum_c, GATE_PAD - M_HEADS, axis=1)
        g_col = gc - b_al
        m_st = m_ref[d:d + 1, :]
        mx = jnp.maximum(m_st, _running_max_rows(g_col, rev))
        e_neg_all = jnp.exp(-(b_al + mx))
        mx_end = mx[end:end + 1, :]
        wk_all = jnp.exp(g_col - mx_end)
        wc_all = jnp.exp(m_st - mx_end)
        m_ref[d:d + 1, :] = b_al[end:end + 1, :] + mx_end

        ones_rep = jnp.ones((L, GATE_PAD), BF16)
        for h in range(M_HEADS):
            s_idx = d * M_HEADS + h
            col = d * 2 * M_HEADS + h

            def rep(x_all):
                return jnp.broadcast_to(x_all[:, col:col + 1], (x_all.shape[0], GATE_PAD))

            def wide(x_rep):
                return jnp.concatenate([x_rep] * (dh // GATE_PAD), axis=1)

            mx_rep = rep(mx)
            g_row = gr[col:col + 1, :] - bcum_r[col + M_HEADS:col + M_HEADS + 1, :]
            p = jnp.exp(jnp.where(mask, g_row - wide(mx_rep), -jnp.inf))
            w_inter = jnp.exp(rep(m_st) - mx_rep)
            e_neg = rep(e_neg_all)
            wk16 = rep(wk_all).astype(BF16)
            wc = wc_all[:, col:col + 1]

            sl = slice(h * dh, (h + 1) * dh)
            q = q_ref[:, sl]
            k = k_ref[:, sl]
            v = v_ref[:, sl]
            scores = (_dot_nt(q, k) * p).astype(BF16)
            num = wide(w_inter) * _dot(q, c16_ref[s_idx]) + _dot(scores, v)
            den = w_inter * _dot(q, n16_ref[s_idx]) + _dot(scores, ones_rep)
            inv = 1.0 / jnp.maximum(jnp.abs(den), e_neg)
            out_ref[:, sl] = (num * wide(inv)).astype(BF16)

            upd = _dot_tn(k, jnp.concatenate([v * wide(wk16), wk16], axis=1))
            c_new = wc * c_ref[s_idx] + upd[:, :dh]
            c_ref[s_idx] = c_new
            c16_ref[s_idx] = c_new.astype(BF16)
            n_new = wc * n_ref[s_idx] + upd[:, dh:]
            n_ref[s_idx] = n_new
            n16_ref[s_idx] = n_new.astype(BF16)


def _mlstm(zm, zg, zg_t, bias_c, bias_r, *, b, s, width):
    r = zm.shape[0]
    L = CHUNK
    n_lat = s // L
    lat_blocks = b * n_lat

    def blk_f(bi, i):
        return jnp.where(i == 0, lat_blocks + bi, bi * n_lat + i - 1)

    def blk_b(bi, i):
        return jnp.where(i == 0, lat_blocks + bi, bi * n_lat + n_lat - i)

    def specs(blk):
        return [pl.BlockSpec((L, width), lambda bi, i: (blk(bi, i), 0)),
                pl.BlockSpec((L, width), lambda bi, i: (blk(bi, i), 1)),
                pl.BlockSpec((L, width), lambda bi, i: (blk(bi, i), 2)),
                pl.BlockSpec((L, GATE_PAD), lambda bi, i: (blk(bi, i), 0)),
                pl.BlockSpec((16, L), lambda bi, i: (0, blk(bi, i)))]

    kern = functools.partial(_mlstm_kernel, dh=width // M_HEADS)
    out = jax.ShapeDtypeStruct((r, width), BF16)
    return pl.pallas_call(
        kern,
        out_shape=(out, out),
        grid=(b, n_lat + 1),
        in_specs=specs(blk_f) + specs(blk_b) + [pl.BlockSpec((1, GATE_PAD), lambda bi, i: (0, 0)),
                                                pl.BlockSpec((16, L), lambda bi, i: (0, 0))],
        out_specs=(pl.BlockSpec((L, width), lambda bi, i: (blk_f(bi, i), 0)),
                   pl.BlockSpec((L, width), lambda bi, i: (blk_b(bi, i), 0))),
        scratch_shapes=[pltpu.VMEM((2 * M_HEADS, width // M_HEADS, width // M_HEADS), F32),
                        pltpu.VMEM((2 * M_HEADS, width // M_HEADS, width // M_HEADS), BF16),
                        pltpu.VMEM((2 * M_HEADS, width // M_HEADS, GATE_PAD), F32),
                        pltpu.VMEM((2 * M_HEADS, width // M_HEADS, GATE_PAD), BF16),
                        pltpu.VMEM((8, GATE_PAD), F32)],
        compiler_params=_cparams(("arbitrary", "arbitrary")),
        name="mlstm",
    )(zm, zm, zm, zg, zg_t, zm, zm, zm, zg, zg_t, bias_c, bias_r)


POOL_BLOCK = 256


def _band(n, w, seg):
    r = lax.broadcasted_iota(jnp.int32, (n, n), 0)
    c = lax.broadcasted_iota(jnp.int32, (n, n), 1)
    d = c - r
    shift = seg.bit_length() - 1
    ok = (d >= -(w // 2)) & (d <= w // 2 - 1) & ((r >> shift) == (c >> shift))
    return jnp.where(ok, 1.0, 0.0).astype(BF16)


def _window_count(idx, n, w):
    return (jnp.minimum(idx + w // 2, n) - jnp.maximum(idx - w // 2, 0)).astype(F32)


def _pool_kernel(z_ref, o_ref, pad_ref, *, grid_rows):
    n = z_ref.shape[0]
    blk = min(POOL_BLOCK, n)
    seg = blk if grid_rows is None else GRID_W
    seg_len = n if grid_rows is None else GRID_W
    halo = (max(POOL_WINDOWS) // 2) * GRID_W
    rowi = lax.broadcasted_iota(jnp.int32, (blk, POOL_GROUP_DIM), 0)
    if grid_rows is not None:
        zeros = jnp.zeros((halo, POOL_GROUP_DIM), F32)
        pad_ref[0:halo, :] = zeros
        pad_ref[halo + n:, :] = zeros
    for g, w in enumerate(POOL_WINDOWS):
        sl = slice(g * POOL_GROUP_DIM, (g + 1) * POOL_GROUP_DIM)
        band = _band(blk, w, seg)
        inv_w = 1.0 / _window_count(rowi & (seg - 1), seg_len, w)
        for b in range(n // blk):
            rows = slice(b * blk, (b + 1) * blk)
            x = z_ref[rows, sl]
            hi = x.astype(BF16)
            lo = (x - hi.astype(F32)).astype(BF16)
            y2 = _dot(band, jnp.concatenate([hi, lo], axis=1))
            y = (y2[:, :POOL_GROUP_DIM] + y2[:, POOL_GROUP_DIM:]) * inv_w
            if grid_rows is None:
                o_ref[rows, sl] = (y - x).astype(BF16)
            else:
                pad_ref[halo + b * blk:halo + (b + 1) * blk, :] = y
        if grid_rows is not None:
            shift = GRID_W.bit_length() - 1
            for b in range(n // blk):
                acc = None
                for d in range(-(w // 2), w // 2):
                    lo_r = halo + b * blk + d * GRID_W
                    t = pad_ref[lo_r:lo_r + blk, :]
                    acc = t if acc is None else acc + t
                inv_h = 1.0 / _window_count((rowi + b * blk) >> shift, grid_rows, w)
                rows = slice(b * blk, (b + 1) * blk)
                o_ref[rows, sl] = (acc * inv_h - z_ref[rows, sl]).astype(BF16)


def _pool(zp, *, n_seq, seq_len, row_block0, grid_rows):
    width = zp.shape[1]
    assert seq_len % POOL_BLOCK == 0 if grid_rows is not None else seq_len <= POOL_BLOCK
    kern = functools.partial(_pool_kernel, grid_rows=grid_rows)
    halo = (max(POOL_WINDOWS) // 2) * GRID_W
    return pl.pallas_call(
        kern,
        out_shape=jax.ShapeDtypeStruct((n_seq * seq_len, width), BF16),
        grid=(n_seq,),
        in_specs=[pl.BlockSpec((seq_len, width), lambda s: (row_block0 + s, 0))],
        out_specs=pl.BlockSpec((seq_len, width), lambda s: (s, 0)),
        scratch_shapes=[pltpu.VMEM((seq_len + 2 * halo, POOL_GROUP_DIM), F32)],
        compiler_params=_cparams(("arbitrary",)),
        name="pool",
    )(zp)


def _mixout_kernel(hf_ref, hb_ref, o_ref, ga_ref, gb_ref, pml_ref, pmc_ref, xa_ref, xb_ref, gt_ref, hg_ref, wpool_ref,
                   ps_ref, wpa_ref, wpb_ref, wout_ref, out_ref, *, dh, n_lat_tiles):
    is_ctx = pl.program_id(0) >= n_lat_tiles
    hm = hf_ref[...].astype(F32) + hb_ref[...].astype(F32)
    parts = []
    for h in range(M_HEADS):
        t = hm[:, h * dh:(h + 1) * dh]
        mu = jnp.mean(t, axis=-1, keepdims=True)
        tc = t - mu
        var = jnp.mean(tc * tc, axis=-1, keepdims=True)
        parts.append(tc * lax.rsqrt(var + EPS))
    hn = jnp.concatenate(parts, axis=-1)
    hn = _sigmoid(o_ref[...].astype(F32)) * (hn * hg_ref[...])
    a = _dot(hn.astype(BF16), wpa_ref[...])
    pm_in = jnp.where(is_ctx, pmc_ref[...], pml_ref[...])
    pparts = []
    for g in range(len(POOL_WINDOWS)):
        sl = slice(g * POOL_GROUP_DIM, (g + 1) * POOL_GROUP_DIM)
        pparts.append(_dot(pm_in[:, sl], wpool_ref[g]))
    pm = jnp.concatenate(pparts, axis=-1) * ps_ref[...]
    bmat = _dot(pm.astype(BF16), wpb_ref[...])
    merged = _sigmoid(ga_ref[...].astype(F32)) * a + _sigmoid(gb_ref[...].astype(F32)) * bmat
    y = _dot(merged.astype(BF16), wout_ref[...])
    out_ref[...] = jnp.where(is_ctx, xb_ref[...], xa_ref[...]) + gt_ref[0] * y


def _mixout(hf, hb, zm, pm_lat, pm_ctx, src, mod3, head_gain, w_pool, pool_scale, w_pa, w_pb, w_out, mod_idx, *, tm,
            n_rows):
    xa, xb, b_off = src
    d = xa.shape[1]
    width = hf.shape[1]
    pw = pm_lat.shape[1]
    n_lat_tiles = pm_lat.shape[0] // tm
    o_blk = 3
    full = lambda shape: pl.BlockSpec(shape, lambda i: (0,) * len(shape))
    kern = functools.partial(_mixout_kernel, dh=width // M_HEADS, n_lat_tiles=n_lat_tiles)
    return pl.pallas_call(
        kern,
        out_shape=jax.ShapeDtypeStruct((n_rows, d), F32),
        grid=(n_rows // tm,),
        in_specs=[pl.BlockSpec((tm, width), lambda i: (i, 0)),
                  pl.BlockSpec((tm, width), lambda i: (i, 0)),
                  pl.BlockSpec((tm, width), lambda i: (i, o_blk)),
                  pl.BlockSpec((tm, d), lambda i: (i, o_blk + 1)),
                  pl.BlockSpec((tm, d), lambda i: (i, o_blk + 2)),
                  pl.BlockSpec((tm, pw), lambda i: (jnp.minimum(i, n_lat_tiles - 1), 0)),
                  pl.BlockSpec((tm, pw), lambda i: (jnp.maximum(i - n_lat_tiles, 0), 0)),
                  pl.BlockSpec((tm, d), lambda i: (jnp.minimum(i, n_lat_tiles - 1), 0)),
                  pl.BlockSpec((tm, d), lambda i: (jnp.maximum(i - n_lat_tiles, 0) + b_off // tm, 0)),
                  pl.BlockSpec((1, 1, d), lambda i: (mod_idx(i) * N_MOD + 2, 0, 0)),
                  full((1, width)), full(w_pool.shape), full((1, pw)),
                  full(w_pa.shape), full(w_pb.shape), full(w_out.shape)],
        out_specs=pl.BlockSpec((tm, d), lambda i: (i, 0)),
        compiler_params=_cparams(("arbitrary",)),
        name="mixout",
    )(hf, hb, zm, zm, zm, pm_lat, pm_ctx, xa, xb, mod3, head_gain, w_pool, pool_scale, w_pa, w_pb, w_out)


SWIGLU_SUB = 512


def _swiglu_partial(u_ref, wg, wu, wd):
    tf = wg.shape[1]
    acc = None
    for c0 in range(0, tf, SWIGLU_SUB):
        cs = slice(c0, min(c0 + SWIGLU_SUB, tf))
        u = u_ref[...]
        act = (_silu(_dot(u, wg[:, cs])) * _dot(u, wu[:, cs])).astype(BF16)
        part = _dot(act, wd[cs, :])
        acc = part if acc is None else acc + part
    return acc


def _ffn_kernel(x_ref, g_ref, sc_ref, sh_ref, gt_ref, wg_ref, wu_ref, wd_ref, out_ref, u_ref):
    j = pl.program_id(1)
    u_ref[...] = _rms_mod(x_ref[...], g_ref[...], sc_ref[0], sh_ref[0]).astype(BF16)
    contrib = gt_ref[0] * _swiglu_partial(u_ref, wg_ref, wu_ref, wd_ref)

    @pl.when(j == 0)
    def _():
        out_ref[...] = x_ref[...] + contrib

    @pl.when(j > 0)
    def _():
        out_ref[...] += contrib


def _ffn(h, g, mod3, wg, wu, wd, mod_idx, *, tm, tf):
    r, d = h.shape
    f = wg.shape[1]
    return pl.pallas_call(
        _ffn_kernel,
        out_shape=jax.ShapeDtypeStruct((r, d), F32),
        grid=(r // tm, f // tf),
        in_specs=[pl.BlockSpec((tm, d), lambda i, j: (i, 0)),
                  pl.BlockSpec((1, d), lambda i, j: (0, 0)),
                  pl.BlockSpec((1, 1, d), lambda i, j: (mod_idx(i) * N_MOD + 4, 0, 0)),
                  pl.BlockSpec((1, 1, d), lambda i, j: (mod_idx(i) * N_MOD + 3, 0, 0)),
                  pl.BlockSpec((1, 1, d), lambda i, j: (mod_idx(i) * N_MOD + 5, 0, 0)),
                  pl.BlockSpec((d, tf), lambda i, j: (0, j)),
                  pl.BlockSpec((d, tf), lambda i, j: (0, j)),
                  pl.BlockSpec((tf, d), lambda i, j: (j, 0))],
        out_specs=pl.BlockSpec((tm, d), lambda i, j: (i, 0)),
        scratch_shapes=[pltpu.VMEM((tm, d), BF16)],
        compiler_params=_cparams(("arbitrary", "arbitrary")),
        name="ffn",
    )(h, g, mod3, mod3, mod3, wg, wu, wd)


def _route(u, wr_ref, n_exp):
    u_hi = u.astype(BF16)
    u_mid = (u - u_hi.astype(F32)).astype(BF16)
    w_hi, w_mid, _ = _split3(wr_ref[...])
    both = _dot(u_hi, jnp.concatenate([w_hi, w_mid], axis=1))
    logits = both[:, :GATE_PAD] + both[:, GATE_PAD:] + _dot(u_mid, w_hi)
    lane = lax.broadcasted_iota(jnp.int32, logits.shape, 1)
    lg = jnp.where(lane < n_exp, logits, -jnp.inf)
    m1 = jnp.max(lg, axis=1, keepdims=True)
    i1 = jnp.min(jnp.where(lg == m1, lane, 2 * GATE_PAD), axis=1, keepdims=True)
    lg2 = jnp.where(lane == i1, -jnp.inf, lg)
    m2 = jnp.max(lg2, axis=1, keepdims=True)
    i2 = jnp.min(jnp.where(lg2 == m2, lane, 2 * GATE_PAD), axis=1, keepdims=True)
    e = jnp.exp(m2 - m1)
    p1 = 1.0 / (1.0 + e)
    p2 = e * p1
    return i1, i2, p1, p2


LANE_I1, LANE_I2, LANE_P1, LANE_P2, LANE_R1, LANE_R2 = 0, 1, 2, 3, 4, 5


def _pack_bf16_pairs(x):
    k = x.shape[1] // 2
    lo = lax.bitcast_convert_type(x[:, :k].astype(BF16).astype(F32), jnp.uint32)
    hi = lax.bitcast_convert_type(x[:, k:].astype(BF16).astype(F32), jnp.uint32)
    return (lo >> 16) | (hi & jnp.uint32(0xFFFF0000))


def _unpack_bf16_pairs(w):
    lo = lax.bitcast_convert_type(w << 16, F32)
    hi = lax.bitcast_convert_type(w & jnp.uint32(0xFFFF0000), F32)
    return jnp.concatenate([lo, hi], axis=1)


def _route_kernel(x_ref, g_ref, sc_ref, sh_ref, wr_ref, up_ref, info_ref, cnt_ref, carry_ref, *, n_exp):
    @pl.when(pl.program_id(0) == 0)
    def _():
        carry_ref[...] = jnp.zeros_like(carry_ref)

    u = _rms_mod(x_ref[...], g_ref[...], sc_ref[0], sh_ref[0])
    up_ref[...] = _pack_bf16_pairs(u)
    i1, i2, p1, p2 = _route(u, wr_ref, n_exp)
    tm = u.shape[0]
    lane = lax.broadcasted_iota(jnp.int32, (tm, GATE_PAD), 1)
    sel = jnp.where((lane == i1) | (lane == i2), 1.0, 0.0)
    ri = lax.broadcasted_iota(jnp.int32, (tm, tm), 0)
    ci = lax.broadcasted_iota(jnp.int32, (tm, tm), 1)
    before = jnp.where(ci < ri, 1.0, 0.0).astype(BF16)
    rank = carry_ref[0:1, :] + _dot(before, sel.astype(BF16))
    r1 = jnp.sum(jnp.where(lane == i1, rank, 0.0), axis=1, keepdims=True)
    r2 = jnp.sum(jnp.where(lane == i2, rank, 0.0), axis=1, keepdims=True)
    carry_ref[0:1, :] = carry_ref[0:1, :] + jnp.sum(sel, axis=0, keepdims=True)
    cnt_ref[...] = carry_ref[...]
    info = jnp.zeros((tm, GATE_PAD), F32)
    for ln, val in ((LANE_I1, i1.astype(F32)), (LANE_I2, i2.astype(F32)), (LANE_P1, p1), (LANE_P2, p2),
                    (LANE_R1, r1), (LANE_R2, r2)):
        info = jnp.where(lane == ln, val, info)
    info_ref[...] = info


def _route_call(h, g, mod3, w_router, mod_idx, *, tm, n_rows):
    d = h.shape[1]
    n_exp = w_router.shape[1]
    w_r = jnp.concatenate([w_router, jnp.zeros((d, GATE_PAD - n_exp), F32)], axis=1)
    kern = functools.partial(_route_kernel, n_exp=n_exp)
    return pl.pallas_call(
        kern,
        out_shape=(jax.ShapeDtypeStruct((n_rows, d // 2), jnp.uint32),
                   jax.ShapeDtypeStruct((n_rows, GATE_PAD), F32),
                   jax.ShapeDtypeStruct((8, GATE_PAD), F32)),
        grid=(n_rows // tm,),
        in_specs=[pl.BlockSpec((tm, d), lambda i: (i, 0)),
                  pl.BlockSpec((1, d), lambda i: (0, 0)),
                  pl.BlockSpec((1, 1, d), lambda i: (mod_idx(i) * N_MOD + 4, 0, 0)),
                  pl.BlockSpec((1, 1, d), lambda i: (mod_idx(i) * N_MOD + 3, 0, 0)),
                  pl.BlockSpec((d, GATE_PAD), lambda i: (0, 0))],
        out_specs=(pl.BlockSpec((tm, d // 2), lambda i: (i, 0)),
                   pl.BlockSpec((tm, GATE_PAD), lambda i: (i, 0)),
                   pl.BlockSpec((8, GATE_PAD), lambda i: (0, 0))),
        scratch_shapes=[pltpu.VMEM((8, GATE_PAD), F32)],
        compiler_params=_cparams(("arbitrary",)),
        name="route",
    )(h, g, mod3, mod3, w_r)


def _sc_workers():
    info = pltpu.get_tpu_info().sparse_core
    return info.num_cores, info.num_subcores


def _sc_scatter_rows(x, pos, n_out):
    n, dw = x.shape
    nc, ns = _sc_workers()
    nw = nc * ns
    t_per_w = n // nw
    w = min(SC_WINDOW, t_per_w // 2)
    n_chunks = t_per_w // w
    assert n % nw == 0 and t_per_w % (2 * w) == 0
    pos_w = pos.reshape(2, nw, n_chunks, w).transpose(1, 0, 2, 3)
    mesh = plsc.VectorSubcoreMesh(core_axis_name="c", subcore_axis_name="s")

    @functools.partial(pl.kernel, mesh=mesh, out_type=jax.ShapeDtypeStruct((n_out, dw), x.dtype),
                       scratch_types=[pltpu.VMEM((2, n_chunks, w), jnp.int32), pltpu.VMEM((2, w, dw), x.dtype),
                                      pltpu.SemaphoreType.DMA((2,)), pltpu.SemaphoreType.DMA((2,))])
    def scatter(x_hbm, pos_hbm, out_hbm, idx_v, rows_v, gsem, osem):
        wid = lax.axis_index("s") * nc + lax.axis_index("c")
        base = wid * t_per_w
        pltpu.sync_copy(pos_hbm.at[wid], idx_v)

        def get(g, slot):
            return pltpu.make_async_copy(x_hbm.at[pl.ds(base + g * w, w)], rows_v.at[slot], gsem.at[slot])

        def put(g, slot, k):
            return pltpu.make_async_copy(rows_v.at[slot], out_hbm.at[idx_v.at[k, g]], osem.at[slot])

        get(0, 0).start()

        @pl.loop(0, n_chunks, step=2)
        def _(g):
            for b in range(2):
                gg = g + b
                get(gg, b).wait()

                @pl.when(gg + 1 < n_chunks)
                def _():
                    @pl.when(gg >= 1)
                    def _():
                        put(gg - 1, 1 - b, 0).wait()
                        put(gg - 1, 1 - b, 1).wait()
                    get(gg + 1, 1 - b).start()

                put(gg, b, 0).start()
                put(gg, b, 1).start()

        for k in range(2):
            put(n_chunks - 2, 0, k).wait()
            put(n_chunks - 1, 1, k).wait()

    return scatter(x, pos_w)


def _sc_gather_rows(table, idx):
    n = idx.shape[0]
    dw = table.shape[1]
    nc, ns = _sc_workers()
    nw = nc * ns
    b_per_w = n // nw
    w = min(SC_WINDOW, b_per_w // 2)
    n_chunks = b_per_w // w
    assert n % nw == 0 and b_per_w % (2 * w) == 0
    mesh = plsc.VectorSubcoreMesh(core_axis_name="c", subcore_axis_name="s")

    @functools.partial(pl.kernel, mesh=mesh, out_type=jax.ShapeDtypeStruct((n, dw), table.dtype),
                       scratch_types=[pltpu.VMEM((b_per_w,), jnp.int32), pltpu.VMEM((2, w, dw), table.dtype),
                                      pltpu.SemaphoreType.DMA((2,)), pltpu.SemaphoreType.DMA((2,))])
    def gather(table_hbm, idx_hbm, out_hbm, idx_v, rows_v, gsem, osem):
        wid = lax.axis_index("s") * nc + lax.axis_index("c")
        base = wid * b_per_w
        pltpu.sync_copy(idx_hbm.at[pl.ds(base, b_per_w)], idx_v)

        def get(g, slot):
            return pltpu.make_async_copy(table_hbm.at[idx_v.at[pl.ds(g * w, w)]], rows_v.at[slot], gsem.at[slot])

        def put(g, slot):
            return pltpu.make_async_copy(rows_v.at[slot], out_hbm.at[pl.ds(base + g * w, w)], osem.at[slot])

        get(0, 0).start()

        @pl.loop(0, n_chunks, step=2)
        def _(g):
            for b in range(2):
                gg = g + b
                get(gg, b).wait()

                @pl.when(gg + 1 < n_chunks)
                def _():
                    @pl.when(gg >= 1)
                    def _():
                        put(gg - 1, 1 - b).wait()
                    get(gg + 1, 1 - b).start()

                put(gg, b).start()

        put(n_chunks - 2, 0).wait()
        put(n_chunks - 1, 1).wait()

    return gather(table, idx)


def _gmm_kernel(te_ref, nv_ref, xs_ref, wg_ref, wu_ref, wd_ref, ys_ref, u_ref):
    @pl.when(pl.program_id(0) < nv_ref[0])
    def _():
        u_ref[...] = _unpack_bf16_pairs(xs_ref[...]).astype(BF16)
        ys_ref[...] = _pack_bf16_pairs(_swiglu_partial(u_ref, wg_ref.at[0], wu_ref.at[0], wd_ref.at[0]))


def _gmm(xs, tile_expert, n_valid, wg, wu, wd, *, tg):
    p, dw = xs.shape
    d = 2 * dw
    f = wg.shape[2]
    grid_spec = pltpu.PrefetchScalarGridSpec(
        num_scalar_prefetch=2,
        grid=(p // tg,),
        in_specs=[pl.BlockSpec((tg, dw), lambda i, te, nv: (i, 0)),
                  pl.BlockSpec((1, d, f), lambda i, te, nv: (te[i], 0, 0)),
                  pl.BlockSpec((1, d, f), lambda i, te, nv: (te[i], 0, 0)),
                  pl.BlockSpec((1, f, d), lambda i, te, nv: (te[i], 0, 0))],
        out_specs=pl.BlockSpec((tg, dw), lambda i, te, nv: (i, 0)),
        scratch_shapes=[pltpu.VMEM((tg, d), BF16)])
    return pl.pallas_call(
        _gmm_kernel,
        out_shape=jax.ShapeDtypeStruct((p, dw), jnp.uint32),
        grid_spec=grid_spec,
        compiler_params=_cparams(("arbitrary",)),
        name="gmm",
    )(tile_expert, n_valid, xs, wg, wu, wd)


def _combine_kernel(x_ref, info_ref, y1_ref, y2_ref, gt_ref, gf_ref, out_ref, *, final):
    info = info_ref[...]
    p1 = info[:, LANE_P1:LANE_P1 + 1]
    p2 = info[:, LANE_P2:LANE_P2 + 1]
    moe = p1 * _unpack_bf16_pairs(y1_ref[...]) + p2 * _unpack_bf16_pairs(y2_ref[...])
    y = x_ref[...] + gt_ref[0] * moe
    if final:
        r = lax.rsqrt(jnp.mean(y * y, axis=-1, keepdims=True) + EPS)
        y = (y * r) * gf_ref[...]
    out_ref[...] = y


def _combine(h, info, yg, mod3, g_final, mod_idx, *, tm, n_rows, final):
    d = h.shape[1]
    n_t = n_rows // tm
    kern = functools.partial(_combine_kernel, final=final)
    return pl.pallas_call(
        kern,
        out_shape=jax.ShapeDtypeStruct((n_rows, d), F32),
        grid=(n_t,),
        in_specs=[pl.BlockSpec((tm, d), lambda i: (i, 0)),
                  pl.BlockSpec((tm, GATE_PAD), lambda i: (i, 0)),
                  pl.BlockSpec((tm, d // 2), lambda i: (i, 0)),
                  pl.BlockSpec((tm, d // 2), lambda i: (n_t + i, 0)),
                  pl.BlockSpec((1, 1, d), lambda i: (mod_idx(i) * N_MOD + 5, 0, 0)),
                  pl.BlockSpec((1, d), lambda i: (0, 0))],
        out_specs=pl.BlockSpec((tm, d), lambda i: (i, 0)),
        compiler_params=_cparams(("arbitrary",)),
        name="combine",
    )(h, info, yg, yg, mod3, g_final)


def _moe(h, g, mod3, w_router, wg, wu, wd, g_final, mod_idx, *, tm, tg, n_rows, final):
    n_exp = wg.shape[0]
    up, info, cnt = _route_call(h, g, mod3, w_router, mod_idx, tm=tm, n_rows=n_rows)
    i1 = info[:, LANE_I1].astype(jnp.int32)
    i2 = info[:, LANE_I2].astype(jnp.int32)
    r1 = info[:, LANE_R1].astype(jnp.int32)
    r2 = info[:, LANE_R2].astype(jnp.int32)
    counts = cnt[0, :n_exp].astype(jnp.int32)
    padded = (counts + tg - 1) // tg * tg
    ends = jnp.cumsum(padded)
    starts = ends - padded
    eye = jnp.arange(n_exp, dtype=jnp.int32)
    pos1 = jnp.sum(jnp.where(i1[:, None] == eye, starts, 0), axis=1) + r1
    pos2 = jnp.sum(jnp.where(i2[:, None] == eye, starts, 0), axis=1) + r2
    p_rows = -(-(TOP_K * n_rows + n_exp * (tg - 1)) // tg) * tg
    tile_row = jnp.arange(p_rows // tg, dtype=jnp.int32) * tg
    tile_expert = jnp.minimum(jnp.sum(tile_row[:, None] >= ends[None, :], axis=1), n_exp - 1).astype(jnp.int32)
    n_valid = (ends[-1] // tg).reshape(1).astype(jnp.int32)
    xs = _sc_scatter_rows(up, jnp.stack([pos1, pos2]), p_rows)
    ys = _gmm(xs, tile_expert, n_valid, wg, wu, wd, tg=tg)
    yg = _sc_gather_rows(ys, jnp.concatenate([pos1, pos2]))
    return _combine(h, info, yg, mod3, g_final, mod_idx, tm=tm, n_rows=n_rows, final=final)


def _final_norm_kernel(x_ref, g_ref, o_ref):
    x = x_ref[...]
    r = lax.rsqrt(jnp.mean(x * x, axis=-1, keepdims=True) + EPS)
    o_ref[...] = (x * r) * g_ref[...]


def _final_norm(h, g, *, tm, n_rows):
    d = h.shape[1]
    return pl.pallas_call(
        _final_norm_kernel,
        out_shape=jax.ShapeDtypeStruct((n_rows, d), F32),
        grid=(n_rows // tm,),
        in_specs=[pl.BlockSpec((tm, d), lambda i: (i, 0)), pl.BlockSpec((1, d), lambda i: (0, 0))],
        out_specs=pl.BlockSpec((tm, d), lambda i: (i, 0)),
        compiler_params=_cparams(("arbitrary",)),
        name="final_norm",
    )(h, g)


def kernel(x, c, ctx, c_ctx, w_ada, b_ada, g_mix, w_in, conv_qk, b_if, head_gain, w_pool, pool_scale, w_pa, w_pb,
           w_out, g_ffn, w_ff_gate, w_ff_up, w_ff_down, w_router, w_exp_gate, w_exp_up, w_exp_down, g_final):
    b, s, d = x.shape
    lc = ctx.shape[1]
    depth = w_ada.shape[0]
    width = w_pa.shape[1]
    pw = w_pb.shape[1]
    n_gate = b_if.shape[1] * b_if.shape[2] * b_if.shape[3]
    assert lc == CHUNK and s % CHUNK == 0 and s % GRID_W == 0 and width == d and n_gate == 16
    n_lat, n_ctx = b * s, b * lc
    tm, tm_ffn = 512, 512
    assert s % tm_ffn == 0 and n_ctx % tm_ffn == 0 and tm_ffn % tm == 0

    def mod_idx(t):
        return lambda i: jnp.where(i < n_lat // t, i // (s // t), b)

    src = (x.reshape(n_lat, d), ctx.reshape(n_ctx, d), 0)

    n_mod_rows = 16
    cc = jnp.zeros((n_mod_rows, d), F32).at[:b].set(c).at[b].set(c_ctx)
    mod = _ada(cc, w_ada, b_ada)

    if_off, pool_off, ga_off = 4 * width, 4 * width + n_gate, 4 * width + n_gate + pw
    n_main = 4 * width + 2 * d

    for l in range(depth):
        last = l == depth - 1
        mod3 = mod[l].reshape(n_mod_rows * N_MOD, 1, d)
        w_l = w_in[l]
        w_cat = jnp.concatenate(
            [w_l[:, :if_off], w_l[:, ga_off:], w_l[:, pool_off:ga_off], w_l[:, if_off:pool_off],
             jnp.zeros((d, GATE_PAD - n_gate), F32)], axis=1).astype(BF16)
        k_scale = float((width // M_HEADS) ** -0.5)
        zm, zp, zg = _inproj(src, g_mix[l].reshape(1, d), mod3, w_cat, conv_qk[l], mod_idx(tm), tm=tm, n_main=n_main,
                             n_pool=pw, n_qk=2 * width, k_scale=k_scale, n_lat=n_lat, lat_len=s, ctx_len=lc)

        bias = b_if[l].reshape(n_gate)
        bias_c = jnp.zeros((1, GATE_PAD), F32).at[0, :n_gate].set(bias)
        bias_r = jnp.broadcast_to(bias[:, None], (n_gate, CHUNK))
        hf, hb = _mlstm(zm, zg, zg[:, :n_gate].T, bias_c, bias_r, b=b, s=s, width=width)

        n_rows = n_lat if last else n_lat + n_ctx
        pm_lat = _pool(zp, n_seq=b, seq_len=s, row_block0=0, grid_rows=s // GRID_W)
        pm_ctx = pm_lat if last else _pool(zp, n_seq=b, seq_len=lc, row_block0=n_lat // lc, grid_rows=None)

        h = _mixout(hf, hb, zm, pm_lat, pm_ctx, src, mod3, head_gain[l].reshape(1, width), w_pool[l].astype(BF16),
                    pool_scale[l].reshape(1, pw), w_pa[l].astype(BF16), w_pb[l].astype(BF16),
                    w_out[l].astype(BF16), mod_idx(tm), tm=tm, n_rows=n_rows)

        j = l // 2
        if l % 2 == 0:
            h = _ffn(h, g_ffn[l].reshape(1, d), mod3, w_ff_gate[j].astype(BF16), w_ff_up[j].astype(BF16),
                     w_ff_down[j].astype(BF16), mod_idx(tm_ffn), tm=tm_ffn,
                     tf=_pick_tile(w_ff_gate.shape[2], 2816))
            if last:
                h = _final_norm(h, g_final.reshape(1, d), tm=tm, n_rows=n_lat)
        else:
            h = _moe(h, g_ffn[l].reshape(1, d), mod3, w_router[j], w_exp_gate[j].astype(BF16),
                     w_exp_up[j].astype(BF16), w_exp_down[j].astype(BF16), g_final.reshape(1, d), mod_idx(tm),
                     tm=tm, tg=512, n_rows=h.shape[0], final=last)
        src = (h, h, n_lat)
    return h[:n_lat].reshape(b, s, d)
```

```python
import functools

import jax
import jax.numpy as jnp
from jax import lax
from jax.experimental import pallas as pl
from jax.experimental.pallas import tpu as pltpu
from jax.experimental.pallas import tpu_sc as plsc

F32 = jnp.float32
BF16 = jnp.bfloat16
EPS = 1e-6
M_HEADS = 4
GRID_W = 64
POOL_WINDOWS = (2, 4, 8, 16)
POOL_GROUP_DIM = 128
CHUNK = 256
N_MOD = 6
TOP_K = 2
SC_WINDOW = 64
GATE_PAD = 128
VMEM_LIMIT = 56 * 1024 * 1024


def _pick_tile(n, target, quantum=128):
    t = min(n, target) // quantum * quantum
    while n % t:
        t -= quantum
    return t


def _cparams(sem):
    return pltpu.CompilerParams(dimension_semantics=sem, vmem_limit_bytes=VMEM_LIMIT)


def _sigmoid(x):
    return 0.5 * jnp.tanh(0.5 * x) + 0.5


def _silu(x):
    h = 0.5 * x
    return h * jnp.tanh(h) + h


def _log_sigmoid(x):
    return jnp.minimum(x, 0.0) - jnp.log(1.0 + jnp.exp(-jnp.abs(x)))


def _split3(x):
    hi = x.astype(BF16)
    r1 = x - hi.astype(F32)
    mid = r1.astype(BF16)
    lo = (r1 - mid.astype(F32)).astype(BF16)
    return hi, mid, lo


def _dot(a, b):
    return jnp.dot(a, b, preferred_element_type=F32)


def _dot_nt(a, b):
    return lax.dot_general(a, b, (((1,), (1,)), ((), ())), preferred_element_type=F32)


def _dot_tn(a, b):
    return lax.dot_general(a, b, (((0,), (0,)), ((), ())), preferred_element_type=F32)


def _rms_mod(x, g, sc, sh):
    r = lax.rsqrt(jnp.mean(x * x, axis=-1, keepdims=True) + EPS)
    return (x * r) * g * (1.0 + sc) + sh


def _ada_kernel(c_ref, w_ref, b_ref, o_ref):
    c = c_ref[...]
    o_ref[0] = _dot(_silu(c).astype(BF16), w_ref[0].astype(BF16)) + b_ref[0]


def _ada(cc, w_ada, b_ada):
    depth, d, n = w_ada.shape
    tn = 1024
    return pl.pallas_call(
        _ada_kernel,
        out_shape=jax.ShapeDtypeStruct((depth, cc.shape[0], n), F32),
        grid=(depth, n // tn),
        in_specs=[pl.BlockSpec(cc.shape, lambda l, j: (0, 0)),
                  pl.BlockSpec((1, d, tn), lambda l, j: (l, 0, j)),
                  pl.BlockSpec((1, 1, tn), lambda l, j: (l, 0, j))],
        out_specs=pl.BlockSpec((1, cc.shape[0], tn), lambda l, j: (l, 0, j)),
        compiler_params=_cparams(("arbitrary", "arbitrary")),
        name="ada",
    )(cc, w_ada, b_ada.reshape(depth, 1, n))


def _inproj_kernel(xa_ref, xb_ref, xpa_ref, xpb_ref, xna_ref, xnb_ref, g_ref, sc_ref, sh_ref, w_ref, cw_ref,
                   zm_ref, zp_ref, zg_ref, u_ref, uh_ref,
                   *, n_main, n_pool, n_qk, cw, k_scale, n_lat_tiles, lat_len, ctx_len):
    i = pl.program_id(0)
    tm = xa_ref.shape[0]
    is_ctx = i >= n_lat_tiles
    g, sc, sh = g_ref[...], sc_ref[0], sh_ref[0]
    u_ref[...] = _rms_mod(jnp.where(is_ctx, xb_ref[...], xa_ref[...]), g, sc, sh).astype(BF16)
    uh_ref[0:8, :] = _rms_mod(jnp.where(is_ctx, xpb_ref[...], xpa_ref[...]), g, sc, sh).astype(BF16)
    uh_ref[8:16, :] = _rms_mod(jnp.where(is_ctx, xnb_ref[...], xna_ref[...]), g, sc, sh).astype(BF16)

    pos0 = lax.rem(i * tm, lat_len)
    lat_first_row = jnp.where(pos0 == 0, 0, -1)
    lat_last_row = jnp.where(pos0 + tm == lat_len, tm - 1, -1)
    sub = 16
    srow = lax.broadcasted_iota(jnp.int32, (sub, cw), 0)
    slabs = sorted({0, tm - sub} | {k * ctx_len - sub for k in range(1, tm // ctx_len)}
                   | {k * ctx_len for k in range(1, tm // ctx_len)})

    for c in range(n_qk // cw):
        cs = slice(c * cw, (c + 1) * cw)
        w0, w1, w2 = cw_ref[0:1, cs], cw_ref[1:2, cs], cw_ref[2:3, cs]
        scale = k_scale if c * cw >= n_qk // 2 else 1.0
        z = _dot(u_ref[...], w_ref[:, cs])
        zh = _dot(uh_ref[...], w_ref[:, cs])
        y = _silu(pltpu.roll(z, 1, 0) * w0 + z * w1 + pltpu.roll(z, tm - 1, 0) * w2) * scale
        zm_ref[:, cs] = y.astype(BF16)
        for r0 in slabs:
            zs = z[r0:r0 + sub, :]
            above = zh[7:8, :] if r0 == 0 else z[r0 - 1:r0, :]
            below = zh[8:9, :] if r0 + sub == tm else z[r0 + sub:r0 + sub + 1, :]
            grow = srow + r0
            in_ctx = grow & (ctx_len - 1)
            first = jnp.where(is_ctx, in_ctx, grow - lat_first_row) == 0
            last = jnp.where(is_ctx, in_ctx - (ctx_len - 1), grow - lat_last_row) == 0
            zm1 = jnp.where(srow == 0, above, pltpu.roll(zs, 1, 0))
            zm1 = jnp.where(first, 0.0, zm1)
            zp1 = jnp.where(srow == sub - 1, below, pltpu.roll(zs, sub - 1, 0))
            zp1 = jnp.where(last, 0.0, zp1)
            ys = _silu(zm1 * w0 + zs * w1 + zp1 * w2) * scale
            zm_ref[r0:r0 + sub, cs] = ys.astype(BF16)
    for c in range(n_qk // cw, n_main // cw):
        zm_ref[:, c * cw:(c + 1) * cw] = _dot(u_ref[...], w_ref[:, c * cw:(c + 1) * cw]).astype(BF16)
    for c in range(n_pool // cw):
        lo = n_main + c * cw
        zp_ref[:, c * cw:(c + 1) * cw] = _dot(u_ref[...], w_ref[:, lo:lo + cw])
    zg_ref[...] = _dot(u_ref[...], w_ref[:, n_main + n_pool:])


def _inproj(src, g, mod3, w_cat, conv_w, mod_idx, *, tm, n_main, n_pool, n_qk, k_scale, n_lat, lat_len, ctx_len):
    a, b, b_off = src
    d = a.shape[1]
    r = n_lat + b.shape[0] - b_off
    ncols = w_cat.shape[1]
    halo = 8
    assert tm % ctx_len == 0 and lat_len % tm == 0 and ctx_len & (ctx_len - 1) == 0 and b_off % tm == 0
    na_t, hb = n_lat // tm, tm // halo
    a_last, b_last = a.shape[0] // halo - 1, b.shape[0] // halo - 1
    kern = functools.partial(_inproj_kernel, n_main=n_main, n_pool=n_pool, n_qk=n_qk, cw=512, k_scale=k_scale,
                             n_lat_tiles=na_t, lat_len=lat_len, ctx_len=ctx_len)
    return pl.pallas_call(
        kern,
        out_shape=(jax.ShapeDtypeStruct((r, n_main), BF16),
                   jax.ShapeDtypeStruct((r, n_pool), F32),
                   jax.ShapeDtypeStruct((r, GATE_PAD), F32)),
        grid=(r // tm,),
        in_specs=[pl.BlockSpec((tm, d), lambda i: (jnp.minimum(i, na_t - 1), 0)),
                  pl.BlockSpec((tm, d), lambda i: (jnp.maximum(i - na_t, 0) + b_off // tm, 0)),
                  pl.BlockSpec((halo, d), lambda i: (jnp.clip(i * hb - 1, 0, a_last), 0)),
                  pl.BlockSpec((halo, d), lambda i: (jnp.clip(b_off // halo + (i - na_t) * hb - 1, 0, b_last), 0)),
                  pl.BlockSpec((halo, d), lambda i: (jnp.clip((i + 1) * hb, 0, a_last), 0)),
                  pl.BlockSpec((halo, d), lambda i: (jnp.clip(b_off // halo + (i - na_t + 1) * hb, 0, b_last), 0)),
                  pl.BlockSpec((1, d), lambda i: (0, 0)),
                  pl.BlockSpec((1, 1, d), lambda i: (mod_idx(i) * N_MOD + 1, 0, 0)),
                  pl.BlockSpec((1, 1, d), lambda i: (mod_idx(i) * N_MOD + 0, 0, 0)),
                  pl.BlockSpec((d, ncols), lambda i: (0, 0)),
                  pl.BlockSpec((3, n_qk), lambda i: (0, 0))],
        out_specs=(pl.BlockSpec((tm, n_main), lambda i: (i, 0)),
                   pl.BlockSpec((tm, n_pool), lambda i: (i, 0)),
                   pl.BlockSpec((tm, GATE_PAD), lambda i: (i, 0))),
        scratch_shapes=[pltpu.VMEM((tm, d), BF16), pltpu.VMEM((2 * halo, d), BF16)],
        compiler_params=_cparams(("arbitrary",)),
        name="inproj",
    )(a, b, a, b, a, b, g, mod3, mod3, w_cat, conv_w)


def _running_max_rows(x, rev):
    n = x.shape[0]
    row = lax.broadcasted_iota(jnp.int32, x.shape, 0)
    sh = 1
    while sh < n:
        if rev:
            shifted = jnp.where(row < n - sh, pltpu.roll(x, n - sh, 0), -jnp.inf)
        else:
            shifted = jnp.where(row >= sh, pltpu.roll(x, sh, 0), -jnp.inf)
        x = jnp.maximum(x, shifted)
        sh *= 2
    return x


def _mlstm_kernel(*refs, dh, dirs):
    L = CHUNK
    nd = len(dirs)
    bias_c, bias_r = refs[5 * nd:5 * nd + 2]
    outs = refs[5 * nd + 2:6 * nd + 2]
    c_ref, c16_ref, n_ref, n16_ref, m_ref = refs[6 * nd + 2:]

    @pl.when(pl.program_id(1) == 0)
    def _():
        c_ref[...] = jnp.zeros_like(c_ref)
        c16_ref[...] = jnp.zeros_like(c16_ref)
        n_ref[...] = jnp.zeros_like(n_ref)
        n16_ref[...] = jnp.zeros_like(n16_ref)
        m_ref[...] = jnp.zeros_like(m_ref)

    ri = lax.broadcasted_iota(jnp.int32, (L, L), 0)
    ci = lax.broadcasted_iota(jnp.int32, (L, L), 1)
    lower = ci <= ri
    upper = ci >= ri
    lower_b = jnp.where(lower, 1.0, 0.0).astype(BF16)
    upper_b = jnp.where(upper, 1.0, 0.0).astype(BF16)

    for di, d in enumerate(dirs):
        q_ref, k_ref, v_ref, gc_ref, gr_ref = refs[5 * di:5 * di + 5]
        out_ref = outs[di]
        rev = d == 1
        gc = gc_ref[...] + bias_c[...]
        gr = gr_ref[...] + bias_r[...]
        fc = _log_sigmoid(gc)
        fr = _log_sigmoid(gr)
        t_col = upper_b if rev else lower_b
        t_row = lower_b if rev else upper_b
        bc3 = _dot(t_col, jnp.concatenate(_split3(fc), axis=1))
        bcum_c = bc3[:, :GATE_PAD] + bc3[:, GATE_PAD:2 * GATE_PAD] + bc3[:, 2 * GATE_PAD:]
        br3 = _dot(jnp.concatenate(_split3(fr), axis=0), t_row)
        bcum_r = br3[0:16, :] + br3[16:32, :] + br3[32:48, :]
        mask = upper if rev else lower
        end = 0 if rev else L - 1

        b_al = pltpu.roll(bcum_c, GATE_PAD - M_HEADS, axis=1)
        g_col = gc - b_al
        m_st = m_ref[di:di + 1, :]
        mx = jnp.maximum(m_st, _running_max_rows(g_col, rev))
        e_neg_all = jnp.exp(-(b_al + mx))
        mx_end = mx[end:end + 1, :]
        wk_all = jnp.exp(g_col - mx_end)
        wc_all = jnp.exp(m_st - mx_end)
        m_ref[di:di + 1, :] = b_al[end:end + 1, :] + mx_end

        ones_rep = jnp.ones((L, GATE_PAD), BF16)
        for h in range(M_HEADS):
            s_idx = di * M_HEADS + h
            col = d * 2 * M_HEADS + h

            def rep(x_all):
                return jnp.broadcast_to(x_all[:, col:col + 1], (x_all.shape[0], GATE_PAD))

            def wide(x_rep):
                return jnp.concatenate([x_rep] * (dh // GATE_PAD), axis=1)

            mx_rep = rep(mx)
            g_row = gr[col:col + 1, :] - bcum_r[col + M_HEADS:col + M_HEADS + 1, :]
            p = jnp.exp(jnp.where(mask, g_row - wide(mx_rep), -jnp.inf))
            w_inter = jnp.exp(rep(m_st) - mx_rep)
            e_neg = rep(e_neg_all)
            wk16 = rep(wk_all).astype(BF16)
            wc = wc_all[:, col:col + 1]

            sl = slice(h * dh, (h + 1) * dh)
            q = q_ref[:, sl]
            k = k_ref[:, sl]
            v = v_ref[:, sl]
            scores = (_dot_nt(q, k) * p).astype(BF16)
            num = wide(w_inter) * _dot(q, c16_ref[s_idx]) + _dot(scores, v)
            den = w_inter * _dot(q, n16_ref[s_idx]) + _dot(scores, ones_rep)
            inv = 1.0 / jnp.maximum(jnp.abs(den), e_neg)
            out_ref[:, sl] = (num * wide(inv)).astype(BF16)

            upd = _dot_tn(k, jnp.concatenate([v * wide(wk16), wk16], axis=1))
            c_new = wc * c_ref[s_idx] + upd[:, :dh]
            c_ref[s_idx] = c_new
            c16_ref[s_idx] = c_new.astype(BF16)
            n_new = wc * n_ref[s_idx] + upd[:, dh:]
            n_ref[s_idx] = n_new
            n16_ref[s_idx] = n_new.astype(BF16)


def _mlstm(zm, zg, zg_t, bias_c, bias_r, *, b, s, width, dirs):
    r = zm.shape[0]
    L = CHUNK
    n_lat = s // L
    lat_blocks = b * n_lat
    nd = len(dirs)

    def blk_f(bi, i):
        return jnp.where(i == 0, lat_blocks + bi, bi * n_lat + i - 1)

    def blk_b(bi, i):
        return jnp.where(i == 0, lat_blocks + bi, bi * n_lat + n_lat - i)

    blks = [blk_b if d == 1 else blk_f for d in dirs]

    def specs(blk):
        return [pl.BlockSpec((L, width), lambda bi, i: (blk(bi, i), 0)),
                pl.BlockSpec((L, width), lambda bi, i: (blk(bi, i), 1)),
                pl.BlockSpec((L, width), lambda bi, i: (blk(bi, i), 2)),
                pl.BlockSpec((L, GATE_PAD), lambda bi, i: (blk(bi, i), 0)),
                pl.BlockSpec((16, L), lambda bi, i: (0, blk(bi, i)))]

    dh = width // M_HEADS
    kern = functools.partial(_mlstm_kernel, dh=dh, dirs=tuple(dirs))
    out = jax.ShapeDtypeStruct((r, width), BF16)
    in_specs = []
    for blk in blks:
        in_specs += specs(blk)
    in_specs += [pl.BlockSpec((1, GATE_PAD), lambda bi, i: (0, 0)), pl.BlockSpec((16, L), lambda bi, i: (0, 0))]
    return pl.pallas_call(
        kern,
        out_shape=(out,) * nd,
        grid=(b, n_lat + 1),
        in_specs=in_specs,
        out_specs=tuple(pl.BlockSpec((L, width), functools.partial(lambda blk, bi, i: (blk(bi, i), 0), blk))
                        for blk in blks),
        scratch_shapes=[pltpu.VMEM((nd * M_HEADS, dh, dh), F32),
                        pltpu.VMEM((nd * M_HEADS, dh, dh), BF16),
                        pltpu.VMEM((nd * M_HEADS, dh, GATE_PAD), F32),
                        pltpu.VMEM((nd * M_HEADS, dh, GATE_PAD), BF16),
                        pltpu.VMEM((8, GATE_PAD), F32)],
        compiler_params=_cparams(("arbitrary", "arbitrary")),
        name="mlstm",
    )(*((zm, zm, zm, zg, zg_t) * nd), bias_c, bias_r)


POOL_BLOCK = 256


def _band(n, w, seg):
    r = lax.broadcasted_iota(jnp.int32, (n, n), 0)
    c = lax.broadcasted_iota(jnp.int32, (n, n), 1)
    d = c - r
    shift = seg.bit_length() - 1
    ok = (d >= -(w // 2)) & (d <= w // 2 - 1) & ((r >> shift) == (c >> shift))
    return jnp.where(ok, 1.0, 0.0).astype(BF16)


def _window_count(idx, n, w):
    return (jnp.minimum(idx + w // 2, n) - jnp.maximum(idx - w // 2, 0)).astype(F32)


def _pool_kernel(z_ref, o_ref, pad_ref, *, grid_rows):
    n = z_ref.shape[0]
    blk = min(POOL_BLOCK, n)
    seg = blk if grid_rows is None else GRID_W
    seg_len = n if grid_rows is None else GRID_W
    halo = (max(POOL_WINDOWS) // 2) * GRID_W
    rowi = lax.broadcasted_iota(jnp.int32, (blk, POOL_GROUP_DIM), 0)
    if grid_rows is not None:
        zeros = jnp.zeros((halo, POOL_GROUP_DIM), F32)
        pad_ref[0:halo, :] = zeros
        pad_ref[halo + n:, :] = zeros
    for g, w in enumerate(POOL_WINDOWS):
        sl = slice(g * POOL_GROUP_DIM, (g + 1) * POOL_GROUP_DIM)
        band = _band(blk, w, seg)
        inv_w = 1.0 / _window_count(rowi & (seg - 1), seg_len, w)
        for b in range(n // blk):
            rows = slice(b * blk, (b + 1) * blk)
            x = z_ref[rows, sl]
            hi = x.astype(BF16)
            lo = (x - hi.astype(F32)).astype(BF16)
            y2 = _dot(band, jnp.concatenate([hi, lo], axis=1))
            y = (y2[:, :POOL_GROUP_DIM] + y2[:, POOL_GROUP_DIM:]) * inv_w
            if grid_rows is None:
                o_ref[rows, sl] = (y - x).astype(BF16)
            else:
                pad_ref[halo + b * blk:halo + (b + 1) * blk, :] = y
        if grid_rows is not None:
            shift = GRID_W.bit_length() - 1
            for b in range(n // blk):
                acc = None
                for d in range(-(w // 2), w // 2):
                    lo_r = halo + b * blk + d * GRID_W
                    t = pad_ref[lo_r:lo_r + blk, :]
                    acc = t if acc is None else acc + t
                inv_h = 1.0 / _window_count((rowi + b * blk) >> shift, grid_rows, w)
                rows = slice(b * blk, (b + 1) * blk)
                o_ref[rows, sl] = (acc * inv_h - z_ref[rows, sl]).astype(BF16)


def _pool(zp, *, n_seq, seq_len, row_block0, grid_rows):
    width = zp.shape[1]
    assert seq_len % POOL_BLOCK == 0 if grid_rows is not None else seq_len <= POOL_BLOCK
    kern = functools.partial(_pool_kernel, grid_rows=grid_rows)
    halo = (max(POOL_WINDOWS) // 2) * GRID_W
    return pl.pallas_call(
        kern,
        out_shape=jax.ShapeDtypeStruct((n_seq * seq_len, width), BF16),
        grid=(n_seq,),
        in_specs=[pl.BlockSpec((seq_len, width), lambda s: (row_block0 + s, 0))],
        out_specs=pl.BlockSpec((seq_len, width), lambda s: (s, 0)),
        scratch_shapes=[pltpu.VMEM((seq_len + 2 * halo, POOL_GROUP_DIM), F32)],
        compiler_params=_cparams(("arbitrary",)),
        name="pool",
    )(zp)


def _mixout_kernel(hf_ref, hb_ref, o_ref, ga_ref, gb_ref, pml_ref, pmc_ref, xa_ref, xb_ref, gt_ref, hg_ref, wpool_ref,
                   ps_ref, wpa_ref, wpb_ref, wout_ref, out_ref, *, dh, n_lat_tiles):
    is_ctx = pl.program_id(0) >= n_lat_tiles
    hm = hf_ref[...].astype(F32) + hb_ref[...].astype(F32)
    parts = []
    for h in range(M_HEADS):
        t = hm[:, h * dh:(h + 1) * dh]
        mu = jnp.mean(t, axis=-1, keepdims=True)
        tc = t - mu
        var = jnp.mean(tc * tc, axis=-1, keepdims=True)
        parts.append(tc * lax.rsqrt(var + EPS))
    hn = jnp.concatenate(parts, axis=-1)
    hn = _sigmoid(o_ref[...].astype(F32)) * (hn * hg_ref[...])
    a = _dot(hn.astype(BF16), wpa_ref[...])
    pm_in = jnp.where(is_ctx, pmc_ref[...], pml_ref[...])
    pparts = []
    for g in range(len(POOL_WINDOWS)):
        sl = slice(g * POOL_GROUP_DIM, (g + 1) * POOL_GROUP_DIM)
        pparts.append(_dot(pm_in[:, sl], wpool_ref[g]))
    pm = jnp.concatenate(pparts, axis=-1) * ps_ref[...]
    bmat = _dot(pm.astype(BF16), wpb_ref[...])
    merged = _sigmoid(ga_ref[...].astype(F32)) * a + _sigmoid(gb_ref[...].astype(F32)) * bmat
    y = _dot(merged.astype(BF16), wout_ref[...])
    out_ref[...] = jnp.where(is_ctx, xb_ref[...], xa_ref[...]) + gt_ref[0] * y


def _mixout(hf, hb, zm, pm_lat, pm_ctx, src, mod3, head_gain, w_pool, pool_scale, w_pa, w_pb, w_out, mod_idx, *, tm,
            n_rows):
    xa, xb, b_off = src
    d = xa.shape[1]
    width = hf.shape[1]
    pw = pm_lat.shape[1]
    n_lat_tiles = pm_lat.shape[0] // tm
    o_blk = 3
    full = lambda shape: pl.BlockSpec(shape, lambda i: (0,) * len(shape))
    kern = functools.partial(_mixout_kernel, dh=width // M_HEADS, n_lat_tiles=n_lat_tiles)
    return pl.pallas_call(
        kern,
        out_shape=jax.ShapeDtypeStruct((n_rows, d), F32),
        grid=(n_rows // tm,),
        in_specs=[pl.BlockSpec((tm, width), lambda i: (i, 0)),
                  pl.BlockSpec((tm, width), lambda i: (i, 0)),
                  pl.BlockSpec((tm, width), lambda i: (i, o_blk)),
                  pl.BlockSpec((tm, d), lambda i: (i, o_blk + 1)),
                  pl.BlockSpec((tm, d), lambda i: (i, o_blk + 2)),
                  pl.BlockSpec((tm, pw), lambda i: (jnp.minimum(i, n_lat_tiles - 1), 0)),
                  pl.BlockSpec((tm, pw), lambda i: (jnp.maximum(i - n_lat_tiles, 0), 0)),
                  pl.BlockSpec((tm, d), lambda i: (jnp.minimum(i, n_lat_tiles - 1), 0)),
                  pl.BlockSpec((tm, d), lambda i: (jnp.maximum(i - n_lat_tiles, 0) + b_off // tm, 0)),
                  pl.BlockSpec((1, 1, d), lambda i: (mod_idx(i) * N_MOD + 2, 0, 0)),
                  full((1, width)), full(w_pool.shape), full((1, pw)),
                  full(w_pa.shape), full(w_pb.shape), full(w_out.shape)],
        out_specs=pl.BlockSpec((tm, d), lambda i: (i, 0)),
        compiler_params=_cparams(("arbitrary",)),
        name="mixout",
    )(hf, hb, zm, zm, zm, pm_lat, pm_ctx, xa, xb, mod3, head_gain, w_pool, pool_scale, w_pa, w_pb, w_out)


SWIGLU_SUB = 512


def _swiglu_partial(u_ref, wg, wu, wd):
    tf = wg.shape[1]
    acc = None
    for c0 in range(0, tf, SWIGLU_SUB):
        cs = slice(c0, min(c0 + SWIGLU_SUB, tf))
        u = u_ref[...]
        act = (_silu(_dot(u, wg[:, cs])) * _dot(u, wu[:, cs])).astype(BF16)
        part = _dot(act, wd[cs, :])
        acc = part if acc is None else acc + part
    return acc


def _ffn_kernel(x_ref, g_ref, sc_ref, sh_ref, gt_ref, wg_ref, wu_ref, wd_ref, *rest, n_side):
    side_in, out_ref, side_out, u_ref = rest[:n_side], rest[n_side], rest[n_side + 1:2 * n_side + 1], rest[-1]
    j = pl.program_id(1)
    u_ref[...] = _rms_mod(x_ref[...], g_ref[...], sc_ref[0], sh_ref[0]).astype(BF16)
    contrib = gt_ref[0] * _swiglu_partial(u_ref, wg_ref, wu_ref, wd_ref)

    @pl.when(j == 0)
    def _():
        out_ref[...] = x_ref[...] + contrib
        for src, dst in zip(side_in, side_out):
            dst[...] = src[...].astype(BF16)

    @pl.when(j > 0)
    def _():
        out_ref[...] += contrib


def _side_rows(n_rows, n_steps):
    rows = (-(-n_rows // n_steps) + 15) // 16 * 16
    while n_rows % rows:
        rows += 16
    return rows


def _ffn(h, g, mod3, wg, wu, wd, mod_idx, *, tm, tf, side=()):
    r, d = h.shape
    f = wg.shape[1]
    n_steps = r // tm
    once = pl.Buffered(1) if f == tf else None
    side_specs_in, side_specs_out, side_shapes = [], [], []
    for a in side:
        rows = _side_rows(a.shape[0], n_steps)
        last = a.shape[0] // rows - 1
        spec = pl.BlockSpec((rows, a.shape[1]), functools.partial(lambda last, i, j: (jnp.minimum(i, last), 0), last))
        side_specs_in.append(spec)
        side_specs_out.append(spec)
        side_shapes.append(jax.ShapeDtypeStruct(a.shape, BF16))
    return pl.pallas_call(
        functools.partial(_ffn_kernel, n_side=len(side)),
        out_shape=(jax.ShapeDtypeStruct((r, d), F32), *side_shapes),
        grid=(n_steps, f // tf),
        in_specs=[pl.BlockSpec((tm, d), lambda i, j: (i, 0)),
                  pl.BlockSpec((1, d), lambda i, j: (0, 0)),
                  pl.BlockSpec((1, 1, d), lambda i, j: (mod_idx(i) * N_MOD + 4, 0, 0)),
                  pl.BlockSpec((1, 1, d), lambda i, j: (mod_idx(i) * N_MOD + 3, 0, 0)),
                  pl.BlockSpec((1, 1, d), lambda i, j: (mod_idx(i) * N_MOD + 5, 0, 0)),
                  pl.BlockSpec((d, tf), lambda i, j: (0, j), pipeline_mode=once),
                  pl.BlockSpec((d, tf), lambda i, j: (0, j), pipeline_mode=once),
                  pl.BlockSpec((tf, d), lambda i, j: (j, 0), pipeline_mode=once),
                  *side_specs_in],
        out_specs=(pl.BlockSpec((tm, d), lambda i, j: (i, 0)), *side_specs_out),
        scratch_shapes=[pltpu.VMEM((tm, d), BF16)],
        compiler_params=_cparams(("arbitrary", "arbitrary")),
        name="ffn",
    )(h, g, mod3, mod3, mod3, wg, wu, wd, *side)


def _route(u, wr_ref, n_exp):
    u_hi = u.astype(BF16)
    u_mid = (u - u_hi.astype(F32)).astype(BF16)
    w_hi, w_mid, _ = _split3(wr_ref[...])
    both = _dot(u_hi, jnp.concatenate([w_hi, w_mid], axis=1))
    logits = both[:, :GATE_PAD] + both[:, GATE_PAD:] + _dot(u_mid, w_hi)
    lane = lax.broadcasted_iota(jnp.int32, logits.shape, 1)
    lg = jnp.where(lane < n_exp, logits, -jnp.inf)
    m1 = jnp.max(lg, axis=1, keepdims=True)
    i1 = jnp.min(jnp.where(lg == m1, lane, 2 * GATE_PAD), axis=1, keepdims=True)
    lg2 = jnp.where(lane == i1, -jnp.inf, lg)
    m2 = jnp.max(lg2, axis=1, keepdims=True)
    i2 = jnp.min(jnp.where(lg2 == m2, lane, 2 * GATE_PAD), axis=1, keepdims=True)
    e = jnp.exp(m2 - m1)
    p1 = 1.0 / (1.0 + e)
    p2 = e * p1
    return i1, i2, p1, p2


LANE_I1, LANE_I2, LANE_P1, LANE_P2, LANE_R1, LANE_R2 = 0, 1, 2, 3, 4, 5


def _pack_bf16_pairs(x):
    k = x.shape[1] // 2
    lo = lax.bitcast_convert_type(x[:, :k].astype(BF16).astype(F32), jnp.uint32)
    hi = lax.bitcast_convert_type(x[:, k:].astype(BF16).astype(F32), jnp.uint32)
    return (lo >> 16) | (hi & jnp.uint32(0xFFFF0000))


def _unpack_bf16_pairs(w):
    lo = lax.bitcast_convert_type(w << 16, F32)
    hi = lax.bitcast_convert_type(w & jnp.uint32(0xFFFF0000), F32)
    return jnp.concatenate([lo, hi], axis=1)


def _route_kernel(x_ref, g_ref, sc_ref, sh_ref, wr_ref, up_ref, info_ref, cnt_ref, carry_ref, *, n_exp):
    @pl.when(pl.program_id(0) == 0)
    def _():
        carry_ref[...] = jnp.zeros_like(carry_ref)

    u = _rms_mod(x_ref[...], g_ref[...], sc_ref[0], sh_ref[0])
    up_ref[...] = _pack_bf16_pairs(u)
    i1, i2, p1, p2 = _route(u, wr_ref, n_exp)
    tm = u.shape[0]
    lane = lax.broadcasted_iota(jnp.int32, (tm, GATE_PAD), 1)
    sel = jnp.where((lane == i1) | (lane == i2), 1.0, 0.0)
    ri = lax.broadcasted_iota(jnp.int32, (tm, tm), 0)
    ci = lax.broadcasted_iota(jnp.int32, (tm, tm), 1)
    before = jnp.where(ci < ri, 1.0, 0.0).astype(BF16)
    rank = carry_ref[0:1, :] + _dot(before, sel.astype(BF16))
    r1 = jnp.sum(jnp.where(lane == i1, rank, 0.0), axis=1, keepdims=True)
    r2 = jnp.sum(jnp.where(lane == i2, rank, 0.0), axis=1, keepdims=True)
    carry_ref[0:1, :] = carry_ref[0:1, :] + jnp.sum(sel, axis=0, keepdims=True)
    cnt_ref[...] = carry_ref[...]
    info = jnp.zeros((tm, GATE_PAD), F32)
    for ln, val in ((LANE_I1, i1.astype(F32)), (LANE_I2, i2.astype(F32)), (LANE_P1, p1), (LANE_P2, p2),
                    (LANE_R1, r1), (LANE_R2, r2)):
        info = jnp.where(lane == ln, val, info)
    info_ref[...] = info


def _route_call(h, g, mod3, w_router, mod_idx, *, tm, n_rows):
    d = h.shape[1]
    n_exp = w_router.shape[1]
    w_r = jnp.concatenate([w_router, jnp.zeros((d, GATE_PAD - n_exp), F32)], axis=1)
    kern = functools.partial(_route_kernel, n_exp=n_exp)
    return pl.pallas_call(
        kern,
        out_shape=(jax.ShapeDtypeStruct((n_rows, d // 2), jnp.uint32),
                   jax.ShapeDtypeStruct((n_rows, GATE_PAD), F32),
                   jax.ShapeDtypeStruct((8, GATE_PAD), F32)),
        grid=(n_rows // tm,),
        in_specs=[pl.BlockSpec((tm, d), lambda i: (i, 0)),
                  pl.BlockSpec((1, d), lambda i: (0, 0)),
                  pl.BlockSpec((1, 1, d), lambda i: (mod_idx(i) * N_MOD + 4, 0, 0)),
                  pl.BlockSpec((1, 1, d), lambda i: (mod_idx(i) * N_MOD + 3, 0, 0)),
                  pl.BlockSpec((d, GATE_PAD), lambda i: (0, 0))],
        out_specs=(pl.BlockSpec((tm, d // 2), lambda i: (i, 0)),
                   pl.BlockSpec((tm, GATE_PAD), lambda i: (i, 0)),
                   pl.BlockSpec((8, GATE_PAD), lambda i: (0, 0))),
        scratch_shapes=[pltpu.VMEM((8, GATE_PAD), F32)],
        compiler_params=_cparams(("arbitrary",)),
        name="route",
    )(h, g, mod3, mod3, w_r)


def _sc_workers():
    info = pltpu.get_tpu_info().sparse_core
    return info.num_cores, info.num_subcores


def _sc_scatter_rows(x, pos, n_out):
    n, dw = x.shape
    nc, ns = _sc_workers()
    nw = nc * ns
    t_per_w = n // nw
    w = min(SC_WINDOW, t_per_w // 2)
    n_chunks = t_per_w // w
    assert n % nw == 0 and t_per_w % (2 * w) == 0
    pos_w = pos.reshape(2, nw, n_chunks, w).transpose(1, 0, 2, 3)
    mesh = plsc.VectorSubcoreMesh(core_axis_name="c", subcore_axis_name="s")

    @functools.partial(pl.kernel, mesh=mesh, out_type=jax.ShapeDtypeStruct((n_out, dw), x.dtype),
                       scratch_types=[pltpu.VMEM((2, n_chunks, w), jnp.int32), pltpu.VMEM((2, w, dw), x.dtype),
                                      pltpu.SemaphoreType.DMA((2,)), pltpu.SemaphoreType.DMA((2,))])
    def scatter(x_hbm, pos_hbm, out_hbm, idx_v, rows_v, gsem, osem):
        wid = lax.axis_index("s") * nc + lax.axis_index("c")
        base = wid * t_per_w
        pltpu.sync_copy(pos_hbm.at[wid], idx_v)

        def get(g, slot):
            return pltpu.make_async_copy(x_hbm.at[pl.ds(base + g * w, w)], rows_v.at[slot], gsem.at[slot])

        def put(g, slot, k):
            return pltpu.make_async_copy(rows_v.at[slot], out_hbm.at[idx_v.at[k, g]], osem.at[slot])

        get(0, 0).start()

        @pl.loop(0, n_chunks, step=2)
        def _(g):
            for b in range(2):
                gg = g + b
                get(gg, b).wait()

                @pl.when(gg + 1 < n_chunks)
                def _():
                    @pl.when(gg >= 1)
                    def _():
                        put(gg - 1, 1 - b, 0).wait()
                        put(gg - 1, 1 - b, 1).wait()
                    get(gg + 1, 1 - b).start()

                put(gg, b, 0).start()
                put(gg, b, 1).start()

        for k in range(2):
            put(n_chunks - 2, 0, k).wait()
            put(n_chunks - 1, 1, k).wait()

    return scatter(x, pos_w)


def _sc_gather_rows(table, idx):
    n = idx.shape[0]
    dw = table.shape[1]
    nc, ns = _sc_workers()
    nw = nc * ns
    b_per_w = n // nw
    w = min(SC_WINDOW, b_per_w // 2)
    n_chunks = b_per_w // w
    assert n % nw == 0 and b_per_w % (2 * w) == 0
    mesh = plsc.VectorSubcoreMesh(core_axis_name="c", subcore_axis_name="s")

    @functools.partial(pl.kernel, mesh=mesh, out_type=jax.ShapeDtypeStruct((n, dw), table.dtype),
                       scratch_types=[pltpu.VMEM((b_per_w,), jnp.int32), pltpu.VMEM((2, w, dw), table.dtype),
                                      pltpu.SemaphoreType.DMA((2,)), pltpu.SemaphoreType.DMA((2,))])
    def gather(table_hbm, idx_hbm, out_hbm, idx_v, rows_v, gsem, osem):
        wid = lax.axis_index("s") * nc + lax.axis_index("c")
        base = wid * b_per_w
        pltpu.sync_copy(idx_hbm.at[pl.ds(base, b_per_w)], idx_v)

        def get(g, slot):
            return pltpu.make_async_copy(table_hbm.at[idx_v.at[pl.ds(g * w, w)]], rows_v.at[slot], gsem.at[slot])

        def put(g, slot):
            return pltpu.make_async_copy(rows_v.at[slot], out_hbm.at[pl.ds(base + g * w, w)], osem.at[slot])

        get(0, 0).start()

        @pl.loop(0, n_chunks, step=2)
        def _(g):
            for b in range(2):
                gg = g + b
                get(gg, b).wait()

                @pl.when(gg + 1 < n_chunks)
                def _():
                    @pl.when(gg >= 1)
                    def _():
                        put(gg - 1, 1 - b).wait()
                    get(gg + 1, 1 - b).start()

                put(gg, b).start()

        put(n_chunks - 2, 0).wait()
        put(n_chunks - 1, 1).wait()

    return gather(table, idx)


def _gmm_kernel(te_ref, nv_ref, xs_ref, wg_ref, wu_ref, wd_ref, ys_ref, u_ref):
    @pl.when(pl.program_id(0) < nv_ref[0])
    def _():
        u_ref[...] = _unpack_bf16_pairs(xs_ref[...]).astype(BF16)
        ys_ref[...] = _pack_bf16_pairs(_swiglu_partial(u_ref, wg_ref.at[0], wu_ref.at[0], wd_ref.at[0]))


def _gmm(xs, tile_expert, n_valid, wg, wu, wd, *, tg):
    p, dw = xs.shape
    d = 2 * dw
    f = wg.shape[2]
    grid_spec = pltpu.PrefetchScalarGridSpec(
        num_scalar_prefetch=2,
        grid=(p // tg,),
        in_specs=[pl.BlockSpec((tg, dw), lambda i, te, nv: (i, 0)),
                  pl.BlockSpec((1, d, f), lambda i, te, nv: (te[i], 0, 0)),
                  pl.BlockSpec((1, d, f), lambda i, te, nv: (te[i], 0, 0)),
                  pl.BlockSpec((1, f, d), lambda i, te, nv: (te[i], 0, 0))],
        out_specs=pl.BlockSpec((tg, dw), lambda i, te, nv: (i, 0)),
        scratch_shapes=[pltpu.VMEM((tg, d), BF16)])
    return pl.pallas_call(
        _gmm_kernel,
        out_shape=jax.ShapeDtypeStruct((p, dw), jnp.uint32),
        grid_spec=grid_spec,
        compiler_params=_cparams(("arbitrary",)),
        name="gmm",
    )(tile_expert, n_valid, xs, wg, wu, wd)


def _combine_kernel(x_ref, info_ref, y1_ref, y2_ref, gt_ref, gf_ref, out_ref, *, final):
    info = info_ref[...]
    p1 = info[:, LANE_P1:LANE_P1 + 1]
    p2 = info[:, LANE_P2:LANE_P2 + 1]
    moe = p1 * _unpack_bf16_pairs(y1_ref[...]) + p2 * _unpack_bf16_pairs(y2_ref[...])
    y = x_ref[...] + gt_ref[0] * moe
    if final:
        r = lax.rsqrt(jnp.mean(y * y, axis=-1, keepdims=True) + EPS)
        y = (y * r) * gf_ref[...]
    out_ref[...] = y


def _combine(h, info, yg, mod3, g_final, mod_idx, *, tm, n_rows, final):
    d = h.shape[1]
    n_t = n_rows // tm
    kern = functools.partial(_combine_kernel, final=final)
    return pl.pallas_call(
        kern,
        out_shape=jax.ShapeDtypeStruct((n_rows, d), F32),
        grid=(n_t,),
        in_specs=[pl.BlockSpec((tm, d), lambda i: (i, 0)),
                  pl.BlockSpec((tm, GATE_PAD), lambda i: (i, 0)),
                  pl.BlockSpec((tm, d // 2), lambda i: (i, 0)),
                  pl.BlockSpec((tm, d // 2), lambda i: (n_t + i, 0)),
                  pl.BlockSpec((1, 1, d), lambda i: (mod_idx(i) * N_MOD + 5, 0, 0)),
                  pl.BlockSpec((1, d), lambda i: (0, 0))],
        out_specs=pl.BlockSpec((tm, d), lambda i: (i, 0)),
        compiler_params=_cparams(("arbitrary",)),
        name="combine",
    )(h, info, yg, yg, mod3, g_final)


def _moe(h, g, mod3, w_router, wg, wu, wd, g_final, mod_idx, *, tm, tg, n_rows, final):
    n_exp = wg.shape[0]
    up, info, cnt = _route_call(h, g, mod3, w_router, mod_idx, tm=tm, n_rows=n_rows)
    i1 = info[:, LANE_I1].astype(jnp.int32)
    i2 = info[:, LANE_I2].astype(jnp.int32)
    r1 = info[:, LANE_R1].astype(jnp.int32)
    r2 = info[:, LANE_R2].astype(jnp.int32)
    counts = cnt[0, :n_exp].astype(jnp.int32)
    padded = (counts + tg - 1) // tg * tg
    ends = jnp.cumsum(padded)
    starts = ends - padded
    eye = jnp.arange(n_exp, dtype=jnp.int32)
    pos1 = jnp.sum(jnp.where(i1[:, None] == eye, starts, 0), axis=1) + r1
    pos2 = jnp.sum(jnp.where(i2[:, None] == eye, starts, 0), axis=1) + r2
    p_rows = -(-(TOP_K * n_rows + n_exp * (tg - 1)) // tg) * tg
    tile_row = jnp.arange(p_rows // tg, dtype=jnp.int32) * tg
    tile_expert = jnp.minimum(jnp.sum(tile_row[:, None] >= ends[None, :], axis=1), n_exp - 1).astype(jnp.int32)
    n_valid = (ends[-1] // tg).reshape(1).astype(jnp.int32)
    xs = _sc_scatter_rows(up, jnp.stack([pos1, pos2]), p_rows)
    ys = _gmm(xs, tile_expert, n_valid, wg, wu, wd, tg=tg)
    yg = _sc_gather_rows(ys, jnp.concatenate([pos1, pos2]))
    return _combine(h, info, yg, mod3, g_final, mod_idx, tm=tm, n_rows=n_rows, final=final)


def _final_norm_kernel(x_ref, g_ref, o_ref):
    x = x_ref[...]
    r = lax.rsqrt(jnp.mean(x * x, axis=-1, keepdims=True) + EPS)
    o_ref[...] = (x * r) * g_ref[...]


def _final_norm(h, g, *, tm, n_rows):
    d = h.shape[1]
    return pl.pallas_call(
        _final_norm_kernel,
        out_shape=jax.ShapeDtypeStruct((n_rows, d), F32),
        grid=(n_rows // tm,),
        in_specs=[pl.BlockSpec((tm, d), lambda i: (i, 0)), pl.BlockSpec((1, d), lambda i: (0, 0))],
        out_specs=pl.BlockSpec((tm, d), lambda i: (i, 0)),
        compiler_params=_cparams(("arbitrary",)),
        name="final_norm",
    )(h, g)


def kernel(x, c, ctx, c_ctx, w_ada, b_ada, g_mix, w_in, conv_qk, b_if, head_gain, w_pool, pool_scale, w_pa, w_pb,
           w_out, g_ffn, w_ff_gate, w_ff_up, w_ff_down, w_router, w_exp_gate, w_exp_up, w_exp_down, g_final):
    b, s, d = x.shape
    lc = ctx.shape[1]
    depth = w_ada.shape[0]
    width = w_pa.shape[1]
    pw = w_pb.shape[1]
    n_gate = b_if.shape[1] * b_if.shape[2] * b_if.shape[3]
    assert lc == CHUNK and s % CHUNK == 0 and s % GRID_W == 0 and width == d and n_gate == 16
    n_lat, n_ctx = b * s, b * lc
    tm, tm_ffn = 512, 512
    assert s % tm_ffn == 0 and n_ctx % tm_ffn == 0 and tm_ffn % tm == 0

    def mod_idx(t):
        return lambda i: jnp.where(i < n_lat // t, i // (s // t), b)

    src = (x.reshape(n_lat, d), ctx.reshape(n_ctx, d), 0)

    n_mod_rows = 16
    cc = jnp.zeros((n_mod_rows, d), F32).at[:b].set(c).at[b].set(c_ctx)
    mod = _ada(cc, w_ada, b_ada)

    if_off, pool_off, ga_off = 4 * width, 4 * width + n_gate, 4 * width + n_gate + pw
    n_main = 4 * width + 2 * d

    exp16 = []
    for l in range(depth):
        last = l == depth - 1
        mod3 = mod[l].reshape(n_mod_rows * N_MOD, 1, d)
        w_l = w_in[l]
        w_cat = jnp.concatenate(
            [w_l[:, :if_off], w_l[:, ga_off:], w_l[:, pool_off:ga_off], w_l[:, if_off:pool_off],
             jnp.zeros((d, GATE_PAD - n_gate), F32)], axis=1).astype(BF16)
        k_scale = float((width // M_HEADS) ** -0.5)
        zm, zp, zg = _inproj(src, g_mix[l].reshape(1, d), mod3, w_cat, conv_qk[l], mod_idx(tm), tm=tm, n_main=n_main,
                             n_pool=pw, n_qk=2 * width, k_scale=k_scale, n_lat=n_lat, lat_len=s, ctx_len=lc)

        bias = b_if[l].reshape(n_gate)
        bias_c = jnp.zeros((1, GATE_PAD), F32).at[0, :n_gate].set(bias)
        bias_r = jnp.broadcast_to(bias[:, None], (n_gate, CHUNK))
        hf, hb = _mlstm(zm, zg, zg[:, :n_gate].T, bias_c, bias_r, b=b, s=s, width=width, dirs=(0, 1))

        n_rows = n_lat if last else n_lat + n_ctx
        pm_lat = _pool(zp, n_seq=b, seq_len=s, row_block0=0, grid_rows=s // GRID_W)
        pm_ctx = pm_lat if last else _pool(zp, n_seq=b, seq_len=lc, row_block0=n_lat // lc, grid_rows=None)

        h = _mixout(hf, hb, zm, pm_lat, pm_ctx, src, mod3, head_gain[l].reshape(1, width), w_pool[l].astype(BF16),
                    pool_scale[l].reshape(1, pw), w_pa[l].astype(BF16), w_pb[l].astype(BF16),
                    w_out[l].astype(BF16), mod_idx(tm), tm=tm, n_rows=n_rows)

        j = l // 2
        if l % 2 == 0:
            side = ()
            if not last:
                side = tuple(w[(l + 1) // 2].reshape(-1, w.shape[-1]) for w in (w_exp_gate, w_exp_up, w_exp_down))
            h, *exp16 = _ffn(h, g_ffn[l].reshape(1, d), mod3, w_ff_gate[j].astype(BF16), w_ff_up[j].astype(BF16),
                             w_ff_down[j].astype(BF16), mod_idx(tm_ffn), tm=tm_ffn,
                             tf=_pick_tile(w_ff_gate.shape[2], 2816), side=side)
            if last:
                h = _final_norm(h, g_final.reshape(1, d), tm=tm, n_rows=n_lat)
        else:
            if not exp16:
                exp16 = [w[j].astype(BF16) for w in (w_exp_gate, w_exp_up, w_exp_down)]
            wg16, wu16, wd16 = (w16.reshape(w.shape[1:]) for w16, w in zip(exp16, (w_exp_gate, w_exp_up, w_exp_down)))
            h = _moe(h, g_ffn[l].reshape(1, d), mod3, w_router[j], wg16, wu16, wd16, g_final.reshape(1, d),
                     mod_idx(tm), tm=tm, tg=512, n_rows=h.shape[0], final=last)
            exp16 = []
        src = (h, h, n_lat)
    return h[:n_lat].reshape(b, s, d)
```

```python
import functools

import jax
import jax.numpy as jnp
from jax import lax
from jax.experimental import pallas as pl
from jax.experimental.pallas import tpu as pltpu
from jax.experimental.pallas import tpu_sc as plsc

F32 = jnp.float32
BF16 = jnp.bfloat16
EPS = 1e-6
M_HEADS = 4
GRID_W = 64
POOL_WINDOWS = (2, 4, 8, 16)
POOL_GROUP_DIM = 128
CHUNK = 256
N_MOD = 6
TOP_K = 2
SC_WINDOW = 64
GATE_PAD = 128
VMEM_LIMIT = 56 * 1024 * 1024


def _pick_tile(n, target, quantum=128):
    t = min(n, target) // quantum * quantum
    while n % t:
        t -= quantum
    return t


def _cparams(sem):
    return pltpu.CompilerParams(dimension_semantics=sem, vmem_limit_bytes=VMEM_LIMIT)


def _sigmoid(x):
    return 0.5 * jnp.tanh(0.5 * x) + 0.5


def _silu(x):
    h = 0.5 * x
    return h * jnp.tanh(h) + h


def _log_sigmoid(x):
    return jnp.minimum(x, 0.0) - jnp.log(1.0 + jnp.exp(-jnp.abs(x)))


def _split3(x):
    hi = x.astype(BF16)
    r1 = x - hi.astype(F32)
    mid = r1.astype(BF16)
    lo = (r1 - mid.astype(F32)).astype(BF16)
    return hi, mid, lo


def _dot(a, b):
    return jnp.dot(a, b, preferred_element_type=F32)


def _dot_nt(a, b):
    return lax.dot_general(a, b, (((1,), (1,)), ((), ())), preferred_element_type=F32)


def _dot_tn(a, b):
    return lax.dot_general(a, b, (((0,), (0,)), ((), ())), preferred_element_type=F32)


def _rms_mod(x, g, sc, sh):
    r = lax.rsqrt(jnp.mean(x * x, axis=-1, keepdims=True) + EPS)
    return (x * r) * g * (1.0 + sc) + sh


def _ada_kernel(c_ref, w_ref, b_ref, o_ref):
    c = c_ref[...]
    o_ref[0] = _dot(_silu(c).astype(BF16), w_ref[0].astype(BF16)) + b_ref[0]


def _ada(cc, w_ada, b_ada):
    depth, d, n = w_ada.shape
    tn = 1024
    return pl.pallas_call(
        _ada_kernel,
        out_shape=jax.ShapeDtypeStruct((depth, cc.shape[0], n), F32),
        grid=(depth, n // tn),
        in_specs=[pl.BlockSpec(cc.shape, lambda l, j: (0, 0)),
                  pl.BlockSpec((1, d, tn), lambda l, j: (l, 0, j)),
                  pl.BlockSpec((1, 1, tn), lambda l, j: (l, 0, j))],
        out_specs=pl.BlockSpec((1, cc.shape[0], tn), lambda l, j: (l, 0, j)),
        compiler_params=_cparams(("arbitrary", "arbitrary")),
        name="ada",
    )(cc, w_ada, b_ada.reshape(depth, 1, n))


def _inproj_kernel(xa_ref, xb_ref, xpa_ref, xpb_ref, xna_ref, xnb_ref, g_ref, sc_ref, sh_ref, w_ref, cw_ref,
                   zm_ref, zp_ref, zg_ref, u_ref, uh_ref,
                   *, n_main, n_pool, n_qk, cw, k_scale, n_lat_tiles, lat_len, ctx_len):
    i = pl.program_id(0)
    tm = xa_ref.shape[0]
    is_ctx = i >= n_lat_tiles
    g, sc, sh = g_ref[...], sc_ref[0], sh_ref[0]
    u_ref[...] = _rms_mod(jnp.where(is_ctx, xb_ref[...], xa_ref[...]), g, sc, sh).astype(BF16)
    uh_ref[0:8, :] = _rms_mod(jnp.where(is_ctx, xpb_ref[...], xpa_ref[...]), g, sc, sh).astype(BF16)
    uh_ref[8:16, :] = _rms_mod(jnp.where(is_ctx, xnb_ref[...], xna_ref[...]), g, sc, sh).astype(BF16)

    pos0 = lax.rem(i * tm, lat_len)
    lat_first_row = jnp.where(pos0 == 0, 0, -1)
    lat_last_row = jnp.where(pos0 + tm == lat_len, tm - 1, -1)
    sub = 16
    srow = lax.broadcasted_iota(jnp.int32, (sub, cw), 0)
    slabs = sorted({0, tm - sub} | {k * ctx_len - sub for k in range(1, tm // ctx_len)}
                   | {k * ctx_len for k in range(1, tm // ctx_len)})

    for c in range(n_qk // cw):
        cs = slice(c * cw, (c + 1) * cw)
        w0, w1, w2 = cw_ref[0:1, cs], cw_ref[1:2, cs], cw_ref[2:3, cs]
        scale = k_scale if c * cw >= n_qk // 2 else 1.0
        z = _dot(u_ref[...], w_ref[:, cs])
        zh = _dot(uh_ref[...], w_ref[:, cs])
        y = _silu(pltpu.roll(z, 1, 0) * w0 + z * w1 + pltpu.roll(z, tm - 1, 0) * w2) * scale
        zm_ref[:, cs] = y.astype(BF16)
        for r0 in slabs:
            zs = z[r0:r0 + sub, :]
            above = zh[7:8, :] if r0 == 0 else z[r0 - 1:r0, :]
            below = zh[8:9, :] if r0 + sub == tm else z[r0 + sub:r0 + sub + 1, :]
            grow = srow + r0
            in_ctx = grow & (ctx_len - 1)
            first = jnp.where(is_ctx, in_ctx, grow - lat_first_row) == 0
            last = jnp.where(is_ctx, in_ctx - (ctx_len - 1), grow - lat_last_row) == 0
            zm1 = jnp.where(srow == 0, above, pltpu.roll(zs, 1, 0))
            zm1 = jnp.where(first, 0.0, zm1)
            zp1 = jnp.where(srow == sub - 1, below, pltpu.roll(zs, sub - 1, 0))
            zp1 = jnp.where(last, 0.0, zp1)
            ys = _silu(zm1 * w0 + zs * w1 + zp1 * w2) * scale
            zm_ref[r0:r0 + sub, cs] = ys.astype(BF16)
    for c in range(n_qk // cw, n_main // cw):
        zm_ref[:, c * cw:(c + 1) * cw] = _dot(u_ref[...], w_ref[:, c * cw:(c + 1) * cw]).astype(BF16)
    for c0 in range(0, n_pool, cw):
        c1 = min(c0 + cw, n_pool)
        zp_ref[:, c0:c1] = _dot(u_ref[...], w_ref[:, n_main + c0:n_main + c1])
    zg_ref[...] = _dot(u_ref[...], w_ref[:, n_main + n_pool:])


def _inproj(src, g, mod3, w_cat, conv_w, mod_idx, *, tm, n_main, n_pool, n_qk, k_scale, n_lat, lat_len, ctx_len):
    a, b, b_off = src
    d = a.shape[1]
    r = n_lat + b.shape[0] - b_off
    ncols = w_cat.shape[1]
    halo = 8
    assert tm % ctx_len == 0 and lat_len % tm == 0 and ctx_len & (ctx_len - 1) == 0 and b_off % tm == 0
    na_t, hb = n_lat // tm, tm // halo
    a_last, b_last = a.shape[0] // halo - 1, b.shape[0] // halo - 1
    kern = functools.partial(_inproj_kernel, n_main=n_main, n_pool=n_pool, n_qk=n_qk, cw=512, k_scale=k_scale,
                             n_lat_tiles=na_t, lat_len=lat_len, ctx_len=ctx_len)
    return pl.pallas_call(
        kern,
        out_shape=(jax.ShapeDtypeStruct((r, n_main), BF16),
                   jax.ShapeDtypeStruct((r, n_pool), F32),
                   jax.ShapeDtypeStruct((r, GATE_PAD), F32)),
        grid=(r // tm,),
        in_specs=[pl.BlockSpec((tm, d), lambda i: (jnp.minimum(i, na_t - 1), 0)),
                  pl.BlockSpec((tm, d), lambda i: (jnp.maximum(i - na_t, 0) + b_off // tm, 0)),
                  pl.BlockSpec((halo, d), lambda i: (jnp.clip(i * hb - 1, 0, a_last), 0)),
                  pl.BlockSpec((halo, d), lambda i: (jnp.clip(b_off // halo + (i - na_t) * hb - 1, 0, b_last), 0)),
                  pl.BlockSpec((halo, d), lambda i: (jnp.clip((i + 1) * hb, 0, a_last), 0)),
                  pl.BlockSpec((halo, d), lambda i: (jnp.clip(b_off // halo + (i - na_t + 1) * hb, 0, b_last), 0)),
                  pl.BlockSpec((1, d), lambda i: (0, 0)),
                  pl.BlockSpec((1, 1, d), lambda i: (mod_idx(i) * N_MOD + 1, 0, 0)),
                  pl.BlockSpec((1, 1, d), lambda i: (mod_idx(i) * N_MOD + 0, 0, 0)),
                  pl.BlockSpec((d, ncols), lambda i: (0, 0)),
                  pl.BlockSpec((3, n_qk), lambda i: (0, 0))],
        out_specs=(pl.BlockSpec((tm, n_main), lambda i: (i, 0)),
                   pl.BlockSpec((tm, n_pool), lambda i: (i, 0)),
                   pl.BlockSpec((tm, GATE_PAD), lambda i: (i, 0))),
        scratch_shapes=[pltpu.VMEM((tm, d), BF16), pltpu.VMEM((2 * halo, d), BF16)],
        compiler_params=_cparams(("arbitrary",)),
        name="inproj",
    )(a, b, a, b, a, b, g, mod3, mod3, w_cat, conv_w)


def _running_max_rows(x, rev):
    n = x.shape[0]
    row = lax.broadcasted_iota(jnp.int32, x.shape, 0)
    sh = 1
    while sh < n:
        if rev:
            shifted = jnp.where(row < n - sh, pltpu.roll(x, n - sh, 0), -jnp.inf)
        else:
            shifted = jnp.where(row >= sh, pltpu.roll(x, sh, 0), -jnp.inf)
        x = jnp.maximum(x, shifted)
        sh *= 2
    return x


def _mlstm_kernel(*refs, dh, dirs):
    L = CHUNK
    nd = len(dirs)
    bias_c, bias_r = refs[5 * nd:5 * nd + 2]
    outs = refs[5 * nd + 2:6 * nd + 2]
    c_ref, c16_ref, n_ref, n16_ref, m_ref = refs[6 * nd + 2:]

    @pl.when(pl.program_id(1) == 0)
    def _():
        c_ref[...] = jnp.zeros_like(c_ref)
        c16_ref[...] = jnp.zeros_like(c16_ref)
        n_ref[...] = jnp.zeros_like(n_ref)
        n16_ref[...] = jnp.zeros_like(n16_ref)
        m_ref[...] = jnp.zeros_like(m_ref)

    ri = lax.broadcasted_iota(jnp.int32, (L, L), 0)
    ci = lax.broadcasted_iota(jnp.int32, (L, L), 1)
    lower = ci <= ri
    upper = ci >= ri
    lower_b = jnp.where(lower, 1.0, 0.0).astype(BF16)
    upper_b = jnp.where(upper, 1.0, 0.0).astype(BF16)

    for di, d in enumerate(dirs):
        q_ref, k_ref, v_ref, gc_ref, gr_ref = refs[5 * di:5 * di + 5]
        out_ref = outs[di]
        rev = d == 1
        gc = gc_ref[...] + bias_c[...]
        gr = gr_ref[...] + bias_r[...]
        fc = _log_sigmoid(gc)
        fr = _log_sigmoid(gr)
        t_col = upper_b if rev else lower_b
        t_row = lower_b if rev else upper_b
        bc3 = _dot(t_col, jnp.concatenate(_split3(fc), axis=1))
        bcum_c = bc3[:, :GATE_PAD] + bc3[:, GATE_PAD:2 * GATE_PAD] + bc3[:, 2 * GATE_PAD:]
        br3 = _dot(jnp.concatenate(_split3(fr), axis=0), t_row)
        bcum_r = br3[0:16, :] + br3[16:32, :] + br3[32:48, :]
        mask = upper if rev else lower
        end = 0 if rev else L - 1

        b_al = pltpu.roll(bcum_c, GATE_PAD - M_HEADS, axis=1)
        g_col = gc - b_al
        m_st = m_ref[di:di + 1, :]
        mx = jnp.maximum(m_st, _running_max_rows(g_col, rev))
        e_neg_all = jnp.exp(-(b_al + mx))
        mx_end = mx[end:end + 1, :]
        wk_all = jnp.exp(g_col - mx_end)
        wc_all = jnp.exp(m_st - mx_end)
        m_ref[di:di + 1, :] = b_al[end:end + 1, :] + mx_end

        ones_rep = jnp.ones((L, GATE_PAD), BF16)
        for h in range(M_HEADS):
            s_idx = di * M_HEADS + h
            col = d * 2 * M_HEADS + h

            def rep(x_all):
                return jnp.broadcast_to(x_all[:, col:col + 1], (x_all.shape[0], GATE_PAD))

            def wide(x_rep):
                return jnp.concatenate([x_rep] * (dh // GATE_PAD), axis=1)

            mx_rep = rep(mx)
            g_row = gr[col:col + 1, :] - bcum_r[col + M_HEADS:col + M_HEADS + 1, :]
            p = jnp.exp(jnp.where(mask, g_row - wide(mx_rep), -jnp.inf))
            w_inter = jnp.exp(rep(m_st) - mx_rep)
            e_neg = rep(e_neg_all)
            wk16 = rep(wk_all).astype(BF16)
            wc = wc_all[:, col:col + 1]

            sl = slice(h * dh, (h + 1) * dh)
            q = q_ref[:, sl]
            k = k_ref[:, sl]
            v = v_ref[:, sl]
            scores = (_dot_nt(q, k) * p).astype(BF16)
            num = wide(w_inter) * _dot(q, c16_ref[s_idx]) + _dot(scores, v)
            den = w_inter * _dot(q, n16_ref[s_idx]) + _dot(scores, ones_rep)
            inv = 1.0 / jnp.maximum(jnp.abs(den), e_neg)
            out_ref[:, sl] = (num * wide(inv)).astype(BF16)

            upd = _dot_tn(k, jnp.concatenate([v * wide(wk16), wk16], axis=1))
            c_new = wc * c_ref[s_idx] + upd[:, :dh]
            c_ref[s_idx] = c_new
            c16_ref[s_idx] = c_new.astype(BF16)
            n_new = wc * n_ref[s_idx] + upd[:, dh:]
            n_ref[s_idx] = n_new
            n16_ref[s_idx] = n_new.astype(BF16)


def _mlstm(zm, zg, zg_t, bias_c, bias_r, *, b, s, width, dirs):
    r = zm.shape[0]
    L = CHUNK
    n_lat = s // L
    lat_blocks = b * n_lat
    nd = len(dirs)

    def blk_f(bi, i):
        return jnp.where(i == 0, lat_blocks + bi, bi * n_lat + i - 1)

    def blk_b(bi, i):
        return jnp.where(i == 0, lat_blocks + bi, bi * n_lat + n_lat - i)

    blks = [blk_b if d == 1 else blk_f for d in dirs]

    def specs(blk):
        return [pl.BlockSpec((L, width), lambda bi, i: (blk(bi, i), 0)),
                pl.BlockSpec((L, width), lambda bi, i: (blk(bi, i), 1)),
                pl.BlockSpec((L, width), lambda bi, i: (blk(bi, i), 2)),
                pl.BlockSpec((L, GATE_PAD), lambda bi, i: (blk(bi, i), 0)),
                pl.BlockSpec((16, L), lambda bi, i: (0, blk(bi, i)))]

    dh = width // M_HEADS
    kern = functools.partial(_mlstm_kernel, dh=dh, dirs=tuple(dirs))
    out = jax.ShapeDtypeStruct((r, width), BF16)
    in_specs = []
    for blk in blks:
        in_specs += specs(blk)
    in_specs += [pl.BlockSpec((1, GATE_PAD), lambda bi, i: (0, 0)), pl.BlockSpec((16, L), lambda bi, i: (0, 0))]
    return pl.pallas_call(
        kern,
        out_shape=(out,) * nd,
        grid=(b, n_lat + 1),
        in_specs=in_specs,
        out_specs=tuple(pl.BlockSpec((L, width), functools.partial(lambda blk, bi, i: (blk(bi, i), 0), blk))
                        for blk in blks),
        scratch_shapes=[pltpu.VMEM((nd * M_HEADS, dh, dh), F32),
                        pltpu.VMEM((nd * M_HEADS, dh, dh), BF16),
                        pltpu.VMEM((nd * M_HEADS, dh, GATE_PAD), F32),
                        pltpu.VMEM((nd * M_HEADS, dh, GATE_PAD), BF16),
                        pltpu.VMEM((8, GATE_PAD), F32)],
        compiler_params=_cparams(("arbitrary", "arbitrary")),
        name="mlstm",
    )(*((zm, zm, zm, zg, zg_t) * nd), bias_c, bias_r)


POOL_BLOCK = 256


def _band(n, w, seg):
    r = lax.broadcasted_iota(jnp.int32, (n, n), 0)
    c = lax.broadcasted_iota(jnp.int32, (n, n), 1)
    d = c - r
    shift = seg.bit_length() - 1
    ok = (d >= -(w // 2)) & (d <= w // 2 - 1) & ((r >> shift) == (c >> shift))
    return jnp.where(ok, 1.0, 0.0).astype(BF16)


def _window_count(idx, n, w):
    return (jnp.minimum(idx + w // 2, n) - jnp.maximum(idx - w // 2, 0)).astype(F32)


def _pool_kernel(z_ref, o_ref, pad_ref, *, grid_rows):
    n = z_ref.shape[0]
    blk = min(POOL_BLOCK, n)
    seg = blk if grid_rows is None else GRID_W
    seg_len = n if grid_rows is None else GRID_W
    halo = (max(POOL_WINDOWS) // 2) * GRID_W
    rowi = lax.broadcasted_iota(jnp.int32, (blk, POOL_GROUP_DIM), 0)
    if grid_rows is not None:
        zeros = jnp.zeros((halo, POOL_GROUP_DIM), F32)
        pad_ref[0:halo, :] = zeros
        pad_ref[halo + n:, :] = zeros
    for g, w in enumerate(POOL_WINDOWS):
        sl = slice(g * POOL_GROUP_DIM, (g + 1) * POOL_GROUP_DIM)
        band = _band(blk, w, seg)
        inv_w = 1.0 / _window_count(rowi & (seg - 1), seg_len, w)
        for b in range(n // blk):
            rows = slice(b * blk, (b + 1) * blk)
            x = z_ref[rows, sl]
            hi = x.astype(BF16)
            lo = (x - hi.astype(F32)).astype(BF16)
            y2 = _dot(band, jnp.concatenate([hi, lo], axis=1))
            y = (y2[:, :POOL_GROUP_DIM] + y2[:, POOL_GROUP_DIM:]) * inv_w
            if grid_rows is None:
                o_ref[rows, sl] = (y - x).astype(BF16)
            else:
                pad_ref[halo + b * blk:halo + (b + 1) * blk, :] = y
        if grid_rows is not None:
            shift = GRID_W.bit_length() - 1
            for b in range(n // blk):
                acc = None
                for d in range(-(w // 2), w // 2):
                    lo_r = halo + b * blk + d * GRID_W
                    t = pad_ref[lo_r:lo_r + blk, :]
                    acc = t if acc is None else acc + t
                inv_h = 1.0 / _window_count((rowi + b * blk) >> shift, grid_rows, w)
                rows = slice(b * blk, (b + 1) * blk)
                o_ref[rows, sl] = (acc * inv_h - z_ref[rows, sl]).astype(BF16)


def _pool(zp, *, n_seq, seq_len, row_block0, grid_rows):
    width = zp.shape[1]
    assert seq_len % POOL_BLOCK == 0 if grid_rows is not None else seq_len <= POOL_BLOCK
    kern = functools.partial(_pool_kernel, grid_rows=grid_rows)
    halo = (max(POOL_WINDOWS) // 2) * GRID_W
    return pl.pallas_call(
        kern,
        out_shape=jax.ShapeDtypeStruct((n_seq * seq_len, width), BF16),
        grid=(n_seq,),
        in_specs=[pl.BlockSpec((seq_len, width), lambda s: (row_block0 + s, 0))],
        out_specs=pl.BlockSpec((seq_len, width), lambda s: (s, 0)),
        scratch_shapes=[pltpu.VMEM((seq_len + 2 * halo, POOL_GROUP_DIM), F32)],
        compiler_params=_cparams(("arbitrary",)),
        name="pool",
    )(zp)


def _fold_pool_kernel(wp_ref, ps_ref, wpb_ref, o_ref):
    for g in range(len(POOL_WINDOWS)):
        sl = slice(g * POOL_GROUP_DIM, (g + 1) * POOL_GROUP_DIM)
        a_hi, a_mid, _ = _split3(wp_ref[g] * ps_ref[:, sl])
        b_hi, b_mid, _ = _split3(wpb_ref[sl, :])
        o_ref[sl, :] = (_dot(a_hi, b_hi) + _dot(a_hi, b_mid) + _dot(a_mid, b_hi)).astype(BF16)


def _fold_pool(w_pool, pool_scale, w_pb):
    return pl.pallas_call(
        _fold_pool_kernel,
        out_shape=jax.ShapeDtypeStruct(w_pb.shape, BF16),
        compiler_params=pltpu.CompilerParams(vmem_limit_bytes=VMEM_LIMIT),
        name="fold_pool",
    )(w_pool, pool_scale, w_pb)


def _mixout_kernel(hf_ref, hb_ref, o_ref, ga_ref, gb_ref, pml_ref, pmc_ref, xa_ref, xb_ref, gt_ref, hg_ref,
                   wpa_ref, wpb_ref, wout_ref, out_ref, *, dh, n_lat_tiles):
    is_ctx = pl.program_id(0) >= n_lat_tiles
    hm = hf_ref[...].astype(F32) + hb_ref[...].astype(F32)
    parts = []
    for h in range(M_HEADS):
        t = hm[:, h * dh:(h + 1) * dh]
        mu = jnp.mean(t, axis=-1, keepdims=True)
        tc = t - mu
        var = jnp.mean(tc * tc, axis=-1, keepdims=True)
        parts.append(tc * lax.rsqrt(var + EPS))
    hn = jnp.concatenate(parts, axis=-1)
    hn = _sigmoid(o_ref[...].astype(F32)) * (hn * hg_ref[...])
    a = _dot(hn.astype(BF16), wpa_ref[...])
    bmat = _dot(jnp.where(is_ctx, pmc_ref[...], pml_ref[...]), wpb_ref[...])
    merged = _sigmoid(ga_ref[...].astype(F32)) * a + _sigmoid(gb_ref[...].astype(F32)) * bmat
    y = _dot(merged.astype(BF16), wout_ref[...])
    out_ref[...] = jnp.where(is_ctx, xb_ref[...], xa_ref[...]) + gt_ref[0] * y


def _mixout(hf, hb, zm, pm_lat, pm_ctx, src, mod3, head_gain, w_pa, w_pb, w_out, mod_idx, *, tm, n_rows):
    xa, xb, b_off = src
    d = xa.shape[1]
    width = hf.shape[1]
    pw = pm_lat.shape[1]
    n_lat_tiles = pm_lat.shape[0] // tm
    o_blk = 3
    full = lambda shape: pl.BlockSpec(shape, lambda i: (0,) * len(shape))
    kern = functools.partial(_mixout_kernel, dh=width // M_HEADS, n_lat_tiles=n_lat_tiles)
    return pl.pallas_call(
        kern,
        out_shape=jax.ShapeDtypeStruct((n_rows, d), F32),
        grid=(n_rows // tm,),
        in_specs=[pl.BlockSpec((tm, width), lambda i: (i, 0)),
                  pl.BlockSpec((tm, width), lambda i: (i, 0)),
                  pl.BlockSpec((tm, width), lambda i: (i, o_blk)),
                  pl.BlockSpec((tm, d), lambda i: (i, o_blk + 1)),
                  pl.BlockSpec((tm, d), lambda i: (i, o_blk + 2)),
                  pl.BlockSpec((tm, pw), lambda i: (jnp.minimum(i, n_lat_tiles - 1), 0)),
                  pl.BlockSpec((tm, pw), lambda i: (jnp.maximum(i - n_lat_tiles, 0), 0)),
                  pl.BlockSpec((tm, d), lambda i: (jnp.minimum(i, n_lat_tiles - 1), 0)),
                  pl.BlockSpec((tm, d), lambda i: (jnp.maximum(i - n_lat_tiles, 0) + b_off // tm, 0)),
                  pl.BlockSpec((1, 1, d), lambda i: (mod_idx(i) * N_MOD + 2, 0, 0)),
                  full((1, width)), full(w_pa.shape), full(w_pb.shape), full(w_out.shape)],
        out_specs=pl.BlockSpec((tm, d), lambda i: (i, 0)),
        compiler_params=_cparams(("arbitrary",)),
        name="mixout",
    )(hf, hb, zm, zm, zm, pm_lat, pm_ctx, xa, xb, mod3, head_gain, w_pa, w_pb, w_out)


SWIGLU_SUB = 512


def _swiglu_partial(u_ref, wg, wu, wd):
    tf = wg.shape[1]
    acc = None
    for c0 in range(0, tf, SWIGLU_SUB):
        cs = slice(c0, min(c0 + SWIGLU_SUB, tf))
        u = u_ref[...]
        act = (_silu(_dot(u, wg[:, cs])) * _dot(u, wu[:, cs])).astype(BF16)
        part = _dot(act, wd[cs, :])
        acc = part if acc is None else acc + part
    return acc


def _ffn_kernel(x_ref, g_ref, sc_ref, sh_ref, gt_ref, wg_ref, wu_ref, wd_ref, *rest, n_side):
    side_in, out_ref, side_out, u_ref = rest[:n_side], rest[n_side], rest[n_side + 1:2 * n_side + 1], rest[-1]
    j = pl.program_id(1)
    u_ref[...] = _rms_mod(x_ref[...], g_ref[...], sc_ref[0], sh_ref[0]).astype(BF16)
    contrib = gt_ref[0] * _swiglu_partial(u_ref, wg_ref, wu_ref, wd_ref)

    @pl.when(j == 0)
    def _():
        out_ref[...] = x_ref[...] + contrib
        for src, dst in zip(side_in, side_out):
            dst[...] = src[...].astype(BF16)

    @pl.when(j > 0)
    def _():
        out_ref[...] += contrib


def _side_rows(n_rows, n_steps):
    rows = (-(-n_rows // n_steps) + 15) // 16 * 16
    while n_rows % rows:
        rows += 16
    return rows


def _ffn(h, g, mod3, wg, wu, wd, mod_idx, *, tm, tf, side=()):
    r, d = h.shape
    f = wg.shape[1]
    n_steps = r // tm
    once = pl.Buffered(1) if f == tf else None
    side_specs_in, side_specs_out, side_shapes = [], [], []
    for a in side:
        rows = _side_rows(a.shape[0], n_steps)
        last = a.shape[0] // rows - 1
        spec = pl.BlockSpec((rows, a.shape[1]), functools.partial(lambda last, i, j: (jnp.minimum(i, last), 0), last))
        side_specs_in.append(spec)
        side_specs_out.append(spec)
        side_shapes.append(jax.ShapeDtypeStruct(a.shape, BF16))
    return pl.pallas_call(
        functools.partial(_ffn_kernel, n_side=len(side)),
        out_shape=(jax.ShapeDtypeStruct((r, d), F32), *side_shapes),
        grid=(n_steps, f // tf),
        in_specs=[pl.BlockSpec((tm, d), lambda i, j: (i, 0)),
                  pl.BlockSpec((1, d), lambda i, j: (0, 0)),
                  pl.BlockSpec((1, 1, d), lambda i, j: (mod_idx(i) * N_MOD + 4, 0, 0)),
                  pl.BlockSpec((1, 1, d), lambda i, j: (mod_idx(i) * N_MOD + 3, 0, 0)),
                  pl.BlockSpec((1, 1, d), lambda i, j: (mod_idx(i) * N_MOD + 5, 0, 0)),
                  pl.BlockSpec((d, tf), lambda i, j: (0, j), pipeline_mode=once),
                  pl.BlockSpec((d, tf), lambda i, j: (0, j), pipeline_mode=once),
                  pl.BlockSpec((tf, d), lambda i, j: (j, 0), pipeline_mode=once),
                  *side_specs_in],
        out_specs=(pl.BlockSpec((tm, d), lambda i, j: (i, 0)), *side_specs_out),
        scratch_shapes=[pltpu.VMEM((tm, d), BF16)],
        compiler_params=_cparams(("arbitrary", "arbitrary")),
        name="ffn",
    )(h, g, mod3, mod3, mod3, wg, wu, wd, *side)


def _route(u, wr_ref, n_exp):
    u_hi = u.astype(BF16)
    u_mid = (u - u_hi.astype(F32)).astype(BF16)
    w_hi, w_mid, _ = _split3(wr_ref[...])
    both = _dot(u_hi, jnp.concatenate([w_hi, w_mid], axis=1))
    logits = both[:, :GATE_PAD] + both[:, GATE_PAD:] + _dot(u_mid, w_hi)
    lane = lax.broadcasted_iota(jnp.int32, logits.shape, 1)
    lg = jnp.where(lane < n_exp, logits, -jnp.inf)
    m1 = jnp.max(lg, axis=1, keepdims=True)
    i1 = jnp.min(jnp.where(lg == m1, lane, 2 * GATE_PAD), axis=1, keepdims=True)
    lg2 = jnp.where(lane == i1, -jnp.inf, lg)
    m2 = jnp.max(lg2, axis=1, keepdims=True)
    i2 = jnp.min(jnp.where(lg2 == m2, lane, 2 * GATE_PAD), axis=1, keepdims=True)
    e = jnp.exp(m2 - m1)
    p1 = 1.0 / (1.0 + e)
    p2 = e * p1
    return i1, i2, p1, p2


LANE_I1, LANE_I2, LANE_P1, LANE_P2, LANE_R1, LANE_R2 = 0, 1, 2, 3, 4, 5


def _pack_bf16_pairs(x):
    k = x.shape[1] // 2
    lo = lax.bitcast_convert_type(x[:, :k].astype(BF16).astype(F32), jnp.uint32)
    hi = lax.bitcast_convert_type(x[:, k:].astype(BF16).astype(F32), jnp.uint32)
    return (lo >> 16) | (hi & jnp.uint32(0xFFFF0000))


def _unpack_bf16_pairs(w):
    lo = lax.bitcast_convert_type(w << 16, F32)
    hi = lax.bitcast_convert_type(w & jnp.uint32(0xFFFF0000), F32)
    return jnp.concatenate([lo, hi], axis=1)


def _route_kernel(x_ref, g_ref, sc_ref, sh_ref, wr_ref, up_ref, info_ref, cnt_ref, carry_ref, *, n_exp):
    @pl.when(pl.program_id(0) == 0)
    def _():
        carry_ref[...] = jnp.zeros_like(carry_ref)

    u = _rms_mod(x_ref[...], g_ref[...], sc_ref[0], sh_ref[0])
    up_ref[...] = _pack_bf16_pairs(u)
    i1, i2, p1, p2 = _route(u, wr_ref, n_exp)
    tm = u.shape[0]
    lane = lax.broadcasted_iota(jnp.int32, (tm, GATE_PAD), 1)
    sel = jnp.where((lane == i1) | (lane == i2), 1.0, 0.0)
    ri = lax.broadcasted_iota(jnp.int32, (tm, tm), 0)
    ci = lax.broadcasted_iota(jnp.int32, (tm, tm), 1)
    before = jnp.where(ci < ri, 1.0, 0.0).astype(BF16)
    rank = carry_ref[0:1, :] + _dot(before, sel.astype(BF16))
    r1 = jnp.sum(jnp.where(lane == i1, rank, 0.0), axis=1, keepdims=True)
    r2 = jnp.sum(jnp.where(lane == i2, rank, 0.0), axis=1, keepdims=True)
    carry_ref[0:1, :] = carry_ref[0:1, :] + jnp.sum(sel, axis=0, keepdims=True)
    cnt_ref[...] = carry_ref[...]
    info = jnp.zeros((tm, GATE_PAD), F32)
    for ln, val in ((LANE_I1, i1.astype(F32)), (LANE_I2, i2.astype(F32)), (LANE_P1, p1), (LANE_P2, p2),
                    (LANE_R1, r1), (LANE_R2, r2)):
        info = jnp.where(lane == ln, val, info)
    info_ref[...] = info


def _route_call(h, g, mod3, w_router, mod_idx, *, tm, n_rows):
    d = h.shape[1]
    n_exp = w_router.shape[1]
    w_r = jnp.concatenate([w_router, jnp.zeros((d, GATE_PAD - n_exp), F32)], axis=1)
    kern = functools.partial(_route_kernel, n_exp=n_exp)
    return pl.pallas_call(
        kern,
        out_shape=(jax.ShapeDtypeStruct((n_rows, d // 2), jnp.uint32),
                   jax.ShapeDtypeStruct((n_rows, GATE_PAD), F32),
                   jax.ShapeDtypeStruct((8, GATE_PAD), F32)),
        grid=(n_rows // tm,),
        in_specs=[pl.BlockSpec((tm, d), lambda i: (i, 0)),
                  pl.BlockSpec((1, d), lambda i: (0, 0)),
                  pl.BlockSpec((1, 1, d), lambda i: (mod_idx(i) * N_MOD + 4, 0, 0)),
                  pl.BlockSpec((1, 1, d), lambda i: (mod_idx(i) * N_MOD + 3, 0, 0)),
                  pl.BlockSpec((d, GATE_PAD), lambda i: (0, 0))],
        out_specs=(pl.BlockSpec((tm, d // 2), lambda i: (i, 0)),
                   pl.BlockSpec((tm, GATE_PAD), lambda i: (i, 0)),
                   pl.BlockSpec((8, GATE_PAD), lambda i: (0, 0))),
        scratch_shapes=[pltpu.VMEM((8, GATE_PAD), F32)],
        compiler_params=_cparams(("arbitrary",)),
        name="route",
    )(h, g, mod3, mod3, w_r)


def _sc_workers():
    info = pltpu.get_tpu_info().sparse_core
    return info.num_cores, info.num_subcores


def _sc_scatter_rows(x, pos, n_out):
    n, dw = x.shape
    nc, ns = _sc_workers()
    nw = nc * ns
    t_per_w = n // nw
    w = min(SC_WINDOW, t_per_w // 2)
    n_chunks = t_per_w // w
    assert n % nw == 0 and t_per_w % (2 * w) == 0
    pos_w = pos.reshape(2, nw, n_chunks, w).transpose(1, 0, 2, 3)
    mesh = plsc.VectorSubcoreMesh(core_axis_name="c", subcore_axis_name="s")

    @functools.partial(pl.kernel, mesh=mesh, out_type=jax.ShapeDtypeStruct((n_out, dw), x.dtype),
                       scratch_types=[pltpu.VMEM((2, n_chunks, w), jnp.int32), pltpu.VMEM((2, w, dw), x.dtype),
                                      pltpu.SemaphoreType.DMA((2,)), pltpu.SemaphoreType.DMA((2,))])
    def scatter(x_hbm, pos_hbm, out_hbm, idx_v, rows_v, gsem, osem):
        wid = lax.axis_index("s") * nc + lax.axis_index("c")
        base = wid * t_per_w
        pltpu.sync_copy(pos_hbm.at[wid], idx_v)

        def get(g, slot):
            return pltpu.make_async_copy(x_hbm.at[pl.ds(base + g * w, w)], rows_v.at[slot], gsem.at[slot])

        def put(g, slot, k):
            return pltpu.make_async_copy(rows_v.at[slot], out_hbm.at[idx_v.at[k, g]], osem.at[slot])

        get(0, 0).start()

        @pl.loop(0, n_chunks, step=2)
        def _(g):
            for b in range(2):
                gg = g + b
                get(gg, b).wait()

                @pl.when(gg + 1 < n_chunks)
                def _():
                    @pl.when(gg >= 1)
                    def _():
                        put(gg - 1, 1 - b, 0).wait()
                        put(gg - 1, 1 - b, 1).wait()
                    get(gg + 1, 1 - b).start()

                put(gg, b, 0).start()
                put(gg, b, 1).start()

        for k in range(2):
            put(n_chunks - 2, 0, k).wait()
            put(n_chunks - 1, 1, k).wait()

    return scatter(x, pos_w)


def _sc_gather_rows(table, idx):
    n = idx.shape[0]
    dw = table.shape[1]
    nc, ns = _sc_workers()
    nw = nc * ns
    b_per_w = n // nw
    w = min(SC_WINDOW, b_per_w // 2)
    n_chunks = b_per_w // w
    assert n % nw == 0 and b_per_w % (2 * w) == 0
    mesh = plsc.VectorSubcoreMesh(core_axis_name="c", subcore_axis_name="s")

    @functools.partial(pl.kernel, mesh=mesh, out_type=jax.ShapeDtypeStruct((n, dw), table.dtype),
                       scratch_types=[pltpu.VMEM((b_per_w,), jnp.int32), pltpu.VMEM((2, w, dw), table.dtype),
                                      pltpu.SemaphoreType.DMA((2,)), pltpu.SemaphoreType.DMA((2,))])
    def gather(table_hbm, idx_hbm, out_hbm, idx_v, rows_v, gsem, osem):
        wid = lax.axis_index("s") * nc + lax.axis_index("c")
        base = wid * b_per_w
        pltpu.sync_copy(idx_hbm.at[pl.ds(base, b_per_w)], idx_v)

        def get(g, slot):
            return pltpu.make_async_copy(table_hbm.at[idx_v.at[pl.ds(g * w, w)]], rows_v.at[slot], gsem.at[slot])

        def put(g, slot):
            return pltpu.make_async_copy(rows_v.at[slot], out_hbm.at[pl.ds(base + g * w, w)], osem.at[slot])

        get(0, 0).start()

        @pl.loop(0, n_chunks, step=2)
        def _(g):
            for b in range(2):
                gg = g + b
                get(gg, b).wait()

                @pl.when(gg + 1 < n_chunks)
                def _():
                    @pl.when(gg >= 1)
                    def _():
                        put(gg - 1, 1 - b).wait()
                    get(gg + 1, 1 - b).start()

                put(gg, b).start()

        put(n_chunks - 2, 0).wait()
        put(n_chunks - 1, 1).wait()

    return gather(table, idx)


def _gmm_kernel(te_ref, nv_ref, xs_ref, wg_ref, wu_ref, wd_ref, ys_ref, u_ref):
    @pl.when(pl.program_id(0) < nv_ref[0])
    def _():
        u_ref[...] = _unpack_bf16_pairs(xs_ref[...]).astype(BF16)
        ys_ref[...] = _pack_bf16_pairs(_swiglu_partial(u_ref, wg_ref.at[0], wu_ref.at[0], wd_ref.at[0]))


def _gmm(xs, tile_expert, n_valid, wg, wu, wd, *, tg):
    p, dw = xs.shape
    d = 2 * dw
    f = wg.shape[2]
    grid_spec = pltpu.PrefetchScalarGridSpec(
        num_scalar_prefetch=2,
        grid=(p // tg,),
        in_specs=[pl.BlockSpec((tg, dw), lambda i, te, nv: (i, 0)),
                  pl.BlockSpec((1, d, f), lambda i, te, nv: (te[i], 0, 0)),
                  pl.BlockSpec((1, d, f), lambda i, te, nv: (te[i], 0, 0)),
                  pl.BlockSpec((1, f, d), lambda i, te, nv: (te[i], 0, 0))],
        out_specs=pl.BlockSpec((tg, dw), lambda i, te, nv: (i, 0)),
        scratch_shapes=[pltpu.VMEM((tg, d), BF16)])
    return pl.pallas_call(
        _gmm_kernel,
        out_shape=jax.ShapeDtypeStruct((p, dw), jnp.uint32),
        grid_spec=grid_spec,
        compiler_params=_cparams(("arbitrary",)),
        name="gmm",
    )(tile_expert, n_valid, xs, wg, wu, wd)


def _combine_kernel(x_ref, info_ref, y1_ref, y2_ref, gt_ref, gf_ref, out_ref, *, final):
    info = info_ref[...]
    p1 = info[:, LANE_P1:LANE_P1 + 1]
    p2 = info[:, LANE_P2:LANE_P2 + 1]
    moe = p1 * _unpack_bf16_pairs(y1_ref[...]) + p2 * _unpack_bf16_pairs(y2_ref[...])
    y = x_ref[...] + gt_ref[0] * moe
    if final:
        r = lax.rsqrt(jnp.mean(y * y, axis=-1, keepdims=True) + EPS)
        y = (y * r) * gf_ref[...]
    out_ref[...] = y


def _combine(h, info, yg, mod3, g_final, mod_idx, *, tm, n_rows, final):
    d = h.shape[1]
    n_t = n_rows // tm
    kern = functools.partial(_combine_kernel, final=final)
    return pl.pallas_call(
        kern,
        out_shape=jax.ShapeDtypeStruct((n_rows, d), F32),
        grid=(n_t,),
        in_specs=[pl.BlockSpec((tm, d), lambda i: (i, 0)),
                  pl.BlockSpec((tm, GATE_PAD), lambda i: (i, 0)),
                  pl.BlockSpec((tm, d // 2), lambda i: (i, 0)),
                  pl.BlockSpec((tm, d // 2), lambda i: (n_t + i, 0)),
                  pl.BlockSpec((1, 1, d), lambda i: (mod_idx(i) * N_MOD + 5, 0, 0)),
                  pl.BlockSpec((1, d), lambda i: (0, 0))],
        out_specs=pl.BlockSpec((tm, d), lambda i: (i, 0)),
        compiler_params=_cparams(("arbitrary",)),
        name="combine",
    )(h, info, yg, yg, mod3, g_final)


def _moe(h, g, mod3, w_router, wg, wu, wd, g_final, mod_idx, *, tm, tg, n_rows, final):
    n_exp = wg.shape[0]
    up, info, cnt = _route_call(h, g, mod3, w_router, mod_idx, tm=tm, n_rows=n_rows)
    i1 = info[:, LANE_I1].astype(jnp.int32)
    i2 = info[:, LANE_I2].astype(jnp.int32)
    r1 = info[:, LANE_R1].astype(jnp.int32)
    r2 = info[:, LANE_R2].astype(jnp.int32)
    counts = cnt[0, :n_exp].astype(jnp.int32)
    padded = (counts + tg - 1) // tg * tg
    ends = jnp.cumsum(padded)
    starts = ends - padded
    eye = jnp.arange(n_exp, dtype=jnp.int32)
    pos1 = jnp.sum(jnp.where(i1[:, None] == eye, starts, 0), axis=1) + r1
    pos2 = jnp.sum(jnp.where(i2[:, None] == eye, starts, 0), axis=1) + r2
    p_rows = -(-(TOP_K * n_rows + n_exp * (tg - 1)) // tg) * tg
    tile_row = jnp.arange(p_rows // tg, dtype=jnp.int32) * tg
    tile_expert = jnp.minimum(jnp.sum(tile_row[:, None] >= ends[None, :], axis=1), n_exp - 1).astype(jnp.int32)
    n_valid = (ends[-1] // tg).reshape(1).astype(jnp.int32)
    xs = _sc_scatter_rows(up, jnp.stack([pos1, pos2]), p_rows)
    ys = _gmm(xs, tile_expert, n_valid, wg, wu, wd, tg=tg)
    yg = _sc_gather_rows(ys, jnp.concatenate([pos1, pos2]))
    return _combine(h, info, yg, mod3, g_final, mod_idx, tm=tm, n_rows=n_rows, final=final)


def _final_norm_kernel(x_ref, g_ref, o_ref):
    x = x_ref[...]
    r = lax.rsqrt(jnp.mean(x * x, axis=-1, keepdims=True) + EPS)
    o_ref[...] = (x * r) * g_ref[...]


def _final_norm(h, g, *, tm, n_rows):
    d = h.shape[1]
    return pl.pallas_call(
        _final_norm_kernel,
        out_shape=jax.ShapeDtypeStruct((n_rows, d), F32),
        grid=(n_rows // tm,),
        in_specs=[pl.BlockSpec((tm, d), lambda i: (i, 0)), pl.BlockSpec((1, d), lambda i: (0, 0))],
        out_specs=pl.BlockSpec((tm, d), lambda i: (i, 0)),
        compiler_params=_cparams(("arbitrary",)),
        name="final_norm",
    )(h, g)


def kernel(x, c, ctx, c_ctx, w_ada, b_ada, g_mix, w_in, conv_qk, b_if, head_gain, w_pool, pool_scale, w_pa, w_pb,
           w_out, g_ffn, w_ff_gate, w_ff_up, w_ff_down, w_router, w_exp_gate, w_exp_up, w_exp_down, g_final):
    b, s, d = x.shape
    lc = ctx.shape[1]
    depth = w_ada.shape[0]
    width = w_pa.shape[1]
    pw = w_pb.shape[1]
    n_gate = b_if.shape[1] * b_if.shape[2] * b_if.shape[3]
    assert lc == CHUNK and s % CHUNK == 0 and s % GRID_W == 0 and width == d and n_gate == 16
    n_lat, n_ctx = b * s, b * lc
    tm, tm_ffn, tm_moe = 512, 512, 1024
    assert all(s % t == 0 and n_ctx % t == 0 for t in (tm, tm_ffn, tm_moe))

    def mod_idx(t):
        return lambda i: jnp.where(i < n_lat // t, i // (s // t), b)

    src = (x.reshape(n_lat, d), ctx.reshape(n_ctx, d), 0)

    n_mod_rows = 16
    cc = jnp.zeros((n_mod_rows, d), F32).at[:b].set(c).at[b].set(c_ctx)
    mod = _ada(cc, w_ada, b_ada)

    if_off, pool_off, ga_off = 4 * width, 4 * width + n_gate, 4 * width + n_gate + pw
    n_main = 4 * width + 2 * d

    exp16 = []
    for l in range(depth):
        last = l == depth - 1
        mod3 = mod[l].reshape(n_mod_rows * N_MOD, 1, d)
        w_l = w_in[l]
        w_cat = jnp.concatenate(
            [w_l[:, :if_off], w_l[:, ga_off:], w_l[:, pool_off:ga_off], w_l[:, if_off:pool_off],
             jnp.zeros((d, GATE_PAD - n_gate), F32)], axis=1).astype(BF16)
        k_scale = float((width // M_HEADS) ** -0.5)
        zm, zp, zg = _inproj(src, g_mix[l].reshape(1, d), mod3, w_cat, conv_qk[l], mod_idx(tm), tm=tm, n_main=n_main,
                             n_pool=pw, n_qk=2 * width, k_scale=k_scale, n_lat=n_lat, lat_len=s, ctx_len=lc)

        bias = b_if[l].reshape(n_gate)
        bias_c = jnp.zeros((1, GATE_PAD), F32).at[0, :n_gate].set(bias)
        bias_r = jnp.broadcast_to(bias[:, None], (n_gate, CHUNK))
        hf, hb = _mlstm(zm, zg, zg[:, :n_gate].T, bias_c, bias_r, b=b, s=s, width=width, dirs=(0, 1))

        n_rows = n_lat if last else n_lat + n_ctx
        pm_lat = _pool(zp, n_seq=b, seq_len=s, row_block0=0, grid_rows=s // GRID_W)
        pm_ctx = pm_lat if last else _pool(zp, n_seq=b, seq_len=lc, row_block0=n_lat // lc, grid_rows=None)

        w_pool_pb = _fold_pool(w_pool[l], pool_scale[l].reshape(1, pw), w_pb[l])
        h = _mixout(hf, hb, zm, pm_lat, pm_ctx, src, mod3, head_gain[l].reshape(1, width), w_pa[l].astype(BF16),
                    w_pool_pb, w_out[l].astype(BF16), mod_idx(tm), tm=tm, n_rows=n_rows)

        j = l // 2
        if l % 2 == 0:
            side = ()
            if not last:
                side = tuple(w[(l + 1) // 2].reshape(-1, w.shape[-1]) for w in (w_exp_gate, w_exp_up, w_exp_down))
            h, *exp16 = _ffn(h, g_ffn[l].reshape(1, d), mod3, w_ff_gate[j].astype(BF16), w_ff_up[j].astype(BF16),
                             w_ff_down[j].astype(BF16), mod_idx(tm_ffn), tm=tm_ffn,
                             tf=_pick_tile(w_ff_gate.shape[2], 2816), side=side)
            if last:
                h = _final_norm(h, g_final.reshape(1, d), tm=tm, n_rows=n_lat)
        else:
            if not exp16:
                exp16 = [w[j].astype(BF16) for w in (w_exp_gate, w_exp_up, w_exp_down)]
            wg16, wu16, wd16 = (w16.reshape(w.shape[1:]) for w16, w in zip(exp16, (w_exp_gate, w_exp_up, w_exp_down)))
            h = _moe(h, g_ffn[l].reshape(1, d), mod3, w_router[j], wg16, wu16, wd16, g_final.reshape(1, d),
                     mod_idx(tm_moe), tm=tm_moe, tg=512, n_rows=h.shape[0], final=last)
            exp16 = []
        src = (h, h, n_lat)
    return h[:n_lat].reshape(b, s, d)
```

```python
import functools

import jax
import jax.numpy as jnp
from jax import lax
from jax.experimental import pallas as pl
from jax.experimental.pallas import tpu as pltpu
from jax.experimental.pallas import tpu_sc as plsc

F32 = jnp.float32
BF16 = jnp.bfloat16
EPS = 1e-6
M_HEADS = 4
GRID_W = 64
POOL_WINDOWS = (2, 4, 8, 16)
POOL_GROUP_DIM = 128
CHUNK = 256
N_MOD = 6
TOP_K = 2
SC_WINDOW = 64
GATE_PAD = 128
VMEM_LIMIT = 56 * 1024 * 1024


def _pick_tile(n, target, quantum=128):
    t = min(n, target) // quantum * quantum
    while n % t:
        t -= quantum
    return t


def _cparams(sem):
    return pltpu.CompilerParams(dimension_semantics=sem, vmem_limit_bytes=VMEM_LIMIT)


def _sigmoid(x):
    return 0.5 * jnp.tanh(0.5 * x) + 0.5


def _silu(x):
    h = 0.5 * x
    return h * jnp.tanh(h) + h


def _log_sigmoid(x):
    return jnp.minimum(x, 0.0) - jnp.log(1.0 + jnp.exp(-jnp.abs(x)))


def _split3(x):
    hi = x.astype(BF16)
    r1 = x - hi.astype(F32)
    mid = r1.astype(BF16)
    lo = (r1 - mid.astype(F32)).astype(BF16)
    return hi, mid, lo


def _dot(a, b):
    return jnp.dot(a, b, preferred_element_type=F32)


def _dot_nt(a, b):
    return lax.dot_general(a, b, (((1,), (1,)), ((), ())), preferred_element_type=F32)


def _dot_tn(a, b):
    return lax.dot_general(a, b, (((0,), (0,)), ((), ())), preferred_element_type=F32)


def _rms_mod(x, g, sc, sh):
    r = lax.rsqrt(jnp.mean(x * x, axis=-1, keepdims=True) + EPS)
    return (x * r) * g * (1.0 + sc) + sh


def _ada_kernel(c_ref, w_ref, b_ref, o_ref):
    c = c_ref[...]
    o_ref[0] = _dot(_silu(c).astype(BF16), w_ref[0].astype(BF16)) + b_ref[0]


def _ada(cc, w_ada, b_ada):
    depth, d, n = w_ada.shape
    tn = 1024
    return pl.pallas_call(
        _ada_kernel,
        out_shape=jax.ShapeDtypeStruct((depth, cc.shape[0], n), F32),
        grid=(depth, n // tn),
        in_specs=[pl.BlockSpec(cc.shape, lambda l, j: (0, 0)),
                  pl.BlockSpec((1, d, tn), lambda l, j: (l, 0, j)),
                  pl.BlockSpec((1, 1, tn), lambda l, j: (l, 0, j))],
        out_specs=pl.BlockSpec((1, cc.shape[0], tn), lambda l, j: (l, 0, j)),
        compiler_params=_cparams(("arbitrary", "arbitrary")),
        name="ada",
    )(cc, w_ada, b_ada.reshape(depth, 1, n))


def _inproj_kernel(xa_ref, xb_ref, xpa_ref, xpb_ref, xna_ref, xnb_ref, g_ref, sc_ref, sh_ref, w_ref, cw_ref,
                   zm_ref, zp_ref, zg_ref, u_ref, uh_ref,
                   *, n_main, n_pool, n_qk, cw, k_scale, n_lat_tiles, lat_len, ctx_len):
    i = pl.program_id(0)
    tm = xa_ref.shape[0]
    is_ctx = i >= n_lat_tiles
    g, sc, sh = g_ref[...], sc_ref[0], sh_ref[0]
    u_ref[...] = _rms_mod(jnp.where(is_ctx, xb_ref[...], xa_ref[...]), g, sc, sh).astype(BF16)
    uh_ref[0:8, :] = _rms_mod(jnp.where(is_ctx, xpb_ref[...], xpa_ref[...]), g, sc, sh).astype(BF16)
    uh_ref[8:16, :] = _rms_mod(jnp.where(is_ctx, xnb_ref[...], xna_ref[...]), g, sc, sh).astype(BF16)

    pos0 = lax.rem(i * tm, lat_len)
    lat_first_row = jnp.where(pos0 == 0, 0, -1)
    lat_last_row = jnp.where(pos0 + tm == lat_len, tm - 1, -1)
    sub = 16
    srow = lax.broadcasted_iota(jnp.int32, (sub, cw), 0)
    slabs = sorted({0, tm - sub} | {k * ctx_len - sub for k in range(1, tm // ctx_len)}
                   | {k * ctx_len for k in range(1, tm // ctx_len)})

    for c in range(n_qk // cw):
        cs = slice(c * cw, (c + 1) * cw)
        w0, w1, w2 = cw_ref[0:1, cs], cw_ref[1:2, cs], cw_ref[2:3, cs]
        scale = k_scale if c * cw >= n_qk // 2 else 1.0
        z = _dot(u_ref[...], w_ref[:, cs])
        zh = _dot(uh_ref[...], w_ref[:, cs])
        y = _silu(pltpu.roll(z, 1, 0) * w0 + z * w1 + pltpu.roll(z, tm - 1, 0) * w2) * scale
        zm_ref[:, cs] = y.astype(BF16)
        for r0 in slabs:
            zs = z[r0:r0 + sub, :]
            above = zh[7:8, :] if r0 == 0 else z[r0 - 1:r0, :]
            below = zh[8:9, :] if r0 + sub == tm else z[r0 + sub:r0 + sub + 1, :]
            grow = srow + r0
            in_ctx = grow & (ctx_len - 1)
            first = jnp.where(is_ctx, in_ctx, grow - lat_first_row) == 0
            last = jnp.where(is_ctx, in_ctx - (ctx_len - 1), grow - lat_last_row) == 0
            zm1 = jnp.where(srow == 0, above, pltpu.roll(zs, 1, 0))
            zm1 = jnp.where(first, 0.0, zm1)
            zp1 = jnp.where(srow == sub - 1, below, pltpu.roll(zs, sub - 1, 0))
            zp1 = jnp.where(last, 0.0, zp1)
            ys = _silu(zm1 * w0 + zs * w1 + zp1 * w2) * scale
            zm_ref[r0:r0 + sub, cs] = ys.astype(BF16)
    for c in range(n_qk // cw, n_main // cw):
        zm_ref[:, c * cw:(c + 1) * cw] = _dot(u_ref[...], w_ref[:, c * cw:(c + 1) * cw]).astype(BF16)
    for c0 in range(0, n_pool, cw):
        c1 = min(c0 + cw, n_pool)
        zp_ref[:, c0:c1] = _dot(u_ref[...], w_ref[:, n_main + c0:n_main + c1])
    zg_ref[...] = _dot(u_ref[...], w_ref[:, n_main + n_pool:])


def _inproj(src, g, mod3, w_cat, conv_w, mod_idx, *, tm, n_main, n_pool, n_qk, k_scale, n_lat, lat_len, ctx_len):
    a, b, b_off = src
    d = a.shape[1]
    r = n_lat + b.shape[0] - b_off
    ncols = w_cat.shape[1]
    halo = 8
    assert tm % ctx_len == 0 and lat_len % tm == 0 and ctx_len & (ctx_len - 1) == 0 and b_off % tm == 0
    na_t, hb = n_lat // tm, tm // halo
    a_last, b_last = a.shape[0] // halo - 1, b.shape[0] // halo - 1
    kern = functools.partial(_inproj_kernel, n_main=n_main, n_pool=n_pool, n_qk=n_qk, cw=512, k_scale=k_scale,
                             n_lat_tiles=na_t, lat_len=lat_len, ctx_len=ctx_len)
    return pl.pallas_call(
        kern,
        out_shape=(jax.ShapeDtypeStruct((r, n_main), BF16),
                   jax.ShapeDtypeStruct((r, n_pool), F32),
                   jax.ShapeDtypeStruct((r, GATE_PAD), F32)),
        grid=(r // tm,),
        in_specs=[pl.BlockSpec((tm, d), lambda i: (jnp.minimum(i, na_t - 1), 0)),
                  pl.BlockSpec((tm, d), lambda i: (jnp.maximum(i - na_t, 0) + b_off // tm, 0)),
                  pl.BlockSpec((halo, d), lambda i: (jnp.clip(i * hb - 1, 0, a_last), 0)),
                  pl.BlockSpec((halo, d), lambda i: (jnp.clip(b_off // halo + (i - na_t) * hb - 1, 0, b_last), 0)),
                  pl.BlockSpec((halo, d), lambda i: (jnp.clip((i + 1) * hb, 0, a_last), 0)),
                  pl.BlockSpec((halo, d), lambda i: (jnp.clip(b_off // halo + (i - na_t + 1) * hb, 0, b_last), 0)),
                  pl.BlockSpec((1, d), lambda i: (0, 0)),
                  pl.BlockSpec((1, 1, d), lambda i: (mod_idx(i) * N_MOD + 1, 0, 0)),
                  pl.BlockSpec((1, 1, d), lambda i: (mod_idx(i) * N_MOD + 0, 0, 0)),
                  pl.BlockSpec((d, ncols), lambda i: (0, 0)),
                  pl.BlockSpec((3, n_qk), lambda i: (0, 0))],
        out_specs=(pl.BlockSpec((tm, n_main), lambda i: (i, 0)),
                   pl.BlockSpec((tm, n_pool), lambda i: (i, 0)),
                   pl.BlockSpec((tm, GATE_PAD), lambda i: (i, 0))),
        scratch_shapes=[pltpu.VMEM((tm, d), BF16), pltpu.VMEM((2 * halo, d), BF16)],
        compiler_params=_cparams(("arbitrary",)),
        name="inproj",
    )(a, b, a, b, a, b, g, mod3, mod3, w_cat, conv_w)


def _running_max_rows(x, rev):
    n = x.shape[0]
    row = lax.broadcasted_iota(jnp.int32, x.shape, 0)
    sh = 1
    while sh < n:
        if rev:
            shifted = jnp.where(row < n - sh, pltpu.roll(x, n - sh, 0), -jnp.inf)
        else:
            shifted = jnp.where(row >= sh, pltpu.roll(x, sh, 0), -jnp.inf)
        x = jnp.maximum(x, shifted)
        sh *= 2
    return x


def _mlstm_kernel(*refs, dh, dirs):
    L = CHUNK
    nd = len(dirs)
    bias_c, bias_r = refs[5 * nd:5 * nd + 2]
    outs = refs[5 * nd + 2:6 * nd + 2]
    s_ref, s16_ref, m_ref = refs[6 * nd + 2:]

    @pl.when(pl.program_id(1) == 0)
    def _():
        s_ref[...] = jnp.zeros_like(s_ref)
        s16_ref[...] = jnp.zeros_like(s16_ref)
        m_ref[...] = jnp.zeros_like(m_ref)

    ri = lax.broadcasted_iota(jnp.int32, (L, L), 0)
    ci = lax.broadcasted_iota(jnp.int32, (L, L), 1)
    lower = ci <= ri
    upper = ci >= ri
    lower_b = jnp.where(lower, 1.0, 0.0).astype(BF16)
    upper_b = jnp.where(upper, 1.0, 0.0).astype(BF16)

    for di, d in enumerate(dirs):
        q_ref, k_ref, v_ref, gc_ref, gr_ref = refs[5 * di:5 * di + 5]
        out_ref = outs[di]
        rev = d == 1
        gc = gc_ref[...] + bias_c[...]
        gr = gr_ref[...] + bias_r[...]
        fc = _log_sigmoid(gc)
        fr = _log_sigmoid(gr)
        t_col = upper_b if rev else lower_b
        t_row = lower_b if rev else upper_b
        bc3 = _dot(t_col, jnp.concatenate(_split3(fc), axis=1))
        bcum_c = bc3[:, :GATE_PAD] + bc3[:, GATE_PAD:2 * GATE_PAD] + bc3[:, 2 * GATE_PAD:]
        br3 = _dot(jnp.concatenate(_split3(fr), axis=0), t_row)
        bcum_r = br3[0:16, :] + br3[16:32, :] + br3[32:48, :]
        mask = upper if rev else lower
        end = 0 if rev else L - 1

        b_al = pltpu.roll(bcum_c, GATE_PAD - M_HEADS, axis=1)
        g_col = gc - b_al
        m_st = m_ref[di:di + 1, :]
        mx = jnp.maximum(m_st, _running_max_rows(g_col, rev))
        e_neg_all = jnp.exp(-(b_al + mx))
        mx_end = mx[end:end + 1, :]
        wk_all = jnp.exp(g_col - mx_end)
        wc_all = jnp.exp(m_st - mx_end)
        m_ref[di:di + 1, :] = b_al[end:end + 1, :] + mx_end

        ones_rep = jnp.ones((L, GATE_PAD), BF16)
        for h in range(M_HEADS):
            s_idx = di * M_HEADS + h
            col = d * 2 * M_HEADS + h

            def rep(x_all):
                return jnp.broadcast_to(x_all[:, col:col + 1], (x_all.shape[0], GATE_PAD))

            def wide(x_rep):
                return jnp.concatenate([x_rep] * (dh // GATE_PAD), axis=1)

            mx_rep = rep(mx)
            g_row = gr[col:col + 1, :] - bcum_r[col + M_HEADS:col + M_HEADS + 1, :]
            p = jnp.exp(jnp.where(mask, g_row - wide(mx_rep), -jnp.inf))
            w_inter = jnp.exp(rep(m_st) - mx_rep)
            e_neg = rep(e_neg_all)
            wk16 = rep(wk_all).astype(BF16)
            wc = wc_all[:, col:col + 1]

            sl = slice(h * dh, (h + 1) * dh)
            q = q_ref[:, sl]
            k = k_ref[:, sl]
            v = v_ref[:, sl]
            scores = (_dot_nt(q, k) * p).astype(BF16)
            inter = _dot(q, s16_ref[s_idx])
            intra = _dot(scores, jnp.concatenate([v, ones_rep], axis=1))
            num = wide(w_inter) * inter[:, :dh] + intra[:, :dh]
            den = w_inter * inter[:, dh:] + intra[:, dh:]
            inv = 1.0 / jnp.maximum(jnp.abs(den), e_neg)
            out_ref[:, sl] = (num * wide(inv)).astype(BF16)

            s_new = wc * s_ref[s_idx] + _dot_tn(k, jnp.concatenate([v * wide(wk16), wk16], axis=1))
            s_ref[s_idx] = s_new
            s16_ref[s_idx] = s_new.astype(BF16)


def _mlstm(zm, zg, zg_t, bias_c, bias_r, *, b, s, width, dirs):
    r = zm.shape[0]
    L = CHUNK
    n_lat = s // L
    lat_blocks = b * n_lat
    nd = len(dirs)

    def blk_f(bi, i):
        return jnp.where(i == 0, lat_blocks + bi, bi * n_lat + i - 1)

    def blk_b(bi, i):
        return jnp.where(i == 0, lat_blocks + bi, bi * n_lat + n_lat - i)

    blks = [blk_b if d == 1 else blk_f for d in dirs]

    def specs(blk):
        return [pl.BlockSpec((L, width), lambda bi, i: (blk(bi, i), 0)),
                pl.BlockSpec((L, width), lambda bi, i: (blk(bi, i), 1)),
                pl.BlockSpec((L, width), lambda bi, i: (blk(bi, i), 2)),
                pl.BlockSpec((L, GATE_PAD), lambda bi, i: (blk(bi, i), 0)),
                pl.BlockSpec((16, L), lambda bi, i: (0, blk(bi, i)))]

    dh = width // M_HEADS
    kern = functools.partial(_mlstm_kernel, dh=dh, dirs=tuple(dirs))
    out = jax.ShapeDtypeStruct((r, width), BF16)
    in_specs = []
    for blk in blks:
        in_specs += specs(blk)
    in_specs += [pl.BlockSpec((1, GATE_PAD), lambda bi, i: (0, 0)), pl.BlockSpec((16, L), lambda bi, i: (0, 0))]
    return pl.pallas_call(
        kern,
        out_shape=(out,) * nd,
        grid=(b, n_lat + 1),
        in_specs=in_specs,
        out_specs=tuple(pl.BlockSpec((L, width), functools.partial(lambda blk, bi, i: (blk(bi, i), 0), blk))
                        for blk in blks),
        scratch_shapes=[pltpu.VMEM((nd * M_HEADS, dh, dh + GATE_PAD), F32),
                        pltpu.VMEM((nd * M_HEADS, dh, dh + GATE_PAD), BF16),
                        pltpu.VMEM((8, GATE_PAD), F32)],
        compiler_params=_cparams(("arbitrary", "arbitrary")),
        name="mlstm",
    )(*((zm, zm, zm, zg, zg_t) * nd), bias_c, bias_r)


POOL_BLOCK = 256


def _band(n, w, seg):
    r = lax.broadcasted_iota(jnp.int32, (n, n), 0)
    c = lax.broadcasted_iota(jnp.int32, (n, n), 1)
    d = c - r
    shift = seg.bit_length() - 1
    ok = (d >= -(w // 2)) & (d <= w // 2 - 1) & ((r >> shift) == (c >> shift))
    return jnp.where(ok, 1.0, 0.0).astype(BF16)


def _window_count(idx, n, w):
    return (jnp.minimum(idx + w // 2, n) - jnp.maximum(idx - w // 2, 0)).astype(F32)


def _pool_kernel(z_ref, o_ref, pad_ref, *, grid_rows):
    n = z_ref.shape[0]
    blk = min(POOL_BLOCK, n)
    seg = blk if grid_rows is None else GRID_W
    seg_len = n if grid_rows is None else GRID_W
    halo = (max(POOL_WINDOWS) // 2) * GRID_W
    rowi = lax.broadcasted_iota(jnp.int32, (blk, POOL_GROUP_DIM), 0)
    if grid_rows is not None:
        zeros = jnp.zeros((halo, POOL_GROUP_DIM), F32)
        pad_ref[0:halo, :] = zeros
        pad_ref[halo + n:, :] = zeros
    for g, w in enumerate(POOL_WINDOWS):
        sl = slice(g * POOL_GROUP_DIM, (g + 1) * POOL_GROUP_DIM)
        band = _band(blk, w, seg)
        inv_w = 1.0 / _window_count(rowi & (seg - 1), seg_len, w)
        for b in range(n // blk):
            rows = slice(b * blk, (b + 1) * blk)
            x = z_ref[rows, sl]
            hi = x.astype(BF16)
            lo = (x - hi.astype(F32)).astype(BF16)
            y2 = _dot(band, jnp.concatenate([hi, lo], axis=1))
            y = (y2[:, :POOL_GROUP_DIM] + y2[:, POOL_GROUP_DIM:]) * inv_w
            if grid_rows is None:
                o_ref[rows, sl] = (y - x).astype(BF16)
            else:
                pad_ref[halo + b * blk:halo + (b + 1) * blk, :] = y
        if grid_rows is not None:
            shift = GRID_W.bit_length() - 1
            for b in range(n // blk):
                acc = None
                for d in range(-(w // 2), w // 2):
                    lo_r = halo + b * blk + d * GRID_W
                    t = pad_ref[lo_r:lo_r + blk, :]
                    acc = t if acc is None else acc + t
                inv_h = 1.0 / _window_count((rowi + b * blk) >> shift, grid_rows, w)
                rows = slice(b * blk, (b + 1) * blk)
                o_ref[rows, sl] = (acc * inv_h - z_ref[rows, sl]).astype(BF16)


def _pool(zp, *, n_seq, seq_len, row_block0, grid_rows):
    width = zp.shape[1]
    assert seq_len % POOL_BLOCK == 0 if grid_rows is not None else seq_len <= POOL_BLOCK
    kern = functools.partial(_pool_kernel, grid_rows=grid_rows)
    halo = (max(POOL_WINDOWS) // 2) * GRID_W
    return pl.pallas_call(
        kern,
        out_shape=jax.ShapeDtypeStruct((n_seq * seq_len, width), BF16),
        grid=(n_seq,),
        in_specs=[pl.BlockSpec((seq_len, width), lambda s: (row_block0 + s, 0))],
        out_specs=pl.BlockSpec((seq_len, width), lambda s: (s, 0)),
        scratch_shapes=[pltpu.VMEM((seq_len + 2 * halo, POOL_GROUP_DIM), F32)],
        compiler_params=_cparams(("arbitrary",)),
        name="pool",
    )(zp)


def _fold_pool_kernel(wp_ref, ps_ref, wpb_ref, o_ref):
    for g in range(len(POOL_WINDOWS)):
        sl = slice(g * POOL_GROUP_DIM, (g + 1) * POOL_GROUP_DIM)
        a_hi, a_mid, _ = _split3(wp_ref[g] * ps_ref[:, sl])
        b_hi, b_mid, _ = _split3(wpb_ref[sl, :])
        o_ref[sl, :] = (_dot(a_hi, b_hi) + _dot(a_hi, b_mid) + _dot(a_mid, b_hi)).astype(BF16)


def _fold_pool(w_pool, pool_scale, w_pb):
    return pl.pallas_call(
        _fold_pool_kernel,
        out_shape=jax.ShapeDtypeStruct(w_pb.shape, BF16),
        compiler_params=pltpu.CompilerParams(vmem_limit_bytes=VMEM_LIMIT),
        name="fold_pool",
    )(w_pool, pool_scale, w_pb)


def _mixout_kernel(hf_ref, hb_ref, o_ref, ga_ref, gb_ref, pml_ref, pmc_ref, xa_ref, xb_ref, gt_ref, hg_ref,
                   wpa_ref, wpb_ref, wout_ref, out_ref, *, dh, n_lat_tiles):
    is_ctx = pl.program_id(0) >= n_lat_tiles
    hm = hf_ref[...].astype(F32) + hb_ref[...].astype(F32)
    parts = []
    for h in range(M_HEADS):
        t = hm[:, h * dh:(h + 1) * dh]
        mu = jnp.mean(t, axis=-1, keepdims=True)
        tc = t - mu
        var = jnp.mean(tc * tc, axis=-1, keepdims=True)
        parts.append(tc * lax.rsqrt(var + EPS))
    hn = jnp.concatenate(parts, axis=-1)
    hn = _sigmoid(o_ref[...].astype(F32)) * (hn * hg_ref[...])
    a = _dot(hn.astype(BF16), wpa_ref[...])
    bmat = _dot(jnp.where(is_ctx, pmc_ref[...], pml_ref[...]), wpb_ref[...])
    merged = _sigmoid(ga_ref[...].astype(F32)) * a + _sigmoid(gb_ref[...].astype(F32)) * bmat
    y = _dot(merged.astype(BF16), wout_ref[...])
    out_ref[...] = jnp.where(is_ctx, xb_ref[...], xa_ref[...]) + gt_ref[0] * y


def _mixout(hf, hb, zm, pm_lat, pm_ctx, src, mod3, head_gain, w_pa, w_pb, w_out, mod_idx, *, tm, n_rows):
    xa, xb, b_off = src
    d = xa.shape[1]
    width = hf.shape[1]
    pw = pm_lat.shape[1]
    n_lat_tiles = pm_lat.shape[0] // tm
    o_blk = 3
    full = lambda shape: pl.BlockSpec(shape, lambda i: (0,) * len(shape))
    kern = functools.partial(_mixout_kernel, dh=width // M_HEADS, n_lat_tiles=n_lat_tiles)
    return pl.pallas_call(
        kern,
        out_shape=jax.ShapeDtypeStruct((n_rows, d), F32),
        grid=(n_rows // tm,),
        in_specs=[pl.BlockSpec((tm, width), lambda i: (i, 0)),
                  pl.BlockSpec((tm, width), lambda i: (i, 0)),
                  pl.BlockSpec((tm, width), lambda i: (i, o_blk)),
                  pl.BlockSpec((tm, d), lambda i: (i, o_blk + 1)),
                  pl.BlockSpec((tm, d), lambda i: (i, o_blk + 2)),
                  pl.BlockSpec((tm, pw), lambda i: (jnp.minimum(i, n_lat_tiles - 1), 0)),
                  pl.BlockSpec((tm, pw), lambda i: (jnp.maximum(i - n_lat_tiles, 0), 0)),
                  pl.BlockSpec((tm, d), lambda i: (jnp.minimum(i, n_lat_tiles - 1), 0)),
                  pl.BlockSpec((tm, d), lambda i: (jnp.maximum(i - n_lat_tiles, 0) + b_off // tm, 0)),
                  pl.BlockSpec((1, 1, d), lambda i: (mod_idx(i) * N_MOD + 2, 0, 0)),
                  full((1, width)), full(w_pa.shape), full(w_pb.shape), full(w_out.shape)],
        out_specs=pl.BlockSpec((tm, d), lambda i: (i, 0)),
        compiler_params=_cparams(("arbitrary",)),
        name="mixout",
    )(hf, hb, zm, zm, zm, pm_lat, pm_ctx, xa, xb, mod3, head_gain, w_pa, w_pb, w_out)


SWIGLU_SUB = 512


def _swiglu_partial(u_ref, wg, wu, wd):
    tf = wg.shape[1]
    acc = None
    for c0 in range(0, tf, SWIGLU_SUB):
        cs = slice(c0, min(c0 + SWIGLU_SUB, tf))
        u = u_ref[...]
        act = (_silu(_dot(u, wg[:, cs])) * _dot(u, wu[:, cs])).astype(BF16)
        part = _dot(act, wd[cs, :])
        acc = part if acc is None else acc + part
    return acc


def _ffn_kernel(x_ref, g_ref, sc_ref, sh_ref, gt_ref, wg_ref, wu_ref, wd_ref, *rest, n_side):
    side_in, out_ref, side_out, u_ref = rest[:n_side], rest[n_side], rest[n_side + 1:2 * n_side + 1], rest[-1]
    j = pl.program_id(1)
    u_ref[...] = _rms_mod(x_ref[...], g_ref[...], sc_ref[0], sh_ref[0]).astype(BF16)
    contrib = gt_ref[0] * _swiglu_partial(u_ref, wg_ref, wu_ref, wd_ref)

    @pl.when(j == 0)
    def _():
        out_ref[...] = x_ref[...] + contrib
        for src, dst in zip(side_in, side_out):
            dst[...] = src[...].astype(BF16)

    @pl.when(j > 0)
    def _():
        out_ref[...] += contrib


def _side_rows(n_rows, n_steps):
    rows = (-(-n_rows // n_steps) + 15) // 16 * 16
    while n_rows % rows:
        rows += 16
    return rows


def _ffn(h, g, mod3, wg, wu, wd, mod_idx, *, tm, tf, side=()):
    r, d = h.shape
    f = wg.shape[1]
    n_steps = r // tm
    once = pl.Buffered(1) if f == tf else None
    side_specs_in, side_specs_out, side_shapes = [], [], []
    for a in side:
        rows = _side_rows(a.shape[0], n_steps)
        last = a.shape[0] // rows - 1
        spec = pl.BlockSpec((rows, a.shape[1]), functools.partial(lambda last, i, j: (jnp.minimum(i, last), 0), last))
        side_specs_in.append(spec)
        side_specs_out.append(spec)
        side_shapes.append(jax.ShapeDtypeStruct(a.shape, BF16))
    return pl.pallas_call(
        functools.partial(_ffn_kernel, n_side=len(side)),
        out_shape=(jax.ShapeDtypeStruct((r, d), F32), *side_shapes),
        grid=(n_steps, f // tf),
        in_specs=[pl.BlockSpec((tm, d), lambda i, j: (i, 0)),
                  pl.BlockSpec((1, d), lambda i, j: (0, 0)),
                  pl.BlockSpec((1, 1, d), lambda i, j: (mod_idx(i) * N_MOD + 4, 0, 0)),
                  pl.BlockSpec((1, 1, d), lambda i, j: (mod_idx(i) * N_MOD + 3, 0, 0)),
                  pl.BlockSpec((1, 1, d), lambda i, j: (mod_idx(i) * N_MOD + 5, 0, 0)),
                  pl.BlockSpec((d, tf), lambda i, j: (0, j), pipeline_mode=once),
                  pl.BlockSpec((d, tf), lambda i, j: (0, j), pipeline_mode=once),
                  pl.BlockSpec((tf, d), lambda i, j: (j, 0), pipeline_mode=once),
                  *side_specs_in],
        out_specs=(pl.BlockSpec((tm, d), lambda i, j: (i, 0)), *side_specs_out),
        scratch_shapes=[pltpu.VMEM((tm, d), BF16)],
        compiler_params=_cparams(("arbitrary", "arbitrary")),
        name="ffn",
    )(h, g, mod3, mod3, mod3, wg, wu, wd, *side)


def _route(u, wr_ref, n_exp):
    u_hi = u.astype(BF16)
    u_mid = (u - u_hi.astype(F32)).astype(BF16)
    w_hi, w_mid, _ = _split3(wr_ref[...])
    both = _dot(u_hi, jnp.concatenate([w_hi, w_mid], axis=1))
    logits = both[:, :GATE_PAD] + both[:, GATE_PAD:] + _dot(u_mid, w_hi)
    lane = lax.broadcasted_iota(jnp.int32, logits.shape, 1)
    lg = jnp.where(lane < n_exp, logits, -jnp.inf)
    m1 = jnp.max(lg, axis=1, keepdims=True)
    i1 = jnp.min(jnp.where(lg == m1, lane, 2 * GATE_PAD), axis=1, keepdims=True)
    lg2 = jnp.where(lane == i1, -jnp.inf, lg)
    m2 = jnp.max(lg2, axis=1, keepdims=True)
    i2 = jnp.min(jnp.where(lg2 == m2, lane, 2 * GATE_PAD), axis=1, keepdims=True)
    e = jnp.exp(m2 - m1)
    p1 = 1.0 / (1.0 + e)
    p2 = e * p1
    return i1, i2, p1, p2


LANE_I1, LANE_I2, LANE_P1, LANE_P2, LANE_R1, LANE_R2 = 0, 1, 2, 3, 4, 5


def _pack_bf16_pairs(x):
    k = x.shape[1] // 2
    lo = lax.bitcast_convert_type(x[:, :k].astype(BF16).astype(F32), jnp.uint32)
    hi = lax.bitcast_convert_type(x[:, k:].astype(BF16).astype(F32), jnp.uint32)
    return (lo >> 16) | (hi & jnp.uint32(0xFFFF0000))


def _unpack_bf16_pairs(w):
    lo = lax.bitcast_convert_type(w << 16, F32)
    hi = lax.bitcast_convert_type(w & jnp.uint32(0xFFFF0000), F32)
    return jnp.concatenate([lo, hi], axis=1)


def _route_kernel(x_ref, g_ref, sc_ref, sh_ref, wr_ref, up_ref, info_ref, cnt_ref, carry_ref, *, n_exp):
    @pl.when(pl.program_id(0) == 0)
    def _():
        carry_ref[...] = jnp.zeros_like(carry_ref)

    u = _rms_mod(x_ref[...], g_ref[...], sc_ref[0], sh_ref[0])
    up_ref[...] = _pack_bf16_pairs(u)
    i1, i2, p1, p2 = _route(u, wr_ref, n_exp)
    tm = u.shape[0]
    lane = lax.broadcasted_iota(jnp.int32, (tm, GATE_PAD), 1)
    sel = jnp.where((lane == i1) | (lane == i2), 1.0, 0.0)
    ri = lax.broadcasted_iota(jnp.int32, (tm, tm), 0)
    ci = lax.broadcasted_iota(jnp.int32, (tm, tm), 1)
    before = jnp.where(ci < ri, 1.0, 0.0).astype(BF16)
    rank = carry_ref[0:1, :] + _dot(before, sel.astype(BF16))
    r1 = jnp.sum(jnp.where(lane == i1, rank, 0.0), axis=1, keepdims=True)
    r2 = jnp.sum(jnp.where(lane == i2, rank, 0.0), axis=1, keepdims=True)
    carry_ref[0:1, :] = carry_ref[0:1, :] + jnp.sum(sel, axis=0, keepdims=True)
    cnt_ref[...] = carry_ref[...]
    info = jnp.zeros((tm, GATE_PAD), F32)
    for ln, val in ((LANE_I1, i1.astype(F32)), (LANE_I2, i2.astype(F32)), (LANE_P1, p1), (LANE_P2, p2),
                    (LANE_R1, r1), (LANE_R2, r2)):
        info = jnp.where(lane == ln, val, info)
    info_ref[...] = info


def _route_call(h, g, mod3, w_router, mod_idx, *, tm, n_rows):
    d = h.shape[1]
    n_exp = w_router.shape[1]
    w_r = jnp.concatenate([w_router, jnp.zeros((d, GATE_PAD - n_exp), F32)], axis=1)
    kern = functools.partial(_route_kernel, n_exp=n_exp)
    return pl.pallas_call(
        kern,
        out_shape=(jax.ShapeDtypeStruct((n_rows, d // 2), jnp.uint32),
                   jax.ShapeDtypeStruct((n_rows, GATE_PAD), F32),
                   jax.ShapeDtypeStruct((8, GATE_PAD), F32)),
        grid=(n_rows // tm,),
        in_specs=[pl.BlockSpec((tm, d), lambda i: (i, 0)),
                  pl.BlockSpec((1, d), lambda i: (0, 0)),
                  pl.BlockSpec((1, 1, d), lambda i: (mod_idx(i) * N_MOD + 4, 0, 0)),
                  pl.BlockSpec((1, 1, d), lambda i: (mod_idx(i) * N_MOD + 3, 0, 0)),
                  pl.BlockSpec((d, GATE_PAD), lambda i: (0, 0))],
        out_specs=(pl.BlockSpec((tm, d // 2), lambda i: (i, 0)),
                   pl.BlockSpec((tm, GATE_PAD), lambda i: (i, 0)),
                   pl.BlockSpec((8, GATE_PAD), lambda i: (0, 0))),
        scratch_shapes=[pltpu.VMEM((8, GATE_PAD), F32)],
        compiler_params=_cparams(("arbitrary",)),
        name="route",
    )(h, g, mod3, mod3, w_r)


def _sc_workers():
    info = pltpu.get_tpu_info().sparse_core
    return info.num_cores, info.num_subcores


def _sc_scatter_rows(x, pos, n_out):
    n, dw = x.shape
    nc, ns = _sc_workers()
    nw = nc * ns
    t_per_w = n // nw
    w = min(SC_WINDOW, t_per_w // 2)
    n_chunks = t_per_w // w
    assert n % nw == 0 and t_per_w % (2 * w) == 0
    pos_w = pos.reshape(2, nw, n_chunks, w).transpose(1, 0, 2, 3)
    mesh = plsc.VectorSubcoreMesh(core_axis_name="c", subcore_axis_name="s")

    @functools.partial(pl.kernel, mesh=mesh, out_type=jax.ShapeDtypeStruct((n_out, dw), x.dtype),
                       scratch_types=[pltpu.VMEM((2, n_chunks, w), jnp.int32), pltpu.VMEM((2, w, dw), x.dtype),
                                      pltpu.SemaphoreType.DMA((2,)), pltpu.SemaphoreType.DMA((2,))])
    def scatter(x_hbm, pos_hbm, out_hbm, idx_v, rows_v, gsem, osem):
        wid = lax.axis_index("s") * nc + lax.axis_index("c")
        base = wid * t_per_w
        pltpu.sync_copy(pos_hbm.at[wid], idx_v)

        def get(g, slot):
            return pltpu.make_async_copy(x_hbm.at[pl.ds(base + g * w, w)], rows_v.at[slot], gsem.at[slot])

        def put(g, slot, k):
            return pltpu.make_async_copy(rows_v.at[slot], out_hbm.at[idx_v.at[k, g]], osem.at[slot])

        get(0, 0).start()

        @pl.loop(0, n_chunks, step=2)
        def _(g):
            for b in range(2):
                gg = g + b
                get(gg, b).wait()

                @pl.when(gg + 1 < n_chunks)
                def _():
                    @pl.when(gg >= 1)
                    def _():
                        put(gg - 1, 1 - b, 0).wait()
                        put(gg - 1, 1 - b, 1).wait()
                    get(gg + 1, 1 - b).start()

                put(gg, b, 0).start()
                put(gg, b, 1).start()

        for k in range(2):
            put(n_chunks - 2, 0, k).wait()
            put(n_chunks - 1, 1, k).wait()

    return scatter(x, pos_w)


def _sc_gather_rows(table, idx):
    n = idx.shape[0]
    dw = table.shape[1]
    nc, ns = _sc_workers()
    nw = nc * ns
    b_per_w = n // nw
    w = min(SC_WINDOW, b_per_w // 2)
    n_chunks = b_per_w // w
    assert n % nw == 0 and b_per_w % (2 * w) == 0
    mesh = plsc.VectorSubcoreMesh(core_axis_name="c", subcore_axis_name="s")

    @functools.partial(pl.kernel, mesh=mesh, out_type=jax.ShapeDtypeStruct((n, dw), table.dtype),
                       scratch_types=[pltpu.VMEM((b_per_w,), jnp.int32), pltpu.VMEM((2, w, dw), table.dtype),
                                      pltpu.SemaphoreType.DMA((2,)), pltpu.SemaphoreType.DMA((2,))])
    def gather(table_hbm, idx_hbm, out_hbm, idx_v, rows_v, gsem, osem):
        wid = lax.axis_index("s") * nc + lax.axis_index("c")
        base = wid * b_per_w
        pltpu.sync_copy(idx_hbm.at[pl.ds(base, b_per_w)], idx_v)

        def get(g, slot):
            return pltpu.make_async_copy(table_hbm.at[idx_v.at[pl.ds(g * w, w)]], rows_v.at[slot], gsem.at[slot])

        def put(g, slot):
            return pltpu.make_async_copy(rows_v.at[slot], out_hbm.at[pl.ds(base + g * w, w)], osem.at[slot])

        get(0, 0).start()

        @pl.loop(0, n_chunks, step=2)
        def _(g):
            for b in range(2):
                gg = g + b
                get(gg, b).wait()

                @pl.when(gg + 1 < n_chunks)
                def _():
                    @pl.when(gg >= 1)
                    def _():
                        put(gg - 1, 1 - b).wait()
                    get(gg + 1, 1 - b).start()

                put(gg, b).start()

        put(n_chunks - 2, 0).wait()
        put(n_chunks - 1, 1).wait()

    return gather(table, idx)


def _gmm_kernel(te_ref, nv_ref, xs_ref, wg_ref, wu_ref, wd_ref, ys_ref, u_ref):
    @pl.when(pl.program_id(0) < nv_ref[0])
    def _():
        u_ref[...] = _unpack_bf16_pairs(xs_ref[...]).astype(BF16)
        ys_ref[...] = _pack_bf16_pairs(_swiglu_partial(u_ref, wg_ref.at[0], wu_ref.at[0], wd_ref.at[0]))


def _gmm(xs, tile_expert, n_valid, wg, wu, wd, *, tg):
    p, dw = xs.shape
    d = 2 * dw
    f = wg.shape[2]
    grid_spec = pltpu.PrefetchScalarGridSpec(
        num_scalar_prefetch=2,
        grid=(p // tg,),
        in_specs=[pl.BlockSpec((tg, dw), lambda i, te, nv: (i, 0)),
                  pl.BlockSpec((1, d, f), lambda i, te, nv: (te[i], 0, 0)),
                  pl.BlockSpec((1, d, f), lambda i, te, nv: (te[i], 0, 0)),
                  pl.BlockSpec((1, f, d), lambda i, te, nv: (te[i], 0, 0))],
        out_specs=pl.BlockSpec((tg, dw), lambda i, te, nv: (i, 0)),
        scratch_shapes=[pltpu.VMEM((tg, d), BF16)])
    return pl.pallas_call(
        _gmm_kernel,
        out_shape=jax.ShapeDtypeStruct((p, dw), jnp.uint32),
        grid_spec=grid_spec,
        compiler_params=_cparams(("arbitrary",)),
        name="gmm",
    )(tile_expert, n_valid, xs, wg, wu, wd)


def _combine_kernel(x_ref, info_ref, y1_ref, y2_ref, gt_ref, gf_ref, out_ref, *, final):
    info = info_ref[...]
    p1 = info[:, LANE_P1:LANE_P1 + 1]
    p2 = info[:, LANE_P2:LANE_P2 + 1]
    moe = p1 * _unpack_bf16_pairs(y1_ref[...]) + p2 * _unpack_bf16_pairs(y2_ref[...])
    y = x_ref[...] + gt_ref[0] * moe
    if final:
        r = lax.rsqrt(jnp.mean(y * y, axis=-1, keepdims=True) + EPS)
        y = (y * r) * gf_ref[...]
    out_ref[...] = y


def _combine(h, info, yg, mod3, g_final, mod_idx, *, tm, n_rows, final):
    d = h.shape[1]
    n_t = n_rows // tm
    kern = functools.partial(_combine_kernel, final=final)
    return pl.pallas_call(
        kern,
        out_shape=jax.ShapeDtypeStruct((n_rows, d), F32),
        grid=(n_t,),
        in_specs=[pl.BlockSpec((tm, d), lambda i: (i, 0)),
                  pl.BlockSpec((tm, GATE_PAD), lambda i: (i, 0)),
                  pl.BlockSpec((tm, d // 2), lambda i: (i, 0)),
                  pl.BlockSpec((tm, d // 2), lambda i: (n_t + i, 0)),
                  pl.BlockSpec((1, 1, d), lambda i: (mod_idx(i) * N_MOD + 5, 0, 0)),
                  pl.BlockSpec((1, d), lambda i: (0, 0))],
        out_specs=pl.BlockSpec((tm, d), lambda i: (i, 0)),
        compiler_params=_cparams(("arbitrary",)),
        name="combine",
    )(h, info, yg, yg, mod3, g_final)


def _moe(h, g, mod3, w_router, wg, wu, wd, g_final, mod_idx, *, tm, tg, n_rows, final):
    n_exp = wg.shape[0]
    up, info, cnt = _route_call(h, g, mod3, w_router, mod_idx, tm=tm, n_rows=n_rows)
    i1 = info[:, LANE_I1].astype(jnp.int32)
    i2 = info[:, LANE_I2].astype(jnp.int32)
    r1 = info[:, LANE_R1].astype(jnp.int32)
    r2 = info[:, LANE_R2].astype(jnp.int32)
    counts = cnt[0, :n_exp].astype(jnp.int32)
    padded = (counts + tg - 1) // tg * tg
    ends = jnp.cumsum(padded)
    starts = ends - padded
    eye = jnp.arange(n_exp, dtype=jnp.int32)
    pos1 = jnp.sum(jnp.where(i1[:, None] == eye, starts, 0), axis=1) + r1
    pos2 = jnp.sum(jnp.where(i2[:, None] == eye, starts, 0), axis=1) + r2
    p_rows = -(-(TOP_K * n_rows + n_exp * (tg - 1)) // tg) * tg
    tile_row = jnp.arange(p_rows // tg, dtype=jnp.int32) * tg
    tile_expert = jnp.minimum(jnp.sum(tile_row[:, None] >= ends[None, :], axis=1), n_exp - 1).astype(jnp.int32)
    n_valid = (ends[-1] // tg).reshape(1).astype(jnp.int32)
    xs = _sc_scatter_rows(up, jnp.stack([pos1, pos2]), p_rows)
    ys = _gmm(xs, tile_expert, n_valid, wg, wu, wd, tg=tg)
    yg = _sc_gather_rows(ys, jnp.concatenate([pos1, pos2]))
    return _combine(h, info, yg, mod3, g_final, mod_idx, tm=tm, n_rows=n_rows, final=final)


def _final_norm_kernel(x_ref, g_ref, o_ref):
    x = x_ref[...]
    r = lax.rsqrt(jnp.mean(x * x, axis=-1, keepdims=True) + EPS)
    o_ref[...] = (x * r) * g_ref[...]


def _final_norm(h, g, *, tm, n_rows):
    d = h.shape[1]
    return pl.pallas_call(
        _final_norm_kernel,
        out_shape=jax.ShapeDtypeStruct((n_rows, d), F32),
        grid=(n_rows // tm,),
        in_specs=[pl.BlockSpec((tm, d), lambda i: (i, 0)), pl.BlockSpec((1, d), lambda i: (0, 0))],
        out_specs=pl.BlockSpec((tm, d), lambda i: (i, 0)),
        compiler_params=_cparams(("arbitrary",)),
        name="final_norm",
    )(h, g)


def kernel(x, c, ctx, c_ctx, w_ada, b_ada, g_mix, w_in, conv_qk, b_if, head_gain, w_pool, pool_scale, w_pa, w_pb,
           w_out, g_ffn, w_ff_gate, w_ff_up, w_ff_down, w_router, w_exp_gate, w_exp_up, w_exp_down, g_final):
    b, s, d = x.shape
    lc = ctx.shape[1]
    depth = w_ada.shape[0]
    width = w_pa.shape[1]
    pw = w_pb.shape[1]
    n_gate = b_if.shape[1] * b_if.shape[2] * b_if.shape[3]
    assert lc == CHUNK and s % CHUNK == 0 and s % GRID_W == 0 and width == d and n_gate == 16
    n_lat, n_ctx = b * s, b * lc
    tm, tm_ffn, tm_moe = 512, 512, 1024
    assert all(s % t == 0 and n_ctx % t == 0 for t in (tm, tm_ffn, tm_moe))

    def mod_idx(t):
        return lambda i: jnp.where(i < n_lat // t, i // (s // t), b)

    src = (x.reshape(n_lat, d), ctx.reshape(n_ctx, d), 0)

    n_mod_rows = 16
    cc = jnp.zeros((n_mod_rows, d), F32).at[:b].set(c).at[b].set(c_ctx)
    mod = _ada(cc, w_ada, b_ada)

    if_off, pool_off, ga_off = 4 * width, 4 * width + n_gate, 4 * width + n_gate + pw
    n_main = 4 * width + 2 * d

    exp16 = []
    for l in range(depth):
        last = l == depth - 1
        mod3 = mod[l].reshape(n_mod_rows * N_MOD, 1, d)
        w_l = w_in[l]
        w_cat = jnp.concatenate(
            [w_l[:, :if_off], w_l[:, ga_off:], w_l[:, pool_off:ga_off], w_l[:, if_off:pool_off],
             jnp.zeros((d, GATE_PAD - n_gate), F32)], axis=1).astype(BF16)
        k_scale = float((width // M_HEADS) ** -0.5)
        zm, zp, zg = _inproj(src, g_mix[l].reshape(1, d), mod3, w_cat, conv_qk[l], mod_idx(tm), tm=tm, n_main=n_main,
                             n_pool=pw, n_qk=2 * width, k_scale=k_scale, n_lat=n_lat, lat_len=s, ctx_len=lc)

        bias = b_if[l].reshape(n_gate)
        bias_c = jnp.zeros((1, GATE_PAD), F32).at[0, :n_gate].set(bias)
        bias_r = jnp.broadcast_to(bias[:, None], (n_gate, CHUNK))
        hf, hb = _mlstm(zm, zg, zg[:, :n_gate].T, bias_c, bias_r, b=b, s=s, width=width, dirs=(0, 1))

        n_rows = n_lat if last else n_lat + n_ctx
        pm_lat = _pool(zp, n_seq=b, seq_len=s, row_block0=0, grid_rows=s // GRID_W)
        pm_ctx = pm_lat if last else _pool(zp, n_seq=b, seq_len=lc, row_block0=n_lat // lc, grid_rows=None)

        w_pool_pb = _fold_pool(w_pool[l], pool_scale[l].reshape(1, pw), w_pb[l])
        h = _mixout(hf, hb, zm, pm_lat, pm_ctx, src, mod3, head_gain[l].reshape(1, width), w_pa[l].astype(BF16),
                    w_pool_pb, w_out[l].astype(BF16), mod_idx(tm), tm=tm, n_rows=n_rows)

        j = l // 2
        if l % 2 == 0:
            side = ()
            if not last:
                side = tuple(w[(l + 1) // 2].reshape(-1, w.shape[-1]) for w in (w_exp_gate, w_exp_up, w_exp_down))
            h, *exp16 = _ffn(h, g_ffn[l].reshape(1, d), mod3, w_ff_gate[j].astype(BF16), w_ff_up[j].astype(BF16),
                             w_ff_down[j].astype(BF16), mod_idx(tm_ffn), tm=tm_ffn,
                             tf=_pick_tile(w_ff_gate.shape[2], 2816), side=side)
            if last:
                h = _final_norm(h, g_final.reshape(1, d), tm=tm, n_rows=n_lat)
        else:
            if not exp16:
                exp16 = [w[j].astype(BF16) for w in (w_exp_gate, w_exp_up, w_exp_down)]
            wg16, wu16, wd16 = (w16.reshape(w.shape[1:]) for w16, w in zip(exp16, (w_exp_gate, w_exp_up, w_exp_down)))
            h = _moe(h, g_ffn[l].reshape(1, d), mod3, w_router[j], wg16, wu16, wd16, g_final.reshape(1, d),
                     mod_idx(tm_moe), tm=tm_moe, tg=512, n_rows=h.shape[0], final=last)
            exp16 = []
        src = (h, h, n_lat)
    return h[:n_lat].reshape(b, s, d)
```

```python
import functools

import jax
import jax.numpy as jnp
from jax import lax
from jax.experimental import pallas as pl
from jax.experimental.pallas import tpu as pltpu
from jax.experimental.pallas import tpu_sc as plsc

F32 = jnp.float32
BF16 = jnp.bfloat16
EPS = 1e-6
M_HEADS = 4
GRID_W = 64
POOL_WINDOWS = (2, 4, 8, 16)
POOL_GROUP_DIM = 128
CHUNK = 256
N_MOD = 6
TOP_K = 2
SC_WINDOW = 64
GATE_PAD = 128
VMEM_LIMIT = 56 * 1024 * 1024


def _pick_tile(n, target, quantum=128):
    t = min(n, target) // quantum * quantum
    while n % t:
        t -= quantum
    return t


def _cparams(sem):
    return pltpu.CompilerParams(dimension_semantics=sem, vmem_limit_bytes=VMEM_LIMIT)


def _sigmoid(x):
    return 0.5 * jnp.tanh(0.5 * x) + 0.5


def _silu(x):
    h = 0.5 * x
    return h * jnp.tanh(h) + h


def _log_sigmoid(x):
    return jnp.minimum(x, 0.0) - jnp.log(1.0 + jnp.exp(-jnp.abs(x)))


def _split3(x):
    hi = x.astype(BF16)
    r1 = x - hi.astype(F32)
    mid = r1.astype(BF16)
    lo = (r1 - mid.astype(F32)).astype(BF16)
    return hi, mid, lo


def _dot(a, b):
    return jnp.dot(a, b, preferred_element_type=F32)


def _dot_nt(a, b):
    return lax.dot_general(a, b, (((1,), (1,)), ((), ())), preferred_element_type=F32)


def _dot_tn(a, b):
    return lax.dot_general(a, b, (((0,), (0,)), ((), ())), preferred_element_type=F32)


def _rms_mod(x, g, sc, sh):
    r = lax.rsqrt(jnp.mean(x * x, axis=-1, keepdims=True) + EPS)
    return (x * r) * g * (1.0 + sc) + sh


def _ada_kernel(c_ref, w_ref, b_ref, o_ref):
    c = c_ref[...]
    o_ref[0] = _dot(_silu(c).astype(BF16), w_ref[0].astype(BF16)) + b_ref[0]


def _ada(cc, w_ada, b_ada):
    depth, d, n = w_ada.shape
    tn = 1024
    return pl.pallas_call(
        _ada_kernel,
        out_shape=jax.ShapeDtypeStruct((depth, cc.shape[0], n), F32),
        grid=(depth, n // tn),
        in_specs=[pl.BlockSpec(cc.shape, lambda l, j: (0, 0)),
                  pl.BlockSpec((1, d, tn), lambda l, j: (l, 0, j)),
                  pl.BlockSpec((1, 1, tn), lambda l, j: (l, 0, j))],
        out_specs=pl.BlockSpec((1, cc.shape[0], tn), lambda l, j: (l, 0, j)),
        compiler_params=_cparams(("arbitrary", "arbitrary")),
        name="ada",
    )(cc, w_ada, b_ada.reshape(depth, 1, n))


def _inproj_kernel(xa_ref, xb_ref, xpa_ref, xpb_ref, xna_ref, xnb_ref, g_ref, sc_ref, sh_ref, w_ref, cw_ref,
                   zm_ref, zp_ref, zg_ref, u_ref, uh_ref,
                   *, n_main, n_pool, n_qk, cw, k_scale, n_lat_tiles, lat_len, ctx_len):
    i = pl.program_id(0)
    tm = xa_ref.shape[0]
    is_ctx = i >= n_lat_tiles
    g, sc, sh = g_ref[...], sc_ref[0], sh_ref[0]
    u_ref[...] = _rms_mod(jnp.where(is_ctx, xb_ref[...], xa_ref[...]), g, sc, sh).astype(BF16)
    uh_ref[0:8, :] = _rms_mod(jnp.where(is_ctx, xpb_ref[...], xpa_ref[...]), g, sc, sh).astype(BF16)
    uh_ref[8:16, :] = _rms_mod(jnp.where(is_ctx, xnb_ref[...], xna_ref[...]), g, sc, sh).astype(BF16)

    pos0 = lax.rem(i * tm, lat_len)
    lat_first_row = jnp.where(pos0 == 0, 0, -1)
    lat_last_row = jnp.where(pos0 + tm == lat_len, tm - 1, -1)
    sub = 16
    srow = lax.broadcasted_iota(jnp.int32, (sub, cw), 0)
    slabs = sorted({0, tm - sub} | {k * ctx_len - sub for k in range(1, tm // ctx_len)}
                   | {k * ctx_len for k in range(1, tm // ctx_len)})

    for c in range(n_qk // cw):
        cs = slice(c * cw, (c + 1) * cw)
        w0, w1, w2 = cw_ref[0:1, cs], cw_ref[1:2, cs], cw_ref[2:3, cs]
        scale = k_scale if c * cw >= n_qk // 2 else 1.0
        z = _dot(u_ref[...], w_ref[:, cs])
        zh = _dot(uh_ref[...], w_ref[:, cs])
        y = _silu(pltpu.roll(z, 1, 0) * w0 + z * w1 + pltpu.roll(z, tm - 1, 0) * w2) * scale
        zm_ref[:, cs] = y.astype(BF16)
        for r0 in slabs:
            zs = z[r0:r0 + sub, :]
            above = zh[7:8, :] if r0 == 0 else z[r0 - 1:r0, :]
            below = zh[8:9, :] if r0 + sub == tm else z[r0 + sub:r0 + sub + 1, :]
            grow = srow + r0
            in_ctx = grow & (ctx_len - 1)
            first = jnp.where(is_ctx, in_ctx, grow - lat_first_row) == 0
            last = jnp.where(is_ctx, in_ctx - (ctx_len - 1), grow - lat_last_row) == 0
            zm1 = jnp.where(srow == 0, above, pltpu.roll(zs, 1, 0))
            zm1 = jnp.where(first, 0.0, zm1)
            zp1 = jnp.where(srow == sub - 1, below, pltpu.roll(zs, sub - 1, 0))
            zp1 = jnp.where(last, 0.0, zp1)
            ys = _silu(zm1 * w0 + zs * w1 + zp1 * w2) * scale
            zm_ref[r0:r0 + sub, cs] = ys.astype(BF16)
    for c in range(n_qk // cw, n_main // cw):
        zm_ref[:, c * cw:(c + 1) * cw] = _dot(u_ref[...], w_ref[:, c * cw:(c + 1) * cw]).astype(BF16)
    for c0 in range(0, n_pool, cw):
        c1 = min(c0 + cw, n_pool)
        zp_ref[:, c0:c1] = _dot(u_ref[...], w_ref[:, n_main + c0:n_main + c1])
    zg_ref[...] = _dot(u_ref[...], w_ref[:, n_main + n_pool:])


def _inproj(src, g, mod3, w_cat, conv_w, mod_idx, *, tm, n_main, n_pool, n_qk, k_scale, n_lat, lat_len, ctx_len):
    a, b, b_off = src
    d = a.shape[1]
    r = n_lat + b.shape[0] - b_off
    ncols = w_cat.shape[1]
    halo = 8
    assert tm % ctx_len == 0 and lat_len % tm == 0 and ctx_len & (ctx_len - 1) == 0 and b_off % tm == 0
    na_t, hb = n_lat // tm, tm // halo
    a_last, b_last = a.shape[0] // halo - 1, b.shape[0] // halo - 1
    kern = functools.partial(_inproj_kernel, n_main=n_main, n_pool=n_pool, n_qk=n_qk, cw=512, k_scale=k_scale,
                             n_lat_tiles=na_t, lat_len=lat_len, ctx_len=ctx_len)
    return pl.pallas_call(
        kern,
        out_shape=(jax.ShapeDtypeStruct((r, n_main), BF16),
                   jax.ShapeDtypeStruct((r, n_pool), F32),
                   jax.ShapeDtypeStruct((r, GATE_PAD), F32)),
        grid=(r // tm,),
        in_specs=[pl.BlockSpec((tm, d), lambda i: (jnp.minimum(i, na_t - 1), 0)),
                  pl.BlockSpec((tm, d), lambda i: (jnp.maximum(i - na_t, 0) + b_off // tm, 0)),
                  pl.BlockSpec((halo, d), lambda i: (jnp.clip(i * hb - 1, 0, a_last), 0)),
                  pl.BlockSpec((halo, d), lambda i: (jnp.clip(b_off // halo + (i - na_t) * hb - 1, 0, b_last), 0)),
                  pl.BlockSpec((halo, d), lambda i: (jnp.clip((i + 1) * hb, 0, a_last), 0)),
                  pl.BlockSpec((halo, d), lambda i: (jnp.clip(b_off // halo + (i - na_t + 1) * hb, 0, b_last), 0)),
                  pl.BlockSpec((1, d), lambda i: (0, 0)),
                  pl.BlockSpec((1, 1, d), lambda i: (mod_idx(i) * N_MOD + 1, 0, 0)),
                  pl.BlockSpec((1, 1, d), lambda i: (mod_idx(i) * N_MOD + 0, 0, 0)),
                  pl.BlockSpec((d, ncols), lambda i: (0, 0)),
                  pl.BlockSpec((3, n_qk), lambda i: (0, 0))],
        out_specs=(pl.BlockSpec((tm, n_main), lambda i: (i, 0)),
                   pl.BlockSpec((tm, n_pool), lambda i: (i, 0)),
                   pl.BlockSpec((tm, GATE_PAD), lambda i: (i, 0))),
        scratch_shapes=[pltpu.VMEM((tm, d), BF16), pltpu.VMEM((2 * halo, d), BF16)],
        compiler_params=_cparams(("arbitrary",)),
        name="inproj",
    )(a, b, a, b, a, b, g, mod3, mod3, w_cat, conv_w)


def _running_max_rows(x, rev):
    n = x.shape[0]
    row = lax.broadcasted_iota(jnp.int32, x.shape, 0)
    sh = 1
    while sh < n:
        if rev:
            shifted = jnp.where(row < n - sh, pltpu.roll(x, n - sh, 0), -jnp.inf)
        else:
            shifted = jnp.where(row >= sh, pltpu.roll(x, sh, 0), -jnp.inf)
        x = jnp.maximum(x, shifted)
        sh *= 2
    return x


def _mlstm_kernel(*refs, dh, dirs, n_side):
    L = CHUNK
    nd = len(dirs)
    bias_c, bias_r = refs[5 * nd:5 * nd + 2]
    side_in = refs[5 * nd + 2:5 * nd + 2 + n_side]
    o0 = 5 * nd + 2 + n_side
    outs = refs[o0:o0 + nd]
    side_out = refs[o0 + nd:o0 + nd + n_side]
    c_ref, c16_ref, n_ref, n16_ref, m_ref = refs[o0 + nd + n_side:]

    for src, dst in zip(side_in, side_out):
        dst[...] = src[...].astype(BF16)

    @pl.when(pl.program_id(1) == 0)
    def _():
        c_ref[...] = jnp.zeros_like(c_ref)
        c16_ref[...] = jnp.zeros_like(c16_ref)
        n_ref[...] = jnp.zeros_like(n_ref)
        n16_ref[...] = jnp.zeros_like(n16_ref)
        m_ref[...] = jnp.zeros_like(m_ref)

    ri = lax.broadcasted_iota(jnp.int32, (L, L), 0)
    ci = lax.broadcasted_iota(jnp.int32, (L, L), 1)
    lower = ci <= ri
    upper = ci >= ri
    lower_b = jnp.where(lower, 1.0, 0.0).astype(BF16)
    upper_b = jnp.where(upper, 1.0, 0.0).astype(BF16)

    for di, d in enumerate(dirs):
        q_ref, k_ref, v_ref, gc_ref, gr_ref = refs[5 * di:5 * di + 5]
        out_ref = outs[di]
        rev = d == 1
        gc = gc_ref[...] + bias_c[...]
        gr = gr_ref[...] + bias_r[...]
        fc = _log_sigmoid(gc)
        fr = _log_sigmoid(gr)
        t_col = upper_b if rev else lower_b
        t_row = lower_b if rev else upper_b
        bc3 = _dot(t_col, jnp.concatenate(_split3(fc), axis=1))
        bcum_c = bc3[:, :GATE_PAD] + bc3[:, GATE_PAD:2 * GATE_PAD] + bc3[:, 2 * GATE_PAD:]
        br3 = _dot(jnp.concatenate(_split3(fr), axis=0), t_row)
        bcum_r = br3[0:16, :] + br3[16:32, :] + br3[32:48, :]
        mask = upper if rev else lower
        end = 0 if rev else L - 1

        b_al = pltpu.roll(bcum_c, GATE_PAD - M_HEADS, axis=1)
        g_col = gc - b_al
        m_st = m_ref[di:di + 1, :]
        mx = jnp.maximum(m_st, _running_max_rows(g_col, rev))
        e_neg_all = jnp.exp(-(b_al + mx))
        mx_end = mx[end:end + 1, :]
        wk_all = jnp.exp(g_col - mx_end)
        wc_all = jnp.exp(m_st - mx_end)
        m_ref[di:di + 1, :] = b_al[end:end + 1, :] + mx_end

        ones_rep = jnp.ones((L, GATE_PAD), BF16)
        for h in range(M_HEADS):
            s_idx = di * M_HEADS + h
            col = d * 2 * M_HEADS + h

            def rep(x_all):
                return jnp.broadcast_to(x_all[:, col:col + 1], (x_all.shape[0], GATE_PAD))

            def wide(x_rep):
                return jnp.concatenate([x_rep] * (dh // GATE_PAD), axis=1)

            mx_rep = rep(mx)
            g_row = gr[col:col + 1, :] - bcum_r[col + M_HEADS:col + M_HEADS + 1, :]
            p = jnp.exp(jnp.where(mask, g_row - wide(mx_rep), -jnp.inf))
            w_inter = jnp.exp(rep(m_st) - mx_rep)
            e_neg = rep(e_neg_all)
            wk16 = rep(wk_all).astype(BF16)
            wc = wc_all[:, col:col + 1]

            sl = slice(h * dh, (h + 1) * dh)
            q = q_ref[:, sl]
            k = k_ref[:, sl]
            v = v_ref[:, sl]
            scores = (_dot_nt(q, k) * p).astype(BF16)
            num = wide(w_inter) * _dot(q, c16_ref[s_idx]) + _dot(scores, v)
            den = w_inter * _dot(q, n16_ref[s_idx]) + _dot(scores, ones_rep)
            inv = 1.0 / jnp.maximum(jnp.abs(den), e_neg)
            out_ref[:, sl] = (num * wide(inv)).astype(BF16)

            upd = _dot_tn(k, jnp.concatenate([v * wide(wk16), wk16], axis=1))
            c_new = wc * c_ref[s_idx] + upd[:, :dh]
            c_ref[s_idx] = c_new
            c16_ref[s_idx] = c_new.astype(BF16)
            n_new = wc * n_ref[s_idx] + upd[:, dh:]
            n_ref[s_idx] = n_new
            n16_ref[s_idx] = n_new.astype(BF16)


def _mlstm(zm, zg, zg_t, bias_c, bias_r, *, b, s, width, dirs, side=()):
    r = zm.shape[0]
    L = CHUNK
    n_lat = s // L
    lat_blocks = b * n_lat
    nd = len(dirs)

    def blk_f(bi, i):
        return jnp.where(i == 0, lat_blocks + bi, bi * n_lat + i - 1)

    def blk_b(bi, i):
        return jnp.where(i == 0, lat_blocks + bi, bi * n_lat + n_lat - i)

    blks = [blk_b if d == 1 else blk_f for d in dirs]

    def specs(blk):
        return [pl.BlockSpec((L, width), lambda bi, i: (blk(bi, i), 0)),
                pl.BlockSpec((L, width), lambda bi, i: (blk(bi, i), 1)),
                pl.BlockSpec((L, width), lambda bi, i: (blk(bi, i), 2)),
                pl.BlockSpec((L, GATE_PAD), lambda bi, i: (blk(bi, i), 0)),
                pl.BlockSpec((16, L), lambda bi, i: (0, blk(bi, i)))]

    dh = width // M_HEADS
    kern = functools.partial(_mlstm_kernel, dh=dh, dirs=tuple(dirs), n_side=len(side))
    out = jax.ShapeDtypeStruct((r, width), BF16)
    in_specs = []
    for blk in blks:
        in_specs += specs(blk)
    in_specs += [pl.BlockSpec((1, GATE_PAD), lambda bi, i: (0, 0)), pl.BlockSpec((16, L), lambda bi, i: (0, 0))]
    n_i = n_lat + 1
    side_specs = []
    for a in side:
        rows = _side_rows(a.shape[0], b * n_i)
        last = a.shape[0] // rows - 1
        side_specs.append(pl.BlockSpec(
            (rows, a.shape[1]), functools.partial(lambda last, bi, i: (jnp.minimum(bi * n_i + i, last), 0), last)))
    return pl.pallas_call(
        kern,
        out_shape=(out,) * nd + tuple(jax.ShapeDtypeStruct(a.shape, BF16) for a in side),
        grid=(b, n_i),
        in_specs=in_specs + side_specs,
        out_specs=tuple(pl.BlockSpec((L, width), functools.partial(lambda blk, bi, i: (blk(bi, i), 0), blk))
                        for blk in blks) + tuple(side_specs),
        scratch_shapes=[pltpu.VMEM((nd * M_HEADS, dh, dh), F32),
                        pltpu.VMEM((nd * M_HEADS, dh, dh), BF16),
                        pltpu.VMEM((nd * M_HEADS, dh, GATE_PAD), F32),
                        pltpu.VMEM((nd * M_HEADS, dh, GATE_PAD), BF16),
                        pltpu.VMEM((8, GATE_PAD), F32)],
        compiler_params=_cparams(("arbitrary", "arbitrary")),
        name="mlstm",
    )(*((zm, zm, zm, zg, zg_t) * nd), bias_c, bias_r, *side)


POOL_BLOCK = 256


def _band(n, w, seg):
    r = lax.broadcasted_iota(jnp.int32, (n, n), 0)
    c = lax.broadcasted_iota(jnp.int32, (n, n), 1)
    d = c - r
    shift = seg.bit_length() - 1
    ok = (d >= -(w // 2)) & (d <= w // 2 - 1) & ((r >> shift) == (c >> shift))
    return jnp.where(ok, 1.0, 0.0).astype(BF16)


def _window_count(idx, n, w):
    return (jnp.minimum(idx + w // 2, n) - jnp.maximum(idx - w // 2, 0)).astype(F32)


def _pool_kernel(z_ref, o_ref, pad_ref, *, grid_rows):
    n = z_ref.shape[0]
    blk = min(POOL_BLOCK, n)
    seg = blk if grid_rows is None else GRID_W
    seg_len = n if grid_rows is None else GRID_W
    halo = (max(POOL_WINDOWS) // 2) * GRID_W
    rowi = lax.broadcasted_iota(jnp.int32, (blk, POOL_GROUP_DIM), 0)
    if grid_rows is not None:
        zeros = jnp.zeros((halo, POOL_GROUP_DIM), F32)
        pad_ref[0:halo, :] = zeros
        pad_ref[halo + n:, :] = zeros
    for g, w in enumerate(POOL_WINDOWS):
        sl = slice(g * POOL_GROUP_DIM, (g + 1) * POOL_GROUP_DIM)
        band = _band(blk, w, seg)
        inv_w = 1.0 / _window_count(rowi & (seg - 1), seg_len, w)
        for b in range(n // blk):
            rows = slice(b * blk, (b + 1) * blk)
            x = z_ref[rows, sl]
            hi = x.astype(BF16)
            lo = (x - hi.astype(F32)).astype(BF16)
            y2 = _dot(band, jnp.concatenate([hi, lo], axis=1))
            y = (y2[:, :POOL_GROUP_DIM] + y2[:, POOL_GROUP_DIM:]) * inv_w
            if grid_rows is None:
                o_ref[rows, sl] = (y - x).astype(BF16)
            else:
                pad_ref[halo + b * blk:halo + (b + 1) * blk, :] = y
        if grid_rows is not None:
            shift = GRID_W.bit_length() - 1
            for b in range(n // blk):
                acc = None
                for d in range(-(w // 2), w // 2):
                    lo_r = halo + b * blk + d * GRID_W
                    t = pad_ref[lo_r:lo_r + blk, :]
                    acc = t if acc is None else acc + t
                inv_h = 1.0 / _window_count((rowi + b * blk) >> shift, grid_rows, w)
                rows = slice(b * blk, (b + 1) * blk)
                o_ref[rows, sl] = (acc * inv_h - z_ref[rows, sl]).astype(BF16)


def _pool(zp, *, n_seq, seq_len, row_block0, grid_rows):
    width = zp.shape[1]
    assert seq_len % POOL_BLOCK == 0 if grid_rows is not None else seq_len <= POOL_BLOCK
    kern = functools.partial(_pool_kernel, grid_rows=grid_rows)
    halo = (max(POOL_WINDOWS) // 2) * GRID_W
    return pl.pallas_call(
        kern,
        out_shape=jax.ShapeDtypeStruct((n_seq * seq_len, width), BF16),
        grid=(n_seq,),
        in_specs=[pl.BlockSpec((seq_len, width), lambda s: (row_block0 + s, 0))],
        out_specs=pl.BlockSpec((seq_len, width), lambda s: (s, 0)),
        scratch_shapes=[pltpu.VMEM((seq_len + 2 * halo, POOL_GROUP_DIM), F32)],
        compiler_params=_cparams(("arbitrary",)),
        name="pool",
    )(zp)


def _fold_pool_kernel(wp_ref, ps_ref, wpb_ref, o_ref):
    for g in range(len(POOL_WINDOWS)):
        sl = slice(g * POOL_GROUP_DIM, (g + 1) * POOL_GROUP_DIM)
        a_hi, a_mid, _ = _split3(wp_ref[g] * ps_ref[:, sl])
        b_hi, b_mid, _ = _split3(wpb_ref[sl, :])
        o_ref[sl, :] = (_dot(a_hi, b_hi) + _dot(a_hi, b_mid) + _dot(a_mid, b_hi)).astype(BF16)


def _fold_pool(w_pool, pool_scale, w_pb):
    return pl.pallas_call(
        _fold_pool_kernel,
        out_shape=jax.ShapeDtypeStruct(w_pb.shape, BF16),
        compiler_params=pltpu.CompilerParams(vmem_limit_bytes=VMEM_LIMIT),
        name="fold_pool",
    )(w_pool, pool_scale, w_pb)


def _mixout_kernel(hf_ref, hb_ref, o_ref, ga_ref, gb_ref, pml_ref, pmc_ref, xa_ref, xb_ref, gt_ref, hg_ref,
                   wpa_ref, wpb_ref, wout_ref, out_ref, *, dh, n_lat_tiles):
    is_ctx = pl.program_id(0) >= n_lat_tiles
    hm = hf_ref[...].astype(F32) + hb_ref[...].astype(F32)
    parts = []
    for h in range(M_HEADS):
        t = hm[:, h * dh:(h + 1) * dh]
        mu = jnp.mean(t, axis=-1, keepdims=True)
        tc = t - mu
        var = jnp.mean(tc * tc, axis=-1, keepdims=True)
        parts.append(tc * lax.rsqrt(var + EPS))
    hn = jnp.concatenate(parts, axis=-1)
    hn = _sigmoid(o_ref[...].astype(F32)) * (hn * hg_ref[...])
    a = _dot(hn.astype(BF16), wpa_ref[...])
    bmat = _dot(jnp.where(is_ctx, pmc_ref[...], pml_ref[...]), wpb_ref[...])
    merged = _sigmoid(ga_ref[...].astype(F32)) * a + _sigmoid(gb_ref[...].astype(F32)) * bmat
    y = _dot(merged.astype(BF16), wout_ref[...])
    out_ref[...] = jnp.where(is_ctx, xb_ref[...], xa_ref[...]) + gt_ref[0] * y


def _mixout(hf, hb, zm, pm_lat, pm_ctx, src, mod3, head_gain, w_pa, w_pb, w_out, mod_idx, *, tm, n_rows):
    xa, xb, b_off = src
    d = xa.shape[1]
    width = hf.shape[1]
    pw = pm_lat.shape[1]
    n_lat_tiles = pm_lat.shape[0] // tm
    o_blk = 3
    full = lambda shape: pl.BlockSpec(shape, lambda i: (0,) * len(shape))
    kern = functools.partial(_mixout_kernel, dh=width // M_HEADS, n_lat_tiles=n_lat_tiles)
    return pl.pallas_call(
        kern,
        out_shape=jax.ShapeDtypeStruct((n_rows, d), F32),
        grid=(n_rows // tm,),
        in_specs=[pl.BlockSpec((tm, width), lambda i: (i, 0)),
                  pl.BlockSpec((tm, width), lambda i: (i, 0)),
                  pl.BlockSpec((tm, width), lambda i: (i, o_blk)),
                  pl.BlockSpec((tm, d), lambda i: (i, o_blk + 1)),
                  pl.BlockSpec((tm, d), lambda i: (i, o_blk + 2)),
                  pl.BlockSpec((tm, pw), lambda i: (jnp.minimum(i, n_lat_tiles - 1), 0)),
                  pl.BlockSpec((tm, pw), lambda i: (jnp.maximum(i - n_lat_tiles, 0), 0)),
                  pl.BlockSpec((tm, d), lambda i: (jnp.minimum(i, n_lat_tiles - 1), 0)),
                  pl.BlockSpec((tm, d), lambda i: (jnp.maximum(i - n_lat_tiles, 0) + b_off // tm, 0)),
                  pl.BlockSpec((1, 1, d), lambda i: (mod_idx(i) * N_MOD + 2, 0, 0)),
                  full((1, width)), full(w_pa.shape), full(w_pb.shape), full(w_out.shape)],
        out_specs=pl.BlockSpec((tm, d), lambda i: (i, 0)),
        compiler_params=_cparams(("arbitrary",)),
        name="mixout",
    )(hf, hb, zm, zm, zm, pm_lat, pm_ctx, xa, xb, mod3, head_gain, w_pa, w_pb, w_out)


SWIGLU_SUB = 512


def _swiglu_partial(u_ref, wg, wu, wd):
    tf = wg.shape[1]
    acc = None
    for c0 in range(0, tf, SWIGLU_SUB):
        cs = slice(c0, min(c0 + SWIGLU_SUB, tf))
        u = u_ref[...]
        act = (_silu(_dot(u, wg[:, cs])) * _dot(u, wu[:, cs])).astype(BF16)
        part = _dot(act, wd[cs, :])
        acc = part if acc is None else acc + part
    return acc


def _ffn_kernel(x_ref, g_ref, sc_ref, sh_ref, gt_ref, wg_ref, wu_ref, wd_ref, *rest, n_side):
    side_in, out_ref, side_out, u_ref = rest[:n_side], rest[n_side], rest[n_side + 1:2 * n_side + 1], rest[-1]
    j = pl.program_id(1)
    u_ref[...] = _rms_mod(x_ref[...], g_ref[...], sc_ref[0], sh_ref[0]).astype(BF16)
    contrib = gt_ref[0] * _swiglu_partial(u_ref, wg_ref, wu_ref, wd_ref)

    @pl.when(j == 0)
    def _():
        out_ref[...] = x_ref[...] + contrib
        for src, dst in zip(side_in, side_out):
            dst[...] = src[...].astype(BF16)

    @pl.when(j > 0)
    def _():
        out_ref[...] += contrib


def _side_rows(n_rows, n_steps):
    rows = (-(-n_rows // n_steps) + 15) // 16 * 16
    while n_rows % rows:
        rows += 16
    return rows


def _ffn(h, g, mod3, wg, wu, wd, mod_idx, *, tm, tf, side=()):
    r, d = h.shape
    f = wg.shape[1]
    n_steps = r // tm
    once = pl.Buffered(1) if f == tf else None
    side_specs_in, side_specs_out, side_shapes = [], [], []
    for a in side:
        rows = _side_rows(a.shape[0], n_steps)
        last = a.shape[0] // rows - 1
        spec = pl.BlockSpec((rows, a.shape[1]), functools.partial(lambda last, i, j: (jnp.minimum(i, last), 0), last))
        side_specs_in.append(spec)
        side_specs_out.append(spec)
        side_shapes.append(jax.ShapeDtypeStruct(a.shape, BF16))
    return pl.pallas_call(
        functools.partial(_ffn_kernel, n_side=len(side)),
        out_shape=(jax.ShapeDtypeStruct((r, d), F32), *side_shapes),
        grid=(n_steps, f // tf),
        in_specs=[pl.BlockSpec((tm, d), lambda i, j: (i, 0)),
                  pl.BlockSpec((1, d), lambda i, j: (0, 0)),
                  pl.BlockSpec((1, 1, d), lambda i, j: (mod_idx(i) * N_MOD + 4, 0, 0)),
                  pl.BlockSpec((1, 1, d), lambda i, j: (mod_idx(i) * N_MOD + 3, 0, 0)),
                  pl.BlockSpec((1, 1, d), lambda i, j: (mod_idx(i) * N_MOD + 5, 0, 0)),
                  pl.BlockSpec((d, tf), lambda i, j: (0, j), pipeline_mode=once),
                  pl.BlockSpec((d, tf), lambda i, j: (0, j), pipeline_mode=once),
                  pl.BlockSpec((tf, d), lambda i, j: (j, 0), pipeline_mode=once),
                  *side_specs_in],
        out_specs=(pl.BlockSpec((tm, d), lambda i, j: (i, 0)), *side_specs_out),
        scratch_shapes=[pltpu.VMEM((tm, d), BF16)],
        compiler_params=_cparams(("arbitrary", "arbitrary")),
        name="ffn",
    )(h, g, mod3, mod3, mod3, wg, wu, wd, *side)


def _route(u, wr_ref, n_exp):
    u_hi = u.astype(BF16)
    u_mid = (u - u_hi.astype(F32)).astype(BF16)
    w_hi, w_mid, _ = _split3(wr_ref[...])
    both = _dot(u_hi, jnp.concatenate([w_hi, w_mid], axis=1))
    logits = both[:, :GATE_PAD] + both[:, GATE_PAD:] + _dot(u_mid, w_hi)
    lane = lax.broadcasted_iota(jnp.int32, logits.shape, 1)
    lg = jnp.where(lane < n_exp, logits, -jnp.inf)
    m1 = jnp.max(lg, axis=1, keepdims=True)
    i1 = jnp.min(jnp.where(lg == m1, lane, 2 * GATE_PAD), axis=1, keepdims=True)
    lg2 = jnp.where(lane == i1, -jnp.inf, lg)
    m2 = jnp.max(lg2, axis=1, keepdims=True)
    i2 = jnp.min(jnp.where(lg2 == m2, lane, 2 * GATE_PAD), axis=1, keepdims=True)
    e = jnp.exp(m2 - m1)
    p1 = 1.0 / (1.0 + e)
    p2 = e * p1
    return i1, i2, p1, p2


LANE_I1, LANE_I2, LANE_P1, LANE_P2, LANE_R1, LANE_R2 = 0, 1, 2, 3, 4, 5


def _pack_bf16_pairs(x):
    k = x.shape[1] // 2
    lo = lax.bitcast_convert_type(x[:, :k].astype(BF16).astype(F32), jnp.uint32)
    hi = lax.bitcast_convert_type(x[:, k:].astype(BF16).astype(F32), jnp.uint32)
    return (lo >> 16) | (hi & jnp.uint32(0xFFFF0000))


def _unpack_bf16_pairs(w):
    lo = lax.bitcast_convert_type(w << 16, F32)
    hi = lax.bitcast_convert_type(w & jnp.uint32(0xFFFF0000), F32)
    return jnp.concatenate([lo, hi], axis=1)


def _route_kernel(x_ref, g_ref, sc_ref, sh_ref, wr_ref, up_ref, info_ref, cnt_ref, carry_ref, *, n_exp):
    @pl.when(pl.program_id(0) == 0)
    def _():
        carry_ref[...] = jnp.zeros_like(carry_ref)

    u = _rms_mod(x_ref[...], g_ref[...], sc_ref[0], sh_ref[0])
    up_ref[...] = _pack_bf16_pairs(u)
    i1, i2, p1, p2 = _route(u, wr_ref, n_exp)
    tm = u.shape[0]
    lane = lax.broadcasted_iota(jnp.int32, (tm, GATE_PAD), 1)
    sel = jnp.where((lane == i1) | (lane == i2), 1.0, 0.0)
    ri = lax.broadcasted_iota(jnp.int32, (tm, tm), 0)
    ci = lax.broadcasted_iota(jnp.int32, (tm, tm), 1)
    before = jnp.where(ci < ri, 1.0, 0.0).astype(BF16)
    rank = carry_ref[0:1, :] + _dot(before, sel.astype(BF16))
    r1 = jnp.sum(jnp.where(lane == i1, rank, 0.0), axis=1, keepdims=True)
    r2 = jnp.sum(jnp.where(lane == i2, rank, 0.0), axis=1, keepdims=True)
    carry_ref[0:1, :] = carry_ref[0:1, :] + jnp.sum(sel, axis=0, keepdims=True)
    cnt_ref[...] = carry_ref[...]
    info = jnp.zeros((tm, GATE_PAD), F32)
    for ln, val in ((LANE_I1, i1.astype(F32)), (LANE_I2, i2.astype(F32)), (LANE_P1, p1), (LANE_P2, p2),
                    (LANE_R1, r1), (LANE_R2, r2)):
        info = jnp.where(lane == ln, val, info)
    info_ref[...] = info


def _route_call(h, g, mod3, w_router, mod_idx, *, tm, n_rows):
    d = h.shape[1]
    n_exp = w_router.shape[1]
    w_r = jnp.concatenate([w_router, jnp.zeros((d, GATE_PAD - n_exp), F32)], axis=1)
    kern = functools.partial(_route_kernel, n_exp=n_exp)
    return pl.pallas_call(
        kern,
        out_shape=(jax.ShapeDtypeStruct((n_rows, d // 2), jnp.uint32),
                   jax.ShapeDtypeStruct((n_rows, GATE_PAD), F32),
                   jax.ShapeDtypeStruct((8, GATE_PAD), F32)),
        grid=(n_rows // tm,),
        in_specs=[pl.BlockSpec((tm, d), lambda i: (i, 0)),
                  pl.BlockSpec((1, d), lambda i: (0, 0)),
                  pl.BlockSpec((1, 1, d), lambda i: (mod_idx(i) * N_MOD + 4, 0, 0)),
                  pl.BlockSpec((1, 1, d), lambda i: (mod_idx(i) * N_MOD + 3, 0, 0)),
                  pl.BlockSpec((d, GATE_PAD), lambda i: (0, 0))],
        out_specs=(pl.BlockSpec((tm, d // 2), lambda i: (i, 0)),
                   pl.BlockSpec((tm, GATE_PAD), lambda i: (i, 0)),
                   pl.BlockSpec((8, GATE_PAD), lambda i: (0, 0))),
        scratch_shapes=[pltpu.VMEM((8, GATE_PAD), F32)],
        compiler_params=_cparams(("arbitrary",)),
        name="route",
    )(h, g, mod3, mod3, w_r)


def _sc_workers():
    info = pltpu.get_tpu_info().sparse_core
    return info.num_cores, info.num_subcores


def _sc_scatter_rows(x, pos, n_out):
    n, dw = x.shape
    nc, ns = _sc_workers()
    nw = nc * ns
    t_per_w = n // nw
    w = min(SC_WINDOW, t_per_w // 2)
    n_chunks = t_per_w // w
    assert n % nw == 0 and t_per_w % (2 * w) == 0
    pos_w = pos.reshape(2, nw, n_chunks, w).transpose(1, 0, 2, 3)
    mesh = plsc.VectorSubcoreMesh(core_axis_name="c", subcore_axis_name="s")

    @functools.partial(pl.kernel, mesh=mesh, out_type=jax.ShapeDtypeStruct((n_out, dw), x.dtype),
                       scratch_types=[pltpu.VMEM((2, n_chunks, w), jnp.int32), pltpu.VMEM((2, w, dw), x.dtype),
                                      pltpu.SemaphoreType.DMA((2,)), pltpu.SemaphoreType.DMA((2,))])
    def scatter(x_hbm, pos_hbm, out_hbm, idx_v, rows_v, gsem, osem):
        wid = lax.axis_index("s") * nc + lax.axis_index("c")
        base = wid * t_per_w
        pltpu.sync_copy(pos_hbm.at[wid], idx_v)

        def get(g, slot):
            return pltpu.make_async_copy(x_hbm.at[pl.ds(base + g * w, w)], rows_v.at[slot], gsem.at[slot])

        def put(g, slot, k):
            return pltpu.make_async_copy(rows_v.at[slot], out_hbm.at[idx_v.at[k, g]], osem.at[slot])

        get(0, 0).start()

        @pl.loop(0, n_chunks, step=2)
        def _(g):
            for b in range(2):
                gg = g + b
                get(gg, b).wait()

                @pl.when(gg + 1 < n_chunks)
                def _():
                    @pl.when(gg >= 1)
                    def _():
                        put(gg - 1, 1 - b, 0).wait()
                        put(gg - 1, 1 - b, 1).wait()
                    get(gg + 1, 1 - b).start()

                put(gg, b, 0).start()
                put(gg, b, 1).start()

        for k in range(2):
            put(n_chunks - 2, 0, k).wait()
            put(n_chunks - 1, 1, k).wait()

    return scatter(x, pos_w)


def _sc_gather_rows(table, idx):
    n = idx.shape[0]
    dw = table.shape[1]
    nc, ns = _sc_workers()
    nw = nc * ns
    b_per_w = n // nw
    w = min(SC_WINDOW, b_per_w // 2)
    n_chunks = b_per_w // w
    assert n % nw == 0 and b_per_w % (2 * w) == 0
    mesh = plsc.VectorSubcoreMesh(core_axis_name="c", subcore_axis_name="s")

    @functools.partial(pl.kernel, mesh=mesh, out_type=jax.ShapeDtypeStruct((n, dw), table.dtype),
                       scratch_types=[pltpu.VMEM((b_per_w,), jnp.int32), pltpu.VMEM((2, w, dw), table.dtype),
                                      pltpu.SemaphoreType.DMA((2,)), pltpu.SemaphoreType.DMA((2,))])
    def gather(table_hbm, idx_hbm, out_hbm, idx_v, rows_v, gsem, osem):
        wid = lax.axis_index("s") * nc + lax.axis_index("c")
        base = wid * b_per_w
        pltpu.sync_copy(idx_hbm.at[pl.ds(base, b_per_w)], idx_v)

        def get(g, slot):
            return pltpu.make_async_copy(table_hbm.at[idx_v.at[pl.ds(g * w, w)]], rows_v.at[slot], gsem.at[slot])

        def put(g, slot):
            return pltpu.make_async_copy(rows_v.at[slot], out_hbm.at[pl.ds(base + g * w, w)], osem.at[slot])

        get(0, 0).start()

        @pl.loop(0, n_chunks, step=2)
        def _(g):
            for b in range(2):
                gg = g + b
                get(gg, b).wait()

                @pl.when(gg + 1 < n_chunks)
                def _():
                    @pl.when(gg >= 1)
                    def _():
                        put(gg - 1, 1 - b).wait()
                    get(gg + 1, 1 - b).start()

                put(gg, b).start()

        put(n_chunks - 2, 0).wait()
        put(n_chunks - 1, 1).wait()

    return gather(table, idx)


def _gmm_kernel(te_ref, nv_ref, xs_ref, wg_ref, wu_ref, wd_ref, ys_ref, u_ref):
    @pl.when(pl.program_id(0) < nv_ref[0])
    def _():
        u_ref[...] = _unpack_bf16_pairs(xs_ref[...]).astype(BF16)
        ys_ref[...] = _pack_bf16_pairs(_swiglu_partial(u_ref, wg_ref.at[0], wu_ref.at[0], wd_ref.at[0]))


def _gmm(xs, tile_expert, n_valid, wg, wu, wd, *, tg):
    p, dw = xs.shape
    d = 2 * dw
    f = wg.shape[2]
    grid_spec = pltpu.PrefetchScalarGridSpec(
        num_scalar_prefetch=2,
        grid=(p // tg,),
        in_specs=[pl.BlockSpec((tg, dw), lambda i, te, nv: (i, 0)),
                  pl.BlockSpec((1, d, f), lambda i, te, nv: (te[i], 0, 0)),
                  pl.BlockSpec((1, d, f), lambda i, te, nv: (te[i], 0, 0)),
                  pl.BlockSpec((1, f, d), lambda i, te, nv: (te[i], 0, 0))],
        out_specs=pl.BlockSpec((tg, dw), lambda i, te, nv: (i, 0)),
        scratch_shapes=[pltpu.VMEM((tg, d), BF16)])
    return pl.pallas_call(
        _gmm_kernel,
        out_shape=jax.ShapeDtypeStruct((p, dw), jnp.uint32),
        grid_spec=grid_spec,
        compiler_params=_cparams(("arbitrary",)),
        name="gmm",
    )(tile_expert, n_valid, xs, wg, wu, wd)


def _combine_kernel(x_ref, info_ref, y1_ref, y2_ref, gt_ref, gf_ref, out_ref, *, final):
    info = info_ref[...]
    p1 = info[:, LANE_P1:LANE_P1 + 1]
    p2 = info[:, LANE_P2:LANE_P2 + 1]
    moe = p1 * _unpack_bf16_pairs(y1_ref[...]) + p2 * _unpack_bf16_pairs(y2_ref[...])
    y = x_ref[...] + gt_ref[0] * moe
    if final:
        r = lax.rsqrt(jnp.mean(y * y, axis=-1, keepdims=True) + EPS)
        y = (y * r) * gf_ref[...]
    out_ref[...] = y


def _combine(h, info, yg, mod3, g_final, mod_idx, *, tm, n_rows, final):
    d = h.shape[1]
    n_t = n_rows // tm
    kern = functools.partial(_combine_kernel, final=final)
    return pl.pallas_call(
        kern,
        out_shape=jax.ShapeDtypeStruct((n_rows, d), F32),
        grid=(n_t,),
        in_specs=[pl.BlockSpec((tm, d), lambda i: (i, 0)),
                  pl.BlockSpec((tm, GATE_PAD), lambda i: (i, 0)),
                  pl.BlockSpec((tm, d // 2), lambda i: (i, 0)),
                  pl.BlockSpec((tm, d // 2), lambda i: (n_t + i, 0)),
                  pl.BlockSpec((1, 1, d), lambda i: (mod_idx(i) * N_MOD + 5, 0, 0)),
                  pl.BlockSpec((1, d), lambda i: (0, 0))],
        out_specs=pl.BlockSpec((tm, d), lambda i: (i, 0)),
        compiler_params=_cparams(("arbitrary",)),
        name="combine",
    )(h, info, yg, yg, mod3, g_final)


def _moe(h, g, mod3, w_router, wg, wu, wd, g_final, mod_idx, *, tm, tg, n_rows, final):
    n_exp = wg.shape[0]
    up, info, cnt = _route_call(h, g, mod3, w_router, mod_idx, tm=tm, n_rows=n_rows)
    i1 = info[:, LANE_I1].astype(jnp.int32)
    i2 = info[:, LANE_I2].astype(jnp.int32)
    r1 = info[:, LANE_R1].astype(jnp.int32)
    r2 = info[:, LANE_R2].astype(jnp.int32)
    counts = cnt[0, :n_exp].astype(jnp.int32)
    padded = (counts + tg - 1) // tg * tg
    ends = jnp.cumsum(padded)
    starts = ends - padded
    eye = jnp.arange(n_exp, dtype=jnp.int32)
    pos1 = jnp.sum(jnp.where(i1[:, None] == eye, starts, 0), axis=1) + r1
    pos2 = jnp.sum(jnp.where(i2[:, None] == eye, starts, 0), axis=1) + r2
    p_rows = -(-(TOP_K * n_rows + n_exp * (tg - 1)) // tg) * tg
    tile_row = jnp.arange(p_rows // tg, dtype=jnp.int32) * tg
    tile_expert = jnp.minimum(jnp.sum(tile_row[:, None] >= ends[None, :], axis=1), n_exp - 1).astype(jnp.int32)
    n_valid = (ends[-1] // tg).reshape(1).astype(jnp.int32)
    xs = _sc_scatter_rows(up, jnp.stack([pos1, pos2]), p_rows)
    ys = _gmm(xs, tile_expert, n_valid, wg, wu, wd, tg=tg)
    yg = _sc_gather_rows(ys, jnp.concatenate([pos1, pos2]))
    return _combine(h, info, yg, mod3, g_final, mod_idx, tm=tm, n_rows=n_rows, final=final)


def _final_norm_kernel(x_ref, g_ref, o_ref):
    x = x_ref[...]
    r = lax.rsqrt(jnp.mean(x * x, axis=-1, keepdims=True) + EPS)
    o_ref[...] = (x * r) * g_ref[...]


def _final_norm(h, g, *, tm, n_rows):
    d = h.shape[1]
    return pl.pallas_call(
        _final_norm_kernel,
        out_shape=jax.ShapeDtypeStruct((n_rows, d), F32),
        grid=(n_rows // tm,),
        in_specs=[pl.BlockSpec((tm, d), lambda i: (i, 0)), pl.BlockSpec((1, d), lambda i: (0, 0))],
        out_specs=pl.BlockSpec((tm, d), lambda i: (i, 0)),
        compiler_params=_cparams(("arbitrary",)),
        name="final_norm",
    )(h, g)


def kernel(x, c, ctx, c_ctx, w_ada, b_ada, g_mix, w_in, conv_qk, b_if, head_gain, w_pool, pool_scale, w_pa, w_pb,
           w_out, g_ffn, w_ff_gate, w_ff_up, w_ff_down, w_router, w_exp_gate, w_exp_up, w_exp_down, g_final):
    b, s, d = x.shape
    lc = ctx.shape[1]
    depth = w_ada.shape[0]
    width = w_pa.shape[1]
    pw = w_pb.shape[1]
    n_gate = b_if.shape[1] * b_if.shape[2] * b_if.shape[3]
    assert lc == CHUNK and s % CHUNK == 0 and s % GRID_W == 0 and width == d and n_gate == 16
    n_lat, n_ctx = b * s, b * lc
    tm, tm_ffn, tm_moe = 512, 512, 1024
    assert all(s % t == 0 and n_ctx % t == 0 for t in (tm, tm_ffn, tm_moe))

    def mod_idx(t):
        return lambda i: jnp.where(i < n_lat // t, i // (s // t), b)

    src = (x.reshape(n_lat, d), ctx.reshape(n_ctx, d), 0)

    n_mod_rows = 16
    cc = jnp.zeros((n_mod_rows, d), F32).at[:b].set(c).at[b].set(c_ctx)
    mod = _ada(cc, w_ada, b_ada)

    if_off, pool_off, ga_off = 4 * width, 4 * width + n_gate, 4 * width + n_gate + pw
    n_main = 4 * width + 2 * d

    exp16 = []
    for l in range(depth):
        last = l == depth - 1
        mod3 = mod[l].reshape(n_mod_rows * N_MOD, 1, d)
        w_l = w_in[l]
        w_cat = jnp.concatenate(
            [w_l[:, :if_off], w_l[:, ga_off:], w_l[:, pool_off:ga_off], w_l[:, if_off:pool_off],
             jnp.zeros((d, GATE_PAD - n_gate), F32)], axis=1).astype(BF16)
        k_scale = float((width // M_HEADS) ** -0.5)
        zm, zp, zg = _inproj(src, g_mix[l].reshape(1, d), mod3, w_cat, conv_qk[l], mod_idx(tm), tm=tm, n_main=n_main,
                             n_pool=pw, n_qk=2 * width, k_scale=k_scale, n_lat=n_lat, lat_len=s, ctx_len=lc)

        bias = b_if[l].reshape(n_gate)
        bias_c = jnp.zeros((1, GATE_PAD), F32).at[0, :n_gate].set(bias)
        bias_r = jnp.broadcast_to(bias[:, None], (n_gate, CHUNK))
        j = l // 2
        late_w = (w_pa[l], w_out[l]) + ((w_ff_gate[j], w_ff_up[j], w_ff_down[j]) if l % 2 == 0 else ())
        hf, hb, w_pa16, w_out16, *ff16 = _mlstm(zm, zg, zg[:, :n_gate].T, bias_c, bias_r, b=b, s=s, width=width,
                                                dirs=(0, 1), side=late_w)

        n_rows = n_lat if last else n_lat + n_ctx
        pm_lat = _pool(zp, n_seq=b, seq_len=s, row_block0=0, grid_rows=s // GRID_W)
        pm_ctx = pm_lat if last else _pool(zp, n_seq=b, seq_len=lc, row_block0=n_lat // lc, grid_rows=None)

        w_pool_pb = _fold_pool(w_pool[l], pool_scale[l].reshape(1, pw), w_pb[l])
        h = _mixout(hf, hb, zm, pm_lat, pm_ctx, src, mod3, head_gain[l].reshape(1, width), w_pa16, w_pool_pb,
                    w_out16, mod_idx(tm), tm=tm, n_rows=n_rows)

        if l % 2 == 0:
            side = ()
            if not last:
                side = tuple(w[(l + 1) // 2].reshape(-1, w.shape[-1]) for w in (w_exp_gate, w_exp_up, w_exp_down))
            h, *exp16 = _ffn(h, g_ffn[l].reshape(1, d), mod3, *ff16, mod_idx(tm_ffn), tm=tm_ffn,
                             tf=_pick_tile(w_ff_gate.shape[2], 2816), side=side)
            if last:
                h = _final_norm(h, g_final.reshape(1, d), tm=tm, n_rows=n_lat)
        else:
            if not exp16:
                exp16 = [w[j].astype(BF16) for w in (w_exp_gate, w_exp_up, w_exp_down)]
            wg16, wu16, wd16 = (w16.reshape(w.shape[1:]) for w16, w in zip(exp16, (w_exp_gate, w_exp_up, w_exp_down)))
            h = _moe(h, g_ffn[l].reshape(1, d), mod3, w_router[j], wg16, wu16, wd16, g_final.reshape(1, d),
                     mod_idx(tm_moe), tm=tm_moe, tg=512, n_rows=h.shape[0], final=last)
            exp16 = []
        src = (h, h, n_lat)
    return h[:n_lat].reshape(b, s, d)
```

```python
import functools

import jax
import jax.numpy as jnp
from jax import lax
from jax.experimental import pallas as pl
from jax.experimental.pallas import tpu as pltpu
from jax.experimental.pallas import tpu_sc as plsc

F32 = jnp.float32
BF16 = jnp.bfloat16
EPS = 1e-6
M_HEADS = 4
GRID_W = 64
POOL_WINDOWS = (2, 4, 8, 16)
POOL_GROUP_DIM = 128
CHUNK = 256
N_MOD = 6
TOP_K = 2
SC_WINDOW = 64
GATE_PAD = 128
VMEM_LIMIT = 56 * 1024 * 1024


def _pick_tile(n, target, quantum=128):
    t = min(n, target) // quantum * quantum
    while n % t:
        t -= quantum
    return t


def _cparams(sem):
    return pltpu.CompilerParams(dimension_semantics=sem, vmem_limit_bytes=VMEM_LIMIT)


def _sigmoid(x):
    return 0.5 * jnp.tanh(0.5 * x) + 0.5


def _silu(x):
    h = 0.5 * x
    return h * jnp.tanh(h) + h


def _log_sigmoid(x):
    return jnp.minimum(x, 0.0) - jnp.log(1.0 + jnp.exp(-jnp.abs(x)))


def _split3(x):
    hi = x.astype(BF16)
    r1 = x - hi.astype(F32)
    mid = r1.astype(BF16)
    lo = (r1 - mid.astype(F32)).astype(BF16)
    return hi, mid, lo


def _dot(a, b):
    return jnp.dot(a, b, preferred_element_type=F32)


def _dot_nt(a, b):
    return lax.dot_general(a, b, (((1,), (1,)), ((), ())), preferred_element_type=F32)


def _dot_tn(a, b):
    return lax.dot_general(a, b, (((0,), (0,)), ((), ())), preferred_element_type=F32)


def _rms_mod(x, g, sc, sh):
    r = lax.rsqrt(jnp.mean(x * x, axis=-1, keepdims=True) + EPS)
    return (x * r) * g * (1.0 + sc) + sh


def _ada_kernel(c_ref, w_ref, b_ref, o_ref):
    c = c_ref[...]
    o_ref[0] = _dot(_silu(c).astype(BF16), w_ref[0].astype(BF16)) + b_ref[0]


def _ada(cc, w_ada, b_ada):
    depth, d, n = w_ada.shape
    tn = 1024
    return pl.pallas_call(
        _ada_kernel,
        out_shape=jax.ShapeDtypeStruct((depth, cc.shape[0], n), F32),
        grid=(depth, n // tn),
        in_specs=[pl.BlockSpec(cc.shape, lambda l, j: (0, 0)),
                  pl.BlockSpec((1, d, tn), lambda l, j: (l, 0, j)),
                  pl.BlockSpec((1, 1, tn), lambda l, j: (l, 0, j))],
        out_specs=pl.BlockSpec((1, cc.shape[0], tn), lambda l, j: (l, 0, j)),
        compiler_params=_cparams(("arbitrary", "arbitrary")),
        name="ada",
    )(cc, w_ada, b_ada.reshape(depth, 1, n))


def _inproj_kernel(xa_ref, xb_ref, xpa_ref, xpb_ref, xna_ref, xnb_ref, g_ref, sc_ref, sh_ref, w_ref, cw_ref,
                   zm_ref, zp_ref, zg_ref, u_ref,
                   *, n_main, n_pool, n_qk, cw, k_scale, n_lat_tiles, lat_len, ctx_len):
    i = pl.program_id(0)
    tm = xa_ref.shape[0]
    is_ctx = i >= n_lat_tiles
    g, sc, sh = g_ref[...], sc_ref[0], sh_ref[0]
    u_ref[0:tm, :] = _rms_mod(jnp.where(is_ctx, xb_ref[...], xa_ref[...]), g, sc, sh).astype(BF16)
    u_ref[tm:tm + 8, :] = _rms_mod(jnp.where(is_ctx, xpb_ref[...], xpa_ref[...]), g, sc, sh).astype(BF16)
    u_ref[tm + 8:tm + 16, :] = _rms_mod(jnp.where(is_ctx, xnb_ref[...], xna_ref[...]), g, sc, sh).astype(BF16)

    pos0 = lax.rem(i * tm, lat_len)
    lat_first_row = jnp.where(pos0 == 0, 0, -1)
    lat_last_row = jnp.where(pos0 + tm == lat_len, tm - 1, -1)
    sub = 16
    srow = lax.broadcasted_iota(jnp.int32, (sub, cw), 0)
    slabs = sorted({0, tm - sub} | {k * ctx_len - sub for k in range(1, tm // ctx_len)}
                   | {k * ctx_len for k in range(1, tm // ctx_len)})

    for c in range(n_qk // cw):
        cs = slice(c * cw, (c + 1) * cw)
        w0, w1, w2 = cw_ref[0:1, cs], cw_ref[1:2, cs], cw_ref[2:3, cs]
        scale = k_scale if c * cw >= n_qk // 2 else 1.0
        zz = _dot(u_ref[...], w_ref[:, cs])
        z, zh = zz[0:tm, :], zz[tm:tm + 16, :]
        y = _silu(pltpu.roll(z, 1, 0) * w0 + z * w1 + pltpu.roll(z, tm - 1, 0) * w2) * scale
        zm_ref[:, cs] = y.astype(BF16)
        for r0 in slabs:
            zs = z[r0:r0 + sub, :]
            above = zh[7:8, :] if r0 == 0 else z[r0 - 1:r0, :]
            below = zh[8:9, :] if r0 + sub == tm else z[r0 + sub:r0 + sub + 1, :]
            grow = srow + r0
            in_ctx = grow & (ctx_len - 1)
            first = jnp.where(is_ctx, in_ctx, grow - lat_first_row) == 0
            last = jnp.where(is_ctx, in_ctx - (ctx_len - 1), grow - lat_last_row) == 0
            zm1 = jnp.where(srow == 0, above, pltpu.roll(zs, 1, 0))
            zm1 = jnp.where(first, 0.0, zm1)
            zp1 = jnp.where(srow == sub - 1, below, pltpu.roll(zs, sub - 1, 0))
            zp1 = jnp.where(last, 0.0, zp1)
            ys = _silu(zm1 * w0 + zs * w1 + zp1 * w2) * scale
            zm_ref[r0:r0 + sub, cs] = ys.astype(BF16)
    for c in range(n_qk // cw, n_main // cw):
        zm_ref[:, c * cw:(c + 1) * cw] = _dot(u_ref[0:tm, :], w_ref[:, c * cw:(c + 1) * cw]).astype(BF16)
    for c0 in range(0, n_pool, cw):
        c1 = min(c0 + cw, n_pool)
        zp_ref[:, c0:c1] = _dot(u_ref[0:tm, :], w_ref[:, n_main + c0:n_main + c1])
    zg_ref[...] = _dot(u_ref[0:tm, :], w_ref[:, n_main + n_pool:])


def _inproj(src, g, mod3, w_cat, conv_w, mod_idx, *, tm, n_main, n_pool, n_qk, k_scale, n_lat, lat_len, ctx_len):
    a, b, b_off = src
    d = a.shape[1]
    r = n_lat + b.shape[0] - b_off
    ncols = w_cat.shape[1]
    halo = 8
    assert tm % ctx_len == 0 and lat_len % tm == 0 and ctx_len & (ctx_len - 1) == 0 and b_off % tm == 0
    na_t, hb = n_lat // tm, tm // halo
    a_last, b_last = a.shape[0] // halo - 1, b.shape[0] // halo - 1
    kern = functools.partial(_inproj_kernel, n_main=n_main, n_pool=n_pool, n_qk=n_qk, cw=512, k_scale=k_scale,
                             n_lat_tiles=na_t, lat_len=lat_len, ctx_len=ctx_len)
    return pl.pallas_call(
        kern,
        out_shape=(jax.ShapeDtypeStruct((r, n_main), BF16),
                   jax.ShapeDtypeStruct((r, n_pool), F32),
                   jax.ShapeDtypeStruct((r, GATE_PAD), F32)),
        grid=(r // tm,),
        in_specs=[pl.BlockSpec((tm, d), lambda i: (jnp.minimum(i, na_t - 1), 0)),
                  pl.BlockSpec((tm, d), lambda i: (jnp.maximum(i - na_t, 0) + b_off // tm, 0)),
                  pl.BlockSpec((halo, d), lambda i: (jnp.clip(i * hb - 1, 0, a_last), 0)),
                  pl.BlockSpec((halo, d), lambda i: (jnp.clip(b_off // halo + (i - na_t) * hb - 1, 0, b_last), 0)),
                  pl.BlockSpec((halo, d), lambda i: (jnp.clip((i + 1) * hb, 0, a_last), 0)),
                  pl.BlockSpec((halo, d), lambda i: (jnp.clip(b_off // halo + (i - na_t + 1) * hb, 0, b_last), 0)),
                  pl.BlockSpec((1, d), lambda i: (0, 0)),
                  pl.BlockSpec((1, 1, d), lambda i: (mod_idx(i) * N_MOD + 1, 0, 0)),
                  pl.BlockSpec((1, 1, d), lambda i: (mod_idx(i) * N_MOD + 0, 0, 0)),
                  pl.BlockSpec((d, ncols), lambda i: (0, 0)),
                  pl.BlockSpec((3, n_qk), lambda i: (0, 0))],
        out_specs=(pl.BlockSpec((tm, n_main), lambda i: (i, 0)),
                   pl.BlockSpec((tm, n_pool), lambda i: (i, 0)),
                   pl.BlockSpec((tm, GATE_PAD), lambda i: (i, 0))),
        scratch_shapes=[pltpu.VMEM((tm + 2 * halo, d), BF16)],
        compiler_params=_cparams(("arbitrary",)),
        name="inproj",
    )(a, b, a, b, a, b, g, mod3, mod3, w_cat, conv_w)


def _running_max_rows(x, rev):
    n = x.shape[0]
    row = lax.broadcasted_iota(jnp.int32, x.shape, 0)
    sh = 1
    while sh < n:
        if rev:
            shifted = jnp.where(row < n - sh, pltpu.roll(x, n - sh, 0), -jnp.inf)
        else:
            shifted = jnp.where(row >= sh, pltpu.roll(x, sh, 0), -jnp.inf)
        x = jnp.maximum(x, shifted)
        sh *= 2
    return x


def _mlstm_kernel(*refs, dh, dirs):
    L = CHUNK
    nd = len(dirs)
    bias_c, bias_r = refs[5 * nd:5 * nd + 2]
    outs = refs[5 * nd + 2:6 * nd + 2]
    c_ref, c16_ref, n_ref, n16_ref, m_ref = refs[6 * nd + 2:]

    @pl.when(pl.program_id(1) == 0)
    def _():
        c_ref[...] = jnp.zeros_like(c_ref)
        c16_ref[...] = jnp.zeros_like(c16_ref)
        n_ref[...] = jnp.zeros_like(n_ref)
        n16_ref[...] = jnp.zeros_like(n16_ref)
        m_ref[...] = jnp.zeros_like(m_ref)

    ri = lax.broadcasted_iota(jnp.int32, (L, L), 0)
    ci = lax.broadcasted_iota(jnp.int32, (L, L), 1)
    lower = ci <= ri
    upper = ci >= ri
    lower_b = jnp.where(lower, 1.0, 0.0).astype(BF16)
    upper_b = jnp.where(upper, 1.0, 0.0).astype(BF16)

    for di, d in enumerate(dirs):
        q_ref, k_ref, v_ref, gc_ref, gr_ref = refs[5 * di:5 * di + 5]
        out_ref = outs[di]
        rev = d == 1
        gc = gc_ref[...] + bias_c[...]
        gr = gr_ref[...] + bias_r[...]
        fc = _log_sigmoid(gc)
        fr = _log_sigmoid(gr)
        t_col = upper_b if rev else lower_b
        t_row = lower_b if rev else upper_b
        bc3 = _dot(t_col, jnp.concatenate(_split3(fc), axis=1))
        bcum_c = bc3[:, :GATE_PAD] + bc3[:, GATE_PAD:2 * GATE_PAD] + bc3[:, 2 * GATE_PAD:]
        br3 = _dot(jnp.concatenate(_split3(fr), axis=0), t_row)
        bcum_r = br3[0:16, :] + br3[16:32, :] + br3[32:48, :]
        mask = upper if rev else lower
        end = 0 if rev else L - 1

        b_al = pltpu.roll(bcum_c, GATE_PAD - M_HEADS, axis=1)
        g_col = gc - b_al
        m_st = m_ref[di:di + 1, :]
        mx = jnp.maximum(m_st, _running_max_rows(g_col, rev))
        e_neg_all = jnp.exp(-(b_al + mx))
        mx_end = mx[end:end + 1, :]
        wk_all = jnp.exp(g_col - mx_end)
        wc_all = jnp.exp(m_st - mx_end)
        m_ref[di:di + 1, :] = b_al[end:end + 1, :] + mx_end

        ones_rep = jnp.ones((L, GATE_PAD), BF16)
        for h in range(M_HEADS):
            s_idx = di * M_HEADS + h
            col = d * 2 * M_HEADS + h

            def rep(x_all):
                return jnp.broadcast_to(x_all[:, col:col + 1], (x_all.shape[0], GATE_PAD))

            def wide(x_rep):
                return jnp.concatenate([x_rep] * (dh // GATE_PAD), axis=1)

            mx_rep = rep(mx)
            g_row = gr[col:col + 1, :] - bcum_r[col + M_HEADS:col + M_HEADS + 1, :]
            p = jnp.exp(jnp.where(mask, g_row - wide(mx_rep), -jnp.inf))
            w_inter = jnp.exp(rep(m_st) - mx_rep)
            e_neg = rep(e_neg_all)
            wk16 = rep(wk_all).astype(BF16)
            wc = wc_all[:, col:col + 1]

            sl = slice(h * dh, (h + 1) * dh)
            q = q_ref[:, sl]
            k = k_ref[:, sl]
            v = v_ref[:, sl]
            scores = (_dot_nt(q, k) * p).astype(BF16)
            num = wide(w_inter) * _dot(q, c16_ref[s_idx]) + _dot(scores, v)
            den = w_inter * _dot(q, n16_ref[s_idx]) + _dot(scores, ones_rep)
            inv = 1.0 / jnp.maximum(jnp.abs(den), e_neg)
            out_ref[:, sl] = (num * wide(inv)).astype(BF16)

            upd = _dot_tn(k, jnp.concatenate([v * wide(wk16), wk16], axis=1))
            c_new = wc * c_ref[s_idx] + upd[:, :dh]
            c_ref[s_idx] = c_new
            c16_ref[s_idx] = c_new.astype(BF16)
            n_new = wc * n_ref[s_idx] + upd[:, dh:]
            n_ref[s_idx] = n_new
            n16_ref[s_idx] = n_new.astype(BF16)


def _mlstm(zm, zg, zg_t, bias_c, bias_r, *, b, s, width, dirs):
    r = zm.shape[0]
    L = CHUNK
    n_lat = s // L
    lat_blocks = b * n_lat
    nd = len(dirs)

    def blk_f(bi, i):
        return jnp.where(i == 0, lat_blocks + bi, bi * n_lat + i - 1)

    def blk_b(bi, i):
        return jnp.where(i == 0, lat_blocks + bi, bi * n_lat + n_lat - i)

    blks = [blk_b if d == 1 else blk_f for d in dirs]

    def specs(blk):
        return [pl.BlockSpec((L, width), lambda bi, i: (blk(bi, i), 0)),
                pl.BlockSpec((L, width), lambda bi, i: (blk(bi, i), 1)),
                pl.BlockSpec((L, width), lambda bi, i: (blk(bi, i), 2)),
                pl.BlockSpec((L, GATE_PAD), lambda bi, i: (blk(bi, i), 0)),
                pl.BlockSpec((16, L), lambda bi, i: (0, blk(bi, i)))]

    dh = width // M_HEADS
    kern = functools.partial(_mlstm_kernel, dh=dh, dirs=tuple(dirs))
    out = jax.ShapeDtypeStruct((r, width), BF16)
    in_specs = []
    for blk in blks:
        in_specs += specs(blk)
    in_specs += [pl.BlockSpec((1, GATE_PAD), lambda bi, i: (0, 0)), pl.BlockSpec((16, L), lambda bi, i: (0, 0))]
    return pl.pallas_call(
        kern,
        out_shape=(out,) * nd,
        grid=(b, n_lat + 1),
        in_specs=in_specs,
        out_specs=tuple(pl.BlockSpec((L, width), functools.partial(lambda blk, bi, i: (blk(bi, i), 0), blk))
                        for blk in blks),
        scratch_shapes=[pltpu.VMEM((nd * M_HEADS, dh, dh), F32),
                        pltpu.VMEM((nd * M_HEADS, dh, dh), BF16),
                        pltpu.VMEM((nd * M_HEADS, dh, GATE_PAD), F32),
                        pltpu.VMEM((nd * M_HEADS, dh, GATE_PAD), BF16),
                        pltpu.VMEM((8, GATE_PAD), F32)],
        compiler_params=_cparams(("arbitrary", "arbitrary")),
        name="mlstm",
    )(*((zm, zm, zm, zg, zg_t) * nd), bias_c, bias_r)


POOL_BLOCK = 256


def _band(n, w, seg):
    r = lax.broadcasted_iota(jnp.int32, (n, n), 0)
    c = lax.broadcasted_iota(jnp.int32, (n, n), 1)
    d = c - r
    shift = seg.bit_length() - 1
    ok = (d >= -(w // 2)) & (d <= w // 2 - 1) & ((r >> shift) == (c >> shift))
    return jnp.where(ok, 1.0, 0.0).astype(BF16)


def _window_count(idx, n, w):
    return (jnp.minimum(idx + w // 2, n) - jnp.maximum(idx - w // 2, 0)).astype(F32)


def _pool_kernel(z_ref, o_ref, pad_ref, *, grid_rows):
    n = z_ref.shape[0]
    blk = min(POOL_BLOCK, n)
    seg = blk if grid_rows is None else GRID_W
    seg_len = n if grid_rows is None else GRID_W
    halo = (max(POOL_WINDOWS) // 2) * GRID_W
    rowi = lax.broadcasted_iota(jnp.int32, (blk, POOL_GROUP_DIM), 0)
    if grid_rows is not None:
        zeros = jnp.zeros((halo, POOL_GROUP_DIM), F32)
        pad_ref[0:halo, :] = zeros
        pad_ref[halo + n:, :] = zeros
    for g, w in enumerate(POOL_WINDOWS):
        sl = slice(g * POOL_GROUP_DIM, (g + 1) * POOL_GROUP_DIM)
        band = _band(blk, w, seg)
        inv_w = 1.0 / _window_count(rowi & (seg - 1), seg_len, w)
        for b in range(n // blk):
            rows = slice(b * blk, (b + 1) * blk)
            x = z_ref[rows, sl]
            hi = x.astype(BF16)
            lo = (x - hi.astype(F32)).astype(BF16)
            y2 = _dot(band, jnp.concatenate([hi, lo], axis=1))
            y = (y2[:, :POOL_GROUP_DIM] + y2[:, POOL_GROUP_DIM:]) * inv_w
            if grid_rows is None:
                o_ref[rows, sl] = (y - x).astype(BF16)
            else:
                pad_ref[halo + b * blk:halo + (b + 1) * blk, :] = y
        if grid_rows is not None:
            shift = GRID_W.bit_length() - 1
            for b in range(n // blk):
                acc = None
                for d in range(-(w // 2), w // 2):
                    lo_r = halo + b * blk + d * GRID_W
                    t = pad_ref[lo_r:lo_r + blk, :]
                    acc = t if acc is None else acc + t
                inv_h = 1.0 / _window_count((rowi + b * blk) >> shift, grid_rows, w)
                rows = slice(b * blk, (b + 1) * blk)
                o_ref[rows, sl] = (acc * inv_h - z_ref[rows, sl]).astype(BF16)


def _pool(zp, *, n_seq, seq_len, row_block0, grid_rows):
    width = zp.shape[1]
    assert seq_len % POOL_BLOCK == 0 if grid_rows is not None else seq_len <= POOL_BLOCK
    kern = functools.partial(_pool_kernel, grid_rows=grid_rows)
    halo = (max(POOL_WINDOWS) // 2) * GRID_W
    return pl.pallas_call(
        kern,
        out_shape=jax.ShapeDtypeStruct((n_seq * seq_len, width), BF16),
        grid=(n_seq,),
        in_specs=[pl.BlockSpec((seq_len, width), lambda s: (row_block0 + s, 0))],
        out_specs=pl.BlockSpec((seq_len, width), lambda s: (s, 0)),
        scratch_shapes=[pltpu.VMEM((seq_len + 2 * halo, POOL_GROUP_DIM), F32)],
        compiler_params=_cparams(("arbitrary",)),
        name="pool",
    )(zp)


def _fold_pool_kernel(wp_ref, ps_ref, wpb_ref, o_ref):
    for g in range(len(POOL_WINDOWS)):
        sl = slice(g * POOL_GROUP_DIM, (g + 1) * POOL_GROUP_DIM)
        a_hi, a_mid, _ = _split3(wp_ref[g] * ps_ref[:, sl])
        b_hi, b_mid, _ = _split3(wpb_ref[sl, :])
        o_ref[sl, :] = (_dot(a_hi, b_hi) + _dot(a_hi, b_mid) + _dot(a_mid, b_hi)).astype(BF16)


def _fold_pool(w_pool, pool_scale, w_pb):
    return pl.pallas_call(
        _fold_pool_kernel,
        out_shape=jax.ShapeDtypeStruct(w_pb.shape, BF16),
        compiler_params=pltpu.CompilerParams(vmem_limit_bytes=VMEM_LIMIT),
        name="fold_pool",
    )(w_pool, pool_scale, w_pb)


def _mixout_kernel(hf_ref, hb_ref, o_ref, ga_ref, gb_ref, pml_ref, pmc_ref, xa_ref, xb_ref, gt_ref, hg_ref,
                   wpa_ref, wpb_ref, wout_ref, out_ref, *, dh, n_lat_tiles):
    is_ctx = pl.program_id(0) >= n_lat_tiles
    hm = hf_ref[...].astype(F32) + hb_ref[...].astype(F32)
    parts = []
    for h in range(M_HEADS):
        t = hm[:, h * dh:(h + 1) * dh]
        mu = jnp.mean(t, axis=-1, keepdims=True)
        tc = t - mu
        var = jnp.mean(tc * tc, axis=-1, keepdims=True)
        parts.append(tc * lax.rsqrt(var + EPS))
    hn = jnp.concatenate(parts, axis=-1)
    hn = _sigmoid(o_ref[...].astype(F32)) * (hn * hg_ref[...])
    a = _dot(hn.astype(BF16), wpa_ref[...])
    bmat = _dot(jnp.where(is_ctx, pmc_ref[...], pml_ref[...]), wpb_ref[...])
    merged = _sigmoid(ga_ref[...].astype(F32)) * a + _sigmoid(gb_ref[...].astype(F32)) * bmat
    y = _dot(merged.astype(BF16), wout_ref[...])
    out_ref[...] = jnp.where(is_ctx, xb_ref[...], xa_ref[...]) + gt_ref[0] * y


def _mixout(hf, hb, zm, pm_lat, pm_ctx, src, mod3, head_gain, w_pa, w_pb, w_out, mod_idx, *, tm, n_rows):
    xa, xb, b_off = src
    d = xa.shape[1]
    width = hf.shape[1]
    pw = pm_lat.shape[1]
    n_lat_tiles = pm_lat.shape[0] // tm
    o_blk = 3
    full = lambda shape: pl.BlockSpec(shape, lambda i: (0,) * len(shape))
    kern = functools.partial(_mixout_kernel, dh=width // M_HEADS, n_lat_tiles=n_lat_tiles)
    return pl.pallas_call(
        kern,
        out_shape=jax.ShapeDtypeStruct((n_rows, d), F32),
        grid=(n_rows // tm,),
        in_specs=[pl.BlockSpec((tm, width), lambda i: (i, 0)),
                  pl.BlockSpec((tm, width), lambda i: (i, 0)),
                  pl.BlockSpec((tm, width), lambda i: (i, o_blk)),
                  pl.BlockSpec((tm, d), lambda i: (i, o_blk + 1)),
                  pl.BlockSpec((tm, d), lambda i: (i, o_blk + 2)),
                  pl.BlockSpec((tm, pw), lambda i: (jnp.minimum(i, n_lat_tiles - 1), 0)),
                  pl.BlockSpec((tm, pw), lambda i: (jnp.maximum(i - n_lat_tiles, 0), 0)),
                  pl.BlockSpec((tm, d), lambda i: (jnp.minimum(i, n_lat_tiles - 1), 0)),
                  pl.BlockSpec((tm, d), lambda i: (jnp.maximum(i - n_lat_tiles, 0) + b_off // tm, 0)),
                  pl.BlockSpec((1, 1, d), lambda i: (mod_idx(i) * N_MOD + 2, 0, 0)),
                  full((1, width)), full(w_pa.shape), full(w_pb.shape), full(w_out.shape)],
        out_specs=pl.BlockSpec((tm, d), lambda i: (i, 0)),
        compiler_params=_cparams(("arbitrary",)),
        name="mixout",
    )(hf, hb, zm, zm, zm, pm_lat, pm_ctx, xa, xb, mod3, head_gain, w_pa, w_pb, w_out)


SWIGLU_SUB = 512


def _swiglu_partial(u_ref, wg, wu, wd):
    tf = wg.shape[1]
    acc = None
    for c0 in range(0, tf, SWIGLU_SUB):
        cs = slice(c0, min(c0 + SWIGLU_SUB, tf))
        u = u_ref[...]
        act = (_silu(_dot(u, wg[:, cs])) * _dot(u, wu[:, cs])).astype(BF16)
        part = _dot(act, wd[cs, :])
        acc = part if acc is None else acc + part
    return acc


def _ffn_kernel(x_ref, g_ref, sc_ref, sh_ref, gt_ref, wg_ref, wu_ref, wd_ref, *rest, n_side):
    side_in, out_ref, side_out, u_ref = rest[:n_side], rest[n_side], rest[n_side + 1:2 * n_side + 1], rest[-1]
    j = pl.program_id(1)
    u_ref[...] = _rms_mod(x_ref[...], g_ref[...], sc_ref[0], sh_ref[0]).astype(BF16)
    contrib = gt_ref[0] * _swiglu_partial(u_ref, wg_ref, wu_ref, wd_ref)

    @pl.when(j == 0)
    def _():
        out_ref[...] = x_ref[...] + contrib
        for src, dst in zip(side_in, side_out):
            dst[...] = src[...].astype(BF16)

    @pl.when(j > 0)
    def _():
        out_ref[...] += contrib


def _side_rows(n_rows, n_steps):
    rows = (-(-n_rows // n_steps) + 15) // 16 * 16
    while n_rows % rows:
        rows += 16
    return rows


def _ffn(h, g, mod3, wg, wu, wd, mod_idx, *, tm, tf, side=()):
    r, d = h.shape
    f = wg.shape[1]
    n_steps = r // tm
    once = pl.Buffered(1) if f == tf else None
    side_specs_in, side_specs_out, side_shapes = [], [], []
    for a in side:
        rows = _side_rows(a.shape[0], n_steps)
        last = a.shape[0] // rows - 1
        spec = pl.BlockSpec((rows, a.shape[1]), functools.partial(lambda last, i, j: (jnp.minimum(i, last), 0), last))
        side_specs_in.append(spec)
        side_specs_out.append(spec)
        side_shapes.append(jax.ShapeDtypeStruct(a.shape, BF16))
    return pl.pallas_call(
        functools.partial(_ffn_kernel, n_side=len(side)),
        out_shape=(jax.ShapeDtypeStruct((r, d), F32), *side_shapes),
        grid=(n_steps, f // tf),
        in_specs=[pl.BlockSpec((tm, d), lambda i, j: (i, 0)),
                  pl.BlockSpec((1, d), lambda i, j: (0, 0)),
                  pl.BlockSpec((1, 1, d), lambda i, j: (mod_idx(i) * N_MOD + 4, 0, 0)),
                  pl.BlockSpec((1, 1, d), lambda i, j: (mod_idx(i) * N_MOD + 3, 0, 0)),
                  pl.BlockSpec((1, 1, d), lambda i, j: (mod_idx(i) * N_MOD + 5, 0, 0)),
                  pl.BlockSpec((d, tf), lambda i, j: (0, j), pipeline_mode=once),
                  pl.BlockSpec((d, tf), lambda i, j: (0, j), pipeline_mode=once),
                  pl.BlockSpec((tf, d), lambda i, j: (j, 0), pipeline_mode=once),
                  *side_specs_in],
        out_specs=(pl.BlockSpec((tm, d), lambda i, j: (i, 0)), *side_specs_out),
        scratch_shapes=[pltpu.VMEM((tm, d), BF16)],
        compiler_params=_cparams(("arbitrary", "arbitrary")),
        name="ffn",
    )(h, g, mod3, mod3, mod3, wg, wu, wd, *side)


def _route(u, wr_ref, n_exp):
    u_hi = u.astype(BF16)
    u_mid = (u - u_hi.astype(F32)).astype(BF16)
    w_hi, w_mid, _ = _split3(wr_ref[...])
    both = _dot(u_hi, jnp.concatenate([w_hi, w_mid], axis=1))
    logits = both[:, :GATE_PAD] + both[:, GATE_PAD:] + _dot(u_mid, w_hi)
    lane = lax.broadcasted_iota(jnp.int32, logits.shape, 1)
    lg = jnp.where(lane < n_exp, logits, -jnp.inf)
    m1 = jnp.max(lg, axis=1, keepdims=True)
    i1 = jnp.min(jnp.where(lg == m1, lane, 2 * GATE_PAD), axis=1, keepdims=True)
    lg2 = jnp.where(lane == i1, -jnp.inf, lg)
    m2 = jnp.max(lg2, axis=1, keepdims=True)
    i2 = jnp.min(jnp.where(lg2 == m2, lane, 2 * GATE_PAD), axis=1, keepdims=True)
    e = jnp.exp(m2 - m1)
    p1 = 1.0 / (1.0 + e)
    p2 = e * p1
    return i1, i2, p1, p2


LANE_I1, LANE_I2, LANE_P1, LANE_P2, LANE_R1, LANE_R2 = 0, 1, 2, 3, 4, 5


def _pack_bf16_pairs(x):
    k = x.shape[1] // 2
    lo = lax.bitcast_convert_type(x[:, :k].astype(BF16).astype(F32), jnp.uint32)
    hi = lax.bitcast_convert_type(x[:, k:].astype(BF16).astype(F32), jnp.uint32)
    return (lo >> 16) | (hi & jnp.uint32(0xFFFF0000))


def _unpack_bf16_pairs(w):
    lo = lax.bitcast_convert_type(w << 16, F32)
    hi = lax.bitcast_convert_type(w & jnp.uint32(0xFFFF0000), F32)
    return jnp.concatenate([lo, hi], axis=1)


def _route_kernel(x_ref, g_ref, sc_ref, sh_ref, wr_ref, up_ref, info_ref, cnt_ref, carry_ref, *, n_exp):
    @pl.when(pl.program_id(0) == 0)
    def _():
        carry_ref[...] = jnp.zeros_like(carry_ref)

    u = _rms_mod(x_ref[...], g_ref[...], sc_ref[0], sh_ref[0])
    up_ref[...] = _pack_bf16_pairs(u)
    i1, i2, p1, p2 = _route(u, wr_ref, n_exp)
    tm = u.shape[0]
    lane = lax.broadcasted_iota(jnp.int32, (tm, GATE_PAD), 1)
    sel = jnp.where((lane == i1) | (lane == i2), 1.0, 0.0)
    ri = lax.broadcasted_iota(jnp.int32, (tm, tm), 0)
    ci = lax.broadcasted_iota(jnp.int32, (tm, tm), 1)
    before = jnp.where(ci < ri, 1.0, 0.0).astype(BF16)
    rank = carry_ref[0:1, :] + _dot(before, sel.astype(BF16))
    r1 = jnp.sum(jnp.where(lane == i1, rank, 0.0), axis=1, keepdims=True)
    r2 = jnp.sum(jnp.where(lane == i2, rank, 0.0), axis=1, keepdims=True)
    carry_ref[0:1, :] = carry_ref[0:1, :] + jnp.sum(sel, axis=0, keepdims=True)
    cnt_ref[...] = carry_ref[...]
    info = jnp.zeros((tm, GATE_PAD), F32)
    for ln, val in ((LANE_I1, i1.astype(F32)), (LANE_I2, i2.astype(F32)), (LANE_P1, p1), (LANE_P2, p2),
                    (LANE_R1, r1), (LANE_R2, r2)):
        info = jnp.where(lane == ln, val, info)
    info_ref[...] = info


def _route_call(h, g, mod3, w_router, mod_idx, *, tm, n_rows):
    d = h.shape[1]
    n_exp = w_router.shape[1]
    w_r = jnp.concatenate([w_router, jnp.zeros((d, GATE_PAD - n_exp), F32)], axis=1)
    kern = functools.partial(_route_kernel, n_exp=n_exp)
    return pl.pallas_call(
        kern,
        out_shape=(jax.ShapeDtypeStruct((n_rows, d // 2), jnp.uint32),
                   jax.ShapeDtypeStruct((n_rows, GATE_PAD), F32),
                   jax.ShapeDtypeStruct((8, GATE_PAD), F32)),
        grid=(n_rows // tm,),
        in_specs=[pl.BlockSpec((tm, d), lambda i: (i, 0)),
                  pl.BlockSpec((1, d), lambda i: (0, 0)),
                  pl.BlockSpec((1, 1, d), lambda i: (mod_idx(i) * N_MOD + 4, 0, 0)),
                  pl.BlockSpec((1, 1, d), lambda i: (mod_idx(i) * N_MOD + 3, 0, 0)),
                  pl.BlockSpec((d, GATE_PAD), lambda i: (0, 0))],
        out_specs=(pl.BlockSpec((tm, d // 2), lambda i: (i, 0)),
                   pl.BlockSpec((tm, GATE_PAD), lambda i: (i, 0)),
                   pl.BlockSpec((8, GATE_PAD), lambda i: (0, 0))),
        scratch_shapes=[pltpu.VMEM((8, GATE_PAD), F32)],
        compiler_params=_cparams(("arbitrary",)),
        name="route",
    )(h, g, mod3, mod3, w_r)


def _sc_workers():
    info = pltpu.get_tpu_info().sparse_core
    return info.num_cores, info.num_subcores


def _sc_scatter_rows(x, pos, n_out):
    n, dw = x.shape
    nc, ns = _sc_workers()
    nw = nc * ns
    t_per_w = n // nw
    w = min(SC_WINDOW, t_per_w // 2)
    n_chunks = t_per_w // w
    assert n % nw == 0 and t_per_w % (2 * w) == 0
    pos_w = pos.reshape(2, nw, n_chunks, w).transpose(1, 0, 2, 3)
    mesh = plsc.VectorSubcoreMesh(core_axis_name="c", subcore_axis_name="s")

    @functools.partial(pl.kernel, mesh=mesh, out_type=jax.ShapeDtypeStruct((n_out, dw), x.dtype),
                       scratch_types=[pltpu.VMEM((2, n_chunks, w), jnp.int32), pltpu.VMEM((2, w, dw), x.dtype),
                                      pltpu.SemaphoreType.DMA((2,)), pltpu.SemaphoreType.DMA((2,))])
    def scatter(x_hbm, pos_hbm, out_hbm, idx_v, rows_v, gsem, osem):
        wid = lax.axis_index("s") * nc + lax.axis_index("c")
        base = wid * t_per_w
        pltpu.sync_copy(pos_hbm.at[wid], idx_v)

        def get(g, slot):
            return pltpu.make_async_copy(x_hbm.at[pl.ds(base + g * w, w)], rows_v.at[slot], gsem.at[slot])

        def put(g, slot, k):
            return pltpu.make_async_copy(rows_v.at[slot], out_hbm.at[idx_v.at[k, g]], osem.at[slot])

        get(0, 0).start()

        @pl.loop(0, n_chunks, step=2)
        def _(g):
            for b in range(2):
                gg = g + b
                get(gg, b).wait()

                @pl.when(gg + 1 < n_chunks)
                def _():
                    @pl.when(gg >= 1)
                    def _():
                        put(gg - 1, 1 - b, 0).wait()
                        put(gg - 1, 1 - b, 1).wait()
                    get(gg + 1, 1 - b).start()

                put(gg, b, 0).start()
                put(gg, b, 1).start()

        for k in range(2):
            put(n_chunks - 2, 0, k).wait()
            put(n_chunks - 1, 1, k).wait()

    return scatter(x, pos_w)


def _sc_gather_rows(table, idx):
    n = idx.shape[0]
    dw = table.shape[1]
    nc, ns = _sc_workers()
    nw = nc * ns
    b_per_w = n // nw
    w = min(SC_WINDOW, b_per_w // 2)
    n_chunks = b_per_w // w
    assert n % nw == 0 and b_per_w % (2 * w) == 0
    mesh = plsc.VectorSubcoreMesh(core_axis_name="c", subcore_axis_name="s")

    @functools.partial(pl.kernel, mesh=mesh, out_type=jax.ShapeDtypeStruct((n, dw), table.dtype),
                       scratch_types=[pltpu.VMEM((b_per_w,), jnp.int32), pltpu.VMEM((2, w, dw), table.dtype),
                                      pltpu.SemaphoreType.DMA((2,)), pltpu.SemaphoreType.DMA((2,))])
    def gather(table_hbm, idx_hbm, out_hbm, idx_v, rows_v, gsem, osem):
        wid = lax.axis_index("s") * nc + lax.axis_index("c")
        base = wid * b_per_w
        pltpu.sync_copy(idx_hbm.at[pl.ds(base, b_per_w)], idx_v)

        def get(g, slot):
            return pltpu.make_async_copy(table_hbm.at[idx_v.at[pl.ds(g * w, w)]], rows_v.at[slot], gsem.at[slot])

        def put(g, slot):
            return pltpu.make_async_copy(rows_v.at[slot], out_hbm.at[pl.ds(base + g * w, w)], osem.at[slot])

        get(0, 0).start()

        @pl.loop(0, n_chunks, step=2)
        def _(g):
            for b in range(2):
                gg = g + b
                get(gg, b).wait()

                @pl.when(gg + 1 < n_chunks)
                def _():
                    @pl.when(gg >= 1)
                    def _():
                        put(gg - 1, 1 - b).wait()
                    get(gg + 1, 1 - b).start()

                put(gg, b).start()

        put(n_chunks - 2, 0).wait()
        put(n_chunks - 1, 1).wait()

    return gather(table, idx)


def _gmm_kernel(te_ref, nv_ref, xs_ref, wg_ref, wu_ref, wd_ref, ys_ref, u_ref):
    @pl.when(pl.program_id(0) < nv_ref[0])
    def _():
        u_ref[...] = _unpack_bf16_pairs(xs_ref[...]).astype(BF16)
        ys_ref[...] = _pack_bf16_pairs(_swiglu_partial(u_ref, wg_ref.at[0], wu_ref.at[0], wd_ref.at[0]))


def _gmm(xs, tile_expert, n_valid, wg, wu, wd, *, tg):
    p, dw = xs.shape
    d = 2 * dw
    f = wg.shape[2]
    grid_spec = pltpu.PrefetchScalarGridSpec(
        num_scalar_prefetch=2,
        grid=(p // tg,),
        in_specs=[pl.BlockSpec((tg, dw), lambda i, te, nv: (i, 0)),
                  pl.BlockSpec((1, d, f), lambda i, te, nv: (te[i], 0, 0)),
                  pl.BlockSpec((1, d, f), lambda i, te, nv: (te[i], 0, 0)),
                  pl.BlockSpec((1, f, d), lambda i, te, nv: (te[i], 0, 0))],
        out_specs=pl.BlockSpec((tg, dw), lambda i, te, nv: (i, 0)),
        scratch_shapes=[pltpu.VMEM((tg, d), BF16)])
    return pl.pallas_call(
        _gmm_kernel,
        out_shape=jax.ShapeDtypeStruct((p, dw), jnp.uint32),
        grid_spec=grid_spec,
        compiler_params=_cparams(("arbitrary",)),
        name="gmm",
    )(tile_expert, n_valid, xs, wg, wu, wd)


def _combine_kernel(x_ref, info_ref, y1_ref, y2_ref, gt_ref, gf_ref, out_ref, *, final):
    info = info_ref[...]
    p1 = info[:, LANE_P1:LANE_P1 + 1]
    p2 = info[:, LANE_P2:LANE_P2 + 1]
    moe = p1 * _unpack_bf16_pairs(y1_ref[...]) + p2 * _unpack_bf16_pairs(y2_ref[...])
    y = x_ref[...] + gt_ref[0] * moe
    if final:
        r = lax.rsqrt(jnp.mean(y * y, axis=-1, keepdims=True) + EPS)
        y = (y * r) * gf_ref[...]
    out_ref[...] = y


def _combine(h, info, yg, mod3, g_final, mod_idx, *, tm, n_rows, final):
    d = h.shape[1]
    n_t = n_rows // tm
    kern = functools.partial(_combine_kernel, final=final)
    return pl.pallas_call(
        kern,
        out_shape=jax.ShapeDtypeStruct((n_rows, d), F32),
        grid=(n_t,),
        in_specs=[pl.BlockSpec((tm, d), lambda i: (i, 0)),
                  pl.BlockSpec((tm, GATE_PAD), lambda i: (i, 0)),
                  pl.BlockSpec((tm, d // 2), lambda i: (i, 0)),
                  pl.BlockSpec((tm, d // 2), lambda i: (n_t + i, 0)),
                  pl.BlockSpec((1, 1, d), lambda i: (mod_idx(i) * N_MOD + 5, 0, 0)),
                  pl.BlockSpec((1, d), lambda i: (0, 0))],
        out_specs=pl.BlockSpec((tm, d), lambda i: (i, 0)),
        compiler_params=_cparams(("arbitrary",)),
        name="combine",
    )(h, info, yg, yg, mod3, g_final)


def _moe(h, g, mod3, w_router, wg, wu, wd, g_final, mod_idx, *, tm, tg, n_rows, final):
    n_exp = wg.shape[0]
    up, info, cnt = _route_call(h, g, mod3, w_router, mod_idx, tm=tm, n_rows=n_rows)
    i1 = info[:, LANE_I1].astype(jnp.int32)
    i2 = info[:, LANE_I2].astype(jnp.int32)
    r1 = info[:, LANE_R1].astype(jnp.int32)
    r2 = info[:, LANE_R2].astype(jnp.int32)
    counts = cnt[0, :n_exp].astype(jnp.int32)
    padded = (counts + tg - 1) // tg * tg
    ends = jnp.cumsum(padded)
    starts = ends - padded
    eye = jnp.arange(n_exp, dtype=jnp.int32)
    pos1 = jnp.sum(jnp.where(i1[:, None] == eye, starts, 0), axis=1) + r1
    pos2 = jnp.sum(jnp.where(i2[:, None] == eye, starts, 0), axis=1) + r2
    p_rows = -(-(TOP_K * n_rows + n_exp * (tg - 1)) // tg) * tg
    tile_row = jnp.arange(p_rows // tg, dtype=jnp.int32) * tg
    tile_expert = jnp.minimum(jnp.sum(tile_row[:, None] >= ends[None, :], axis=1), n_exp - 1).astype(jnp.int32)
    n_valid = (ends[-1] // tg).reshape(1).astype(jnp.int32)
    xs = _sc_scatter_rows(up, jnp.stack([pos1, pos2]), p_rows)
    ys = _gmm(xs, tile_expert, n_valid, wg, wu, wd, tg=tg)
    yg = _sc_gather_rows(ys, jnp.concatenate([pos1, pos2]))
    return _combine(h, info, yg, mod3, g_final, mod_idx, tm=tm, n_rows=n_rows, final=final)


def _final_norm_kernel(x_ref, g_ref, o_ref):
    x = x_ref[...]
    r = lax.rsqrt(jnp.mean(x * x, axis=-1, keepdims=True) + EPS)
    o_ref[...] = (x * r) * g_ref[...]


def _final_norm(h, g, *, tm, n_rows):
    d = h.shape[1]
    return pl.pallas_call(
        _final_norm_kernel,
        out_shape=jax.ShapeDtypeStruct((n_rows, d), F32),
        grid=(n_rows // tm,),
        in_specs=[pl.BlockSpec((tm, d), lambda i: (i, 0)), pl.BlockSpec((1, d), lambda i: (0, 0))],
        out_specs=pl.BlockSpec((tm, d), lambda i: (i, 0)),
        compiler_params=_cparams(("arbitrary",)),
        name="final_norm",
    )(h, g)


def kernel(x, c, ctx, c_ctx, w_ada, b_ada, g_mix, w_in, conv_qk, b_if, head_gain, w_pool, pool_scale, w_pa, w_pb,
           w_out, g_ffn, w_ff_gate, w_ff_up, w_ff_down, w_router, w_exp_gate, w_exp_up, w_exp_down, g_final):
    b, s, d = x.shape
    lc = ctx.shape[1]
    depth = w_ada.shape[0]
    width = w_pa.shape[1]
    pw = w_pb.shape[1]
    n_gate = b_if.shape[1] * b_if.shape[2] * b_if.shape[3]
    assert lc == CHUNK and s % CHUNK == 0 and s % GRID_W == 0 and width == d and n_gate == 16
    n_lat, n_ctx = b * s, b * lc
    tm, tm_ffn, tm_moe = 512, 512, 1024
    assert all(s % t == 0 and n_ctx % t == 0 for t in (tm, tm_ffn, tm_moe))

    def mod_idx(t):
        return lambda i: jnp.where(i < n_lat // t, i // (s // t), b)

    src = (x.reshape(n_lat, d), ctx.reshape(n_ctx, d), 0)

    n_mod_rows = 16
    cc = jnp.zeros((n_mod_rows, d), F32).at[:b].set(c).at[b].set(c_ctx)
    mod = _ada(cc, w_ada, b_ada)

    if_off, pool_off, ga_off = 4 * width, 4 * width + n_gate, 4 * width + n_gate + pw
    n_main = 4 * width + 2 * d

    exp16 = []
    for l in range(depth):
        last = l == depth - 1
        mod3 = mod[l].reshape(n_mod_rows * N_MOD, 1, d)
        w_l = w_in[l]
        w_cat = jnp.concatenate(
            [w_l[:, :if_off], w_l[:, ga_off:], w_l[:, pool_off:ga_off], w_l[:, if_off:pool_off],
             jnp.zeros((d, GATE_PAD - n_gate), F32)], axis=1).astype(BF16)
        k_scale = float((width // M_HEADS) ** -0.5)
        zm, zp, zg = _inproj(src, g_mix[l].reshape(1, d), mod3, w_cat, conv_qk[l], mod_idx(tm), tm=tm, n_main=n_main,
                             n_pool=pw, n_qk=2 * width, k_scale=k_scale, n_lat=n_lat, lat_len=s, ctx_len=lc)

        bias = b_if[l].reshape(n_gate)
        bias_c = jnp.zeros((1, GATE_PAD), F32).at[0, :n_gate].set(bias)
        bias_r = jnp.broadcast_to(bias[:, None], (n_gate, CHUNK))
        hf, hb = _mlstm(zm, zg, zg[:, :n_gate].T, bias_c, bias_r, b=b, s=s, width=width, dirs=(0, 1))

        n_rows = n_lat if last else n_lat + n_ctx
        pm_lat = _pool(zp, n_seq=b, seq_len=s, row_block0=0, grid_rows=s // GRID_W)
        pm_ctx = pm_lat if last else _pool(zp, n_seq=b, seq_len=lc, row_block0=n_lat // lc, grid_rows=None)

        w_pool_pb = _fold_pool(w_pool[l], pool_scale[l].reshape(1, pw), w_pb[l])
        h = _mixout(hf, hb, zm, pm_lat, pm_ctx, src, mod3, head_gain[l].reshape(1, width), w_pa[l].astype(BF16),
                    w_pool_pb, w_out[l].astype(BF16), mod_idx(tm), tm=tm, n_rows=n_rows)

        j = l // 2
        if l % 2 == 0:
            side = ()
            if not last:
                side = tuple(w[(l + 1) // 2].reshape(-1, w.shape[-1]) for w in (w_exp_gate, w_exp_up, w_exp_down))
            h, *exp16 = _ffn(h, g_ffn[l].reshape(1, d), mod3, w_ff_gate[j].astype(BF16), w_ff_up[j].astype(BF16),
                             w_ff_down[j].astype(BF16), mod_idx(tm_ffn), tm=tm_ffn,
                             tf=_pick_tile(w_ff_gate.shape[2], 2816), side=side)
            if last:
                h = _final_norm(h, g_final.reshape(1, d), tm=tm, n_rows=n_lat)
        else:
            if not exp16:
                exp16 = [w[j].astype(BF16) for w in (w_exp_gate, w_exp_up, w_exp_down)]
            wg16, wu16, wd16 = (w16.reshape(w.shape[1:]) for w16, w in zip(exp16, (w_exp_gate, w_exp_up, w_exp_down)))
            h = _moe(h, g_ffn[l].reshape(1, d), mod3, w_router[j], wg16, wu16, wd16, g_final.reshape(1, d),
                     mod_idx(tm_moe), tm=tm_moe, tg=512, n_rows=h.shape[0], final=last)
            exp16 = []
        src = (h, h, n_lat)
    return h[:n_lat].reshape(b, s, d)
```

```python
import functools

import jax
import jax.numpy as jnp
from jax import lax
from jax.experimental import pallas as pl
from jax.experimental.pallas import tpu as pltpu
from jax.experimental.pallas import tpu_sc as plsc

F32 = jnp.float32
BF16 = jnp.bfloat16
EPS = 1e-6
M_HEADS = 4
GRID_W = 64
POOL_WINDOWS = (2, 4, 8, 16)
POOL_GROUP_DIM = 128
CHUNK = 256
N_MOD = 6
TOP_K = 2
SC_WINDOW = 64
GATE_PAD = 128
VMEM_LIMIT = 56 * 1024 * 1024


def _pick_tile(n, target, quantum=128):
    t = min(n, target) // quantum * quantum
    while n % t:
        t -= quantum
    return t


def _cparams(sem):
    return pltpu.CompilerParams(dimension_semantics=sem, vmem_limit_bytes=VMEM_LIMIT)


def _sigmoid(x):
    return 0.5 * jnp.tanh(0.5 * x) + 0.5


def _silu(x):
    h = 0.5 * x
    return h * jnp.tanh(h) + h


def _log_sigmoid(x):
    return jnp.minimum(x, 0.0) - jnp.log(1.0 + jnp.exp(-jnp.abs(x)))


def _split3(x):
    hi = x.astype(BF16)
    r1 = x - hi.astype(F32)
    mid = r1.astype(BF16)
    lo = (r1 - mid.astype(F32)).astype(BF16)
    return hi, mid, lo


def _dot(a, b):
    return jnp.dot(a, b, preferred_element_type=F32)


def _dot_nt(a, b):
    return lax.dot_general(a, b, (((1,), (1,)), ((), ())), preferred_element_type=F32)


def _dot_tn(a, b):
    return lax.dot_general(a, b, (((0,), (0,)), ((), ())), preferred_element_type=F32)


def _rms_mod(x, g, sc, sh):
    r = lax.rsqrt(jnp.mean(x * x, axis=-1, keepdims=True) + EPS)
    return (x * r) * g * (1.0 + sc) + sh


def _ada_kernel(c_ref, w_ref, b_ref, o_ref):
    c = c_ref[...]
    o_ref[0] = _dot(_silu(c).astype(BF16), w_ref[0].astype(BF16)) + b_ref[0]


def _ada(cc, w_ada, b_ada):
    depth, d, n = w_ada.shape
    tn = 1024
    return pl.pallas_call(
        _ada_kernel,
        out_shape=jax.ShapeDtypeStruct((depth, cc.shape[0], n), F32),
        grid=(depth, n // tn),
        in_specs=[pl.BlockSpec(cc.shape, lambda l, j: (0, 0)),
                  pl.BlockSpec((1, d, tn), lambda l, j: (l, 0, j)),
                  pl.BlockSpec((1, 1, tn), lambda l, j: (l, 0, j))],
        out_specs=pl.BlockSpec((1, cc.shape[0], tn), lambda l, j: (l, 0, j)),
        compiler_params=_cparams(("arbitrary", "arbitrary")),
        name="ada",
    )(cc, w_ada, b_ada.reshape(depth, 1, n))


def _inproj_kernel(xa_ref, xb_ref, xpa_ref, xpb_ref, xna_ref, xnb_ref, g_ref, sc_ref, sh_ref, w_ref, cw_ref,
                   zm_ref, zp_ref, zg_ref, u_ref,
                   *, n_main, n_pool, n_qk, cw, k_scale, n_lat_tiles, lat_len, ctx_len):
    i = pl.program_id(0)
    tm = xa_ref.shape[0]
    is_ctx = i >= n_lat_tiles
    g, sc, sh = g_ref[...], sc_ref[0], sh_ref[0]
    u_ref[0:tm, :] = _rms_mod(jnp.where(is_ctx, xb_ref[...], xa_ref[...]), g, sc, sh).astype(BF16)
    u_ref[tm:tm + 8, :] = _rms_mod(jnp.where(is_ctx, xpb_ref[...], xpa_ref[...]), g, sc, sh).astype(BF16)
    u_ref[tm + 8:tm + 16, :] = _rms_mod(jnp.where(is_ctx, xnb_ref[...], xna_ref[...]), g, sc, sh).astype(BF16)

    pos0 = lax.rem(i * tm, lat_len)
    lat_first_row = jnp.where(pos0 == 0, 0, -1)
    lat_last_row = jnp.where(pos0 + tm == lat_len, tm - 1, -1)
    sub = 16
    srow = lax.broadcasted_iota(jnp.int32, (sub, cw), 0)
    slabs = sorted({0, tm - sub} | {k * ctx_len - sub for k in range(1, tm // ctx_len)}
                   | {k * ctx_len for k in range(1, tm // ctx_len)})

    for c in range(n_qk // cw):
        cs = slice(c * cw, (c + 1) * cw)
        w0, w1, w2 = cw_ref[0:1, cs], cw_ref[1:2, cs], cw_ref[2:3, cs]
        scale = k_scale if c * cw >= n_qk // 2 else 1.0
        zz = _dot(u_ref[...], w_ref[:, cs])
        z, zh = zz[0:tm, :], zz[tm:tm + 16, :]
        y = _silu(pltpu.roll(z, 1, 0) * w0 + z * w1 + pltpu.roll(z, tm - 1, 0) * w2) * scale
        zm_ref[:, cs] = y.astype(BF16)
        for r0 in slabs:
            zs = z[r0:r0 + sub, :]
            above = zh[7:8, :] if r0 == 0 else z[r0 - 1:r0, :]
            below = zh[8:9, :] if r0 + sub == tm else z[r0 + sub:r0 + sub + 1, :]
            grow = srow + r0
            in_ctx = grow & (ctx_len - 1)
            first = jnp.where(is_ctx, in_ctx, grow - lat_first_row) == 0
            last = jnp.where(is_ctx, in_ctx - (ctx_len - 1), grow - lat_last_row) == 0
            zm1 = jnp.where(srow == 0, above, pltpu.roll(zs, 1, 0))
            zm1 = jnp.where(first, 0.0, zm1)
            zp1 = jnp.where(srow == sub - 1, below, pltpu.roll(zs, sub - 1, 0))
            zp1 = jnp.where(last, 0.0, zp1)
            ys = _silu(zm1 * w0 + zs * w1 + zp1 * w2) * scale
            zm_ref[r0:r0 + sub, cs] = ys.astype(BF16)
    for c in range(n_qk // cw, n_main // cw):
        zm_ref[:, c * cw:(c + 1) * cw] = _dot(u_ref[0:tm, :], w_ref[:, c * cw:(c + 1) * cw]).astype(BF16)
    for c0 in range(0, n_pool, cw):
        c1 = min(c0 + cw, n_pool)
        zp_ref[:, c0:c1] = _dot(u_ref[0:tm, :], w_ref[:, n_main + c0:n_main + c1])
    zg_ref[...] = _dot(u_ref[0:tm, :], w_ref[:, n_main + n_pool:])


def _inproj(src, g, mod3, w_cat, conv_w, mod_idx, *, tm, n_main, n_pool, n_qk, k_scale, n_lat, lat_len, ctx_len):
    a, b, b_off = src
    d = a.shape[1]
    r = n_lat + b.shape[0] - b_off
    ncols = w_cat.shape[1]
    halo = 8
    assert tm % ctx_len == 0 and lat_len % tm == 0 and ctx_len & (ctx_len - 1) == 0 and b_off % tm == 0
    na_t, hb = n_lat // tm, tm // halo
    a_last, b_last = a.shape[0] // halo - 1, b.shape[0] // halo - 1
    kern = functools.partial(_inproj_kernel, n_main=n_main, n_pool=n_pool, n_qk=n_qk, cw=512, k_scale=k_scale,
                             n_lat_tiles=na_t, lat_len=lat_len, ctx_len=ctx_len)
    return pl.pallas_call(
        kern,
        out_shape=(jax.ShapeDtypeStruct((r, n_main), BF16),
                   jax.ShapeDtypeStruct((r, n_pool), F32),
                   jax.ShapeDtypeStruct((r, GATE_PAD), F32)),
        grid=(r // tm,),
        in_specs=[pl.BlockSpec((tm, d), lambda i: (jnp.minimum(i, na_t - 1), 0)),
                  pl.BlockSpec((tm, d), lambda i: (jnp.maximum(i - na_t, 0) + b_off // tm, 0)),
                  pl.BlockSpec((halo, d), lambda i: (jnp.clip(i * hb - 1, 0, a_last), 0)),
                  pl.BlockSpec((halo, d), lambda i: (jnp.clip(b_off // halo + (i - na_t) * hb - 1, 0, b_last), 0)),
                  pl.BlockSpec((halo, d), lambda i: (jnp.clip((i + 1) * hb, 0, a_last), 0)),
                  pl.BlockSpec((halo, d), lambda i: (jnp.clip(b_off // halo + (i - na_t + 1) * hb, 0, b_last), 0)),
                  pl.BlockSpec((1, d), lambda i: (0, 0)),
                  pl.BlockSpec((1, 1, d), lambda i: (mod_idx(i) * N_MOD + 1, 0, 0)),
                  pl.BlockSpec((1, 1, d), lambda i: (mod_idx(i) * N_MOD + 0, 0, 0)),
                  pl.BlockSpec((d, ncols), lambda i: (0, 0)),
                  pl.BlockSpec((3, n_qk), lambda i: (0, 0))],
        out_specs=(pl.BlockSpec((tm, n_main), lambda i: (i, 0)),
                   pl.BlockSpec((tm, n_pool), lambda i: (i, 0)),
                   pl.BlockSpec((tm, GATE_PAD), lambda i: (i, 0))),
        scratch_shapes=[pltpu.VMEM((tm + 2 * halo, d), BF16)],
        compiler_params=_cparams(("arbitrary",)),
        name="inproj",
    )(a, b, a, b, a, b, g, mod3, mod3, w_cat, conv_w)


def _running_max_rows(x, rev):
    n = x.shape[0]
    row = lax.broadcasted_iota(jnp.int32, x.shape, 0)
    sh = 1
    while sh < n:
        if rev:
            shifted = jnp.where(row < n - sh, pltpu.roll(x, n - sh, 0), -jnp.inf)
        else:
            shifted = jnp.where(row >= sh, pltpu.roll(x, sh, 0), -jnp.inf)
        x = jnp.maximum(x, shifted)
        sh *= 2
    return x


def _mlstm_kernel(*refs, dh, dirs):
    L = CHUNK
    nd = len(dirs)
    bias_c, bias_r = refs[5 * nd:5 * nd + 2]
    outs = refs[5 * nd + 2:6 * nd + 2]
    c_ref, c16_ref, n_ref, n16_ref, m_ref = refs[6 * nd + 2:]

    @pl.when(pl.program_id(1) == 0)
    def _():
        c_ref[...] = jnp.zeros_like(c_ref)
        c16_ref[...] = jnp.zeros_like(c16_ref)
        n_ref[...] = jnp.zeros_like(n_ref)
        n16_ref[...] = jnp.zeros_like(n16_ref)
        m_ref[...] = jnp.zeros_like(m_ref)

    ri = lax.broadcasted_iota(jnp.int32, (L, L), 0)
    ci = lax.broadcasted_iota(jnp.int32, (L, L), 1)
    lower = ci <= ri
    upper = ci >= ri
    lower_b = jnp.where(lower, 1.0, 0.0).astype(BF16)
    upper_b = jnp.where(upper, 1.0, 0.0).astype(BF16)

    for di, d in enumerate(dirs):
        q_ref, k_ref, v_ref, gc_ref, gr_ref = refs[5 * di:5 * di + 5]
        out_ref = outs[di]
        rev = d == 1
        gc = gc_ref[...] + bias_c[...]
        gr = gr_ref[...] + bias_r[...]
        fc = _log_sigmoid(gc)
        fr = _log_sigmoid(gr)
        t_col = upper_b if rev else lower_b
        t_row = lower_b if rev else upper_b
        bc3 = _dot(t_col, jnp.concatenate(_split3(fc), axis=1))
        bcum_c = bc3[:, :GATE_PAD] + bc3[:, GATE_PAD:2 * GATE_PAD] + bc3[:, 2 * GATE_PAD:]
        br3 = _dot(jnp.concatenate(_split3(fr), axis=0), t_row)
        bcum_r = br3[0:16, :] + br3[16:32, :] + br3[32:48, :]
        mask = upper if rev else lower
        end = 0 if rev else L - 1

        b_al = pltpu.roll(bcum_c, GATE_PAD - M_HEADS, axis=1)
        g_col = gc - b_al
        m_st = m_ref[di:di + 1, :]
        mx = jnp.maximum(m_st, _running_max_rows(g_col, rev))
        e_neg_all = jnp.exp(-(b_al + mx))
        mx_end = mx[end:end + 1, :]
        wk_all = jnp.exp(g_col - mx_end)
        wc_all = jnp.exp(m_st - mx_end)
        m_ref[di:di + 1, :] = b_al[end:end + 1, :] + mx_end

        ones_rep = jnp.ones((L, GATE_PAD), BF16)
        for h in range(M_HEADS):
            s_idx = di * M_HEADS + h
            col = d * 2 * M_HEADS + h

            def rep(x_all):
                return jnp.broadcast_to(x_all[:, col:col + 1], (x_all.shape[0], GATE_PAD))

            def wide(x_rep):
                return jnp.concatenate([x_rep] * (dh // GATE_PAD), axis=1)

            mx_rep = rep(mx)
            g_row = gr[col:col + 1, :] - bcum_r[col + M_HEADS:col + M_HEADS + 1, :]
            p = jnp.exp(jnp.where(mask, g_row - wide(mx_rep), -jnp.inf))
            w_inter = jnp.exp(rep(m_st) - mx_rep)
            e_neg = rep(e_neg_all)
            wk16 = rep(wk_all).astype(BF16)
            wc = wc_all[:, col:col + 1]

            sl = slice(h * dh, (h + 1) * dh)
            q = q_ref[:, sl]
            k = k_ref[:, sl]
            v = v_ref[:, sl]
            scores = (_dot_nt(q, k) * p).astype(BF16)
            num = wide(w_inter) * _dot(q, c16_ref[s_idx]) + _dot(scores, v)
            den = w_inter * _dot(q, n16_ref[s_idx]) + _dot(scores, ones_rep)
            inv = 1.0 / jnp.maximum(jnp.abs(den), e_neg)
            out_ref[:, sl] = (num * wide(inv)).astype(BF16)

            upd = _dot_tn(k, jnp.concatenate([v * wide(wk16), wk16], axis=1))
            c_new = wc * c_ref[s_idx] + upd[:, :dh]
            c_ref[s_idx] = c_new
            c16_ref[s_idx] = c_new.astype(BF16)
            n_new = wc * n_ref[s_idx] + upd[:, dh:]
            n_ref[s_idx] = n_new
            n16_ref[s_idx] = n_new.astype(BF16)


def _mlstm(zm, zg, zg_t, bias_c, bias_r, *, b, s, width, dirs):
    r = zm.shape[0]
    L = CHUNK
    n_lat = s // L
    lat_blocks = b * n_lat
    nd = len(dirs)

    def blk_f(bi, i):
        return jnp.where(i == 0, lat_blocks + bi, bi * n_lat + i - 1)

    def blk_b(bi, i):
        return jnp.where(i == 0, lat_blocks + bi, bi * n_lat + n_lat - i)

    blks = [blk_b if d == 1 else blk_f for d in dirs]

    def specs(blk):
        return [pl.BlockSpec((L, width), lambda bi, i: (blk(bi, i), 0)),
                pl.BlockSpec((L, width), lambda bi, i: (blk(bi, i), 1)),
                pl.BlockSpec((L, width), lambda bi, i: (blk(bi, i), 2)),
                pl.BlockSpec((L, GATE_PAD), lambda bi, i: (blk(bi, i), 0)),
                pl.BlockSpec((16, L), lambda bi, i: (0, blk(bi, i)))]

    dh = width // M_HEADS
    kern = functools.partial(_mlstm_kernel, dh=dh, dirs=tuple(dirs))
    out = jax.ShapeDtypeStruct((r, width), BF16)
    in_specs = []
    for blk in blks:
        in_specs += specs(blk)
    in_specs += [pl.BlockSpec((1, GATE_PAD), lambda bi, i: (0, 0)), pl.BlockSpec((16, L), lambda bi, i: (0, 0))]
    return pl.pallas_call(
        kern,
        out_shape=(out,) * nd,
        grid=(b, n_lat + 1),
        in_specs=in_specs,
        out_specs=tuple(pl.BlockSpec((L, width), functools.partial(lambda blk, bi, i: (blk(bi, i), 0), blk))
                        for blk in blks),
        scratch_shapes=[pltpu.VMEM((nd * M_HEADS, dh, dh), F32),
                        pltpu.VMEM((nd * M_HEADS, dh, dh), BF16),
                        pltpu.VMEM((nd * M_HEADS, dh, GATE_PAD), F32),
                        pltpu.VMEM((nd * M_HEADS, dh, GATE_PAD), BF16),
                        pltpu.VMEM((8, GATE_PAD), F32)],
        compiler_params=_cparams(("arbitrary", "arbitrary")),
        name="mlstm",
    )(*((zm, zm, zm, zg, zg_t) * nd), bias_c, bias_r)


POOL_BLOCK = 256


def _band(n, w, seg):
    r = lax.broadcasted_iota(jnp.int32, (n, n), 0)
    c = lax.broadcasted_iota(jnp.int32, (n, n), 1)
    d = c - r
    shift = seg.bit_length() - 1
    ok = (d >= -(w // 2)) & (d <= w // 2 - 1) & ((r >> shift) == (c >> shift))
    return jnp.where(ok, 1.0, 0.0).astype(BF16)


def _window_count(idx, n, w):
    return (jnp.minimum(idx + w // 2, n) - jnp.maximum(idx - w // 2, 0)).astype(F32)


def _pool_kernel(z_ref, o_ref, pad_ref, *, grid_rows):
    n = z_ref.shape[0]
    blk = min(POOL_BLOCK, n)
    seg = blk if grid_rows is None else GRID_W
    seg_len = n if grid_rows is None else GRID_W
    halo = (max(POOL_WINDOWS) // 2) * GRID_W
    rowi = lax.broadcasted_iota(jnp.int32, (blk, POOL_GROUP_DIM), 0)
    if grid_rows is not None:
        zeros = jnp.zeros((halo, POOL_GROUP_DIM), F32)
        pad_ref[0:halo, :] = zeros
        pad_ref[halo + n:, :] = zeros
    for g, w in enumerate(POOL_WINDOWS):
        sl = slice(g * POOL_GROUP_DIM, (g + 1) * POOL_GROUP_DIM)
        band = _band(blk, w, seg)
        inv_w = 1.0 / _window_count(rowi & (seg - 1), seg_len, w)
        for b in range(n // blk):
            rows = slice(b * blk, (b + 1) * blk)
            x = z_ref[rows, sl]
            hi = x.astype(BF16)
            lo = (x - hi.astype(F32)).astype(BF16)
            y2 = _dot(band, jnp.concatenate([hi, lo], axis=1))
            y = (y2[:, :POOL_GROUP_DIM] + y2[:, POOL_GROUP_DIM:]) * inv_w
            if grid_rows is None:
                o_ref[rows, sl] = (y - x).astype(BF16)
            else:
                pad_ref[halo + b * blk:halo + (b + 1) * blk, :] = y
        if grid_rows is not None:
            shift = GRID_W.bit_length() - 1
            for b in range(n // blk):
                acc = None
                for d in range(-(w // 2), w // 2):
                    lo_r = halo + b * blk + d * GRID_W
                    t = pad_ref[lo_r:lo_r + blk, :]
                    acc = t if acc is None else acc + t
                inv_h = 1.0 / _window_count((rowi + b * blk) >> shift, grid_rows, w)
                rows = slice(b * blk, (b + 1) * blk)
                o_ref[rows, sl] = (acc * inv_h - z_ref[rows, sl]).astype(BF16)


def _pool(zp, *, n_seq, seq_len, row_block0, grid_rows):
    width = zp.shape[1]
    assert seq_len % POOL_BLOCK == 0 if grid_rows is not None else seq_len <= POOL_BLOCK
    kern = functools.partial(_pool_kernel, grid_rows=grid_rows)
    halo = (max(POOL_WINDOWS) // 2) * GRID_W
    return pl.pallas_call(
        kern,
        out_shape=jax.ShapeDtypeStruct((n_seq * seq_len, width), BF16),
        grid=(n_seq,),
        in_specs=[pl.BlockSpec((seq_len, width), lambda s: (row_block0 + s, 0))],
        out_specs=pl.BlockSpec((seq_len, width), lambda s: (s, 0)),
        scratch_shapes=[pltpu.VMEM((seq_len + 2 * halo, POOL_GROUP_DIM), F32)],
        compiler_params=_cparams(("arbitrary",)),
        name="pool",
    )(zp)


def _fold_pool_kernel(wp_ref, ps_ref, wpb_ref, o_ref):
    for g in range(len(POOL_WINDOWS)):
        sl = slice(g * POOL_GROUP_DIM, (g + 1) * POOL_GROUP_DIM)
        a_hi, a_mid, _ = _split3(wp_ref[g] * ps_ref[:, sl])
        b_hi, b_mid, _ = _split3(wpb_ref[sl, :])
        o_ref[sl, :] = (_dot(a_hi, b_hi) + _dot(a_hi, b_mid) + _dot(a_mid, b_hi)).astype(BF16)


def _fold_pool(w_pool, pool_scale, w_pb):
    return pl.pallas_call(
        _fold_pool_kernel,
        out_shape=jax.ShapeDtypeStruct(w_pb.shape, BF16),
        compiler_params=pltpu.CompilerParams(vmem_limit_bytes=VMEM_LIMIT),
        name="fold_pool",
    )(w_pool, pool_scale, w_pb)


def _mixout_kernel(hf_ref, hb_ref, o_ref, ga_ref, gb_ref, pml_ref, pmc_ref, xa_ref, xb_ref, gt_ref, hg_ref,
                   wpa_ref, wpb_ref, wout_ref, out_ref, *, dh, n_lat_tiles):
    is_ctx = pl.program_id(0) >= n_lat_tiles
    hm = hf_ref[...].astype(F32) + hb_ref[...].astype(F32)
    parts = []
    for h in range(M_HEADS):
        t = hm[:, h * dh:(h + 1) * dh]
        mu = jnp.mean(t, axis=-1, keepdims=True)
        tc = t - mu
        var = jnp.mean(tc * tc, axis=-1, keepdims=True)
        parts.append(tc * lax.rsqrt(var + EPS))
    hn = jnp.concatenate(parts, axis=-1)
    hn = _sigmoid(o_ref[...].astype(F32)) * (hn * hg_ref[...])
    a = _dot(hn.astype(BF16), wpa_ref[...])
    bmat = _dot(jnp.where(is_ctx, pmc_ref[...], pml_ref[...]), wpb_ref[...])
    merged = _sigmoid(ga_ref[...].astype(F32)) * a + _sigmoid(gb_ref[...].astype(F32)) * bmat
    y = _dot(merged.astype(BF16), wout_ref[...])
    out_ref[...] = jnp.where(is_ctx, xb_ref[...], xa_ref[...]) + gt_ref[0] * y


def _mixout(hf, hb, zm, pm_lat, pm_ctx, src, mod3, head_gain, w_pa, w_pb, w_out, mod_idx, *, tm, n_rows):
    xa, xb, b_off = src
    d = xa.shape[1]
    width = hf.shape[1]
    pw = pm_lat.shape[1]
    n_lat_tiles = pm_lat.shape[0] // tm
    o_blk = 3
    full = lambda shape: pl.BlockSpec(shape, lambda i: (0,) * len(shape))
    kern = functools.partial(_mixout_kernel, dh=width // M_HEADS, n_lat_tiles=n_lat_tiles)
    return pl.pallas_call(
        kern,
        out_shape=jax.ShapeDtypeStruct((n_rows, d), F32),
        grid=(n_rows // tm,),
        in_specs=[pl.BlockSpec((tm, width), lambda i: (i, 0)),
                  pl.BlockSpec((tm, width), lambda i: (i, 0)),
                  pl.BlockSpec((tm, width), lambda i: (i, o_blk)),
                  pl.BlockSpec((tm, d), lambda i: (i, o_blk + 1)),
                  pl.BlockSpec((tm, d), lambda i: (i, o_blk + 2)),
                  pl.BlockSpec((tm, pw), lambda i: (jnp.minimum(i, n_lat_tiles - 1), 0)),
                  pl.BlockSpec((tm, pw), lambda i: (jnp.maximum(i - n_lat_tiles, 0), 0)),
                  pl.BlockSpec((tm, d), lambda i: (jnp.minimum(i, n_lat_tiles - 1), 0)),
                  pl.BlockSpec((tm, d), lambda i: (jnp.maximum(i - n_lat_tiles, 0) + b_off // tm, 0)),
                  pl.BlockSpec((1, 1, d), lambda i: (mod_idx(i) * N_MOD + 2, 0, 0)),
                  full((1, width)), full(w_pa.shape), full(w_pb.shape), full(w_out.shape)],
        out_specs=pl.BlockSpec((tm, d), lambda i: (i, 0)),
        compiler_params=_cparams(("arbitrary",)),
        name="mixout",
    )(hf, hb, zm, zm, zm, pm_lat, pm_ctx, xa, xb, mod3, head_gain, w_pa, w_pb, w_out)


SWIGLU_SUB = 512


def _swiglu_partial(u_ref, wg, wu, wd):
    tf = wg.shape[1]
    acc = None
    for c0 in range(0, tf, SWIGLU_SUB):
        cs = slice(c0, min(c0 + SWIGLU_SUB, tf))
        u = u_ref[...]
        act = (_silu(_dot(u, wg[:, cs])) * _dot(u, wu[:, cs])).astype(BF16)
        part = _dot(act, wd[cs, :])
        acc = part if acc is None else acc + part
    return acc


def _ffn_kernel(x_ref, g_ref, sc_ref, sh_ref, gt_ref, wg_ref, wu_ref, wd_ref, *rest, n_side):
    side_in, out_ref, side_out, u_ref = rest[:n_side], rest[n_side], rest[n_side + 1:2 * n_side + 1], rest[-1]
    j = pl.program_id(1)
    u_ref[...] = _rms_mod(x_ref[...], g_ref[...], sc_ref[0], sh_ref[0]).astype(BF16)
    contrib = gt_ref[0] * _swiglu_partial(u_ref, wg_ref, wu_ref, wd_ref)

    @pl.when(j == 0)
    def _():
        out_ref[...] = x_ref[...] + contrib
        for src, dst in zip(side_in, side_out):
            dst[...] = src[...].astype(BF16)

    @pl.when(j > 0)
    def _():
        out_ref[...] += contrib


def _side_rows(n_rows, n_steps):
    rows = (-(-n_rows // n_steps) + 15) // 16 * 16
    while n_rows % rows:
        rows += 16
    return rows


def _ffn(h, g, mod3, wg, wu, wd, mod_idx, *, tm, tf, side=()):
    r, d = h.shape
    f = wg.shape[1]
    n_steps = r // tm
    once = pl.Buffered(1) if f == tf else None
    side_specs_in, side_specs_out, side_shapes = [], [], []
    for a in side:
        rows = _side_rows(a.shape[0], n_steps)
        last = a.shape[0] // rows - 1
        spec = pl.BlockSpec((rows, a.shape[1]), functools.partial(lambda last, i, j: (jnp.minimum(i, last), 0), last))
        side_specs_in.append(spec)
        side_specs_out.append(spec)
        side_shapes.append(jax.ShapeDtypeStruct(a.shape, BF16))
    return pl.pallas_call(
        functools.partial(_ffn_kernel, n_side=len(side)),
        out_shape=(jax.ShapeDtypeStruct((r, d), F32), *side_shapes),
        grid=(n_steps, f // tf),
        in_specs=[pl.BlockSpec((tm, d), lambda i, j: (i, 0)),
                  pl.BlockSpec((1, d), lambda i, j: (0, 0)),
                  pl.BlockSpec((1, 1, d), lambda i, j: (mod_idx(i) * N_MOD + 4, 0, 0)),
                  pl.BlockSpec((1, 1, d), lambda i, j: (mod_idx(i) * N_MOD + 3, 0, 0)),
                  pl.BlockSpec((1, 1, d), lambda i, j: (mod_idx(i) * N_MOD + 5, 0, 0)),
                  pl.BlockSpec((d, tf), lambda i, j: (0, j), pipeline_mode=once),
                  pl.BlockSpec((d, tf), lambda i, j: (0, j), pipeline_mode=once),
                  pl.BlockSpec((tf, d), lambda i, j: (j, 0), pipeline_mode=once),
                  *side_specs_in],
        out_specs=(pl.BlockSpec((tm, d), lambda i, j: (i, 0)), *side_specs_out),
        scratch_shapes=[pltpu.VMEM((tm, d), BF16)],
        compiler_params=_cparams(("arbitrary", "arbitrary")),
        name="ffn",
    )(h, g, mod3, mod3, mod3, wg, wu, wd, *side)


def _route(u, wr_ref, n_exp):
    u_hi = u.astype(BF16)
    u_mid = (u - u_hi.astype(F32)).astype(BF16)
    w_hi, w_mid, _ = _split3(wr_ref[...])
    tm = u.shape[0]
    both = _dot(jnp.concatenate([u_hi, u_mid], axis=0), jnp.concatenate([w_hi, w_mid], axis=1))
    logits = (both[:tm, :GATE_PAD] + both[:tm, GATE_PAD:]) + (both[tm:, :GATE_PAD] + both[tm:, GATE_PAD:])
    lane = lax.broadcasted_iota(jnp.int32, logits.shape, 1)
    lg = jnp.where(lane < n_exp, logits, -jnp.inf)
    m1 = jnp.max(lg, axis=1, keepdims=True)
    i1 = jnp.min(jnp.where(lg == m1, lane, 2 * GATE_PAD), axis=1, keepdims=True)
    lg2 = jnp.where(lane == i1, -jnp.inf, lg)
    m2 = jnp.max(lg2, axis=1, keepdims=True)
    i2 = jnp.min(jnp.where(lg2 == m2, lane, 2 * GATE_PAD), axis=1, keepdims=True)
    e = jnp.exp(m2 - m1)
    p1 = 1.0 / (1.0 + e)
    p2 = e * p1
    return i1, i2, p1, p2


LANE_I1, LANE_I2, LANE_P1, LANE_P2, LANE_R1, LANE_R2 = 0, 1, 2, 3, 4, 5


def _pack_bf16_pairs(x):
    k = x.shape[1] // 2
    lo = lax.bitcast_convert_type(x[:, :k].astype(BF16).astype(F32), jnp.uint32)
    hi = lax.bitcast_convert_type(x[:, k:].astype(BF16).astype(F32), jnp.uint32)
    return (lo >> 16) | (hi & jnp.uint32(0xFFFF0000))


def _unpack_bf16_pairs(w):
    lo = lax.bitcast_convert_type(w << 16, F32)
    hi = lax.bitcast_convert_type(w & jnp.uint32(0xFFFF0000), F32)
    return jnp.concatenate([lo, hi], axis=1)


def _route_kernel(x_ref, g_ref, sc_ref, sh_ref, wr_ref, up_ref, info_ref, cnt_ref, carry_ref, *, n_exp):
    @pl.when(pl.program_id(0) == 0)
    def _():
        carry_ref[...] = jnp.zeros_like(carry_ref)

    u = _rms_mod(x_ref[...], g_ref[...], sc_ref[0], sh_ref[0])
    up_ref[...] = _pack_bf16_pairs(u)
    i1, i2, p1, p2 = _route(u, wr_ref, n_exp)
    tm = u.shape[0]
    lane = lax.broadcasted_iota(jnp.int32, (tm, GATE_PAD), 1)
    sel = jnp.where((lane == i1) | (lane == i2), 1.0, 0.0)
    ri = lax.broadcasted_iota(jnp.int32, (tm, tm), 0)
    ci = lax.broadcasted_iota(jnp.int32, (tm, tm), 1)
    before = jnp.where(ci < ri, 1.0, 0.0).astype(BF16)
    rank = carry_ref[0:1, :] + _dot(before, sel.astype(BF16))
    r1 = jnp.sum(jnp.where(lane == i1, rank, 0.0), axis=1, keepdims=True)
    r2 = jnp.sum(jnp.where(lane == i2, rank, 0.0), axis=1, keepdims=True)
    carry_ref[0:1, :] = carry_ref[0:1, :] + jnp.sum(sel, axis=0, keepdims=True)
    cnt_ref[...] = carry_ref[...]
    info = jnp.zeros((tm, GATE_PAD), F32)
    for ln, val in ((LANE_I1, i1.astype(F32)), (LANE_I2, i2.astype(F32)), (LANE_P1, p1), (LANE_P2, p2),
                    (LANE_R1, r1), (LANE_R2, r2)):
        info = jnp.where(lane == ln, val, info)
    info_ref[...] = info


def _route_call(h, g, mod3, w_router, mod_idx, *, tm, n_rows):
    d = h.shape[1]
    n_exp = w_router.shape[1]
    w_r = jnp.concatenate([w_router, jnp.zeros((d, GATE_PAD - n_exp), F32)], axis=1)
    kern = functools.partial(_route_kernel, n_exp=n_exp)
    return pl.pallas_call(
        kern,
        out_shape=(jax.ShapeDtypeStruct((n_rows, d // 2), jnp.uint32),
                   jax.ShapeDtypeStruct((n_rows, GATE_PAD), F32),
                   jax.ShapeDtypeStruct((8, GATE_PAD), F32)),
        grid=(n_rows // tm,),
        in_specs=[pl.BlockSpec((tm, d), lambda i: (i, 0)),
                  pl.BlockSpec((1, d), lambda i: (0, 0)),
                  pl.BlockSpec((1, 1, d), lambda i: (mod_idx(i) * N_MOD + 4, 0, 0)),
                  pl.BlockSpec((1, 1, d), lambda i: (mod_idx(i) * N_MOD + 3, 0, 0)),
                  pl.BlockSpec((d, GATE_PAD), lambda i: (0, 0))],
        out_specs=(pl.BlockSpec((tm, d // 2), lambda i: (i, 0)),
                   pl.BlockSpec((tm, GATE_PAD), lambda i: (i, 0)),
                   pl.BlockSpec((8, GATE_PAD), lambda i: (0, 0))),
        scratch_shapes=[pltpu.VMEM((8, GATE_PAD), F32)],
        compiler_params=_cparams(("arbitrary",)),
        name="route",
    )(h, g, mod3, mod3, w_r)


def _sc_workers():
    info = pltpu.get_tpu_info().sparse_core
    return info.num_cores, info.num_subcores


def _sc_scatter_rows(x, pos, n_out):
    n, dw = x.shape
    nc, ns = _sc_workers()
    nw = nc * ns
    t_per_w = n // nw
    w = min(SC_WINDOW, t_per_w // 2)
    n_chunks = t_per_w // w
    assert n % nw == 0 and t_per_w % (2 * w) == 0
    pos_w = pos.reshape(2, nw, n_chunks, w).transpose(1, 0, 2, 3)
    mesh = plsc.VectorSubcoreMesh(core_axis_name="c", subcore_axis_name="s")

    @functools.partial(pl.kernel, mesh=mesh, out_type=jax.ShapeDtypeStruct((n_out, dw), x.dtype),
                       scratch_types=[pltpu.VMEM((2, n_chunks, w), jnp.int32), pltpu.VMEM((2, w, dw), x.dtype),
                                      pltpu.SemaphoreType.DMA((2,)), pltpu.SemaphoreType.DMA((2,))])
    def scatter(x_hbm, pos_hbm, out_hbm, idx_v, rows_v, gsem, osem):
        wid = lax.axis_index("s") * nc + lax.axis_index("c")
        base = wid * t_per_w
        pltpu.sync_copy(pos_hbm.at[wid], idx_v)

        def get(g, slot):
            return pltpu.make_async_copy(x_hbm.at[pl.ds(base + g * w, w)], rows_v.at[slot], gsem.at[slot])

        def put(g, slot, k):
            return pltpu.make_async_copy(rows_v.at[slot], out_hbm.at[idx_v.at[k, g]], osem.at[slot])

        get(0, 0).start()

        @pl.loop(0, n_chunks, step=2)
        def _(g):
            for b in range(2):
                gg = g + b
                get(gg, b).wait()

                @pl.when(gg + 1 < n_chunks)
                def _():
                    @pl.when(gg >= 1)
                    def _():
                        put(gg - 1, 1 - b, 0).wait()
                        put(gg - 1, 1 - b, 1).wait()
                    get(gg + 1, 1 - b).start()

                put(gg, b, 0).start()
                put(gg, b, 1).start()

        for k in range(2):
            put(n_chunks - 2, 0, k).wait()
            put(n_chunks - 1, 1, k).wait()

    return scatter(x, pos_w)


def _sc_gather_rows(table, idx):
    n = idx.shape[0]
    dw = table.shape[1]
    nc, ns = _sc_workers()
    nw = nc * ns
    b_per_w = n // nw
    w = min(SC_WINDOW, b_per_w // 2)
    n_chunks = b_per_w // w
    assert n % nw == 0 and b_per_w % (2 * w) == 0
    mesh = plsc.VectorSubcoreMesh(core_axis_name="c", subcore_axis_name="s")

    @functools.partial(pl.kernel, mesh=mesh, out_type=jax.ShapeDtypeStruct((n, dw), table.dtype),
                       scratch_types=[pltpu.VMEM((b_per_w,), jnp.int32), pltpu.VMEM((2, w, dw), table.dtype),
                                      pltpu.SemaphoreType.DMA((2,)), pltpu.SemaphoreType.DMA((2,))])
    def gather(table_hbm, idx_hbm, out_hbm, idx_v, rows_v, gsem, osem):
        wid = lax.axis_index("s") * nc + lax.axis_index("c")
        base = wid * b_per_w
        pltpu.sync_copy(idx_hbm.at[pl.ds(base, b_per_w)], idx_v)

        def get(g, slot):
            return pltpu.make_async_copy(table_hbm.at[idx_v.at[pl.ds(g * w, w)]], rows_v.at[slot], gsem.at[slot])

        def put(g, slot):
            return pltpu.make_async_copy(rows_v.at[slot], out_hbm.at[pl.ds(base + g * w, w)], osem.at[slot])

        get(0, 0).start()

        @pl.loop(0, n_chunks, step=2)
        def _(g):
            for b in range(2):
                gg = g + b
                get(gg, b).wait()

                @pl.when(gg + 1 < n_chunks)
                def _():
                    @pl.when(gg >= 1)
                    def _():
                        put(gg - 1, 1 - b).wait()
                    get(gg + 1, 1 - b).start()

                put(gg, b).start()

        put(n_chunks - 2, 0).wait()
        put(n_chunks - 1, 1).wait()

    return gather(table, idx)


def _gmm_kernel(te_ref, nv_ref, xs_ref, wg_ref, wu_ref, wd_ref, ys_ref, u_ref):
    @pl.when(pl.program_id(0) < nv_ref[0])
    def _():
        u_ref[...] = _unpack_bf16_pairs(xs_ref[...]).astype(BF16)
        ys_ref[...] = _pack_bf16_pairs(_swiglu_partial(u_ref, wg_ref.at[0], wu_ref.at[0], wd_ref.at[0]))


def _gmm(xs, tile_expert, n_valid, wg, wu, wd, *, tg):
    p, dw = xs.shape
    d = 2 * dw
    f = wg.shape[2]
    grid_spec = pltpu.PrefetchScalarGridSpec(
        num_scalar_prefetch=2,
        grid=(p // tg,),
        in_specs=[pl.BlockSpec((tg, dw), lambda i, te, nv: (i, 0)),
                  pl.BlockSpec((1, d, f), lambda i, te, nv: (te[i], 0, 0)),
                  pl.BlockSpec((1, d, f), lambda i, te, nv: (te[i], 0, 0)),
                  pl.BlockSpec((1, f, d), lambda i, te, nv: (te[i], 0, 0))],
        out_specs=pl.BlockSpec((tg, dw), lambda i, te, nv: (i, 0)),
        scratch_shapes=[pltpu.VMEM((tg, d), BF16)])
    return pl.pallas_call(
        _gmm_kernel,
        out_shape=jax.ShapeDtypeStruct((p, dw), jnp.uint32),
        grid_spec=grid_spec,
        compiler_params=_cparams(("arbitrary",)),
        name="gmm",
    )(tile_expert, n_valid, xs, wg, wu, wd)


def _combine_kernel(x_ref, info_ref, y1_ref, y2_ref, gt_ref, gf_ref, out_ref, *, final):
    info = info_ref[...]
    p1 = info[:, LANE_P1:LANE_P1 + 1]
    p2 = info[:, LANE_P2:LANE_P2 + 1]
    moe = p1 * _unpack_bf16_pairs(y1_ref[...]) + p2 * _unpack_bf16_pairs(y2_ref[...])
    y = x_ref[...] + gt_ref[0] * moe
    if final:
        r = lax.rsqrt(jnp.mean(y * y, axis=-1, keepdims=True) + EPS)
        y = (y * r) * gf_ref[...]
    out_ref[...] = y


def _combine(h, info, yg, mod3, g_final, mod_idx, *, tm, n_rows, final):
    d = h.shape[1]
    n_t = n_rows // tm
    kern = functools.partial(_combine_kernel, final=final)
    return pl.pallas_call(
        kern,
        out_shape=jax.ShapeDtypeStruct((n_rows, d), F32),
        grid=(n_t,),
        in_specs=[pl.BlockSpec((tm, d), lambda i: (i, 0)),
                  pl.BlockSpec((tm, GATE_PAD), lambda i: (i, 0)),
                  pl.BlockSpec((tm, d // 2), lambda i: (i, 0)),
                  pl.BlockSpec((tm, d // 2), lambda i: (n_t + i, 0)),
                  pl.BlockSpec((1, 1, d), lambda i: (mod_idx(i) * N_MOD + 5, 0, 0)),
                  pl.BlockSpec((1, d), lambda i: (0, 0))],
        out_specs=pl.BlockSpec((tm, d), lambda i: (i, 0)),
        compiler_params=_cparams(("arbitrary",)),
        name="combine",
    )(h, info, yg, yg, mod3, g_final)


def _moe(h, g, mod3, w_router, wg, wu, wd, g_final, mod_idx, *, tm, tg, n_rows, final):
    n_exp = wg.shape[0]
    up, info, cnt = _route_call(h, g, mod3, w_router, mod_idx, tm=tm, n_rows=n_rows)
    i1 = info[:, LANE_I1].astype(jnp.int32)
    i2 = info[:, LANE_I2].astype(jnp.int32)
    r1 = info[:, LANE_R1].astype(jnp.int32)
    r2 = info[:, LANE_R2].astype(jnp.int32)
    counts = cnt[0, :n_exp].astype(jnp.int32)
    padded = (counts + tg - 1) // tg * tg
    ends = jnp.cumsum(padded)
    starts = ends - padded
    eye = jnp.arange(n_exp, dtype=jnp.int32)
    pos1 = jnp.sum(jnp.where(i1[:, None] == eye, starts, 0), axis=1) + r1
    pos2 = jnp.sum(jnp.where(i2[:, None] == eye, starts, 0), axis=1) + r2
    p_rows = -(-(TOP_K * n_rows + n_exp * (tg - 1)) // tg) * tg
    tile_row = jnp.arange(p_rows // tg, dtype=jnp.int32) * tg
    tile_expert = jnp.minimum(jnp.sum(tile_row[:, None] >= ends[None, :], axis=1), n_exp - 1).astype(jnp.int32)
    n_valid = (ends[-1] // tg).reshape(1).astype(jnp.int32)
    xs = _sc_scatter_rows(up, jnp.stack([pos1, pos2]), p_rows)
    ys = _gmm(xs, tile_expert, n_valid, wg, wu, wd, tg=tg)
    yg = _sc_gather_rows(ys, jnp.concatenate([pos1, pos2]))
    return _combine(h, info, yg, mod3, g_final, mod_idx, tm=tm, n_rows=n_rows, final=final)


def _final_norm_kernel(x_ref, g_ref, o_ref):
    x = x_ref[...]
    r = lax.rsqrt(jnp.mean(x * x, axis=-1, keepdims=True) + EPS)
    o_ref[...] = (x * r) * g_ref[...]


def _final_norm(h, g, *, tm, n_rows):
    d = h.shape[1]
    return pl.pallas_call(
        _final_norm_kernel,
        out_shape=jax.ShapeDtypeStruct((n_rows, d), F32),
        grid=(n_rows // tm,),
        in_specs=[pl.BlockSpec((tm, d), lambda i: (i, 0)), pl.BlockSpec((1, d), lambda i: (0, 0))],
        out_specs=pl.BlockSpec((tm, d), lambda i: (i, 0)),
        compiler_params=_cparams(("arbitrary",)),
        name="final_norm",
    )(h, g)


def kernel(x, c, ctx, c_ctx, w_ada, b_ada, g_mix, w_in, conv_qk, b_if, head_gain, w_pool, pool_scale, w_pa, w_pb,
           w_out, g_ffn, w_ff_gate, w_ff_up, w_ff_down, w_router, w_exp_gate, w_exp_up, w_exp_down, g_final):
    b, s, d = x.shape
    lc = ctx.shape[1]
    depth = w_ada.shape[0]
    width = w_pa.shape[1]
    pw = w_pb.shape[1]
    n_gate = b_if.shape[1] * b_if.shape[2] * b_if.shape[3]
    assert lc == CHUNK and s % CHUNK == 0 and s % GRID_W == 0 and width == d and n_gate == 16
    n_lat, n_ctx = b * s, b * lc
    tm, tm_ffn, tm_moe = 512, 512, 1024
    assert all(s % t == 0 and n_ctx % t == 0 for t in (tm, tm_ffn, tm_moe))

    def mod_idx(t):
        return lambda i: jnp.where(i < n_lat // t, i // (s // t), b)

    src = (x.reshape(n_lat, d), ctx.reshape(n_ctx, d), 0)

    n_mod_rows = 16
    cc = jnp.zeros((n_mod_rows, d), F32).at[:b].set(c).at[b].set(c_ctx)
    mod = _ada(cc, w_ada, b_ada)

    if_off, pool_off, ga_off = 4 * width, 4 * width + n_gate, 4 * width + n_gate + pw
    n_main = 4 * width + 2 * d

    exp16 = []
    for l in range(depth):
        last = l == depth - 1
        mod3 = mod[l].reshape(n_mod_rows * N_MOD, 1, d)
        w_l = w_in[l]
        w_cat = jnp.concatenate(
            [w_l[:, :if_off], w_l[:, ga_off:], w_l[:, pool_off:ga_off], w_l[:, if_off:pool_off],
             jnp.zeros((d, GATE_PAD - n_gate), F32)], axis=1).astype(BF16)
        k_scale = float((width // M_HEADS) ** -0.5)
        zm, zp, zg = _inproj(src, g_mix[l].reshape(1, d), mod3, w_cat, conv_qk[l], mod_idx(tm), tm=tm, n_main=n_main,
                             n_pool=pw, n_qk=2 * width, k_scale=k_scale, n_lat=n_lat, lat_len=s, ctx_len=lc)

        bias = b_if[l].reshape(n_gate)
        bias_c = jnp.zeros((1, GATE_PAD), F32).at[0, :n_gate].set(bias)
        bias_r = jnp.broadcast_to(bias[:, None], (n_gate, CHUNK))
        hf, hb = _mlstm(zm, zg, zg[:, :n_gate].T, bias_c, bias_r, b=b, s=s, width=width, dirs=(0, 1))

        n_rows = n_lat if last else n_lat + n_ctx
        pm_lat = _pool(zp, n_seq=b, seq_len=s, row_block0=0, grid_rows=s // GRID_W)
        pm_ctx = pm_lat if last else _pool(zp, n_seq=b, seq_len=lc, row_block0=n_lat // lc, grid_rows=None)

        w_pool_pb = _fold_pool(w_pool[l], pool_scale[l].reshape(1, pw), w_pb[l])
        h = _mixout(hf, hb, zm, pm_lat, pm_ctx, src, mod3, head_gain[l].reshape(1, width), w_pa[l].astype(BF16),
                    w_pool_pb, w_out[l].astype(BF16), mod_idx(tm), tm=tm, n_rows=n_rows)

        j = l // 2
        if l % 2 == 0:
            side = ()
            if not last:
                side = tuple(w[(l + 1) // 2].reshape(-1, w.shape[-1]) for w in (w_exp_gate, w_exp_up, w_exp_down))
            h, *exp16 = _ffn(h, g_ffn[l].reshape(1, d), mod3, w_ff_gate[j].astype(BF16), w_ff_up[j].astype(BF16),
                             w_ff_down[j].astype(BF16), mod_idx(tm_ffn), tm=tm_ffn,
                             tf=_pick_tile(w_ff_gate.shape[2], 2816), side=side)
            if last:
                h = _final_norm(h, g_final.reshape(1, d), tm=tm, n_rows=n_lat)
        else:
            if not exp16:
                exp16 = [w[j].astype(BF16) for w in (w_exp_gate, w_exp_up, w_exp_down)]
            wg16, wu16, wd16 = (w16.reshape(w.shape[1:]) for w16, w in zip(exp16, (w_exp_gate, w_exp_up, w_exp_down)))
            h = _moe(h, g_ffn[l].reshape(1, d), mod3, w_router[j], wg16, wu16, wd16, g_final.reshape(1, d),
                     mod_idx(tm_moe), tm=tm_moe, tg=512, n_rows=h.shape[0], final=last)
            exp16 = []
        src = (h, h, n_lat)
    return h[:n_lat].reshape(b, s, d)
```
